```python
import jax, jax.numpy as jnp
from jax import lax
import numpy as np

D_MODEL = 2048
BATCH = 2
SEQ = 4096
DEPTH = 2
DEC_BATCH = 32
DEC_SEQ = 1
PAST_LEN = 8192
PAGE_SIZE = 128

A_GROUPS = 8
A_GROUP_DIM = 128
A_WIDTH = A_GROUPS * A_GROUP_DIM
CHUNK = 128
N_HEADS = 16
N_KV = 4
HEAD_DIM = 64
HPG = N_HEADS // N_KV
B_WIDTH = N_HEADS * HEAD_DIM
KV_WIDTH = 2 * N_KV * HEAD_DIM
N_BRANCH = 3
CMP_LEN = 32
CMP_STRIDE = 16
CMP_HID = 2 * HEAD_DIM
SLC_LEN = 64
N_SEL = 16
WINDOW = 512
Q_BLOCK = 128
FORCE_BONUS = 1e4
ROPE_THETA = 10000.0
AB_IN = 2 * A_WIDTH + B_WIDTH + 3 * KV_WIDTH + N_BRANCH * N_HEADS
AB_MIX = A_WIDTH + B_WIDTH
D_RNN = D_MODEL
RNN_BLOCKS = 16
RNN_BS = D_RNN // RNN_BLOCKS
CONV_W = 4
RG_C = 8.0
N_EXPERTS = 16
N_GROUPS = 4
EXPERTS_PER_GROUP = N_EXPERTS // N_GROUPS
TOP_K = 2
D_EXPERT = 1024
MOE_BLOCK = 128
ALPHA = (2 * DEPTH) ** 0.25
BETA = (8 * DEPTH) ** -0.25
LN_EPS = 1e-5

kernel_name = 'hybrid_gmlp_nsa_rglru_moe_step'


def layer_norm(x, g, b):
    xf = x.astype(jnp.float32)
    mu = jnp.mean(xf, -1, keepdims=True)
    var = jnp.mean(jnp.square(xf - mu), -1, keepdims=True)
    return ((xf - mu) * lax.rsqrt(var + LN_EPS)).astype(x.dtype) * g + b


def ada_mod(c, w, b):
    m = jax.nn.silu(c) @ w + b
    return jnp.split(m[:, None, :], 3, axis=-1)


def rope(x, pos):
    half = HEAD_DIM // 2
    inv = ROPE_THETA ** (-jnp.arange(half, dtype=jnp.float32) / half)
    ang = pos.astype(jnp.float32)[:, None] * inv[None, :]
    cos = jnp.cos(ang)[:, None, :]
    sin = jnp.sin(ang)[:, None, :]
    x1 = x[..., :half].astype(jnp.float32)
    x2 = x[..., half:].astype(jnp.float32)
    return jnp.concatenate([x1 * cos - x2 * sin, x2 * cos + x1 * sin], -1).astype(x.dtype)


def masked_softmax(s, mask):
    s = jnp.where(mask, s.astype(jnp.float32), -jnp.inf)
    m = jnp.max(s, -1, keepdims=True)
    m = jnp.where(jnp.isfinite(m), m, 0.0)
    p = jnp.exp(s - m)
    return p / jnp.maximum(jnp.sum(p, -1, keepdims=True), 1e-30)


def gmlp_mix(u, v, ws, bs):
    Bn, T = u.shape[:2]
    cl = min(T, CHUNK)
    nc = T // cl
    w = ws[:, :cl, :cl] * jnp.tril(jnp.ones((cl, cl), ws.dtype))
    vc = v.reshape(Bn, nc, cl, A_GROUPS, A_GROUP_DIM)
    mix = jnp.einsum('gts,bnsgc->bntgc', w, vc) + bs[:, :cl].T[None, None, :, :, None]
    return u * mix.reshape(u.shape)


def ab_project(h, pos, w_in, v_ln_g, v_ln_b, ws, bs):
    Bn, T = h.shape[:2]
    o1 = A_WIDTH
    o2 = 2 * A_WIDTH
    o3 = o2 + B_WIDTH
    o4 = o3 + KV_WIDTH
    o5 = o4 + KV_WIDTH
    o6 = o5 + KV_WIDTH
    z = h @ w_in
    u, v, q, kvc, kvs, kvw, g = jnp.split(z, [o1, o2, o3, o4, o5, o6], axis=-1)
    v = layer_norm(v.reshape(Bn, T, A_GROUPS, A_GROUP_DIM), v_ln_g, v_ln_b)
    a_out = gmlp_mix(u.reshape(Bn, T, A_GROUPS, A_GROUP_DIM), v, ws, bs).reshape(Bn, T, A_WIDTH)
    q = rope(q.reshape(Bn, T, N_HEADS, HEAD_DIM), pos)

    def kv(t):
        t = t.reshape(Bn, T, 2, N_KV, HEAD_DIM)
        return jnp.stack([rope(t[:, :, 0], pos), t[:, :, 1]], axis=2)

    gates = jax.nn.sigmoid(g.reshape(Bn, T, N_HEADS, N_BRANCH))
    return a_out, v.reshape(Bn, T, A_WIDTH), q, kv(kvc), kv(kvs), kv(kvw), gates


def nsa_compress(k, pe, w1, w2):
    Bn, L = k.shape[:2]
    n16 = L // CMP_STRIDE
    ch = k[:, :n16 * CMP_STRIDE].reshape(Bn, n16, CMP_STRIDE, N_KV, HEAD_DIM)
    lo = jnp.einsum('bnsgd,sdh->bngh', ch + pe[:CMP_STRIDE, None, :], w1[:CMP_STRIDE])
    hi = jnp.einsum('bnsgd,sdh->bngh', ch + pe[CMP_STRIDE:, None, :], w1[CMP_STRIDE:])
    return jax.nn.gelu(lo[:, :-1] + hi[:, 1:]) @ w2


def to_slc_blocks(k):
    Bn, L = k.shape[:2]
    n_slc = -(-L // SLC_LEN)
    kp = jnp.pad(k, ((0, 0), (0, n_slc * SLC_LEN - L), (0, 0), (0, 0)))
    return kp.reshape(Bn, n_slc, SLC_LEN, N_KV, HEAD_DIM).transpose(0, 3, 1, 2, 4)


def cmp_to_slc_map(n_cmp, n_slc):
    cs = np.arange(n_cmp)[:, None] * CMP_STRIDE
    ss = np.arange(n_slc)[None, :] * SLC_LEN
    return jnp.asarray(((cs < ss + SLC_LEN) & (cs + CMP_LEN > ss)).astype(np.float32))


def nsa_block(q, q_pos, kc, vc, ks_blk, vs_blk, kw, vw, kw_pos, gates):
    Bn, T = q.shape[:2]
    scale = HEAD_DIM ** -0.5
    qg = q.reshape(Bn, T, N_KV, HPG, HEAD_DIM)
    n_cmp = kc.shape[1]
    n_slc = ks_blk.shape[2]
    n_sel = min(N_SEL, n_slc)
    cmp_end = jnp.arange(n_cmp) * CMP_STRIDE + CMP_LEN - 1
    p_c = masked_softmax(jnp.einsum('btghd,bngd->bghtn', qg, kc) * scale,
                         cmp_end[None, :] <= q_pos[:, None])
    o_c = jnp.einsum('bghtn,bngd->btghd', p_c.astype(vc.dtype), vc)
    imp = jnp.einsum('bgtn,ns->bgts', jnp.sum(p_c, 2), cmp_to_slc_map(n_cmp, n_slc))
    blk = jnp.arange(n_slc)[None, :]
    cur = (q_pos // SLC_LEN)[:, None]
    forced = (blk == 0) | (blk == cur) | (blk == cur - 1)
    score = jnp.where(blk <= cur, imp + jnp.where(forced, FORCE_BONUS, 0.0), -jnp.inf)
    top_s, top_i = lax.top_k(score, n_sel)
    gather = jax.vmap(jax.vmap(lambda kb, ix: kb[ix]))
    k_sel = gather(ks_blk, top_i).reshape(Bn, N_KV, T, n_sel * SLC_LEN, HEAD_DIM)
    v_sel = gather(vs_blk, top_i).reshape(Bn, N_KV, T, n_sel * SLC_LEN, HEAD_DIM)
    pos_sel = (top_i[..., None] * SLC_LEN + jnp.arange(SLC_LEN)).reshape(Bn, N_KV, T, n_sel * SLC_LEN)
    m_sel = (pos_sel <= q_pos[:, None]) & jnp.repeat(jnp.isfinite(top_s), SLC_LEN, axis=-1)
    p_s = masked_softmax(jnp.einsum('btghd,bgtkd->bghtk', qg, k_sel) * scale, m_sel[:, :, None])
    o_s = jnp.einsum('bghtk,bgtkd->btghd', p_s.astype(v_sel.dtype), v_sel)
    kp = kw_pos[None, :]
    qp = q_pos[:, None]
    m_w = (kp >= 0) & (kp <= qp) & (kp > qp - WINDOW)
    p_w = masked_softmax(jnp.einsum('btghd,blgd->bghtl', qg, kw) * scale, m_w)
    o_w = jnp.einsum('bghtl,blgd->btghd', p_w.astype(vw.dtype), vw)
    g = gates.reshape(Bn, T, N_KV, HPG, N_BRANCH).astype(o_c.dtype)
    o = g[..., 0:1] * o_c + g[..., 1:2] * o_s + g[..., 2:3] * o_w
    return o.reshape(Bn, T, B_WIDTH)


def nsa_prompt(q, kv_c, kv_s, kv_w, gates, pe, w1, w2):
    Bn, S = q.shape[:2]
    kc = nsa_compress(kv_c[:, :, 0], pe[0], w1[0], w2[0])
    vc = nsa_compress(kv_c[:, :, 1], pe[1], w1[1], w2[1])
    ks = to_slc_blocks(kv_s[:, :, 0])
    vs = to_slc_blocks(kv_s[:, :, 1])
    kw_pad = jnp.pad(kv_w, ((0, 0), (WINDOW, 0), (0, 0), (0, 0), (0, 0)))

    def body(i):
        q0 = i * Q_BLOCK
        wb = lax.dynamic_slice_in_dim(kw_pad, q0, WINDOW + Q_BLOCK, axis=1)
        return nsa_block(lax.dynamic_slice_in_dim(q, q0, Q_BLOCK, 1), q0 + jnp.arange(Q_BLOCK),
                         kc, vc, ks, vs, wb[:, :, 0], wb[:, :, 1],
                         q0 - WINDOW + jnp.arange(WINDOW + Q_BLOCK),
                         lax.dynamic_slice_in_dim(gates, q0, Q_BLOCK, 1))

    out = lax.map(body, jnp.arange(S // Q_BLOCK))
    return out.transpose(1, 0, 2, 3).reshape(Bn, S, B_WIDTH)


def nsa_sample(q, kv_c, kv_s, kv_w, gates, pos, cache_c, cache_s, win_buf, page_table, pe, w1, w2):
    def past(cache):
        rows = cache[page_table]
        return rows.reshape(rows.shape[0], -1, *rows.shape[3:])

    full_c = jnp.concatenate([past(cache_c), kv_c], 1)
    full_s = jnp.concatenate([past(cache_s), kv_s], 1)
    kc = nsa_compress(full_c[:, :, 0], pe[0], w1[0], w2[0])
    vc = nsa_compress(full_c[:, :, 1], pe[1], w1[1], w2[1])
    ks = to_slc_blocks(full_s[:, :, 0])
    vs = to_slc_blocks(full_s[:, :, 1])
    wb = win_buf.shape[1]
    kw = jnp.concatenate([win_buf, kv_w], 1)
    lw = kw.shape[1]
    o = nsa_block(q, pos, kc, vc, ks, vs, kw[:, :, 0], kw[:, :, 1], pos[0] - wb + jnp.arange(lw), gates)
    keep = min(WINDOW, lw)
    return o, kw[:, lw - keep:]


def causal_conv(x, buf, w, b):
    T = x.shape[1]
    xp = jnp.concatenate([buf.astype(x.dtype), x], 1)
    y = b + sum(xp[:, k:k + T] * w[k] for k in range(CONV_W))
    return y, xp[:, -(CONV_W - 1):]


def block_diag(x, w, b):
    xb = x.reshape(*x.shape[:-1], RNN_BLOCKS, RNN_BS)
    return jnp.einsum('btni,nij->btnj', xb, w).reshape(x.shape) + b


def rglru(x, h0, wa, ba, wx, bx, lam):
    r = jax.nn.sigmoid(block_diag(x, wa, ba)).astype(jnp.float32)
    i = jax.nn.sigmoid(block_diag(x, wx, bx)).astype(jnp.float32)
    log_a = -RG_C * r * jax.nn.softplus(-lam.astype(jnp.float32))
    a = jnp.exp(log_a)
    bterm = jnp.sqrt(-jnp.expm1(2.0 * log_a)) * (i * x.astype(jnp.float32))

    def step(h, inp):
        a_t, b_t = inp
        h = a_t * h + b_t
        return h, h

    hT, hs = lax.scan(step, h0.astype(jnp.float32), (a.transpose(1, 0, 2), bterm.transpose(1, 0, 2)))
    return hs.transpose(1, 0, 2).astype(x.dtype), hT.astype(x.dtype)


def rec_mixer(h, conv_buf, h0, w_in, conv_w, conv_b, wa, ba, wx, bx, lam, w_out):
    gate_br, x_br = jnp.split(h @ w_in, 2, axis=-1)
    xc, new_buf = causal_conv(x_br, conv_buf, conv_w, conv_b)
    y, hT = rglru(xc, h0, wa, ba, wx, bx, lam)
    return (jax.nn.gelu(gate_br) * y) @ w_out, new_buf, hT


def route(x, w_r, b_r):
    N = x.shape[0]
    s = jax.nn.sigmoid((x @ w_r).astype(jnp.float32))
    sb = s + b_r.astype(jnp.float32)
    gscore = jnp.sum(lax.top_k(sb.reshape(N, N_GROUPS, EXPERTS_PER_GROUP), TOP_K)[0], -1)
    g_sel = jnp.argmax(gscore, -1)
    in_grp = (jnp.arange(N_EXPERTS) // EXPERTS_PER_GROUP)[None, :] == g_sel[:, None]
    _, idx = lax.top_k(jnp.where(in_grp, sb, -jnp.inf), TOP_K)
    w = jnp.take_along_axis(s, idx, -1)
    return idx, w / jnp.sum(w, -1, keepdims=True)


def moe_ffn(x, idx, wts, w_in, w_out):
    N, D = x.shape
    A = N * TOP_K
    blk = min(MOE_BLOCK, A)
    n_blocks = -(-A // blk) + N_EXPERTS
    e_flat = idx.reshape(-1)
    tok = jnp.repeat(jnp.arange(N, dtype=jnp.int32), TOP_K)
    order = jnp.argsort(e_flat)
    e_sorted = e_flat[order]
    counts = jnp.bincount(e_flat, length=N_EXPERTS)
    padded = (counts + blk - 1) // blk * blk
    pad_end = jnp.cumsum(padded)
    pad_start = pad_end - padded
    start = jnp.cumsum(counts) - counts
    dest = pad_start[e_sorted] + jnp.arange(A) - start[e_sorted]
    n_slots = n_blocks * blk
    slot_tok = jnp.zeros((n_slots,), jnp.int32).at[dest].set(tok[order])
    slot_w = jnp.zeros((n_slots,), jnp.float32).at[dest].set(wts.reshape(-1)[order])
    block_e = jnp.minimum(jnp.searchsorted(pad_end, jnp.arange(n_blocks) * blk, side='right'), N_EXPERTS - 1)
    xs = x[slot_tok].reshape(n_blocks, blk, D)

    def expert_block(args):
        xb, e = args
        g, u = jnp.split(xb @ w_in[e], 2, axis=-1)
        return (jax.nn.silu(g) * u) @ w_out[e]

    ys = lax.map(expert_block, (xs, block_e)).reshape(n_slots, D)
    return jnp.zeros_like(x).at[slot_tok].add(ys * slot_w[:, None].astype(ys.dtype))


def channel_sublayer(x, c, layer, ada_w, ada_b, ln_g, ln_b, router_w, router_b, moe_w_in, moe_w_out):
    shift, scale, gate = ada_mod(c, ada_w[layer, 1], ada_b[layer, 1])
    h = x * (1 + scale) + shift
    Bn, T, D = h.shape
    hf = h.reshape(Bn * T, D)
    idx, w = route(hf, router_w, router_b)
    f = moe_ffn(hf, idx, w, moe_w_in[layer], moe_w_out[layer]).reshape(Bn, T, D)
    return layer_norm(ALPHA * x + gate * f, ln_g[layer, 1], ln_b[layer, 1])


def setup_inputs(seed: int = 0) -> dict:
    key = jax.random.key(seed)
    keys = iter(jax.random.split(key, 48))

    def nrm(shape, scale):
        return jax.random.normal(next(keys), shape, jnp.float32) * scale

    n_pages = PAST_LEN // PAGE_SIZE
    n_used = DEC_BATCH * n_pages
    n_pool = n_used + max(1, n_used // 4)
    win_buf = min(WINDOW, PAST_LEN)
    page_table = jax.random.permutation(next(keys), n_pool)[:n_used].reshape(DEC_BATCH, n_pages).astype(jnp.int32)
    a_init = jax.random.uniform(next(keys), (D_RNN,), jnp.float32, 0.9, 0.999) ** (1.0 / RG_C)
    D = D_MODEL
    return {
        'x_prompt': nrm((BATCH, SEQ, D), 1.0),
        'x_sample': nrm((DEC_BATCH, DEC_SEQ, D), 1.0),
        'c_prompt': nrm((BATCH, D), 1.0),
        'c_sample': nrm((DEC_BATCH, D), 1.0),
        'cache_nsa_cmp': nrm((n_pool, PAGE_SIZE, 2, N_KV, HEAD_DIM), 1.0),
        'cache_nsa_slc': nrm((n_pool, PAGE_SIZE, 2, N_KV, HEAD_DIM), 1.0),
        'state_nsa_win': nrm((DEC_BATCH, win_buf, 2, N_KV, HEAD_DIM), 1.0),
        'state_rglru_conv': nrm((DEC_BATCH, CONV_W - 1, D_RNN), 1.0),
        'state_rglru_h': nrm((DEC_BATCH, D_RNN), 0.5),
        'page_table': page_table,
        'ada_w': nrm((DEPTH, 2, D, 3 * D), D ** -0.5),
        'ada_b': nrm((DEPTH, 2, 3 * D), 0.02),
        'ln_g': 1.0 + nrm((DEPTH, 2, D), 0.02),
        'ln_b': nrm((DEPTH, 2, D), 0.02),
        'ab_w_in': nrm((D, AB_IN), D ** -0.5),
        'ab_w_out': nrm((AB_MIX, D), AB_MIX ** -0.5 * BETA),
        'gmlp_ln_g': 1.0 + nrm((A_GROUPS, A_GROUP_DIM), 0.02),
        'gmlp_ln_b': nrm((A_GROUPS, A_GROUP_DIM), 0.02),
        'gmlp_ws': nrm((A_GROUPS, CHUNK, CHUNK), 0.5 * CHUNK ** -0.5),
        'gmlp_bs': 1.0 + nrm((A_GROUPS, CHUNK), 0.02),
        'nsa_cmp_pe': nrm((2, CMP_LEN, HEAD_DIM), 0.02),
        'nsa_cmp_w1': nrm((2, CMP_LEN, HEAD_DIM, CMP_HID), (CMP_LEN * HEAD_DIM) ** -0.5),
        'nsa_cmp_w2': nrm((2, CMP_HID, HEAD_DIM), CMP_HID ** -0.5),
        'rec_w_in': nrm((D, 2 * D_RNN), D ** -0.5),
        'rec_conv_w': nrm((CONV_W, D_RNN), CONV_W ** -0.5),
        'rec_conv_b': nrm((D_RNN,), 0.02),
        'rg_wa': nrm((RNN_BLOCKS, RNN_BS, RNN_BS), RNN_BS ** -0.5),
        'rg_ba': nrm((D_RNN,), 0.02),
        'rg_wx': nrm((RNN_BLOCKS, RNN_BS, RNN_BS), RNN_BS ** -0.5),
        'rg_bx': nrm((D_RNN,), 0.02),
        'rg_lambda': jnp.log(a_init) - jnp.log1p(-a_init),
        'rec_w_out': nrm((D_RNN, D), D_RNN ** -0.5 * BETA),
        'router_w': nrm((D, N_EXPERTS), D ** -0.5),
        'router_b': nrm((N_EXPERTS,), 0.01),
        'moe_w_in': nrm((DEPTH, N_EXPERTS, D, 2 * D_EXPERT), D ** -0.5),
        'moe_w_out': nrm((DEPTH, N_EXPERTS, D_EXPERT, D), D_EXPERT ** -0.5 * BETA),
    }


def reference(x_prompt, x_sample, c_prompt, c_sample, cache_nsa_cmp, cache_nsa_slc, state_nsa_win,
              state_rglru_conv, state_rglru_h, page_table, ada_w, ada_b, ln_g, ln_b, ab_w_in, ab_w_out,
              gmlp_ln_g, gmlp_ln_b, gmlp_ws, gmlp_bs, nsa_cmp_pe, nsa_cmp_w1, nsa_cmp_w2, rec_w_in,
              rec_conv_w, rec_conv_b, rg_wa, rg_ba, rg_wx, rg_bx, rg_lambda, rec_w_out, router_w,
              router_b, moe_w_in, moe_w_out):
    past_len = page_table.shape[1] * cache_nsa_cmp.shape[1]
    Bp, S = x_prompt.shape[:2]
    T = x_sample.shape[1]
    pos_p = jnp.arange(S, dtype=jnp.int32)
    pos_s = past_len + jnp.arange(T, dtype=jnp.int32)
    xp, xs = x_prompt, x_sample
    for layer in range(DEPTH):
        sh_p, sc_p, g_p = ada_mod(c_prompt, ada_w[layer, 0], ada_b[layer, 0])
        sh_s, sc_s, g_s = ada_mod(c_sample, ada_w[layer, 0], ada_b[layer, 0])
        hp = xp * (1 + sc_p) + sh_p
        hs = xs * (1 + sc_s) + sh_s
        if layer % 2 == 0:
            a_p, _, q_p, kvc_p, kvs_p, kvw_p, gt_p = ab_project(hp, pos_p, ab_w_in, gmlp_ln_g, gmlp_ln_b, gmlp_ws, gmlp_bs)
            o_p = nsa_prompt(q_p, kvc_p, kvs_p, kvw_p, gt_p, nsa_cmp_pe, nsa_cmp_w1, nsa_cmp_w2)
            fp = jnp.concatenate([a_p, o_p], -1) @ ab_w_out
            a_s, gmlp_v_s, q_s, kvc_s, kvs_s, kvw_s, gt_s = ab_project(hs, pos_s, ab_w_in, gmlp_ln_g, gmlp_ln_b, gmlp_ws, gmlp_bs)
            o_s, win_s = nsa_sample(q_s, kvc_s, kvs_s, kvw_s, gt_s, pos_s, cache_nsa_cmp, cache_nsa_slc,
                                    state_nsa_win, page_table, nsa_cmp_pe, nsa_cmp_w1, nsa_cmp_w2)
            fs = jnp.concatenate([a_s, o_s], -1) @ ab_w_out
            win_p = kvw_p[:, S - min(WINDOW, S):]
        else:
            fp, conv_p, h_p = rec_mixer(hp, jnp.zeros((Bp, CONV_W - 1, D_RNN), hp.dtype), jnp.zeros((Bp, D_RNN), hp.dtype),
                                        rec_w_in, rec_conv_w, rec_conv_b, rg_wa, rg_ba, rg_wx, rg_bx, rg_lambda, rec_w_out)
            fs, conv_s, h_s = rec_mixer(hs, state_rglru_conv, state_rglru_h,
                                        rec_w_in, rec_conv_w, rec_conv_b, rg_wa, rg_ba, rg_wx, rg_bx, rg_lambda, rec_w_out)
        xp = layer_norm(ALPHA * xp + g_p * fp, ln_g[layer, 0], ln_b[layer, 0])
        xs = layer_norm(ALPHA * xs + g_s * fs, ln_g[layer, 0], ln_b[layer, 0])
        xp = channel_sublayer(xp, c_prompt, layer, ada_w, ada_b, ln_g, ln_b, router_w, router_b, moe_w_in, moe_w_out)
        xs = channel_sublayer(xs, c_sample, layer, ada_w, ada_b, ln_g, ln_b, router_w, router_b, moe_w_in, moe_w_out)
    return (xp, xs, kvc_p, kvc_s, kvs_p, kvs_s, win_p, win_s, gmlp_v_s, conv_p, conv_s, h_p, h_s)
```

```python
import functools
import math

import numpy as np
import jax
import jax.numpy as jnp
from jax import lax
from jax.experimental import pallas as pl
from jax.experimental.pallas import tpu as pltpu

F32 = jnp.float32
BF16 = jnp.bfloat16
I32 = jnp.int32
HI = lax.Precision.HIGHEST

A_GROUPS = 8
A_GROUP_DIM = 128
A_WIDTH = A_GROUPS * A_GROUP_DIM
CHUNK = 128
N_HEADS = 16
N_KV = 4
HEAD_DIM = 64
HPG = N_HEADS // N_KV
B_WIDTH = N_HEADS * HEAD_DIM
KV_WIDTH = 2 * N_KV * HEAD_DIM
N_BRANCH = 3
CMP_LEN = 32
CMP_STRIDE = 16
CMP_HID = 2 * HEAD_DIM
SLC_LEN = 64
N_SEL = 16
WINDOW = 512
Q_BLOCK = 128
FORCE_BONUS = 1e4
ROPE_THETA = 10000.0
RNN_BLOCKS = 16
CONV_W = 4
RG_C = 8.0
N_EXPERTS = 16
N_GROUPS = 4
EXPERTS_PER_GROUP = N_EXPERTS // N_GROUPS
TOP_K = 2
DEPTH = 2
ALPHA = (2 * DEPTH) ** 0.25
LN_EPS = 1e-5
NEG = -1e30

O_U = 0
O_V = A_WIDTH
O_Q = 2 * A_WIDTH
O_KV = O_Q + B_WIDTH
O_G = O_KV + 3 * KV_WIDTH
N_GATE = N_BRANCH * N_HEADS

VMEM_LIMIT_BYTES = 56 * 1024 * 1024


def _cparams(sem, vmem=None):
    return pltpu.CompilerParams(dimension_semantics=sem, vmem_limit_bytes=vmem)


def _dot(a, b, precise=False):
    if precise:
        return jnp.dot(a, b, preferred_element_type=F32, precision=HI)
    return jnp.dot(a.astype(BF16), b.astype(BF16), preferred_element_type=F32)


def _dot_nt(a, b, precise=False):
    dn = (((1,), (1,)), ((), ()))
    if precise:
        return lax.dot_general(a, b, dn, preferred_element_type=F32, precision=HI)
    return lax.dot_general(a.astype(BF16), b.astype(BF16), dn, preferred_element_type=F32)


def _ln(x):
    mu = jnp.mean(x, -1, keepdims=True)
    xc = x - mu
    var = jnp.mean(xc * xc, -1, keepdims=True)
    return xc * lax.rsqrt(var + LN_EPS)


def _silu(x):
    return x * jax.nn.sigmoid(x)


def _gelu(x):
    return jax.nn.gelu(x, approximate=True)


def _rope(x, cos2, sin2):
    w = x.shape[1]
    rep = w // 128
    cos = jnp.concatenate([cos2] * rep, axis=1) if rep > 1 else cos2
    sin = jnp.concatenate([sin2] * rep, axis=1) if rep > 1 else sin2
    lane = lax.broadcasted_iota(I32, x.shape, 1)
    first = (lane & (HEAD_DIM - 1)) < HEAD_DIM // 2
    rot = jnp.where(first, pltpu.roll(x, w - HEAD_DIM // 2, 1), pltpu.roll(x, HEAD_DIM // 2, 1))
    return x * cos + rot * sin


def _rope_tables(pos):
    half = HEAD_DIM // 2
    inv = ROPE_THETA ** (-np.arange(half, dtype=np.float64) / half)
    ang = np.asarray(pos, np.float64)[:, None] * inv[None, :]
    cos = np.tile(np.cos(ang), (1, 4))
    sin = np.tile(np.concatenate([-np.sin(ang), np.sin(ang)], axis=1), (1, 2))
    return jnp.asarray(cos, F32), jnp.asarray(sin, F32)


def _softmax_rows(s, mask):
    sm = jnp.where(mask, s, NEG)
    m = jnp.max(sm, -1, keepdims=True)
    p = jnp.where(mask, jnp.exp(sm - m), 0.0)
    return p, jnp.sum(p, -1, keepdims=True)


def _ada_kernel(c_ref, w_ref, b_ref, o_ref):
    o_ref[...] = _dot(_silu(c_ref[...]), w_ref[...], precise=True) + b_ref[...]


def _ada_all(c_all, ada_w, ada_b):
    r, d = c_all.shape
    n_mod = ada_w.shape[0] * ada_w.shape[1]
    d3 = ada_w.shape[-1]
    tn = 512
    return pl.pallas_call(
        _ada_kernel,
        grid=(n_mod, d3 // tn),
        in_specs=[
            pl.BlockSpec((r, d), lambda l, j: (0, 0)),
            pl.BlockSpec((None, d, tn), lambda l, j: (l, 0, j)),
            pl.BlockSpec((None, 1, tn), lambda l, j: (l, 0, j)),
        ],
        out_specs=pl.BlockSpec((None, r, tn), lambda l, j: (l, 0, j)),
        out_shape=jax.ShapeDtypeStruct((n_mod, r, d3), F32),
        compiler_params=_cparams(("parallel", "parallel")),
        name="ada_mod",
    )(c_all, ada_w.reshape(n_mod, d, d3), ada_b.reshape(n_mod, 1, d3))


def _small_mm_kernel(*refs, n_x, pre):
    xs = [r[...] for r in refs[:n_x]]
    w_ref, o_ref = refs[n_x], refs[n_x + 1]
    o_ref[...] = _dot(pre(*xs), w_ref[...], precise=True)


def _small_mm(pre, xs, w, n_out, tn, name):
    m = xs[0].shape[0]
    k = w.shape[0]
    in_specs = [pl.BlockSpec(x.shape, lambda j, nd=x.ndim: (0,) * nd) for x in xs]
    in_specs.append(pl.BlockSpec((k, tn), lambda j: (0, j)))
    return pl.pallas_call(
        functools.partial(_small_mm_kernel, n_x=len(xs), pre=pre),
        grid=(n_out // tn,),
        in_specs=in_specs,
        out_specs=pl.BlockSpec((m, tn), lambda j: (0, j)),
        out_shape=jax.ShapeDtypeStruct((m, n_out), F32),
        compiler_params=_cparams(("parallel",)),
        name=name,
    )(*xs, w)


def _vmem_call(fn, out_shapes, args, name):
    n_in = len(args)

    def kern(*refs):
        res = fn(*[r[...] for r in refs[:n_in]])
        for o, v in zip(refs[n_in:], res):
            o[...] = v

    return pl.pallas_call(kern, out_shape=out_shapes, name=name)(*args)


def _modulate(x, sc, sh):
    return x * (1.0 + sc) + sh


def _ab_proj_kernel(x_ref, sc_ref, sh_ref, w_ref, cos_ref, sin_ref, lng_ref, lnb_ref, wtril_ref, bst_ref,
                    a_ref, q_ref, kvc_ref, kvs_ref, kvw_ref, kvt_ref, gt_ref, *, tm):
    hb = _modulate(x_ref[...], sc_ref[...], sh_ref[...]).astype(BF16)

    def proj(lo, hi):
        return jnp.dot(hb, w_ref[:, lo:hi], preferred_element_type=F32)

    cos2 = cos_ref[...]
    sin2 = sin_ref[...]
    zu = proj(O_U, O_V)
    zv = proj(O_V, O_Q)
    for g in range(A_GROUPS):
        sl = slice(g * A_GROUP_DIM, (g + 1) * A_GROUP_DIM)
        vg = _ln(zv[:, sl]) * lng_ref[:, sl] + lnb_ref[:, sl]
        for c in range(tm // CHUNK):
            rs = slice(c * CHUNK, (c + 1) * CHUNK)
            mix = jnp.dot(wtril_ref[g], vg[rs].astype(BF16), preferred_element_type=F32) + bst_ref[:, g:g + 1]
            a_ref[rs, sl] = zu[rs, sl] * mix
    q_ref[...] = _rope(proj(O_Q, O_KV), cos2, sin2)
    half = KV_WIDTH // 2
    for br, ref in enumerate((kvc_ref, kvs_ref, kvw_ref)):
        lo = O_KV + br * KV_WIDTH
        k = _rope(proj(lo, lo + half), cos2, sin2)
        v = proj(lo + half, lo + KV_WIDTH)
        ref[:, 0:half] = k
        ref[:, half:KV_WIDTH] = v
        for g in range(N_KV):
            hs = slice(g * HEAD_DIM, (g + 1) * HEAD_DIM)
            kvt_ref[br, 0, g] = k[:, hs]
            kvt_ref[br, 1, g] = v[:, hs]
    zg = jax.nn.sigmoid(proj(O_G, O_G + N_GATE))
    per_g = N_GATE // N_KV
    for g in range(N_KV):
        gt_ref[g] = zg[:, g * per_g:(g + 1) * per_g]


def _ab_proj_prompt(x, sc, sh, w_bf, cos2, sin2, lng, lnb, wtril, bst, seq):
    n, d = x.shape
    tm = 256
    tpb = seq // tm
    n_in = w_bf.shape[1]
    row = lambda i: (i, 0)
    mod = lambda i: (i // tpb, 0, 0)
    const2 = lambda i: (0, 0)
    pos = lambda i: (i % tpb, 0)
    out_shape = (
        jax.ShapeDtypeStruct((n, A_WIDTH), F32),
        jax.ShapeDtypeStruct((n, B_WIDTH), F32),
        jax.ShapeDtypeStruct((n, KV_WIDTH), F32),
        jax.ShapeDtypeStruct((n, KV_WIDTH), F32),
        jax.ShapeDtypeStruct((n, KV_WIDTH), F32),
        jax.ShapeDtypeStruct((3, 2, N_KV, n, HEAD_DIM), F32),
        jax.ShapeDtypeStruct((N_KV, n, N_GATE // N_KV), F32),
    )
    return pl.pallas_call(
        functools.partial(_ab_proj_kernel, tm=tm),
        grid=(n // tm,),
        in_specs=[
            pl.BlockSpec((tm, d), row),
            pl.BlockSpec((None, 1, d), mod),
            pl.BlockSpec((None, 1, d), mod),
            pl.BlockSpec((d, n_in), const2, pipeline_mode=pl.Buffered(1)),
            pl.BlockSpec((tm, 128), pos),
            pl.BlockSpec((tm, 128), pos),
            pl.BlockSpec((1, A_WIDTH), const2),
            pl.BlockSpec((1, A_WIDTH), const2),
            pl.BlockSpec((A_GROUPS, CHUNK, CHUNK), lambda i: (0, 0, 0)),
            pl.BlockSpec((CHUNK, A_GROUPS), const2),
        ],
        out_specs=(
            pl.BlockSpec((tm, A_WIDTH), row),
            pl.BlockSpec((tm, B_WIDTH), row),
            pl.BlockSpec((tm, KV_WIDTH), row),
            pl.BlockSpec((tm, KV_WIDTH), row),
            pl.BlockSpec((tm, KV_WIDTH), row),
            pl.BlockSpec((3, 2, N_KV, tm, HEAD_DIM), lambda i: (0, 0, 0, i, 0)),
            pl.BlockSpec((N_KV, tm, N_GATE // N_KV), lambda i: (0, i, 0)),
        ),
        out_shape=out_shape,
        compiler_params=_cparams(("parallel",), VMEM_LIMIT_BYTES),
        name="ab_proj_prompt",
    )(x, sc, sh, w_bf, cos2, sin2, lng, lnb, wtril, bst)


def _ab_post_sample(z, zg, cos2, sin2, lng, lnb, ws0, bs0):
    zu = z[:, O_U:O_V]
    zv = z[:, O_V:O_Q]
    vs = []
    for g in range(A_GROUPS):
        sl = slice(g * A_GROUP_DIM, (g + 1) * A_GROUP_DIM)
        vs.append(_ln(zv[:, sl]) * lng[:, sl] + lnb[:, sl])
    v = jnp.concatenate(vs, axis=1)
    a = zu * (ws0 * v + bs0)
    q = _rope(z[:, O_Q:O_KV], cos2, sin2)
    half = KV_WIDTH // 2
    kvs = []
    for br in range(3):
        lo = O_KV + br * KV_WIDTH
        k = _rope(z[:, lo:lo + half], cos2, sin2)
        kvs.append(jnp.concatenate([k, z[:, lo + half:lo + KV_WIDTH]], axis=1))
    return a, v, q, kvs[0], kvs[1], kvs[2], jax.nn.sigmoid(zg)


def _compress_pair(x_ref, pe_ref, w1_ref, w2_ref, kv, gp, n):
    base = kv * (KV_WIDTH // 2) + gp * 128
    xg = jnp.concatenate([x_ref[:, s * KV_WIDTH + base:s * KV_WIDTH + base + 128] for s in range(CMP_STRIDE)], axis=1)
    lo = _dot(xg + pe_ref[kv, 0], w1_ref[kv, 0])
    hi = _dot(xg + pe_ref[kv, 1], w1_ref[kv, 1])
    hsum = lo + pltpu.roll(hi, n - 1, 0)
    return _dot(_gelu(hsum), w2_ref[kv])


def _compress_prompt_kernel(x_ref, pe_ref, w1_ref, w2_ref, o_ref, *, n16):
    for kv in range(2):
        for gp in range(N_KV // 2):
            r = _compress_pair(x_ref, pe_ref, w1_ref, w2_ref, kv, gp, n16)
            o_ref[kv, 2 * gp] = r[:, 0:HEAD_DIM]
            o_ref[kv, 2 * gp + 1] = r[:, HEAD_DIM:2 * HEAD_DIM]


def _compress_prompt(kvc3, pe_rows, w1bd, w2bd):
    b, n16, wid = kvc3.shape
    return pl.pallas_call(
        functools.partial(_compress_prompt_kernel, n16=n16),
        grid=(b,),
        in_specs=[
            pl.BlockSpec((None, n16, wid), lambda i: (i, 0, 0)),
            pl.BlockSpec(pe_rows.shape, lambda i: (0, 0, 0, 0)),
            pl.BlockSpec(w1bd.shape, lambda i: (0, 0, 0, 0)),
            pl.BlockSpec(w2bd.shape, lambda i: (0, 0, 0)),
        ],
        out_specs=pl.BlockSpec((None, 2, N_KV, n16, HEAD_DIM), lambda i: (i, 0, 0, 0, 0)),
        out_shape=jax.ShapeDtypeStruct((b, 2, N_KV, n16, HEAD_DIM), F32),
        compiler_params=_cparams(("parallel",), VMEM_LIMIT_BYTES),
        name="nsa_compress_prompt",
    )(kvc3, pe_rows, w1bd, w2bd)


def _compress_weights(pe, w1, w2):
    eye2 = jnp.eye(2, dtype=F32)
    w1r = w1.reshape(2, 2, CMP_STRIDE, HEAD_DIM, CMP_HID)
    w1bd = jnp.einsum('khsdc,gG->khsgdGc', w1r, eye2).reshape(2, 2, CMP_STRIDE * 128, 2 * CMP_HID).astype(BF16)
    pe_rows = jnp.broadcast_to(pe.reshape(2, 2, CMP_STRIDE, 1, HEAD_DIM), (2, 2, CMP_STRIDE, 2, HEAD_DIM))
    pe_rows = pe_rows.reshape(2, 2, 1, CMP_STRIDE * 128)
    w2bd = jnp.einsum('kcd,gG->kgcGd', w2, eye2).reshape(2, 2 * CMP_HID, 2 * HEAD_DIM).astype(BF16)
    return pe_rows, w1bd, w2bd


def _cmp_slc_map(n_rows, n_cmp, n_slc, n_cols):
    cs = np.arange(n_rows)[:, None] * CMP_STRIDE
    ss = np.arange(n_cols)[None, :] * SLC_LEN
    m = (cs < ss + SLC_LEN) & (cs + CMP_LEN > ss)
    m &= (np.arange(n_rows)[:, None] < n_cmp) & (np.arange(n_cols)[None, :] < n_slc)
    return m.astype(np.float32)


def _nsa_prompt_kernel(q_ref, gt_ref, kc_ref, vc_ref, ks_ref, vs_ref, kw_ref, vw_ref, mapt_ref, e_ref,
                       o_ref, selexp_ref, *, n_cmp, n_slc, n_sel, kt_len, win_len):
    i = pl.program_id(2)
    q0 = i * Q_BLOCK
    rows = HPG * Q_BLOCK
    qb = q_ref[...]
    q4 = jnp.concatenate([qb[:, h * HEAD_DIM:(h + 1) * HEAD_DIM] for h in range(HPG)], axis=0)
    q4 = (q4 * HEAD_DIM ** -0.5).astype(BF16)
    pos = q0 + (lax.broadcasted_iota(I32, (rows, 1), 0) & (Q_BLOCK - 1))

    n16 = kc_ref.shape[0]
    s = _dot_nt(q4, kc_ref[...])
    n_idx = lax.broadcasted_iota(I32, (1, n16), 1)
    valid = (n_idx * CMP_STRIDE + CMP_LEN - 1 <= pos) & (n_idx < n_cmp)
    p, l = _softmax_rows(s, valid)
    pn = p * (1.0 / jnp.maximum(l, 1e-30))
    o_c = _dot(pn, vc_ref[...])
    psum = pn[0:Q_BLOCK]
    for h in range(1, HPG):
        psum = psum + pn[h * Q_BLOCK:(h + 1) * Q_BLOCK]

    imp_t = _dot_nt(mapt_ref[...], psum, precise=True)
    blk_t = lax.broadcasted_iota(I32, (n_slc, Q_BLOCK), 0)
    cur_t = (q0 + lax.broadcasted_iota(I32, (n_slc, Q_BLOCK), 1)) // SLC_LEN
    forced = (blk_t == 0) | (blk_t == cur_t) | (blk_t == cur_t - 1)
    score = jnp.where(blk_t <= cur_t, imp_t + jnp.where(forced, FORCE_BONUS, 0.0), -jnp.inf)
    rank = jnp.zeros((n_slc, Q_BLOCK), I32)
    for j in range(n_slc):
        r = score[j:j + 1, :]
        rank = rank + ((r > score) | ((r == score) & (blk_t > j))).astype(I32)
    sel_t = ((rank < n_sel) & (score > -jnp.inf)).astype(F32)
    selexp_ref[...] = _dot(sel_t.T, e_ref[...])

    def sel_body(kt, carry):
        m, l, acc = carry
        k0 = pl.multiple_of(kt * kt_len, kt_len)
        s = _dot_nt(q4, ks_ref[pl.ds(k0, kt_len), :])
        kp = k0 + lax.broadcasted_iota(I32, (1, kt_len), 1)
        se = selexp_ref[:, pl.ds(k0, kt_len)]
        mask = (jnp.concatenate([se] * HPG, axis=0) > 0.5) & (kp <= pos)
        sm = jnp.where(mask, s, NEG)
        m_new = jnp.maximum(m, jnp.max(sm, -1, keepdims=True))
        alpha = jnp.exp(m - m_new)
        p = jnp.where(mask, jnp.exp(sm - m_new), 0.0)
        l = alpha * l + jnp.sum(p, -1, keepdims=True)
        acc = alpha * acc + _dot(p, vs_ref[pl.ds(k0, kt_len), :])
        return m_new, l, acc

    n_kt = (q0 + Q_BLOCK + kt_len - 1) // kt_len
    init = (jnp.full((rows, 1), NEG, F32), jnp.zeros((rows, 1), F32), jnp.zeros((rows, HEAD_DIM), F32))
    _, l, acc = lax.fori_loop(0, n_kt, sel_body, init)
    o_s = acc * (1.0 / jnp.maximum(l, 1e-30))

    start = pl.multiple_of(jnp.maximum(q0 - WINDOW, 0), Q_BLOCK)
    s = _dot_nt(q4, kw_ref[pl.ds(start, win_len), :])
    kp = start + lax.broadcasted_iota(I32, (1, win_len), 1)
    p, l = _softmax_rows(s, (kp <= pos) & (kp > pos - WINDOW))
    o_w = _dot(p, vw_ref[pl.ds(start, win_len), :]) * (1.0 / jnp.maximum(l, 1e-30))

    gt = gt_ref[...]

    def gcol(br):
        return jnp.concatenate([gt[:, br * HPG + h:br * HPG + h + 1] for h in range(HPG)], axis=0)

    o = gcol(0) * o_c + gcol(1) * o_s + gcol(2) * o_w
    o_ref[...] = jnp.concatenate([o[h * Q_BLOCK:(h + 1) * Q_BLOCK] for h in range(HPG)], axis=1)


def _nsa_prompt(q, gates, kcv, kvt, batch, seq):
    n = q.shape[0]
    nq = seq // Q_BLOCK
    n16 = kcv.shape[3]
    n_cmp = n16 - 1
    n_slc = -(-seq // SLC_LEN)
    n_sel = min(N_SEL, n_slc)
    kt_len = min(512, seq)
    win_len = min(WINDOW + Q_BLOCK, seq)
    mapt = jnp.asarray(_cmp_slc_map(n16, n_cmp, n_slc, n_slc).T)
    expand = jnp.asarray((np.arange(n_slc)[:, None] == np.arange(seq)[None, :] // SLC_LEN).astype(np.float32), BF16)
    per_g = N_GATE // N_KV
    qrow = lambda b, g, i: (b * nq + i, g)
    kvspec = lambda br, kv: pl.BlockSpec((None, None, None, seq, HEAD_DIM), lambda b, g, i: (br, kv, g, b, 0))
    return pl.pallas_call(
        functools.partial(_nsa_prompt_kernel, n_cmp=n_cmp, n_slc=n_slc, n_sel=n_sel, kt_len=kt_len, win_len=win_len),
        grid=(batch, N_KV, nq),
        in_specs=[
            pl.BlockSpec((Q_BLOCK, HPG * HEAD_DIM), qrow),
            pl.BlockSpec((None, Q_BLOCK, per_g), lambda b, g, i: (g, b * nq + i, 0)),
            pl.BlockSpec((None, None, None, n16, HEAD_DIM), lambda b, g, i: (b, 0, g, 0, 0)),
            pl.BlockSpec((None, None, None, n16, HEAD_DIM), lambda b, g, i: (b, 1, g, 0, 0)),
            kvspec(1, 0), kvspec(1, 1), kvspec(2, 0), kvspec(2, 1),
            pl.BlockSpec(mapt.shape, lambda b, g, i: (0, 0)),
            pl.BlockSpec(expand.shape, lambda b, g, i: (0, 0)),
        ],
        out_specs=pl.BlockSpec((Q_BLOCK, HPG * HEAD_DIM), qrow),
        out_shape=jax.ShapeDtypeStruct((n, B_WIDTH), F32),
        scratch_shapes=[pltpu.VMEM((Q_BLOCK, seq), F32)],
        compiler_params=_cparams(("parallel", "parallel", "arbitrary"), VMEM_LIMIT_BYTES),
        name="nsa_attn_prompt",
    )(q, gates, kcv, kcv, kvt, kvt, kvt, kvt, mapt, expand)


def _page_copy(cache_hbm, pt_ref, xbuf, sem, b, slot, p, n_pages):
    rows = xbuf.shape[1] // n_pages
    return pltpu.make_async_copy(cache_hbm.at[pt_ref[b * n_pages + p]],
                                 xbuf.at[slot, pl.ds(p * rows, rows), :], sem.at[slot])


def _nsa_sample_cmp_kernel(pt_ref, q_ref, cache_hbm, pe_ref, w1_ref, w2_ref, map_ref,
                           oc_ref, idx_ref, xbuf, sem, *, n_pages, n_cmp, n_slc, n_sel, pos):
    b = pl.program_id(0)
    nb = pl.num_programs(0)
    slot = b % 2
    n16 = xbuf.shape[1]

    def start_all(bb, sl):
        def body(p, c):
            _page_copy(cache_hbm, pt_ref, xbuf, sem, bb, sl, p, n_pages).start()
            return c
        lax.fori_loop(0, n_pages, body, 0)

    @pl.when(b == 0)
    def _():
        start_all(0, 0)

    @pl.when(b + 1 < nb)
    def _():
        start_all(b + 1, 1 - slot)

    def wait_body(p, c):
        _page_copy(cache_hbm, pt_ref, xbuf, sem, b, slot, p, n_pages).wait()
        return c
    lax.fori_loop(0, n_pages, wait_body, 0)

    x_ref = xbuf.at[slot]
    kc, vc = [], []
    for kv, dst in ((0, kc), (1, vc)):
        for gp in range(N_KV // 2):
            r = _compress_pair(x_ref, pe_ref, w1_ref, w2_ref, kv, gp, n16)
            dst.append(r[:, 0:HEAD_DIM])
            dst.append(r[:, HEAD_DIM:2 * HEAD_DIM])

    qrow = q_ref[...] * HEAD_DIM ** -0.5
    n_idx = lax.broadcasted_iota(I32, (1, n16), 1)
    valid = (n_idx * CMP_STRIDE + CMP_LEN - 1 <= pos) & (n_idx < n_cmp)
    head_row = lax.broadcasted_iota(I32, (8, 1), 0) < HPG
    ncol = map_ref.shape[1]
    blk_r = lax.broadcasted_iota(I32, (1, ncol), 1)
    blk_c = lax.broadcasted_iota(I32, (ncol, 1), 0)
    cur = pos // SLC_LEN
    forced = (blk_r == 0) | (blk_r == cur) | (blk_r == cur - 1)
    oc_parts = []
    for g in range(N_KV):
        q8 = jnp.concatenate(
            [qrow[:, (g * HPG + h) * HEAD_DIM:(g * HPG + h + 1) * HEAD_DIM] for h in range(HPG)]
            + [jnp.zeros((8 - HPG, HEAD_DIM), F32)], axis=0)
        s = _dot_nt(q8, kc[g])
        p, l = _softmax_rows(s, valid)
        pn = jnp.where(head_row, p * (1.0 / jnp.maximum(l, 1e-30)), 0.0)
        o8 = _dot(pn, vc[g])
        oc_parts += [o8[h:h + 1, :] for h in range(HPG)]
        imp = jnp.sum(_dot(pn, map_ref[...], precise=True), axis=0, keepdims=True)
        score = jnp.where((blk_r <= cur) & (blk_r < n_slc), imp + jnp.where(forced, FORCE_BONUS, 0.0), -jnp.inf)
        score_c = jnp.broadcast_to(score, (8, ncol)).T[:, 0:1]
        beats = (score_c > score) | ((score_c == score) & (blk_c < blk_r))
        rank = jnp.sum(beats.astype(F32), axis=0, keepdims=True)
        r_iota = lax.broadcasted_iota(I32, (n_sel, ncol), 0).astype(F32)
        hit = (rank == r_iota) & (score > -jnp.inf)
        idx = jnp.sum(jnp.where(hit, blk_r.astype(F32), 0.0), axis=1, keepdims=True)
        idx_ref[:, g:g + 1] = idx.astype(I32)
    oc_ref[...] = jnp.concatenate(oc_parts, axis=1)


def _nsa_sample_cmp(page_table, q3, cache_c3, pe_rows, w1bd, w2bd, past_len):
    db, n_pages = page_table.shape
    chunks_per_page = cache_c3.shape[1]
    n16 = n_pages * chunks_per_page
    n_cmp = (past_len + 1) // CMP_STRIDE - 1
    n_slc = -(-(past_len + 1) // SLC_LEN)
    n_sel = min(N_SEL, n_slc)
    ncol = -(-n_slc // 128) * 128
    cmap = jnp.asarray(_cmp_slc_map(n16, n_cmp, n_slc, ncol))
    grid_spec = pltpu.PrefetchScalarGridSpec(
        num_scalar_prefetch=1,
        grid=(db,),
        in_specs=[
            pl.BlockSpec((None, 1, B_WIDTH), lambda b, pt: (b, 0, 0)),
            pl.BlockSpec(memory_space=pl.ANY),
            pl.BlockSpec(pe_rows.shape, lambda b, pt: (0, 0, 0, 0)),
            pl.BlockSpec(w1bd.shape, lambda b, pt: (0, 0, 0, 0)),
            pl.BlockSpec(w2bd.shape, lambda b, pt: (0, 0, 0)),
            pl.BlockSpec(cmap.shape, lambda b, pt: (0, 0)),
        ],
        out_specs=(
            pl.BlockSpec((None, 1, B_WIDTH), lambda b, pt: (b, 0, 0)),
            pl.BlockSpec((None, n_sel, N_KV), lambda b, pt: (b, 0, 0)),
        ),
        scratch_shapes=[pltpu.VMEM((2, n16, cache_c3.shape[2]), F32), pltpu.SemaphoreType.DMA((2,))],
    )
    return pl.pallas_call(
        functools.partial(_nsa_sample_cmp_kernel, n_pages=n_pages, n_cmp=n_cmp, n_slc=n_slc, n_sel=n_sel, pos=past_len),
        grid_spec=grid_spec,
        out_shape=(jax.ShapeDtypeStruct((db, 1, B_WIDTH), F32), jax.ShapeDtypeStruct((db, n_sel, N_KV), I32)),
        compiler_params=_cparams(("arbitrary",), VMEM_LIMIT_BYTES),
        name="nsa_cmp_sample",
    )(page_table.reshape(-1), q3, cache_c3, pe_rows, w1bd, w2bd, cmap)


def _sel_copy(cache_hbm, pt_ref, idx_ref, kbuf, sem, b, slot, j, n_pages, n_sel, last_real):
    blk = jnp.minimum(idx_ref[b * (N_KV * n_sel) + j], last_real)
    per_page = cache_hbm.shape[1] // SLC_LEN
    page = pt_ref[b * n_pages + blk // per_page]
    return pltpu.make_async_copy(cache_hbm.at[page, pl.ds((blk % per_page) * SLC_LEN, SLC_LEN), :],
                                 kbuf.at[slot, j], sem.at[slot])


def _nsa_sample_attn_kernel(pt_ref, idx_ref, q_ref, gt_ref, oc_ref, ks_new_ref, kw_new_ref, win_ref, cache_hbm,
                            o_ref, kbuf, sem, *, n_pages, n_sel, pos):
    b = pl.program_id(0)
    nb = pl.num_programs(0)
    slot = b % 2
    n_copies = N_KV * n_sel
    new_blk = pos // SLC_LEN
    last_real = new_blk - 1

    def start_all(bb, sl):
        def body(j, c):
            _sel_copy(cache_hbm, pt_ref, idx_ref, kbuf, sem, bb, sl, j, n_pages, n_sel, last_real).start()
            return c
        lax.fori_loop(0, n_copies, body, 0)

    @pl.when(b == 0)
    def _():
        start_all(0, 0)

    @pl.when(b + 1 < nb)
    def _():
        start_all(b + 1, 1 - slot)

    def wait_body(j, c):
        _sel_copy(cache_hbm, pt_ref, idx_ref, kbuf, sem, b, slot, j, n_pages, n_sel, last_real).wait()
        return c
    lax.fori_loop(0, n_copies, wait_body, 0)

    qrow = q_ref[...] * HEAD_DIM ** -0.5
    gt = gt_ref[...]
    oc = oc_ref[...]
    ks_new = ks_new_ref[...]
    kw_new = kw_new_ref[...]
    half = KV_WIDTH // 2
    wlen = win_ref.shape[0]
    kp_w = pos - wlen + lax.broadcasted_iota(I32, (1, wlen), 1)
    mask_w = (kp_w >= 0) & (kp_w <= pos) & (kp_w > pos - WINDOW)
    out_parts = []
    for g in range(N_KV):
        q8 = jnp.concatenate(
            [qrow[:, (g * HPG + h) * HEAD_DIM:(g * HPG + h + 1) * HEAD_DIM] for h in range(HPG)]
            + [jnp.zeros((8 - HPG, HEAD_DIM), F32)], axis=0)
        ksl = slice(g * HEAD_DIM, (g + 1) * HEAD_DIM)
        vsl = slice(half + g * HEAD_DIM, half + (g + 1) * HEAD_DIM)

        def attend(s, mask, v, k_new, v_new, has_new):
            s_new = jnp.sum(q8 * k_new, axis=-1, keepdims=True)
            sm = jnp.where(mask, s, NEG)
            m = jnp.maximum(jnp.max(sm, -1, keepdims=True), jnp.where(has_new, s_new, NEG))
            p = jnp.where(mask, jnp.exp(sm - m), 0.0)
            p_new = jnp.where(has_new, jnp.exp(s_new - m), 0.0)
            l = jnp.sum(p, -1, keepdims=True) + p_new
            return (_dot(p, v) + p_new * v_new) * (1.0 / jnp.maximum(l, 1e-30))

        k_sel = jnp.concatenate([kbuf[slot, g * n_sel + r, :, ksl] for r in range(n_sel)], axis=0)
        v_sel = jnp.concatenate([kbuf[slot, g * n_sel + r, :, vsl] for r in range(n_sel)], axis=0)
        blk_ids = [idx_ref[b * n_copies + g * n_sel + r] for r in range(n_sel)]
        real = jnp.concatenate([jnp.full((1, SLC_LEN), blk_ids[r], I32) for r in range(n_sel)], axis=1) <= last_real
        has_new = functools.reduce(jnp.logical_or, [bid == new_blk for bid in blk_ids])
        o_s = attend(_dot_nt(q8, k_sel), real, v_sel, ks_new[:, ksl], ks_new[:, vsl], has_new)
        o_w = attend(_dot_nt(q8, win_ref[:, ksl]), mask_w, win_ref[:, vsl], kw_new[:, ksl], kw_new[:, vsl], True)
        for h in range(HPG):
            hh = g * HPG + h
            c0 = g * (N_BRANCH * HPG)
            g0 = gt[:, c0 + h:c0 + h + 1]
            g1 = gt[:, c0 + HPG + h:c0 + HPG + h + 1]
            g2 = gt[:, c0 + 2 * HPG + h:c0 + 2 * HPG + h + 1]
            out_parts.append(g0 * oc[:, hh * HEAD_DIM:(hh + 1) * HEAD_DIM] + g1 * o_s[h:h + 1, :] + g2 * o_w[h:h + 1, :])
    o_ref[...] = jnp.concatenate(out_parts, axis=1)


def _nsa_sample_attn(page_table, sel_idx, q3, gates3, oc3, ks_new3, kw_new3, win_state, cache_s, past_len):
    db, n_pages = page_table.shape
    n_sel = sel_idx.shape[-1]
    wlen = win_state.shape[1]
    row = lambda wdt: pl.BlockSpec((None, 1, wdt), lambda b, pt, ix: (b, 0, 0))
    grid_spec = pltpu.PrefetchScalarGridSpec(
        num_scalar_prefetch=2,
        grid=(db,),
        in_specs=[
            row(B_WIDTH), row(N_GATE), row(B_WIDTH), row(KV_WIDTH), row(KV_WIDTH),
            pl.BlockSpec((None, wlen, KV_WIDTH), lambda b, pt, ix: (b, 0, 0)),
            pl.BlockSpec(memory_space=pl.ANY),
        ],
        out_specs=row(B_WIDTH),
        scratch_shapes=[pltpu.VMEM((2, N_KV * n_sel, SLC_LEN, KV_WIDTH), F32), pltpu.SemaphoreType.DMA((2,))],
    )
    return pl.pallas_call(
        functools.partial(_nsa_sample_attn_kernel, n_pages=n_pages, n_sel=n_sel, pos=past_len),
        grid_spec=grid_spec,
        out_shape=jax.ShapeDtypeStruct((db, 1, B_WIDTH), F32),
        compiler_params=_cparams(("arbitrary",), VMEM_LIMIT_BYTES),
        name="nsa_attn_sample",
    )(page_table.reshape(-1), sel_idx.reshape(-1), q3, gates3, oc3, ks_new3, kw_new3, win_state, cache_s)


def _post_ln_router(acc, x, gate, lng, lnb, sc2, sh2, rw):
    x1 = _ln(ALPHA * x + gate * acc) * lng + lnb
    h2 = _modulate(x1, sc2, sh2)
    return x1, h2, _dot(h2, rw, precise=True)


def _mix_out_kernel(l0_ref, l1_ref, w_ref, x_ref, gate_ref, lng_ref, lnb_ref, sc2_ref, sh2_ref, rw_ref,
                    x1_ref, h2_ref, lg_ref, *, concat):
    if concat:
        k0 = l0_ref.shape[1]
        acc = _dot(l0_ref[...], w_ref[0:k0, :]) + _dot(l1_ref[...], w_ref[k0:, :])
    else:
        acc = _dot(l0_ref[...] * l1_ref[...], w_ref[...])
    x1, h2, lg = _post_ln_router(acc, x_ref[...], gate_ref[...], lng_ref[...], lnb_ref[...],
                                 sc2_ref[...], sh2_ref[...], rw_ref[...])
    x1_ref[...] = x1
    h2_ref[...] = h2
    lg_ref[...] = lg


def _mix_out_prompt(l0, l1, w_bf, x, gate, lng, lnb, sc2, sh2, rw, seq, concat, name):
    n, d = x.shape
    tm = 256
    tpb = seq // tm
    row = lambda i: (i, 0)
    mod = lambda i: (i // tpb, 0, 0)
    const2 = lambda i: (0, 0)
    return pl.pallas_call(
        functools.partial(_mix_out_kernel, concat=concat),
        grid=(n // tm,),
        in_specs=[
            pl.BlockSpec((tm, l0.shape[1]), row),
            pl.BlockSpec((tm, l1.shape[1]), row),
            pl.BlockSpec(w_bf.shape, const2),
            pl.BlockSpec((tm, d), row),
            pl.BlockSpec((None, 1, d), mod),
            pl.BlockSpec((1, d), const2),
            pl.BlockSpec((1, d), const2),
            pl.BlockSpec((None, 1, d), mod),
            pl.BlockSpec((None, 1, d), mod),
            pl.BlockSpec(rw.shape, const2),
        ],
        out_specs=(pl.BlockSpec((tm, d), row), pl.BlockSpec((tm, d), row), pl.BlockSpec((tm, N_EXPERTS), row)),
        out_shape=(jax.ShapeDtypeStruct((n, d), F32), jax.ShapeDtypeStruct((n, d), F32),
                   jax.ShapeDtypeStruct((n, N_EXPERTS), F32)),
        compiler_params=_cparams(("parallel",), VMEM_LIMIT_BYTES),
        name=name,
    )(l0, l1, w_bf, x, gate, lng, lnb, sc2, sh2, rw)


def _route_tables(logits, router_b, blk):
    n = logits.shape[0]
    s = jax.nn.sigmoid(logits)
    sb = s + router_b.astype(F32)
    gscore = jnp.sum(lax.top_k(sb.reshape(n, N_GROUPS, EXPERTS_PER_GROUP), TOP_K)[0], -1)
    g_sel = jnp.argmax(gscore, -1)
    in_grp = (jnp.arange(N_EXPERTS) // EXPERTS_PER_GROUP)[None, :] == g_sel[:, None]
    _, idx = lax.top_k(jnp.where(in_grp, sb, -jnp.inf), TOP_K)
    w = jnp.take_along_axis(s, idx, -1)
    w = w / jnp.sum(w, -1, keepdims=True)
    a = n * TOP_K
    n_blocks = -(-a // blk) + N_EXPERTS
    e_flat = idx.reshape(-1).astype(I32)
    tok = jnp.repeat(jnp.arange(n, dtype=I32), TOP_K)
    order = jnp.argsort(e_flat)
    e_sorted = e_flat[order]
    counts = jnp.bincount(e_flat, length=N_EXPERTS).astype(I32)
    padded = (counts + blk - 1) // blk * blk
    pad_end = jnp.cumsum(padded)
    pad_start = pad_end - padded
    start = jnp.cumsum(counts) - counts
    dest = (pad_start[e_sorted] + jnp.arange(a, dtype=I32) - start[e_sorted]).astype(I32)
    slot_tok = jnp.zeros((n_blocks * blk,), I32).at[dest].set(tok[order])
    slot_of = jnp.zeros((a,), I32).at[order].set(dest)
    block_e = jnp.minimum(jnp.searchsorted(pad_end, jnp.arange(n_blocks, dtype=I32) * blk, side='right'),
                          N_EXPERTS - 1).astype(I32)
    n_used = (pad_end[-1] // blk).astype(I32).reshape(1)
    return slot_tok, slot_of, w, block_e, n_used, n_blocks


def _row_copy(src_hbm, row, buf, slot, r, sem):
    return pltpu.make_async_copy(src_hbm.at[pl.ds(row, 1), :], buf.at[slot, pl.ds(r, 1), :], sem.at[slot])


def _moe_kernel(be_ref, tok_ref, nu_ref, x_hbm, win_all, wout_all, y_ref,
                xs_buf, w1_scr, w2_scr, stage, gsem, wsem, *, blk, precise, layer):
    win_hbm = win_all.at[layer]
    wout_hbm = wout_all.at[layer]
    i = pl.program_id(0)
    n_used = nu_ref[0]
    slot = i % 2
    e = be_ref[i]

    def start_gather(bi, sl):
        def body(r, c):
            _row_copy(x_hbm, tok_ref[bi * blk + r], xs_buf, sl, r, gsem).start()
            return c
        lax.fori_loop(0, blk, body, 0)

    @pl.when(i == 0)
    def _():
        start_gather(0, 0)

    @pl.when(i + 1 < n_used)
    def _():
        start_gather(i + 1, 1 - slot)

    changed = (i == 0) | (e != be_ref[jnp.maximum(i - 1, 0)])

    @pl.when(changed & (i < n_used))
    def _():
        if precise:
            c1 = pltpu.make_async_copy(win_hbm.at[e], w1_scr, wsem.at[0])
            c2 = pltpu.make_async_copy(wout_hbm.at[e], w2_scr, wsem.at[1])
            c1.start()
            c2.start()
            c1.wait()
            c2.wait()
        else:
            rows = stage.shape[1]
            chunks = [(win_hbm, w1_scr, c) for c in range(w1_scr.shape[0] // rows)]
            chunks += [(wout_hbm, w2_scr, c) for c in range(w2_scr.shape[0] // rows)]

            def cp(j):
                src, _, c = chunks[j]
                return pltpu.make_async_copy(src.at[e, pl.ds(c * rows, rows), :], stage.at[j % 2], wsem.at[j % 2])

            cp(0).start()
            for j in range(len(chunks)):
                if j + 1 < len(chunks):
                    cp(j + 1).start()
                cp(j).wait()
                _, dst, c = chunks[j]
                dst[pl.ds(c * rows, rows), :] = stage[j % 2].astype(BF16)

    @pl.when(i < n_used)
    def _():
        def wait_body(r, c):
            _row_copy(x_hbm, 0, xs_buf, slot, r, gsem).wait()
            return c
        lax.fori_loop(0, blk, wait_body, 0)
        xb = xs_buf[slot]
        z = _dot(xb, w1_scr[...], precise)
        de = z.shape[1] // 2
        act = _silu(z[:, :de]) * z[:, de:]
        y_ref[...] = _dot(act, w2_scr[...], precise)

    @pl.when(i >= n_used)
    def _():
        y_ref[...] = jnp.zeros_like(y_ref)


def _moe_ffn(h2, slot_tok, block_e, n_used, n_blocks, w_in, w_out, layer, blk, precise):
    n, d = h2.shape
    de2 = w_in.shape[-1]
    wdt = F32 if precise else BF16
    stage_rows = 256
    grid_spec = pltpu.PrefetchScalarGridSpec(
        num_scalar_prefetch=3,
        grid=(n_blocks,),
        in_specs=[pl.BlockSpec(memory_space=pl.ANY)] * 3,
        out_specs=pl.BlockSpec((blk, d), lambda i, be, tk, nu: (i, 0)),
        scratch_shapes=[
            pltpu.VMEM((2, blk, d), F32),
            pltpu.VMEM((d, de2), wdt),
            pltpu.VMEM((de2 // 2, d), wdt),
            pltpu.VMEM((2, stage_rows, max(d, de2)), F32),
            pltpu.SemaphoreType.DMA((2,)),
            pltpu.SemaphoreType.DMA((2,)),
        ],
    )
    return pl.pallas_call(
        functools.partial(_moe_kernel, blk=blk, precise=precise, layer=layer),
        grid_spec=grid_spec,
        out_shape=jax.ShapeDtypeStruct((n_blocks * blk, d), F32),
        compiler_params=_cparams(("arbitrary",), VMEM_LIMIT_BYTES),
        name="moe_ffn_sample" if precise else "moe_ffn_prompt",
    )(block_e, slot_tok, n_used, h2, w_in, w_out)


def _combine_kernel(so_ref, x_ref, wt_ref, gate_ref, lng_ref, lnb_ref, y_hbm, o_ref, ybuf, sem, *, tm):
    i = pl.program_id(0)
    nt = pl.num_programs(0)
    slot = i % 2

    def start_gather(ti, sl):
        def body(r, c):
            for k in range(TOP_K):
                _row_copy(y_hbm, so_ref[(ti * tm + r) * TOP_K + k], ybuf.at[k], sl, r, sem.at[k]).start()
            return c
        lax.fori_loop(0, tm, body, 0)

    @pl.when(i == 0)
    def _():
        start_gather(0, 0)

    @pl.when(i + 1 < nt)
    def _():
        start_gather(i + 1, 1 - slot)

    def wait_body(r, c):
        for k in range(TOP_K):
            _row_copy(y_hbm, 0, ybuf.at[k], slot, r, sem.at[k]).wait()
        return c
    lax.fori_loop(0, tm, wait_body, 0)

    wt = wt_ref[...]
    f = wt[:, 0:1] * ybuf[0, slot] + wt[:, 1:2] * ybuf[1, slot]
    o_ref[...] = _ln(ALPHA * x_ref[...] + gate_ref[...] * f) * lng_ref[...] + lnb_ref[...]


def _moe_combine(slot_of, x1, wts, gate, lng, lnb, y, tm, tpb, name):
    n, d = x1.shape
    grid_spec = pltpu.PrefetchScalarGridSpec(
        num_scalar_prefetch=1,
        grid=(n // tm,),
        in_specs=[
            pl.BlockSpec((tm, d), lambda i, so: (i, 0)),
            pl.BlockSpec((tm, TOP_K), lambda i, so: (i, 0)),
            pl.BlockSpec((None, gate.shape[1], d), lambda i, so: (i // tpb, 0, 0)),
            pl.BlockSpec((1, d), lambda i, so: (0, 0)),
            pl.BlockSpec((1, d), lambda i, so: (0, 0)),
            pl.BlockSpec(memory_space=pl.ANY),
        ],
        out_specs=pl.BlockSpec((tm, d), lambda i, so: (i, 0)),
        scratch_shapes=[pltpu.VMEM((TOP_K, 2, tm, d), F32), pltpu.SemaphoreType.DMA((TOP_K, 2))],
    )
    return pl.pallas_call(
        functools.partial(_combine_kernel, tm=tm),
        grid_spec=grid_spec,
        out_shape=jax.ShapeDtypeStruct((n, d), F32),
        compiler_params=_cparams(("arbitrary",)),
        name=name,
    )(slot_of, x1, wts, gate, lng, lnb, y)


def _channel_sublayer(x1, h2, logits, gate, lng, lnb, router_b, w_in, w_out, layer, blk, tm, tpb, precise, name):
    slot_tok, slot_of, wts, block_e, n_used, n_blocks = _route_tables(logits, router_b, blk)
    y = _moe_ffn(h2, slot_tok, block_e, n_used, n_blocks, w_in, w_out, layer, blk, precise)
    return _moe_combine(slot_of, x1, wts, gate, lng, lnb, y, tm, tpb, name)


def _rec_in_kernel(x_ref, sc_ref, sh_ref, w_ref, gg_ref, xbr_ref):
    hb = _modulate(x_ref[...], sc_ref[...], sh_ref[...]).astype(BF16)
    d = gg_ref.shape[1]
    gg_ref[...] = _gelu(jnp.dot(hb, w_ref[:, 0:d], preferred_element_type=F32))
    xbr_ref[...] = jnp.dot(hb, w_ref[:, d:], preferred_element_type=F32)


def _rec_in_prompt(x, sc, sh, w_bf, seq):
    n, d = x.shape
    dr = w_bf.shape[1] // 2
    tm = 256
    tpb = seq // tm
    row = lambda i: (i, 0)
    mod = lambda i: (i // tpb, 0, 0)
    return pl.pallas_call(
        _rec_in_kernel,
        grid=(n // tm,),
        in_specs=[
            pl.BlockSpec((tm, d), row),
            pl.BlockSpec((None, 1, d), mod),
            pl.BlockSpec((None, 1, d), mod),
            pl.BlockSpec(w_bf.shape, lambda i: (0, 0), pipeline_mode=pl.Buffered(1)),
        ],
        out_specs=(pl.BlockSpec((tm, dr), row), pl.BlockSpec((tm, dr), row)),
        out_shape=(jax.ShapeDtypeStruct((n, dr), F32), jax.ShapeDtypeStruct((n, dr), F32)),
        compiler_params=_cparams(("parallel",), VMEM_LIMIT_BYTES),
        name="rec_in_prompt",
    )(x, sc, sh, w_bf)


def _log1p(y):
    w = 1.0 + y
    return jnp.where(w == 1.0, y, jnp.log(w) * (y / jnp.where(w == 1.0, 1.0, w - 1.0)))


def _expm1(x):
    u = jnp.exp(x)
    safe = (u != 1.0) & (u > 0.0)
    return jnp.where(u == 1.0, x, jnp.where(u > 0.0, (u - 1.0) * (x / jnp.where(safe, jnp.log(u), 1.0)), -1.0))


def _softplus(x):
    return jnp.maximum(x, 0.0) + _log1p(jnp.exp(-jnp.abs(x)))


def _rg_terms(xb, z, ba, bx, lam):
    bs = xb.shape[1]
    r = jax.nn.sigmoid(z[:, :bs] + ba)
    gi = jax.nn.sigmoid(z[:, bs:] + bx)
    log_a = -RG_C * r * _softplus(-lam)
    a = jnp.exp(log_a)
    return a, jnp.sqrt(-_expm1(2.0 * log_a)) * (gi * xb)


def _rec_gate_kernel(x_ref, cw_ref, cb_ref, wab_ref, ba_ref, bx_ref, lam_ref, a_ref, bt_ref, carry_ref, *, tpb, tm):
    i = pl.program_id(0)

    @pl.when(i % tpb == 0)
    def _():
        carry_ref[...] = jnp.zeros_like(carry_ref)

    x = x_ref[...]
    carry = carry_ref[...]
    d = x.shape[1]
    row8 = lax.broadcasted_iota(I32, (8, d), 0)
    xc = cb_ref[...] + x * cw_ref[CONV_W - 1:CONV_W, :]
    for k in range(1, CONV_W):
        xr = pltpu.roll(x, k, 0)
        head = jnp.where(row8 < k, pltpu.roll(carry, k, 0), xr[0:8])
        xk = jnp.concatenate([head, xr[8:]], axis=0)
        xc = xc + xk * cw_ref[CONV_W - 1 - k:CONV_W - k, :]
    carry_ref[...] = x[tm - 8:tm]
    bs = d // RNN_BLOCKS
    for nb in range(RNN_BLOCKS):
        sl = slice(nb * bs, (nb + 1) * bs)
        xb = xc[:, sl]
        a, bt = _rg_terms(xb, _dot(xb, wab_ref[nb]), ba_ref[:, sl], bx_ref[:, sl], lam_ref[:, sl])
        a_ref[:, sl] = a
        bt_ref[:, sl] = bt


def _rec_gate_prompt(xbr, cw, cb, wab, ba, bx, lam, seq):
    n, d = xbr.shape
    tm = 256
    tpb = seq // tm
    row = lambda i: (i, 0)
    const2 = lambda i: (0, 0)
    return pl.pallas_call(
        functools.partial(_rec_gate_kernel, tpb=tpb, tm=tm),
        grid=(n // tm,),
        in_specs=[
            pl.BlockSpec((tm, d), row),
            pl.BlockSpec(cw.shape, const2),
            pl.BlockSpec((1, d), const2),
            pl.BlockSpec(wab.shape, lambda i: (0, 0, 0)),
            pl.BlockSpec((1, d), const2),
            pl.BlockSpec((1, d), const2),
            pl.BlockSpec((1, d), const2),
        ],
        out_specs=(pl.BlockSpec((tm, d), row), pl.BlockSpec((tm, d), row)),
        out_shape=(jax.ShapeDtypeStruct((n, d), F32), jax.ShapeDtypeStruct((n, d), F32)),
        scratch_shapes=[pltpu.VMEM((8, d), F32)],
        compiler_params=_cparams(("arbitrary",), VMEM_LIMIT_BYTES),
        name="rec_gate_prompt",
    )(xbr, cw, cb, wab, ba, bx, lam)


def _scan_kernel(a_ref, b_ref, hs_ref, ht_ref, h_ref, *, tt):
    t = pl.program_id(1)

    @pl.when(t == 0)
    def _():
        h_ref[...] = jnp.zeros_like(h_ref)

    def body(j, h):
        h = a_ref[j] * h + b_ref[j]
        hs_ref[j] = h
        return h

    h = lax.fori_loop(0, tt, body, h_ref[...], unroll=8)
    h_ref[...] = h
    ht_ref[...] = h


def _scan_prompt(a4, b4):
    bsz, seq, s8, c8 = a4.shape
    tt = min(512, seq)
    blk = pl.BlockSpec((None, tt, s8, c8), lambda b, t: (b, t, 0, 0))
    return pl.pallas_call(
        functools.partial(_scan_kernel, tt=tt),
        grid=(bsz, seq // tt),
        in_specs=[blk, blk],
        out_specs=(blk, pl.BlockSpec((None, s8, c8), lambda b, t: (b, 0, 0))),
        out_shape=(jax.ShapeDtypeStruct(a4.shape, F32), jax.ShapeDtypeStruct((bsz, s8, c8), F32)),
        scratch_shapes=[pltpu.VMEM((s8, c8), F32)],
        compiler_params=_cparams(("parallel", "arbitrary"), VMEM_LIMIT_BYTES),
        name="rglru_scan_prompt",
    )(a4, b4)


def _rec_step_sample(z, buf0, buf1, buf2, h0, cw, cb, wa, wx, ba, bx, lam):
    d = h0.shape[1]
    gate_br = z[:, :d]
    xbr = z[:, d:]
    xc = cb + buf0 * cw[0:1] + buf1 * cw[1:2] + buf2 * cw[2:3] + xbr * cw[3:4]
    bs = d // RNN_BLOCKS
    a_parts, b_parts = [], []
    for nb in range(RNN_BLOCKS):
        sl = slice(nb * bs, (nb + 1) * bs)
        xb = xc[:, sl]
        zz = jnp.concatenate([_dot(xb, wa[nb], precise=True), _dot(xb, wx[nb], precise=True)], axis=1)
        a, bt = _rg_terms(xb, zz, ba[:, sl], bx[:, sl], lam[:, sl])
        a_parts.append(a)
        b_parts.append(bt)
    h = jnp.concatenate(a_parts, axis=1) * h0 + jnp.concatenate(b_parts, axis=1)
    return _gelu(gate_br) * h, h, xbr


def kernel(x_prompt, x_sample, c_prompt, c_sample, cache_nsa_cmp, cache_nsa_slc, state_nsa_win, state_rglru_conv, state_rglru_h, page_table, ada_w, ada_b, ln_g, ln_b, ab_w_in, ab_w_out, gmlp_ln_g, gmlp_ln_b, gmlp_ws, gmlp_bs, nsa_cmp_pe, nsa_cmp_w1, nsa_cmp_w2, rec_w_in, rec_conv_w, rec_conv_b, rg_wa, rg_ba, rg_wx, rg_bx, rg_lambda, rec_w_out, router_w, router_b, moe_w_in, moe_w_out):
    bp, seq, d = x_prompt.shape
    db = x_sample.shape[0]
    assert x_sample.shape[1] == 1
    n_pool, page = cache_nsa_cmp.shape[:2]
    n_pages = page_table.shape[1]
    past_len = n_pages * page
    assert seq % 256 == 0 and past_len % SLC_LEN == 0 and page % SLC_LEN == 0 and past_len >= WINDOW
    n_p = bp * seq
    d_rnn = rec_conv_b.shape[0]

    mods = _ada_all(jnp.concatenate([c_prompt, c_sample], axis=0), ada_w, ada_b)

    def mod_rows(layer, sub):
        m = mods[layer * 2 + sub]
        parts = [m[:, j * d:(j + 1) * d] for j in range(3)]
        return [p[:bp].reshape(bp, 1, d) for p in parts], [p[bp:] for p in parts]

    lnrow = lambda a: a.reshape(1, d)

    gperm = np.array([(g * HPG + h) * N_BRANCH + br for g in range(N_KV) for br in range(N_BRANCH) for h in range(HPG)])
    w_gate = ab_w_in[:, O_G:][:, gperm]
    w_ab_bf = jnp.concatenate([ab_w_in[:, :O_G], w_gate], axis=1).astype(BF16)
    tril = jnp.tril(jnp.ones((CHUNK, CHUNK), F32))
    wtril = (gmlp_ws * tril).astype(BF16)
    bst = gmlp_bs.T
    glng = gmlp_ln_g.reshape(1, A_WIDTH)
    glnb = gmlp_ln_b.reshape(1, A_WIDTH)
    pe_rows, w1bd, w2bd = _compress_weights(nsa_cmp_pe, nsa_cmp_w1, nsa_cmp_w2)
    xp = x_prompt.reshape(n_p, d)
    xs = x_sample.reshape(db, d)

    (sh_p, sc_p, g_p), (sh_s, sc_s, g_s) = mod_rows(0, 0)
    (sh2_p, sc2_p, g2_p), (sh2_s, sc2_s, g2_s) = mod_rows(0, 1)
    cos_p, sin_p = _rope_tables(np.arange(seq))
    a_p, q_p, kvc_p, kvs_p, kvw_p, kvt_p, gt_p = _ab_proj_prompt(
        xp, sc_p, sh_p, w_ab_bf, cos_p, sin_p, glng, glnb, wtril, bst, seq)
    kcv_p = _compress_prompt(kvc_p.reshape(bp, seq // CMP_STRIDE, CMP_STRIDE * KV_WIDTH), pe_rows, w1bd, w2bd)
    o_p = _nsa_prompt(q_p, gt_p, kcv_p, kvt_p, bp, seq)
    w_out_bf = ab_w_out.astype(BF16)
    x1_p, h2_p, lg_p = _mix_out_prompt(a_p, o_p, w_out_bf, xp, g_p, lnrow(ln_g[0, 0]), lnrow(ln_b[0, 0]),
                                       sc2_p, sh2_p, router_w, seq, True, "mix_out_l0_prompt")

    z_s = _small_mm(_modulate, [xs, sc_s, sh_s], ab_w_in, O_G, 512, "ab_proj_sample")
    zg_s = _small_mm(_modulate, [xs, sc_s, sh_s], w_gate, N_GATE, N_GATE, "ab_gate_sample")
    cos_s, sin_s = _rope_tables(np.full((db,), past_len))
    ws0 = jnp.repeat(gmlp_ws[:, 0, 0], A_GROUP_DIM).reshape(1, A_WIDTH)
    bs0 = jnp.repeat(gmlp_bs[:, 0], A_GROUP_DIM).reshape(1, A_WIDTH)
    sds = lambda *s: jax.ShapeDtypeStruct(s, F32)
    a_s, v_s, q_s, kvc_s, kvs_s, kvw_s, gt_s = _vmem_call(
        _ab_post_sample,
        (sds(db, A_WIDTH), sds(db, A_WIDTH), sds(db, B_WIDTH), sds(db, KV_WIDTH), sds(db, KV_WIDTH),
         sds(db, KV_WIDTH), sds(db, N_GATE)),
        (z_s, zg_s, cos_s, sin_s, glng, glnb, ws0, bs0), "ab_post_sample")
    q3 = q_s.reshape(db, 1, B_WIDTH)
    cache_c3 = cache_nsa_cmp.reshape(n_pool, page // CMP_STRIDE, CMP_STRIDE * KV_WIDTH)
    oc3, sel_idx = _nsa_sample_cmp(page_table, q3, cache_c3, pe_rows, w1bd, w2bd, past_len)
    sel_idx = jnp.transpose(sel_idx, (0, 2, 1))
    o_s = _nsa_sample_attn(page_table, sel_idx, q3, gt_s.reshape(db, 1, N_GATE), oc3,
                           kvs_s.reshape(db, 1, KV_WIDTH), kvw_s.reshape(db, 1, KV_WIDTH),
                           state_nsa_win.reshape(db, -1, KV_WIDTH), cache_nsa_slc.reshape(n_pool, page, KV_WIDTH),
                           past_len).reshape(db, B_WIDTH)
    f_s = _small_mm(lambda a, o: jnp.concatenate([a, o], axis=1), [a_s, o_s], ab_w_out, d, 512, "mix_out_l0_sample")
    x1_s, h2_s, lg_s = _vmem_call(
        _post_ln_router, (sds(db, d), sds(db, d), sds(db, N_EXPERTS)),
        (f_s, xs, g_s, lnrow(ln_g[0, 0]), lnrow(ln_b[0, 0]), sc2_s, sh2_s, router_w), "post_l0_sample")

    x2_p = _channel_sublayer(x1_p, h2_p, lg_p, g2_p, lnrow(ln_g[0, 1]), lnrow(ln_b[0, 1]), router_b,
                             moe_w_in, moe_w_out, 0,128, 128, seq // 128, False, "moe_combine_l0_prompt")
    x2_s = _channel_sublayer(x1_s, h2_s, lg_s, g2_s.reshape(1, db, d), lnrow(ln_g[0, 1]), lnrow(ln_b[0, 1]), router_b,
                             moe_w_in, moe_w_out, 0,32, db, 1, True, "moe_combine_l0_sample")

    (sh_p, sc_p, g_p), (sh_s, sc_s, g_s) = mod_rows(1, 0)
    (sh2_p, sc2_p, g2_p), (sh2_s, sc2_s, g2_s) = mod_rows(1, 1)
    gg_p, xbr_p = _rec_in_prompt(x2_p, sc_p, sh_p, rec_w_in.astype(BF16), seq)
    wab = jnp.concatenate([rg_wa, rg_wx], axis=2).astype(BF16)
    row_r = lambda a: a.reshape(1, d_rnn)
    a_t, b_t = _rec_gate_prompt(xbr_p, rec_conv_w, row_r(rec_conv_b), wab, row_r(rg_ba), row_r(rg_bx),
                                row_r(rg_lambda), seq)
    hs4, ht = _scan_prompt(a_t.reshape(bp, seq, 8, d_rnn // 8), b_t.reshape(bp, seq, 8, d_rnn // 8))
    x3_p, h4_p, lg_p = _mix_out_prompt(gg_p, hs4.reshape(n_p, d_rnn), rec_w_out.astype(BF16), x2_p, g_p,
                                       lnrow(ln_g[1, 0]), lnrow(ln_b[1, 0]), sc2_p, sh2_p, router_w, seq, False,
                                       "mix_out_l1_prompt")
    conv_p = xbr_p.reshape(bp, seq, d_rnn)[:, seq - (CONV_W - 1):]
    h_p = ht.reshape(bp, d_rnn)

    zr_s = _small_mm(_modulate, [x2_s, sc_s, sh_s], rec_w_in, 2 * d_rnn, 512, "rec_in_sample")
    y_s, h_s, xbr_s = _vmem_call(
        _rec_step_sample, (sds(db, d_rnn), sds(db, d_rnn), sds(db, d_rnn)),
        (zr_s, state_rglru_conv[:, 0], state_rglru_conv[:, 1], state_rglru_conv[:, 2], state_rglru_h,
         rec_conv_w, row_r(rec_conv_b), rg_wa, rg_wx, row_r(rg_ba), row_r(rg_bx), row_r(rg_lambda)),
        "rec_step_sample")
    f_s = _small_mm(lambda y: y, [y_s], rec_w_out, d, 512, "mix_out_l1_sample")
    x3_s, h4_s, lg_s = _vmem_call(
        _post_ln_router, (sds(db, d), sds(db, d), sds(db, N_EXPERTS)),
        (f_s, x2_s, g_s, lnrow(ln_g[1, 0]), lnrow(ln_b[1, 0]), sc2_s, sh2_s, router_w), "post_l1_sample")
    conv_s = jnp.concatenate([state_rglru_conv[:, 1:], xbr_s[:, None, :]], axis=1)

    y_p = _channel_sublayer(x3_p, h4_p, lg_p, g2_p, lnrow(ln_g[1, 1]), lnrow(ln_b[1, 1]), router_b,
                            moe_w_in, moe_w_out, 1,128, 128, seq // 128, False, "moe_combine_l1_prompt")
    y_s = _channel_sublayer(x3_s, h4_s, lg_s, g2_s.reshape(1, db, d), lnrow(ln_g[1, 1]), lnrow(ln_b[1, 1]), router_b,
                            moe_w_in, moe_w_out, 1,32, db, 1, True, "moe_combine_l1_sample")

    kv5 = lambda a, b_: a.reshape(b_, -1, 2, N_KV, HEAD_DIM)
    keep = min(WINDOW, seq)
    win_p = kv5(kvw_p, bp)[:, seq - keep:]
    kw_full = jnp.concatenate([state_nsa_win, kv5(kvw_s, db)], axis=1)
    win_s = kw_full[:, kw_full.shape[1] - min(WINDOW, kw_full.shape[1]):]
    return (y_p.reshape(bp, seq, d), y_s.reshape(db, 1, d), kv5(kvc_p, bp), kv5(kvc_s, db), kv5(kvs_p, bp),
            kv5(kvs_s, db), win_p, win_s, v_s.reshape(db, 1, A_WIDTH), conv_p, conv_s, h_p, h_s)
```

```python
import functools
import math

import numpy as np
import jax
import jax.numpy as jnp
from jax import lax
from jax.experimental import pallas as pl
from jax.experimental.pallas import tpu as pltpu

F32 = jnp.float32
BF16 = jnp.bfloat16
I32 = jnp.int32

A_GROUPS = 8
A_GROUP_DIM = 128
A_WIDTH = A_GROUPS * A_GROUP_DIM
CHUNK = 128
N_HEADS = 16
N_KV = 4
HEAD_DIM = 64
HPG = N_HEADS // N_KV
B_WIDTH = N_HEADS * HEAD_DIM
KV_WIDTH = 2 * N_KV * HEAD_DIM
N_BRANCH = 3
CMP_LEN = 32
CMP_STRIDE = 16
CMP_HID = 2 * HEAD_DIM
SLC_LEN = 64
N_SEL = 16
WINDOW = 512
Q_BLOCK = 128
FORCE_BONUS = 1e4
ROPE_THETA = 10000.0
RNN_BLOCKS = 16
CONV_W = 4
RG_C = 8.0
N_EXPERTS = 16
N_GROUPS = 4
EXPERTS_PER_GROUP = N_EXPERTS // N_GROUPS
TOP_K = 2
DEPTH = 2
ALPHA = (2 * DEPTH) ** 0.25
LN_EPS = 1e-5
NEG = -1e30

O_U = 0
O_V = A_WIDTH
O_Q = 2 * A_WIDTH
O_KV = O_Q + B_WIDTH
O_G = O_KV + 3 * KV_WIDTH
N_GATE = N_BRANCH * N_HEADS

VMEM_LIMIT_BYTES = 56 * 1024 * 1024


def _cparams(sem, vmem=None):
    return pltpu.CompilerParams(dimension_semantics=sem, vmem_limit_bytes=vmem)


def _dot(a, b):
    return jnp.dot(a.astype(BF16), b.astype(BF16), preferred_element_type=F32)


def _dot_nt(a, b):
    dn = (((1,), (1,)), ((), ()))
    return lax.dot_general(a.astype(BF16), b.astype(BF16), dn, preferred_element_type=F32)


def _ln(x):
    mu = jnp.mean(x, -1, keepdims=True)
    xc = x - mu
    var = jnp.mean(xc * xc, -1, keepdims=True)
    return xc * lax.rsqrt(var + LN_EPS)


def _silu(x):
    return x * jax.nn.sigmoid(x)


def _gelu(x):
    return jax.nn.gelu(x, approximate=True)


def _rope(x, cos2, sin2):
    w = x.shape[1]
    rep = w // 128
    cos = jnp.concatenate([cos2] * rep, axis=1) if rep > 1 else cos2
    sin = jnp.concatenate([sin2] * rep, axis=1) if rep > 1 else sin2
    lane = lax.broadcasted_iota(I32, x.shape, 1)
    first = (lane & (HEAD_DIM - 1)) < HEAD_DIM // 2
    rot = jnp.where(first, pltpu.roll(x, w - HEAD_DIM // 2, 1), pltpu.roll(x, HEAD_DIM // 2, 1))
    return x * cos + rot * sin


def _rope_tables(pos):
    half = HEAD_DIM // 2
    inv = ROPE_THETA ** (-np.arange(half, dtype=np.float64) / half)
    ang = np.asarray(pos, np.float64)[:, None] * inv[None, :]
    cos = np.tile(np.cos(ang), (1, 4))
    sin = np.tile(np.concatenate([-np.sin(ang), np.sin(ang)], axis=1), (1, 2))
    return jnp.asarray(cos, F32), jnp.asarray(sin, F32)


def _softmax_rows(s, mask):
    sm = jnp.where(mask, s, NEG)
    m = jnp.max(sm, -1, keepdims=True)
    p = jnp.where(mask, jnp.exp(sm - m), 0.0)
    return p, jnp.sum(p, -1, keepdims=True)


def _ada_kernel(c_ref, w_ref, b_ref, o_ref):
    o_ref[...] = _dot(_silu(c_ref[...]), w_ref[...]) + b_ref[...]


def _ada_all(c_all, ada_w, ada_b):
    r, d = c_all.shape
    n_mod = ada_w.shape[0] * ada_w.shape[1]
    d3 = ada_w.shape[-1]
    tn = 512
    return pl.pallas_call(
        _ada_kernel,
        grid=(n_mod, d3 // tn),
        in_specs=[
            pl.BlockSpec((r, d), lambda l, j: (0, 0)),
            pl.BlockSpec((None, d, tn), lambda l, j: (l, 0, j)),
            pl.BlockSpec((None, 1, tn), lambda l, j: (l, 0, j)),
        ],
        out_specs=pl.BlockSpec((None, r, tn), lambda l, j: (l, 0, j)),
        out_shape=jax.ShapeDtypeStruct((n_mod, r, d3), F32),
        compiler_params=_cparams(("parallel", "parallel")),
        name="ada_mod",
    )(c_all, ada_w.reshape(n_mod, d, d3), ada_b.reshape(n_mod, 1, d3))


def _small_mm_kernel(*refs, n_x, pre):
    xs = [r[...] for r in refs[:n_x]]
    w_ref, o_ref = refs[n_x], refs[n_x + 1]
    o_ref[...] = _dot(pre(*xs), w_ref[...])


def _small_mm(pre, xs, w, n_out, tn, name):
    m = xs[0].shape[0]
    k = w.shape[0]
    in_specs = [pl.BlockSpec(x.shape, lambda j, nd=x.ndim: (0,) * nd) for x in xs]
    in_specs.append(pl.BlockSpec((k, tn), lambda j: (0, j)))
    return pl.pallas_call(
        functools.partial(_small_mm_kernel, n_x=len(xs), pre=pre),
        grid=(n_out // tn,),
        in_specs=in_specs,
        out_specs=pl.BlockSpec((m, tn), lambda j: (0, j)),
        out_shape=jax.ShapeDtypeStruct((m, n_out), F32),
        compiler_params=_cparams(("parallel",)),
        name=name,
    )(*xs, w)


def _vmem_call(fn, out_shapes, args, name):
    n_in = len(args)

    def kern(*refs):
        res = fn(*[r[...] for r in refs[:n_in]])
        for o, v in zip(refs[n_in:], res):
            o[...] = v

    return pl.pallas_call(kern, out_shape=out_shapes, name=name)(*args)


def _modulate(x, sc, sh):
    return x * (1.0 + sc) + sh


def _ab_proj_kernel(x_ref, sc_ref, sh_ref, w_ref, cos_ref, sin_ref, lng_ref, lnb_ref, wtril_ref, bst_ref,
                    a_ref, q_ref, kvc_ref, kvs_ref, kvw_ref, kvt_ref, gt_ref, *, tm):
    hb = _modulate(x_ref[...], sc_ref[...], sh_ref[...]).astype(BF16)

    def proj(lo, hi):
        return jnp.dot(hb, w_ref[:, lo:hi], preferred_element_type=F32)

    cos2 = cos_ref[...]
    sin2 = sin_ref[...]
    zu = proj(O_U, O_V)
    zv = proj(O_V, O_Q)
    for g in range(A_GROUPS):
        sl = slice(g * A_GROUP_DIM, (g + 1) * A_GROUP_DIM)
        vg = _ln(zv[:, sl]) * lng_ref[:, sl] + lnb_ref[:, sl]
        for c in range(tm // CHUNK):
            rs = slice(c * CHUNK, (c + 1) * CHUNK)
            mix = jnp.dot(wtril_ref[g], vg[rs].astype(BF16), preferred_element_type=F32) + bst_ref[:, g:g + 1]
            a_ref[rs, sl] = zu[rs, sl] * mix
    q_ref[...] = _rope(proj(O_Q, O_KV), cos2, sin2)
    half = KV_WIDTH // 2
    for br, ref in enumerate((kvc_ref, kvs_ref, kvw_ref)):
        lo = O_KV + br * KV_WIDTH
        k = _rope(proj(lo, lo + half), cos2, sin2)
        v = proj(lo + half, lo + KV_WIDTH)
        ref[:, 0:half] = k
        ref[:, half:KV_WIDTH] = v
        for g in range(N_KV):
            hs = slice(g * HEAD_DIM, (g + 1) * HEAD_DIM)
            kvt_ref[br, 0, g] = k[:, hs]
            kvt_ref[br, 1, g] = v[:, hs]
    zg = jax.nn.sigmoid(proj(O_G, O_G + N_GATE))
    per_g = N_GATE // N_KV
    for g in range(N_KV):
        gt_ref[g] = zg[:, g * per_g:(g + 1) * per_g]


def _ab_proj_prompt(x, sc, sh, w_bf, cos2, sin2, lng, lnb, wtril, bst, seq):
    n, d = x.shape
    tm = 256
    tpb = seq // tm
    n_in = w_bf.shape[1]
    row = lambda i: (i, 0)
    mod = lambda i: (i // tpb, 0, 0)
    const2 = lambda i: (0, 0)
    pos = lambda i: (i % tpb, 0)
    out_shape = (
        jax.ShapeDtypeStruct((n, A_WIDTH), F32),
        jax.ShapeDtypeStruct((n, B_WIDTH), F32),
        jax.ShapeDtypeStruct((n, KV_WIDTH), F32),
        jax.ShapeDtypeStruct((n, KV_WIDTH), F32),
        jax.ShapeDtypeStruct((n, KV_WIDTH), F32),
        jax.ShapeDtypeStruct((3, 2, N_KV, n, HEAD_DIM), F32),
        jax.ShapeDtypeStruct((N_KV, n, N_GATE // N_KV), F32),
    )
    return pl.pallas_call(
        functools.partial(_ab_proj_kernel, tm=tm),
        grid=(n // tm,),
        in_specs=[
            pl.BlockSpec((tm, d), row),
            pl.BlockSpec((None, 1, d), mod),
            pl.BlockSpec((None, 1, d), mod),
            pl.BlockSpec((d, n_in), const2, pipeline_mode=pl.Buffered(1)),
            pl.BlockSpec((tm, 128), pos),
            pl.BlockSpec((tm, 128), pos),
            pl.BlockSpec((1, A_WIDTH), const2),
            pl.BlockSpec((1, A_WIDTH), const2),
            pl.BlockSpec((A_GROUPS, CHUNK, CHUNK), lambda i: (0, 0, 0)),
            pl.BlockSpec((CHUNK, A_GROUPS), const2),
        ],
        out_specs=(
            pl.BlockSpec((tm, A_WIDTH), row),
            pl.BlockSpec((tm, B_WIDTH), row),
            pl.BlockSpec((tm, KV_WIDTH), row),
            pl.BlockSpec((tm, KV_WIDTH), row),
            pl.BlockSpec((tm, KV_WIDTH), row),
            pl.BlockSpec((3, 2, N_KV, tm, HEAD_DIM), lambda i: (0, 0, 0, i, 0)),
            pl.BlockSpec((N_KV, tm, N_GATE // N_KV), lambda i: (0, i, 0)),
        ),
        out_shape=out_shape,
        compiler_params=_cparams(("parallel",), VMEM_LIMIT_BYTES),
        name="ab_proj_prompt",
    )(x, sc, sh, w_bf, cos2, sin2, lng, lnb, wtril, bst)


def _ab_post_sample(z, zg, cos2, sin2, lng, lnb, ws0, bs0):
    zu = z[:, O_U:O_V]
    zv = z[:, O_V:O_Q]
    vs = []
    for g in range(A_GROUPS):
        sl = slice(g * A_GROUP_DIM, (g + 1) * A_GROUP_DIM)
        vs.append(_ln(zv[:, sl]) * lng[:, sl] + lnb[:, sl])
    v = jnp.concatenate(vs, axis=1)
    a = zu * (ws0 * v + bs0)
    q = _rope(z[:, O_Q:O_KV], cos2, sin2)
    half = KV_WIDTH // 2
    kvs = []
    for br in range(3):
        lo = O_KV + br * KV_WIDTH
        k = _rope(z[:, lo:lo + half], cos2, sin2)
        kvs.append(jnp.concatenate([k, z[:, lo + half:lo + KV_WIDTH]], axis=1))
    return a, v, q, kvs[0], kvs[1], kvs[2], jax.nn.sigmoid(zg)


def _compress_mlp(lhs_lo, lhs_hi, w1_ref, w2_ref, kv, n):
    lo = _dot(lhs_lo, w1_ref[kv, 0])
    hi = _dot(lhs_hi, w1_ref[kv, 1])
    hsum = lo + pltpu.roll(hi, n - 1, 0)
    return _dot(_gelu(hsum), w2_ref[kv])


def _compress_prompt_kernel(x_ref, pe_ref, w1_ref, w2_ref, o_ref, *, n16):
    for kv in range(2):
        for gp in range(N_KV // 2):
            base = kv * (KV_WIDTH // 2) + gp * 128
            xg = jnp.concatenate(
                [x_ref[:, s * KV_WIDTH + base:s * KV_WIDTH + base + 128] for s in range(CMP_STRIDE)], axis=1)
            r = _compress_mlp(xg + pe_ref[kv, 0], xg + pe_ref[kv, 1], w1_ref, w2_ref, kv, n16)
            o_ref[kv, 2 * gp] = r[:, 0:HEAD_DIM]
            o_ref[kv, 2 * gp + 1] = r[:, HEAD_DIM:2 * HEAD_DIM]


def _compress_prompt(kvc3, pe_rows, w1bd, w2bd):
    b, n16, wid = kvc3.shape
    return pl.pallas_call(
        functools.partial(_compress_prompt_kernel, n16=n16),
        grid=(b,),
        in_specs=[
            pl.BlockSpec((None, n16, wid), lambda i: (i, 0, 0)),
            pl.BlockSpec(pe_rows.shape, lambda i: (0, 0, 0, 0)),
            pl.BlockSpec(w1bd.shape, lambda i: (0, 0, 0, 0)),
            pl.BlockSpec(w2bd.shape, lambda i: (0, 0, 0)),
        ],
        out_specs=pl.BlockSpec((None, 2, N_KV, n16, HEAD_DIM), lambda i: (i, 0, 0, 0, 0)),
        out_shape=jax.ShapeDtypeStruct((b, 2, N_KV, n16, HEAD_DIM), F32),
        compiler_params=_cparams(("parallel",), VMEM_LIMIT_BYTES),
        name="nsa_compress_prompt",
    )(kvc3, pe_rows, w1bd, w2bd)


def _compress_weights(pe, w1, w2):
    eye2 = jnp.eye(2, dtype=F32)
    w1r = w1.reshape(2, 2, CMP_STRIDE, HEAD_DIM, CMP_HID)
    w1bd = jnp.einsum('khsdc,gG->khsgdGc', w1r, eye2).reshape(2, 2, CMP_STRIDE * 128, 2 * CMP_HID).astype(BF16)
    pe_rows = jnp.broadcast_to(pe.reshape(2, 2, CMP_STRIDE, 1, HEAD_DIM), (2, 2, CMP_STRIDE, 2, HEAD_DIM))
    pe_rows = pe_rows.reshape(2, 2, 1, CMP_STRIDE * 128)
    w2bd = jnp.einsum('kcd,gG->kgcGd', w2, eye2).reshape(2, 2 * CMP_HID, 2 * HEAD_DIM).astype(BF16)
    return pe_rows, w1bd, w2bd


def _cmp_slc_map(n_rows, n_cmp, n_slc, n_cols):
    cs = np.arange(n_rows)[:, None] * CMP_STRIDE
    ss = np.arange(n_cols)[None, :] * SLC_LEN
    m = (cs < ss + SLC_LEN) & (cs + CMP_LEN > ss)
    m &= (np.arange(n_rows)[:, None] < n_cmp) & (np.arange(n_cols)[None, :] < n_slc)
    return m.astype(np.float32)


def _nsa_prompt_kernel(q_ref, gt_ref, kc_ref, vc_ref, ks_ref, vs_ref, kw_ref, vw_ref, mapt_ref, e_ref,
                       o_ref, selexp_ref, *, n_cmp, n_slc, n_sel, kt_len, win_len):
    i = pl.program_id(2)
    q0 = i * Q_BLOCK
    rows = HPG * Q_BLOCK
    qb = q_ref[...]
    q4 = jnp.concatenate([qb[:, h * HEAD_DIM:(h + 1) * HEAD_DIM] for h in range(HPG)], axis=0)
    q4 = (q4 * HEAD_DIM ** -0.5).astype(BF16)
    pos = q0 + (lax.broadcasted_iota(I32, (rows, 1), 0) & (Q_BLOCK - 1))

    n16 = kc_ref.shape[0]
    s = _dot_nt(q4, kc_ref[...])
    n_idx = lax.broadcasted_iota(I32, (1, n16), 1)
    valid = (n_idx * CMP_STRIDE + CMP_LEN - 1 <= pos) & (n_idx < n_cmp)
    p, l = _softmax_rows(s, valid)
    pn = p * (1.0 / jnp.maximum(l, 1e-30))
    o_c = _dot(pn, vc_ref[...])
    psum = pn[0:Q_BLOCK]
    for h in range(1, HPG):
        psum = psum + pn[h * Q_BLOCK:(h + 1) * Q_BLOCK]

    imp_t = _dot_nt(mapt_ref[...], psum)
    blk_t = lax.broadcasted_iota(I32, (n_slc, Q_BLOCK), 0)
    cur_t = (q0 + lax.broadcasted_iota(I32, (n_slc, Q_BLOCK), 1)) // SLC_LEN
    forced = (blk_t == 0) | (blk_t == cur_t) | (blk_t == cur_t - 1)
    score = jnp.where(blk_t <= cur_t, imp_t + jnp.where(forced, FORCE_BONUS, 0.0), -jnp.inf)
    rank = jnp.zeros((n_slc, Q_BLOCK), I32)
    for j in range(n_slc):
        r = score[j:j + 1, :]
        rank = rank + ((r > score) | ((r == score) & (blk_t > j))).astype(I32)
    sel_t = ((rank < n_sel) & (score > -jnp.inf)).astype(F32)
    selexp_ref[...] = _dot(sel_t.T, e_ref[...])

    def sel_body(kt, carry):
        m, l, acc = carry
        k0 = pl.multiple_of(kt * kt_len, kt_len)
        s = _dot_nt(q4, ks_ref[pl.ds(k0, kt_len), :])
        kp = k0 + lax.broadcasted_iota(I32, (1, kt_len), 1)
        se = selexp_ref[:, pl.ds(k0, kt_len)]
        mask = (jnp.concatenate([se] * HPG, axis=0) > 0.5) & (kp <= pos)
        sm = jnp.where(mask, s, NEG)
        m_new = jnp.maximum(m, jnp.max(sm, -1, keepdims=True))
        alpha = jnp.exp(m - m_new)
        p = jnp.where(mask, jnp.exp(sm - m_new), 0.0)
        l = alpha * l + jnp.sum(p, -1, keepdims=True)
        acc = alpha * acc + _dot(p, vs_ref[pl.ds(k0, kt_len), :])
        return m_new, l, acc

    n_kt = (q0 + Q_BLOCK + kt_len - 1) // kt_len
    init = (jnp.full((rows, 1), NEG, F32), jnp.zeros((rows, 1), F32), jnp.zeros((rows, HEAD_DIM), F32))
    _, l, acc = lax.fori_loop(0, n_kt, sel_body, init)
    o_s = acc * (1.0 / jnp.maximum(l, 1e-30))

    start = pl.multiple_of(jnp.maximum(q0 - WINDOW, 0), Q_BLOCK)
    s = _dot_nt(q4, kw_ref[pl.ds(start, win_len), :])
    kp = start + lax.broadcasted_iota(I32, (1, win_len), 1)
    p, l = _softmax_rows(s, (kp <= pos) & (kp > pos - WINDOW))
    o_w = _dot(p, vw_ref[pl.ds(start, win_len), :]) * (1.0 / jnp.maximum(l, 1e-30))

    gt = gt_ref[...]

    def gcol(br):
        return jnp.concatenate([gt[:, br * HPG + h:br * HPG + h + 1] for h in range(HPG)], axis=0)

    o = gcol(0) * o_c + gcol(1) * o_s + gcol(2) * o_w
    o_ref[...] = jnp.concatenate([o[h * Q_BLOCK:(h + 1) * Q_BLOCK] for h in range(HPG)], axis=1)


def _nsa_prompt(q, gates, kcv, kvt, batch, seq):
    n = q.shape[0]
    nq = seq // Q_BLOCK
    n16 = kcv.shape[3]
    n_cmp = n16 - 1
    n_slc = -(-seq // SLC_LEN)
    n_sel = min(N_SEL, n_slc)
    kt_len = min(512, seq)
    win_len = min(WINDOW + Q_BLOCK, seq)
    mapt = jnp.asarray(_cmp_slc_map(n16, n_cmp, n_slc, n_slc).T)
    expand = jnp.asarray((np.arange(n_slc)[:, None] == np.arange(seq)[None, :] // SLC_LEN).astype(np.float32), BF16)
    per_g = N_GATE // N_KV
    qrow = lambda b, g, i: (b * nq + i, g)
    kvspec = lambda br, kv: pl.BlockSpec((None, None, None, seq, HEAD_DIM), lambda b, g, i: (br, kv, g, b, 0))
    return pl.pallas_call(
        functools.partial(_nsa_prompt_kernel, n_cmp=n_cmp, n_slc=n_slc, n_sel=n_sel, kt_len=kt_len, win_len=win_len),
        grid=(batch, N_KV, nq),
        in_specs=[
            pl.BlockSpec((Q_BLOCK, HPG * HEAD_DIM), qrow),
            pl.BlockSpec((None, Q_BLOCK, per_g), lambda b, g, i: (g, b * nq + i, 0)),
            pl.BlockSpec((None, None, None, n16, HEAD_DIM), lambda b, g, i: (b, 0, g, 0, 0)),
            pl.BlockSpec((None, None, None, n16, HEAD_DIM), lambda b, g, i: (b, 1, g, 0, 0)),
            kvspec(1, 0), kvspec(1, 1), kvspec(2, 0), kvspec(2, 1),
            pl.BlockSpec(mapt.shape, lambda b, g, i: (0, 0)),
            pl.BlockSpec(expand.shape, lambda b, g, i: (0, 0)),
        ],
        out_specs=pl.BlockSpec((Q_BLOCK, HPG * HEAD_DIM), qrow),
        out_shape=jax.ShapeDtypeStruct((n, B_WIDTH), F32),
        scratch_shapes=[pltpu.VMEM((Q_BLOCK, seq), F32)],
        compiler_params=_cparams(("parallel", "parallel", "arbitrary"), VMEM_LIMIT_BYTES),
        name="nsa_attn_prompt",
    )(q, gates, kcv, kcv, kvt, kvt, kvt, kvt, mapt, expand)


def _page_copy(cache_hbm, pt_ref, xbuf, sem, b, slot, p, n_pages):
    return pltpu.make_async_copy(cache_hbm.at[pt_ref[b * n_pages + p]], xbuf.at[slot, p], sem.at[slot])


def _nsa_sample_cmp_kernel(pt_ref, q_ref, cache_hbm, pet_ref, perm_ref, w1_ref, w2_ref, map_ref,
                           oc_ref, idx_ref, xbuf, lhs_lo, lhs_hi, sem, *, n_pages, n_cmp, n_slc, n_sel, pos):
    b = pl.program_id(0)
    nb = pl.num_programs(0)
    slot = b % 2
    n16 = lhs_lo.shape[0]
    cpp = n16 // n_pages

    def start_all(bb, sl):
        def body(p, c):
            _page_copy(cache_hbm, pt_ref, xbuf, sem, bb, sl, p, n_pages).start()
            return c
        lax.fori_loop(0, n_pages, body, 0)

    @pl.when(b == 0)
    def _():
        start_all(0, 0)

    @pl.when(b + 1 < nb)
    def _():
        start_all(b + 1, 1 - slot)

    def wait_body(p, c):
        _page_copy(cache_hbm, pt_ref, xbuf, sem, b, slot, p, n_pages).wait()
        return c
    lax.fori_loop(0, n_pages, wait_body, 0)

    kc, vc = [], []
    for kv, dst in ((0, kc), (1, vc)):
        for gp in range(N_KV // 2):
            base = (kv * N_KV + 2 * gp) * HEAD_DIM

            def fill(p, c):
                xt = xbuf[slot, p, base:base + 2 * HEAD_DIM, :]
                r0 = pl.multiple_of(p * cpp, cpp)
                for half, lhs in ((0, lhs_lo), (1, lhs_hi)):
                    y = _dot_nt(perm_ref[...], xt + pet_ref[kv, half])
                    for s in range(CMP_STRIDE):
                        lhs[pl.ds(r0, cpp), s * 128:(s + 1) * 128] = y[s * cpp:(s + 1) * cpp, :]
                return c
            lax.fori_loop(0, n_pages, fill, 0)
            r = _compress_mlp(lhs_lo[...], lhs_hi[...], w1_ref, w2_ref, kv, n16)
            dst.append(r[:, 0:HEAD_DIM])
            dst.append(r[:, HEAD_DIM:2 * HEAD_DIM])

    qrow = q_ref[...] * HEAD_DIM ** -0.5
    n_idx = lax.broadcasted_iota(I32, (1, n16), 1)
    valid = (n_idx * CMP_STRIDE + CMP_LEN - 1 <= pos) & (n_idx < n_cmp)
    head_row = lax.broadcasted_iota(I32, (8, 1), 0) < HPG
    ncol = map_ref.shape[1]
    blk_r = lax.broadcasted_iota(I32, (1, ncol), 1)
    blk_c = lax.broadcasted_iota(I32, (ncol, 1), 0)
    cur = pos // SLC_LEN
    forced = (blk_r == 0) | (blk_r == cur) | (blk_r == cur - 1)
    oc_parts = []
    for g in range(N_KV):
        q8 = jnp.concatenate(
            [qrow[:, (g * HPG + h) * HEAD_DIM:(g * HPG + h + 1) * HEAD_DIM] for h in range(HPG)]
            + [jnp.zeros((8 - HPG, HEAD_DIM), F32)], axis=0)
        s = _dot_nt(q8, kc[g])
        p, l = _softmax_rows(s, valid)
        pn = jnp.where(head_row, p / jnp.maximum(l, 1e-30), 0.0)
        o8 = _dot(pn, vc[g])
        oc_parts += [o8[h:h + 1, :] for h in range(HPG)]
        psum = jnp.broadcast_to(jnp.sum(pn, axis=0, keepdims=True), pn.shape)
        imp = _dot(psum, map_ref[...])[0:1, :]
        score = jnp.where((blk_r <= cur) & (blk_r < n_slc), imp + jnp.where(forced, FORCE_BONUS, 0.0), -jnp.inf)
        score_c = jnp.broadcast_to(score, (8, ncol)).T[:, 0:1]
        beats = (score_c > score) | ((score_c == score) & (blk_c < blk_r))
        rank = jnp.sum(beats.astype(F32), axis=0, keepdims=True)
        r_iota = lax.broadcasted_iota(I32, (n_sel, ncol), 0).astype(F32)
        hit = (rank == r_iota) & (score > -jnp.inf)
        idx = jnp.sum(jnp.where(hit, blk_r.astype(F32), 0.0), axis=1, keepdims=True)
        idx_ref[:, g:g + 1] = idx.astype(I32)
    oc_ref[...] = jnp.concatenate(oc_parts, axis=1)


def _nsa_sample_cmp(page_table, q3, cache_t, pe, w1bd, w2bd, past_len):
    db, n_pages = page_table.shape
    page = cache_t.shape[2]
    cpp = page // CMP_STRIDE
    n16 = n_pages * cpp
    n_cmp = (past_len + 1) // CMP_STRIDE - 1
    n_slc = -(-(past_len + 1) // SLC_LEN)
    n_sel = min(N_SEL, n_slc)
    ncol = -(-n_slc // 128) * 128
    cmap = jnp.asarray(_cmp_slc_map(n16, n_cmp, n_slc, ncol))
    pe_t = jnp.transpose(pe.reshape(2, 2, CMP_STRIDE, HEAD_DIM), (0, 1, 3, 2))
    pe_t = jnp.tile(pe_t, (1, 1, 2, cpp))
    r = np.arange(page)
    perm = np.zeros((page, page), np.float32)
    perm[(r % CMP_STRIDE) * cpp + r // CMP_STRIDE, r] = 1.0
    perm = jnp.asarray(perm, BF16)
    grid_spec = pltpu.PrefetchScalarGridSpec(
        num_scalar_prefetch=1,
        grid=(db,),
        in_specs=[
            pl.BlockSpec((None, 1, B_WIDTH), lambda b, pt: (b, 0, 0)),
            pl.BlockSpec(memory_space=pl.ANY),
            pl.BlockSpec(pe_t.shape, lambda b, pt: (0, 0, 0, 0)),
            pl.BlockSpec(perm.shape, lambda b, pt: (0, 0)),
            pl.BlockSpec(w1bd.shape, lambda b, pt: (0, 0, 0, 0)),
            pl.BlockSpec(w2bd.shape, lambda b, pt: (0, 0, 0)),
            pl.BlockSpec(cmap.shape, lambda b, pt: (0, 0)),
        ],
        out_specs=(
            pl.BlockSpec((None, 1, B_WIDTH), lambda b, pt: (b, 0, 0)),
            pl.BlockSpec((None, n_sel, N_KV), lambda b, pt: (b, 0, 0)),
        ),
        scratch_shapes=[pltpu.VMEM((2, n_pages) + cache_t.shape[1:], F32),
                        pltpu.VMEM((n16, CMP_STRIDE * 128), F32), pltpu.VMEM((n16, CMP_STRIDE * 128), F32),
                        pltpu.SemaphoreType.DMA((2,))],
    )
    return pl.pallas_call(
        functools.partial(_nsa_sample_cmp_kernel, n_pages=n_pages, n_cmp=n_cmp, n_slc=n_slc, n_sel=n_sel, pos=past_len),
        grid_spec=grid_spec,
        out_shape=(jax.ShapeDtypeStruct((db, 1, B_WIDTH), F32), jax.ShapeDtypeStruct((db, n_sel, N_KV), I32)),
        compiler_params=_cparams(("arbitrary",), VMEM_LIMIT_BYTES),
        name="nsa_cmp_sample",
    )(page_table.reshape(-1), q3, cache_t, pe_t, perm, w1bd, w2bd, cmap)


def _sel_copy(cache_hbm, pt_ref, idx_ref, kbuf, sem, b, slot, j2, n_pages, n_sel, last_real):
    j = j2 // 2
    kv = j2 % 2
    g = j // n_sel
    blk = jnp.minimum(idx_ref[b * (N_KV * n_sel) + j], last_real)
    per_page = cache_hbm.shape[2] // SLC_LEN
    page = pt_ref[b * n_pages + blk // per_page]
    row0 = pl.multiple_of((kv * N_KV + g) * HEAD_DIM, HEAD_DIM)
    return pltpu.make_async_copy(cache_hbm.at[page, pl.ds(row0, HEAD_DIM), :], kbuf.at[slot, j2], sem.at[slot])


def _nsa_sample_attn_kernel(pt_ref, idx_ref, q_ref, gt_ref, oc_ref, ks_new_ref, kw_new_ref, win_ref, cache_hbm,
                            o_ref, kbuf, sem, *, n_pages, n_sel, pos):
    b = pl.program_id(0)
    nb = pl.num_programs(0)
    slot = b % 2
    n_copies = 2 * N_KV * n_sel
    page = kbuf.shape[3]
    per_page = page // SLC_LEN
    new_blk = pos // SLC_LEN
    last_real = new_blk - 1

    def start_all(bb, sl):
        def body(j, c):
            _sel_copy(cache_hbm, pt_ref, idx_ref, kbuf, sem, bb, sl, j, n_pages, n_sel, last_real).start()
            return c
        lax.fori_loop(0, n_copies, body, 0)

    @pl.when(b == 0)
    def _():
        start_all(0, 0)

    @pl.when(b + 1 < nb)
    def _():
        start_all(b + 1, 1 - slot)

    def wait_body(j, c):
        _sel_copy(cache_hbm, pt_ref, idx_ref, kbuf, sem, b, slot, j, n_pages, n_sel, last_real).wait()
        return c
    lax.fori_loop(0, n_copies, wait_body, 0)

    qrow = q_ref[...] * HEAD_DIM ** -0.5
    gt = gt_ref[...]
    oc = oc_ref[...]
    ks_new = ks_new_ref[...]
    kw_new = kw_new_ref[...]
    half = KV_WIDTH // 2
    wlen = win_ref.shape[0]
    kp_w = pos - wlen + lax.broadcasted_iota(I32, (1, wlen), 1)
    mask_w = (kp_w >= 0) & (kp_w <= pos) & (kp_w > pos - WINDOW)
    out_parts = []
    for g in range(N_KV):
        q8 = jnp.concatenate(
            [qrow[:, (g * HPG + h) * HEAD_DIM:(g * HPG + h + 1) * HEAD_DIM] for h in range(HPG)]
            + [jnp.zeros((8 - HPG, HEAD_DIM), F32)], axis=0)
        ksl = slice(g * HEAD_DIM, (g + 1) * HEAD_DIM)
        vsl = slice(half + g * HEAD_DIM, half + (g + 1) * HEAD_DIM)

        def attend(s, mask, pv, k_new, v_new, has_new):
            rb = lambda t: t.astype(BF16).astype(F32)
            s_new = jnp.sum(rb(q8) * rb(k_new), axis=-1, keepdims=True)
            sm = jnp.where(mask, s, NEG)
            m = jnp.maximum(jnp.max(sm, -1, keepdims=True), jnp.where(has_new, s_new, NEG))
            p = jnp.where(mask, jnp.exp(sm - m), 0.0)
            p_new = jnp.where(has_new, jnp.exp(s_new - m), 0.0)
            l = jnp.maximum(jnp.sum(p, -1, keepdims=True) + p_new, 1e-30)
            return pv(p / l) + rb(p_new / l) * rb(v_new)

        kt_sel = jnp.concatenate([kbuf[slot, (g * n_sel + r) * 2] for r in range(n_sel)], axis=1)
        vt_sel = jnp.concatenate([kbuf[slot, (g * n_sel + r) * 2 + 1] for r in range(n_sel)], axis=1)
        blk_ids = [idx_ref[b * (N_KV * n_sel) + g * n_sel + r] for r in range(n_sel)]
        row_of = lambda vals: jnp.concatenate([jnp.full((1, page), v, I32) for v in vals], axis=1)
        lane_blk = (lax.broadcasted_iota(I32, (1, n_sel * page), 1) & (page - 1)) // SLC_LEN
        picked = (row_of(blk_ids) <= last_real) & (lane_blk == row_of([bid % per_page for bid in blk_ids]))
        has_new = functools.reduce(jnp.logical_or, [bid == new_blk for bid in blk_ids])
        o_s = attend(_dot(q8, kt_sel), picked, lambda p: _dot_nt(p, vt_sel), ks_new[:, ksl], ks_new[:, vsl], has_new)
        o_w = attend(_dot_nt(q8, win_ref[:, ksl]), mask_w, lambda p: _dot(p, win_ref[:, vsl]),
                     kw_new[:, ksl], kw_new[:, vsl], True)
        for h in range(HPG):
            hh = g * HPG + h
            c0 = g * (N_BRANCH * HPG)
            g0 = gt[:, c0 + h:c0 + h + 1]
            g1 = gt[:, c0 + HPG + h:c0 + HPG + h + 1]
            g2 = gt[:, c0 + 2 * HPG + h:c0 + 2 * HPG + h + 1]
            out_parts.append(g0 * oc[:, hh * HEAD_DIM:(hh + 1) * HEAD_DIM] + g1 * o_s[h:h + 1, :] + g2 * o_w[h:h + 1, :])
    o_ref[...] = jnp.concatenate(out_parts, axis=1)


def _nsa_sample_attn(page_table, sel_idx, q3, gates3, oc3, ks_new3, kw_new3, win_state, cache_s, past_len):
    db, n_pages = page_table.shape
    n_sel = sel_idx.shape[-1]
    wlen = win_state.shape[1]
    row = lambda wdt: pl.BlockSpec((None, 1, wdt), lambda b, pt, ix: (b, 0, 0))
    grid_spec = pltpu.PrefetchScalarGridSpec(
        num_scalar_prefetch=2,
        grid=(db,),
        in_specs=[
            row(B_WIDTH), row(N_GATE), row(B_WIDTH), row(KV_WIDTH), row(KV_WIDTH),
            pl.BlockSpec((None, wlen, KV_WIDTH), lambda b, pt, ix: (b, 0, 0)),
            pl.BlockSpec(memory_space=pl.ANY),
        ],
        out_specs=row(B_WIDTH),
        scratch_shapes=[pltpu.VMEM((2, 2 * N_KV * n_sel, HEAD_DIM, cache_s.shape[2]), F32),
                        pltpu.SemaphoreType.DMA((2,))],
    )
    return pl.pallas_call(
        functools.partial(_nsa_sample_attn_kernel, n_pages=n_pages, n_sel=n_sel, pos=past_len),
        grid_spec=grid_spec,
        out_shape=jax.ShapeDtypeStruct((db, 1, B_WIDTH), F32),
        compiler_params=_cparams(("arbitrary",), VMEM_LIMIT_BYTES),
        name="nsa_attn_sample",
    )(page_table.reshape(-1), sel_idx.reshape(-1), q3, gates3, oc3, ks_new3, kw_new3, win_state, cache_s)


def _post_ln_router(acc, x, gate, lng, lnb, sc2, sh2, rw_t):
    x1 = _ln(ALPHA * x + gate * acc) * lng + lnb
    h2 = _modulate(x1, sc2, sh2)
    return x1, h2, _dot_nt(rw_t, h2)


def _mix_out_kernel(l0_ref, l1_ref, w_ref, x_ref, gate_ref, lng_ref, lnb_ref, sc2_ref, sh2_ref, rw_ref,
                    x1_ref, h2_ref, lg_ref, *, concat):
    if concat:
        k0 = l0_ref.shape[1]
        acc = _dot(l0_ref[...], w_ref[0:k0, :]) + _dot(l1_ref[...], w_ref[k0:, :])
    else:
        acc = _dot(l0_ref[...] * l1_ref[...], w_ref[...])
    x1, h2, lg = _post_ln_router(acc, x_ref[...], gate_ref[...], lng_ref[...], lnb_ref[...],
                                 sc2_ref[...], sh2_ref[...], rw_ref[...])
    x1_ref[...] = x1
    h2_ref[...] = h2
    lg_ref[...] = lg


def _mix_out_prompt(l0, l1, w_bf, x, gate, lng, lnb, sc2, sh2, rw, seq, concat, name):
    n, d = x.shape
    tm = 256
    tpb = seq // tm
    row = lambda i: (i, 0)
    mod = lambda i: (i // tpb, 0, 0)
    const2 = lambda i: (0, 0)
    return pl.pallas_call(
        functools.partial(_mix_out_kernel, concat=concat),
        grid=(n // tm,),
        in_specs=[
            pl.BlockSpec((tm, l0.shape[1]), row),
            pl.BlockSpec((tm, l1.shape[1]), row),
            pl.BlockSpec(w_bf.shape, const2),
            pl.BlockSpec((tm, d), row),
            pl.BlockSpec((None, 1, d), mod),
            pl.BlockSpec((1, d), const2),
            pl.BlockSpec((1, d), const2),
            pl.BlockSpec((None, 1, d), mod),
            pl.BlockSpec((None, 1, d), mod),
            pl.BlockSpec(rw.shape, const2),
        ],
        out_specs=(pl.BlockSpec((tm, d), row), pl.BlockSpec((tm, d), row),
                   pl.BlockSpec((N_EXPERTS, tm), lambda i: (0, i))),
        out_shape=(jax.ShapeDtypeStruct((n, d), F32), jax.ShapeDtypeStruct((n, d), F32),
                   jax.ShapeDtypeStruct((N_EXPERTS, n), F32)),
        compiler_params=_cparams(("parallel",), VMEM_LIMIT_BYTES),
        name=name,
    )(l0, l1, w_bf, x, gate, lng, lnb, sc2, sh2, rw)


def _top2_route(lg, rb):
    s = jax.nn.sigmoid(lg)
    sb = s + rb
    rows = [sb[e:e + 1, :] for e in range(N_EXPERTS)]
    gs = []
    for g in range(N_GROUPS):
        v = rows[g * EXPERTS_PER_GROUP:(g + 1) * EXPERTS_PER_GROUP]
        pair = [v[i] + v[j] for i in range(EXPERTS_PER_GROUP) for j in range(i + 1, EXPERTS_PER_GROUP)]
        gs.append(functools.reduce(jnp.maximum, pair))
    best, gi = gs[0], jnp.zeros(gs[0].shape, I32)
    for g in range(1, N_GROUPS):
        better = gs[g] > best
        gi = jnp.where(better, g, gi)
        best = jnp.where(better, gs[g], best)
    cand = [jnp.where(gi == e // EXPERTS_PER_GROUP, rows[e], -jnp.inf) for e in range(N_EXPERTS)]
    ids = []
    for k in range(TOP_K):
        vk = jnp.full(cand[0].shape, -jnp.inf, F32)
        ik = jnp.zeros(cand[0].shape, I32)
        for e in range(N_EXPERTS):
            c = cand[e]
            for prev in ids:
                c = jnp.where(prev == e, -jnp.inf, c)
            better = c > vk
            ik = jnp.where(better, e, ik)
            vk = jnp.where(better, c, vk)
        ids.append(ik)
    ws = [functools.reduce(jnp.add, [jnp.where(ik == e, s[e:e + 1, :], 0.0) for e in range(N_EXPERTS)]) for ik in ids]
    tot = functools.reduce(jnp.add, ws)
    return ids, [w / tot for w in ws]


def _route_kernel(lg_ref, rb_ref, w_ref, dest_ref, be_ref, nu_ref, tot_ref, run_ref, ps_ref, *, blk):
    ph = pl.program_id(0)
    i = pl.program_id(1)
    tm = lg_ref.shape[1]
    ids, ws = _top2_route(lg_ref[...], rb_ref[...])
    e_iota = lax.broadcasted_iota(I32, (N_EXPERTS, tm), 0)
    oh = [(e_iota == ik).astype(F32) for ik in ids]
    ohsum = functools.reduce(jnp.add, oh)
    tile_cnt = jnp.sum(ohsum, axis=1, keepdims=True)

    @pl.when((ph == 0) & (i == 0))
    def _():
        tot_ref[...] = jnp.zeros_like(tot_ref)

    @pl.when(ph == 0)
    def _():
        tot_ref[...] = tot_ref[...] + tile_cnt

    @pl.when((ph == 1) & (i == 0))
    def _():
        cnt = tot_ref[...]
        padded = jnp.floor((cnt + (blk - 1)) * (1.0 / blk)) * blk
        sub = lax.broadcasted_iota(I32, cnt.shape, 0)
        start = jnp.zeros_like(cnt)
        for e in range(N_EXPERTS):
            start = start + jnp.where(sub > e, padded[e:e + 1, :], 0.0)
        ps_ref[...] = start
        run_ref[...] = jnp.zeros_like(run_ref)
        pad_end = start[:, 0:1] + padded[:, 0:1]
        blk_lo = (lax.broadcasted_iota(I32, (N_EXPERTS, be_ref.shape[1]), 1) * blk).astype(F32)
        n_le = jnp.sum((pad_end <= blk_lo).astype(F32), axis=0, keepdims=True)
        be_ref[...] = jnp.minimum(n_le, N_EXPERTS - 1.0).astype(I32)
        nu_ref[...] = (jnp.max(pad_end, axis=0, keepdims=True) * (1.0 / blk) + jnp.zeros(nu_ref.shape, F32)).astype(I32)

    @pl.when(ph == 1)
    def _():
        t_r = lax.broadcasted_iota(I32, (tm, tm), 0)
        t_c = lax.broadcasted_iota(I32, (tm, tm), 1)
        before = _dot(ohsum, (t_r < t_c).astype(F32))
        base = before + run_ref[:, 0:1] + ps_ref[:, 0:1]
        for k in range(TOP_K):
            w_ref[k:k + 1, :] = ws[k]
            dest_ref[k:k + 1, :] = jnp.sum(oh[k] * base, axis=0, keepdims=True).astype(I32)
        run_ref[...] = run_ref[...] + tile_cnt


def _route_tables(logits_t, router_b, blk):
    n = logits_t.shape[1]
    tm = min(512, n)
    a = n * TOP_K
    n_blocks = -(-a // blk) + N_EXPERTS
    nb_pad = -(-n_blocks // 128) * 128
    tok_blk = lambda p, i: (0, i * p)
    const = lambda p, i: (0, 0)
    w, dest, block_e, n_used = pl.pallas_call(
        functools.partial(_route_kernel, blk=blk),
        grid=(2, n // tm),
        in_specs=[pl.BlockSpec((N_EXPERTS, tm), lambda p, i: (0, i)), pl.BlockSpec((N_EXPERTS, 1), const)],
        out_specs=(pl.BlockSpec((TOP_K, tm), tok_blk), pl.BlockSpec((TOP_K, tm), tok_blk),
                   pl.BlockSpec((1, nb_pad), const), pl.BlockSpec((1, 128), const)),
        out_shape=(jax.ShapeDtypeStruct((TOP_K, n), F32), jax.ShapeDtypeStruct((TOP_K, n), I32),
                   jax.ShapeDtypeStruct((1, nb_pad), I32), jax.ShapeDtypeStruct((1, 128), I32)),
        scratch_shapes=[pltpu.VMEM((N_EXPERTS, 128), F32)] * 3,
        compiler_params=_cparams(("arbitrary", "arbitrary")),
        name="moe_route",
    )(logits_t, router_b.reshape(N_EXPERTS, 1).astype(F32))
    slot_of = dest.T.reshape(-1)
    tok = jnp.repeat(jnp.arange(n, dtype=I32), TOP_K)
    slot_tok = jnp.zeros((n_blocks * blk,), I32).at[slot_of].set(tok)
    return slot_tok, slot_of, w.T, block_e[0, :n_blocks], n_used[0, :1], n_blocks


def _row_copy(src_hbm, row, buf, slot, r, sem):
    return pltpu.make_async_copy(src_hbm.at[pl.ds(row, 1), :], buf.at[slot, pl.ds(r, 1), :], sem.at[slot])


def _moe_kernel(be_ref, tok_ref, nu_ref, x_hbm, win_all, wout_all, y_ref,
                xs_buf, w1_scr, w2_scr, stage, gsem, wsem, *, blk, layer):
    win_hbm = win_all.at[layer]
    wout_hbm = wout_all.at[layer]
    i = pl.program_id(0)
    n_used = nu_ref[0]
    slot = i % 2
    e = be_ref[i]

    def start_gather(bi, sl):
        def body(r, c):
            _row_copy(x_hbm, tok_ref[bi * blk + r], xs_buf, sl, r, gsem).start()
            return c
        lax.fori_loop(0, blk, body, 0)

    @pl.when(i == 0)
    def _():
        start_gather(0, 0)

    @pl.when(i + 1 < n_used)
    def _():
        start_gather(i + 1, 1 - slot)

    changed = (i == 0) | (e != be_ref[jnp.maximum(i - 1, 0)])

    @pl.when(changed & (i < n_used))
    def _():
        rows = stage.shape[1]
        chunks = [(win_hbm, w1_scr, c) for c in range(w1_scr.shape[0] // rows)]
        chunks += [(wout_hbm, w2_scr, c) for c in range(w2_scr.shape[0] // rows)]

        def cp(j):
            src, _, c = chunks[j]
            return pltpu.make_async_copy(src.at[e, pl.ds(c * rows, rows), :], stage.at[j % 2], wsem.at[j % 2])

        cp(0).start()
        for j in range(len(chunks)):
            if j + 1 < len(chunks):
                cp(j + 1).start()
            cp(j).wait()
            _, dst, c = chunks[j]
            dst[pl.ds(c * rows, rows), :] = stage[j % 2].astype(BF16)

    @pl.when(i < n_used)
    def _():
        def wait_body(r, c):
            _row_copy(x_hbm, 0, xs_buf, slot, r, gsem).wait()
            return c
        lax.fori_loop(0, blk, wait_body, 0)
        z = _dot(xs_buf[slot], w1_scr[...])
        de = z.shape[1] // 2
        act = _silu(z[:, :de]) * z[:, de:]
        y_ref[...] = _dot(act, w2_scr[...])

    @pl.when(i >= n_used)
    def _():
        y_ref[...] = jnp.zeros_like(y_ref)


def _moe_ffn(h2, slot_tok, block_e, n_used, n_blocks, w_in, w_out, layer, blk, name):
    n, d = h2.shape
    de2 = w_in.shape[-1]
    wdt = BF16
    stage_rows = 256
    grid_spec = pltpu.PrefetchScalarGridSpec(
        num_scalar_prefetch=3,
        grid=(n_blocks,),
        in_specs=[pl.BlockSpec(memory_space=pl.ANY)] * 3,
        out_specs=pl.BlockSpec((blk, d), lambda i, be, tk, nu: (i, 0)),
        scratch_shapes=[
            pltpu.VMEM((2, blk, d), F32),
            pltpu.VMEM((d, de2), wdt),
            pltpu.VMEM((de2 // 2, d), wdt),
            pltpu.VMEM((2, stage_rows, max(d, de2)), F32),
            pltpu.SemaphoreType.DMA((2,)),
            pltpu.SemaphoreType.DMA((2,)),
        ],
    )
    return pl.pallas_call(
        functools.partial(_moe_kernel, blk=blk, layer=layer),
        grid_spec=grid_spec,
        out_shape=jax.ShapeDtypeStruct((n_blocks * blk, d), F32),
        compiler_params=_cparams(("arbitrary",), VMEM_LIMIT_BYTES),
        name=name,
    )(block_e, slot_tok, n_used, h2, w_in, w_out)


def _combine_kernel(so_ref, x_ref, wt_ref, gate_ref, lng_ref, lnb_ref, y_hbm, o_ref, ybuf, sem, *, tm):
    i = pl.program_id(0)
    nt = pl.num_programs(0)
    slot = i % 2

    def start_gather(ti, sl):
        def body(r, c):
            for k in range(TOP_K):
                _row_copy(y_hbm, so_ref[(ti * tm + r) * TOP_K + k], ybuf.at[k], sl, r, sem.at[k]).start()
            return c
        lax.fori_loop(0, tm, body, 0)

    @pl.when(i == 0)
    def _():
        start_gather(0, 0)

    @pl.when(i + 1 < nt)
    def _():
        start_gather(i + 1, 1 - slot)

    def wait_body(r, c):
        for k in range(TOP_K):
            _row_copy(y_hbm, 0, ybuf.at[k], slot, r, sem.at[k]).wait()
        return c
    lax.fori_loop(0, tm, wait_body, 0)

    wt = wt_ref[...]
    f = wt[:, 0:1] * ybuf[0, slot] + wt[:, 1:2] * ybuf[1, slot]
    o_ref[...] = _ln(ALPHA * x_ref[...] + gate_ref[...] * f) * lng_ref[...] + lnb_ref[...]


def _moe_combine(slot_of, x1, wts, gate, lng, lnb, y, tm, tpb, name):
    n, d = x1.shape
    grid_spec = pltpu.PrefetchScalarGridSpec(
        num_scalar_prefetch=1,
        grid=(n // tm,),
        in_specs=[
            pl.BlockSpec((tm, d), lambda i, so: (i, 0)),
            pl.BlockSpec((tm, TOP_K), lambda i, so: (i, 0)),
            pl.BlockSpec((None, gate.shape[1], d), lambda i, so: (i // tpb, 0, 0)),
            pl.BlockSpec((1, d), lambda i, so: (0, 0)),
            pl.BlockSpec((1, d), lambda i, so: (0, 0)),
            pl.BlockSpec(memory_space=pl.ANY),
        ],
        out_specs=pl.BlockSpec((tm, d), lambda i, so: (i, 0)),
        scratch_shapes=[pltpu.VMEM((TOP_K, 2, tm, d), F32), pltpu.SemaphoreType.DMA((TOP_K, 2))],
    )
    return pl.pallas_call(
        functools.partial(_combine_kernel, tm=tm),
        grid_spec=grid_spec,
        out_shape=jax.ShapeDtypeStruct((n, d), F32),
        compiler_params=_cparams(("arbitrary",)),
        name=name,
    )(slot_of, x1, wts, gate, lng, lnb, y)


def _channel_sublayer(x1, h2, logits, gate, lng, lnb, router_b, w_in, w_out, layer, blk, tm, tpb, tag):
    slot_tok, slot_of, wts, block_e, n_used, n_blocks = _route_tables(logits, router_b, blk)
    y = _moe_ffn(h2, slot_tok, block_e, n_used, n_blocks, w_in, w_out, layer, blk, "moe_ffn_" + tag)
    return _moe_combine(slot_of, x1, wts, gate, lng, lnb, y, tm, tpb, "moe_combine_" + tag)


def _rec_in_kernel(x_ref, sc_ref, sh_ref, w_ref, gg_ref, xbr_ref):
    hb = _modulate(x_ref[...], sc_ref[...], sh_ref[...]).astype(BF16)
    d = gg_ref.shape[1]
    gg_ref[...] = _gelu(jnp.dot(hb, w_ref[:, 0:d], preferred_element_type=F32))
    xbr_ref[...] = jnp.dot(hb, w_ref[:, d:], preferred_element_type=F32)


def _rec_in_prompt(x, sc, sh, w_bf, seq):
    n, d = x.shape
    dr = w_bf.shape[1] // 2
    tm = 256
    tpb = seq // tm
    row = lambda i: (i, 0)
    mod = lambda i: (i // tpb, 0, 0)
    return pl.pallas_call(
        _rec_in_kernel,
        grid=(n // tm,),
        in_specs=[
            pl.BlockSpec((tm, d), row),
            pl.BlockSpec((None, 1, d), mod),
            pl.BlockSpec((None, 1, d), mod),
            pl.BlockSpec(w_bf.shape, lambda i: (0, 0), pipeline_mode=pl.Buffered(1)),
        ],
        out_specs=(pl.BlockSpec((tm, dr), row), pl.BlockSpec((tm, dr), row)),
        out_shape=(jax.ShapeDtypeStruct((n, dr), F32), jax.ShapeDtypeStruct((n, dr), F32)),
        compiler_params=_cparams(("parallel",), VMEM_LIMIT_BYTES),
        name="rec_in_prompt",
    )(x, sc, sh, w_bf)


def _log1p(y):
    w = 1.0 + y
    return jnp.where(w == 1.0, y, jnp.log(w) * (y / jnp.where(w == 1.0, 1.0, w - 1.0)))


def _expm1(x):
    u = jnp.exp(x)
    safe = (u != 1.0) & (u > 0.0)
    return jnp.where(u == 1.0, x, jnp.where(u > 0.0, (u - 1.0) * (x / jnp.where(safe, jnp.log(u), 1.0)), -1.0))


def _softplus(x):
    return jnp.maximum(x, 0.0) + _log1p(jnp.exp(-jnp.abs(x)))


def _rg_terms(xb, z, ba, bx, lam):
    bs = xb.shape[1]
    r = jax.nn.sigmoid(z[:, :bs] + ba)
    gi = jax.nn.sigmoid(z[:, bs:] + bx)
    log_a = -RG_C * r * _softplus(-lam)
    a = jnp.exp(log_a)
    return a, jnp.sqrt(-_expm1(2.0 * log_a)) * (gi * xb)


def _rec_gate_kernel(x_ref, cw_ref, cb_ref, wab_ref, ba_ref, bx_ref, lam_ref, a_ref, bt_ref, carry_ref, *, tpb, tm):
    i = pl.program_id(0)

    @pl.when(i % tpb == 0)
    def _():
        carry_ref[...] = jnp.zeros_like(carry_ref)

    x = x_ref[...]
    carry = carry_ref[...]
    d = x.shape[1]
    row8 = lax.broadcasted_iota(I32, (8, d), 0)
    xc = cb_ref[...] + x * cw_ref[CONV_W - 1:CONV_W, :]
    for k in range(1, CONV_W):
        xr = pltpu.roll(x, k, 0)
        head = jnp.where(row8 < k, pltpu.roll(carry, k, 0), xr[0:8])
        xk = jnp.concatenate([head, xr[8:]], axis=0)
        xc = xc + xk * cw_ref[CONV_W - 1 - k:CONV_W - k, :]
    carry_ref[...] = x[tm - 8:tm]
    bs = d // RNN_BLOCKS
    for nb in range(RNN_BLOCKS):
        sl = slice(nb * bs, (nb + 1) * bs)
        xb = xc[:, sl]
        a, bt = _rg_terms(xb, _dot(xb, wab_ref[nb]), ba_ref[:, sl], bx_ref[:, sl], lam_ref[:, sl])
        a_ref[:, sl] = a
        bt_ref[:, sl] = bt


def _rec_gate_prompt(xbr, cw, cb, wab, ba, bx, lam, seq):
    n, d = xbr.shape
    tm = 256
    tpb = seq // tm
    row = lambda i: (i, 0)
    const2 = lambda i: (0, 0)
    return pl.pallas_call(
        functools.partial(_rec_gate_kernel, tpb=tpb, tm=tm),
        grid=(n // tm,),
        in_specs=[
            pl.BlockSpec((tm, d), row),
            pl.BlockSpec(cw.shape, const2),
            pl.BlockSpec((1, d), const2),
            pl.BlockSpec(wab.shape, lambda i: (0, 0, 0)),
            pl.BlockSpec((1, d), const2),
            pl.BlockSpec((1, d), const2),
            pl.BlockSpec((1, d), const2),
        ],
        out_specs=(pl.BlockSpec((tm, d), row), pl.BlockSpec((tm, d), row)),
        out_shape=(jax.ShapeDtypeStruct((n, d), F32), jax.ShapeDtypeStruct((n, d), F32)),
        scratch_shapes=[pltpu.VMEM((8, d), F32)],
        compiler_params=_cparams(("arbitrary",), VMEM_LIMIT_BYTES),
        name="rec_gate_prompt",
    )(xbr, cw, cb, wab, ba, bx, lam)


def _scan_kernel(a_ref, b_ref, hs_ref, ht_ref, h_ref, *, tt):
    t = pl.program_id(1)

    @pl.when(t == 0)
    def _():
        h_ref[...] = jnp.zeros_like(h_ref)

    def body(j, h):
        h = a_ref[j] * h + b_ref[j]
        hs_ref[j] = h
        return h

    h = lax.fori_loop(0, tt, body, h_ref[...], unroll=8)
    h_ref[...] = h
    ht_ref[...] = h


def _scan_prompt(a4, b4):
    bsz, seq, s8, c8 = a4.shape
    tt = min(512, seq)
    blk = pl.BlockSpec((None, tt, s8, c8), lambda b, t: (b, t, 0, 0))
    return pl.pallas_call(
        functools.partial(_scan_kernel, tt=tt),
        grid=(bsz, seq // tt),
        in_specs=[blk, blk],
        out_specs=(blk, pl.BlockSpec((None, s8, c8), lambda b, t: (b, 0, 0))),
        out_shape=(jax.ShapeDtypeStruct(a4.shape, F32), jax.ShapeDtypeStruct((bsz, s8, c8), F32)),
        scratch_shapes=[pltpu.VMEM((s8, c8), F32)],
        compiler_params=_cparams(("parallel", "arbitrary"), VMEM_LIMIT_BYTES),
        name="rglru_scan_prompt",
    )(a4, b4)


def _rec_step_sample(z, buf0, buf1, buf2, h0, cw, cb, wa, wx, ba, bx, lam):
    d = h0.shape[1]
    gate_br = z[:, :d]
    xbr = z[:, d:]
    xc = cb + buf0 * cw[0:1] + buf1 * cw[1:2] + buf2 * cw[2:3] + xbr * cw[3:4]
    bs = d // RNN_BLOCKS
    a_parts, b_parts = [], []
    for nb in range(RNN_BLOCKS):
        sl = slice(nb * bs, (nb + 1) * bs)
        xb = xc[:, sl]
        zz = jnp.concatenate([_dot(xb, wa[nb]), _dot(xb, wx[nb])], axis=1)
        a, bt = _rg_terms(xb, zz, ba[:, sl], bx[:, sl], lam[:, sl])
        a_parts.append(a)
        b_parts.append(bt)
    h = jnp.concatenate(a_parts, axis=1) * h0 + jnp.concatenate(b_parts, axis=1)
    return _gelu(gate_br) * h, h, xbr


def kernel(x_prompt, x_sample, c_prompt, c_sample, cache_nsa_cmp, cache_nsa_slc, state_nsa_win, state_rglru_conv, state_rglru_h, page_table, ada_w, ada_b, ln_g, ln_b, ab_w_in, ab_w_out, gmlp_ln_g, gmlp_ln_b, gmlp_ws, gmlp_bs, nsa_cmp_pe, nsa_cmp_w1, nsa_cmp_w2, rec_w_in, rec_conv_w, rec_conv_b, rg_wa, rg_ba, rg_wx, rg_bx, rg_lambda, rec_w_out, router_w, router_b, moe_w_in, moe_w_out):
    bp, seq, d = x_prompt.shape
    db = x_sample.shape[0]
    assert x_sample.shape[1] == 1
    n_pool, page = cache_nsa_cmp.shape[:2]
    n_pages = page_table.shape[1]
    past_len = n_pages * page
    assert seq % 256 == 0 and past_len % SLC_LEN == 0 and page % SLC_LEN == 0 and past_len >= WINDOW
    n_p = bp * seq
    d_rnn = rec_conv_b.shape[0]

    mods = _ada_all(jnp.concatenate([c_prompt, c_sample], axis=0), ada_w, ada_b)

    def mod_rows(layer, sub):
        m = mods[layer * 2 + sub]
        parts = [m[:, j * d:(j + 1) * d] for j in range(3)]
        return [p[:bp].reshape(bp, 1, d) for p in parts], [p[bp:] for p in parts]

    lnrow = lambda a: a.reshape(1, d)

    gperm = np.array([(g * HPG + h) * N_BRANCH + br for g in range(N_KV) for br in range(N_BRANCH) for h in range(HPG)])
    w_gate = ab_w_in[:, O_G:][:, gperm]
    w_ab_bf = jnp.concatenate([ab_w_in[:, :O_G], w_gate], axis=1).astype(BF16)
    tril = jnp.tril(jnp.ones((CHUNK, CHUNK), F32))
    wtril = (gmlp_ws * tril).astype(BF16)
    bst = gmlp_bs.T
    glng = gmlp_ln_g.reshape(1, A_WIDTH)
    glnb = gmlp_ln_b.reshape(1, A_WIDTH)
    pe_rows, w1bd, w2bd = _compress_weights(nsa_cmp_pe, nsa_cmp_w1, nsa_cmp_w2)
    rw_t = router_w.T
    xp = x_prompt.reshape(n_p, d)
    xs = x_sample.reshape(db, d)

    (sh_p, sc_p, g_p), (sh_s, sc_s, g_s) = mod_rows(0, 0)
    (sh2_p, sc2_p, g2_p), (sh2_s, sc2_s, g2_s) = mod_rows(0, 1)
    cos_p, sin_p = _rope_tables(np.arange(seq))
    a_p, q_p, kvc_p, kvs_p, kvw_p, kvt_p, gt_p = _ab_proj_prompt(
        xp, sc_p, sh_p, w_ab_bf, cos_p, sin_p, glng, glnb, wtril, bst, seq)
    kcv_p = _compress_prompt(kvc_p.reshape(bp, seq // CMP_STRIDE, CMP_STRIDE * KV_WIDTH), pe_rows, w1bd, w2bd)
    o_p = _nsa_prompt(q_p, gt_p, kcv_p, kvt_p, bp, seq)
    w_out_bf = ab_w_out.astype(BF16)
    x1_p, h2_p, lg_p = _mix_out_prompt(a_p, o_p, w_out_bf, xp, g_p, lnrow(ln_g[0, 0]), lnrow(ln_b[0, 0]),
                                       sc2_p, sh2_p, rw_t, seq, True, "mix_out_l0_prompt")

    z_s = _small_mm(_modulate, [xs, sc_s, sh_s], ab_w_in, O_G, 512, "ab_proj_sample")
    zg_s = _small_mm(_modulate, [xs, sc_s, sh_s], w_gate, N_GATE, N_GATE, "ab_gate_sample")
    cos_s, sin_s = _rope_tables(np.full((db,), past_len))
    ws0 = jnp.repeat(gmlp_ws[:, 0, 0], A_GROUP_DIM).reshape(1, A_WIDTH)
    bs0 = jnp.repeat(gmlp_bs[:, 0], A_GROUP_DIM).reshape(1, A_WIDTH)
    sds = lambda *s: jax.ShapeDtypeStruct(s, F32)
    a_s, v_s, q_s, kvc_s, kvs_s, kvw_s, gt_s = _vmem_call(
        _ab_post_sample,
        (sds(db, A_WIDTH), sds(db, A_WIDTH), sds(db, B_WIDTH), sds(db, KV_WIDTH), sds(db, KV_WIDTH),
         sds(db, KV_WIDTH), sds(db, N_GATE)),
        (z_s, zg_s, cos_s, sin_s, glng, glnb, ws0, bs0), "ab_post_sample")
    q3 = q_s.reshape(db, 1, B_WIDTH)
    page_t = lambda c: jnp.transpose(c, (0, 2, 3, 4, 1)).reshape(n_pool, KV_WIDTH, page)
    oc3, sel_idx = _nsa_sample_cmp(page_table, q3, page_t(cache_nsa_cmp), nsa_cmp_pe, w1bd, w2bd, past_len)
    sel_idx = jnp.transpose(sel_idx, (0, 2, 1))
    o_s = _nsa_sample_attn(page_table, sel_idx, q3, gt_s.reshape(db, 1, N_GATE), oc3,
                           kvs_s.reshape(db, 1, KV_WIDTH), kvw_s.reshape(db, 1, KV_WIDTH),
                           state_nsa_win.reshape(db, -1, KV_WIDTH), page_t(cache_nsa_slc),
                           past_len).reshape(db, B_WIDTH)
    f_s = _small_mm(lambda a, o: jnp.concatenate([a, o], axis=1), [a_s, o_s], ab_w_out, d, 512, "mix_out_l0_sample")
    x1_s, h2_s, lg_s = _vmem_call(
        _post_ln_router, (sds(db, d), sds(db, d), sds(N_EXPERTS, db)),
        (f_s, xs, g_s, lnrow(ln_g[0, 0]), lnrow(ln_b[0, 0]), sc2_s, sh2_s, rw_t), "post_l0_sample")

    x2_p = _channel_sublayer(x1_p, h2_p, lg_p, g2_p, lnrow(ln_g[0, 1]), lnrow(ln_b[0, 1]), router_b,
                             moe_w_in, moe_w_out, 0, 128, 128, seq // 128, "l0_prompt")
    x2_s = _channel_sublayer(x1_s, h2_s, lg_s, g2_s.reshape(1, db, d), lnrow(ln_g[0, 1]), lnrow(ln_b[0, 1]), router_b,
                             moe_w_in, moe_w_out, 0, 32, db, 1, "l0_sample")

    (sh_p, sc_p, g_p), (sh_s, sc_s, g_s) = mod_rows(1, 0)
    (sh2_p, sc2_p, g2_p), (sh2_s, sc2_s, g2_s) = mod_rows(1, 1)
    gg_p, xbr_p = _rec_in_prompt(x2_p, sc_p, sh_p, rec_w_in.astype(BF16), seq)
    wab = jnp.concatenate([rg_wa, rg_wx], axis=2).astype(BF16)
    row_r = lambda a: a.reshape(1, d_rnn)
    a_t, b_t = _rec_gate_prompt(xbr_p, rec_conv_w, row_r(rec_conv_b), wab, row_r(rg_ba), row_r(rg_bx),
                                row_r(rg_lambda), seq)
    hs4, ht = _scan_prompt(a_t.reshape(bp, seq, 8, d_rnn // 8), b_t.reshape(bp, seq, 8, d_rnn // 8))
    x3_p, h4_p, lg_p = _mix_out_prompt(gg_p, hs4.reshape(n_p, d_rnn), rec_w_out.astype(BF16), x2_p, g_p,
                                       lnrow(ln_g[1, 0]), lnrow(ln_b[1, 0]), sc2_p, sh2_p, rw_t, seq, False,
                                       "mix_out_l1_prompt")
    conv_p = xbr_p.reshape(bp, seq, d_rnn)[:, seq - (CONV_W - 1):]
    h_p = ht.reshape(bp, d_rnn)

    zr_s = _small_mm(_modulate, [x2_s, sc_s, sh_s], rec_w_in, 2 * d_rnn, 512, "rec_in_sample")
    y_s, h_s, xbr_s = _vmem_call(
        _rec_step_sample, (sds(db, d_rnn), sds(db, d_rnn), sds(db, d_rnn)),
        (zr_s, state_rglru_conv[:, 0], state_rglru_conv[:, 1], state_rglru_conv[:, 2], state_rglru_h,
         rec_conv_w, row_r(rec_conv_b), rg_wa, rg_wx, row_r(rg_ba), row_r(rg_bx), row_r(rg_lambda)),
        "rec_step_sample")
    f_s = _small_mm(lambda y: y, [y_s], rec_w_out, d, 512, "mix_out_l1_sample")
    x3_s, h4_s, lg_s = _vmem_call(
        _post_ln_router, (sds(db, d), sds(db, d), sds(N_EXPERTS, db)),
        (f_s, x2_s, g_s, lnrow(ln_g[1, 0]), lnrow(ln_b[1, 0]), sc2_s, sh2_s, rw_t), "post_l1_sample")
    conv_s = jnp.concatenate([state_rglru_conv[:, 1:], xbr_s[:, None, :]], axis=1)

    y_p = _channel_sublayer(x3_p, h4_p, lg_p, g2_p, lnrow(ln_g[1, 1]), lnrow(ln_b[1, 1]), router_b,
                            moe_w_in, moe_w_out, 1, 128, 128, seq // 128, "l1_prompt")
    y_s = _channel_sublayer(x3_s, h4_s, lg_s, g2_s.reshape(1, db, d), lnrow(ln_g[1, 1]), lnrow(ln_b[1, 1]), router_b,
                            moe_w_in, moe_w_out, 1, 32, db, 1, "l1_sample")

    kv5 = lambda a, b_: a.reshape(b_, -1, 2, N_KV, HEAD_DIM)
    keep = min(WINDOW, seq)
    win_p = kv5(kvw_p, bp)[:, seq - keep:]
    kw_full = jnp.concatenate([state_nsa_win, kv5(kvw_s, db)], axis=1)
    win_s = kw_full[:, kw_full.shape[1] - min(WINDOW, kw_full.shape[1]):]
    return (y_p.reshape(bp, seq, d), y_s.reshape(db, 1, d), kv5(kvc_p, bp), kv5(kvc_s, db), kv5(kvs_p, bp),
            kv5(kvs_s, db), win_p, win_s, v_s.reshape(db, 1, A_WIDTH), conv_p, conv_s, h_p, h_s)
```

```python
import functools
import math

import numpy as np
import jax
import jax.numpy as jnp
from jax import lax
from jax.experimental import pallas as pl
from jax.experimental.pallas import tpu as pltpu

F32 = jnp.float32
BF16 = jnp.bfloat16
I32 = jnp.int32

A_GROUPS = 8
A_GROUP_DIM = 128
A_WIDTH = A_GROUPS * A_GROUP_DIM
CHUNK = 128
N_HEADS = 16
N_KV = 4
HEAD_DIM = 64
HPG = N_HEADS // N_KV
B_WIDTH = N_HEADS * HEAD_DIM
KV_WIDTH = 2 * N_KV * HEAD_DIM
N_BRANCH = 3
CMP_LEN = 32
CMP_STRIDE = 16
CMP_HID = 2 * HEAD_DIM
SLC_LEN = 64
N_SEL = 16
WINDOW = 512
Q_BLOCK = 128
FORCE_BONUS = 1e4
ROPE_THETA = 10000.0
RNN_BLOCKS = 16
CONV_W = 4
RG_C = 8.0
N_EXPERTS = 16
N_GROUPS = 4
EXPERTS_PER_GROUP = N_EXPERTS // N_GROUPS
TOP_K = 2
DEPTH = 2
ALPHA = (2 * DEPTH) ** 0.25
LN_EPS = 1e-5
NEG = -1e30

O_U = 0
O_V = A_WIDTH
O_Q = 2 * A_WIDTH
O_KV = O_Q + B_WIDTH
O_G = O_KV + 3 * KV_WIDTH
N_GATE = N_BRANCH * N_HEADS

VMEM_LIMIT_BYTES = 56 * 1024 * 1024


def _cparams(sem, vmem=None):
    return pltpu.CompilerParams(dimension_semantics=sem, vmem_limit_bytes=vmem)


def _dot(a, b):
    return jnp.dot(a.astype(BF16), b.astype(BF16), preferred_element_type=F32)


def _dot_nt(a, b):
    dn = (((1,), (1,)), ((), ()))
    return lax.dot_general(a.astype(BF16), b.astype(BF16), dn, preferred_element_type=F32)


def _ln(x):
    mu = jnp.mean(x, -1, keepdims=True)
    xc = x - mu
    var = jnp.mean(xc * xc, -1, keepdims=True)
    return xc * lax.rsqrt(var + LN_EPS)


def _silu(x):
    return x * jax.nn.sigmoid(x)


def _gelu(x):
    return jax.nn.gelu(x, approximate=True)


def _rope(x, cos2, sin2):
    w = x.shape[1]
    rep = w // 128
    cos = jnp.concatenate([cos2] * rep, axis=1) if rep > 1 else cos2
    sin = jnp.concatenate([sin2] * rep, axis=1) if rep > 1 else sin2
    lane = lax.broadcasted_iota(I32, x.shape, 1)
    first = (lane & (HEAD_DIM - 1)) < HEAD_DIM // 2
    rot = jnp.where(first, pltpu.roll(x, w - HEAD_DIM // 2, 1), pltpu.roll(x, HEAD_DIM // 2, 1))
    return x * cos + rot * sin


def _rope_tables(pos):
    half = HEAD_DIM // 2
    inv = ROPE_THETA ** (-np.arange(half, dtype=np.float64) / half)
    ang = np.asarray(pos, np.float64)[:, None] * inv[None, :]
    cos = np.tile(np.cos(ang), (1, 4))
    sin = np.tile(np.concatenate([-np.sin(ang), np.sin(ang)], axis=1), (1, 2))
    return jnp.asarray(cos, F32), jnp.asarray(sin, F32)


def _softmax_rows(s, mask):
    sm = jnp.where(mask, s, NEG)
    m = jnp.max(sm, -1, keepdims=True)
    p = jnp.where(mask, jnp.exp(sm - m), 0.0)
    return p, jnp.sum(p, -1, keepdims=True)


def _ada_kernel(c_ref, w_ref, b_ref, o_ref):
    o_ref[...] = _dot(_silu(c_ref[...]), w_ref[...]) + b_ref[...]


def _ada_all(c_all, ada_w, ada_b):
    r, d = c_all.shape
    n_mod = ada_w.shape[0] * ada_w.shape[1]
    d3 = ada_w.shape[-1]
    tn = 512
    return pl.pallas_call(
        _ada_kernel,
        grid=(n_mod, d3 // tn),
        in_specs=[
            pl.BlockSpec((r, d), lambda l, j: (0, 0)),
            pl.BlockSpec((None, d, tn), lambda l, j: (l, 0, j)),
            pl.BlockSpec((None, 1, tn), lambda l, j: (l, 0, j)),
        ],
        out_specs=pl.BlockSpec((None, r, tn), lambda l, j: (l, 0, j)),
        out_shape=jax.ShapeDtypeStruct((n_mod, r, d3), F32),
        compiler_params=_cparams(("parallel", "parallel")),
        name="ada_mod",
    )(c_all, ada_w.reshape(n_mod, d, d3), ada_b.reshape(n_mod, 1, d3))


def _small_mm_kernel(*refs, n_x, pre):
    xs = [r[...] for r in refs[:n_x]]
    w_ref, o_ref = refs[n_x], refs[n_x + 1]
    o_ref[...] = _dot(pre(*xs), w_ref[...])


def _small_mm(pre, xs, w, n_out, tn, name):
    m = xs[0].shape[0]
    k = w.shape[0]
    in_specs = [pl.BlockSpec(x.shape, lambda j, nd=x.ndim: (0,) * nd) for x in xs]
    in_specs.append(pl.BlockSpec((k, tn), lambda j: (0, j)))
    return pl.pallas_call(
        functools.partial(_small_mm_kernel, n_x=len(xs), pre=pre),
        grid=(n_out // tn,),
        in_specs=in_specs,
        out_specs=pl.BlockSpec((m, tn), lambda j: (0, j)),
        out_shape=jax.ShapeDtypeStruct((m, n_out), F32),
        compiler_params=_cparams(("parallel",)),
        name=name,
    )(*xs, w)


def _vmem_call(fn, out_shapes, args, name):
    n_in = len(args)

    def kern(*refs):
        res = fn(*[r[...] for r in refs[:n_in]])
        for o, v in zip(refs[n_in:], res):
            o[...] = v

    return pl.pallas_call(kern, out_shape=out_shapes, name=name)(*args)


def _modulate(x, sc, sh):
    return x * (1.0 + sc) + sh


def _ab_proj_kernel(x_ref, sc_ref, sh_ref, w_ref, cos_ref, sin_ref, lng_ref, lnb_ref, wtril_ref, bst_ref,
                    a_ref, q_ref, kvc_ref, kvs_ref, kvw_ref, kvt_ref, gt_ref, *, tm):
    hb = _modulate(x_ref[...], sc_ref[...], sh_ref[...]).astype(BF16)

    def proj(lo, hi):
        return jnp.dot(hb, w_ref[:, lo:hi], preferred_element_type=F32)

    cos2 = cos_ref[...]
    sin2 = sin_ref[...]
    zu = proj(O_U, O_V)
    zv = proj(O_V, O_Q)
    for g in range(A_GROUPS):
        sl = slice(g * A_GROUP_DIM, (g + 1) * A_GROUP_DIM)
        vg = _ln(zv[:, sl]) * lng_ref[:, sl] + lnb_ref[:, sl]
        for c in range(tm // CHUNK):
            rs = slice(c * CHUNK, (c + 1) * CHUNK)
            mix = jnp.dot(wtril_ref[g], vg[rs].astype(BF16), preferred_element_type=F32) + bst_ref[:, g:g + 1]
            a_ref[rs, sl] = zu[rs, sl] * mix
    q_ref[...] = _rope(proj(O_Q, O_KV), cos2, sin2)
    half = KV_WIDTH // 2
    for br, ref in enumerate((kvc_ref, kvs_ref, kvw_ref)):
        lo = O_KV + br * KV_WIDTH
        k = _rope(proj(lo, lo + half), cos2, sin2)
        v = proj(lo + half, lo + KV_WIDTH)
        ref[:, 0:half] = k
        ref[:, half:KV_WIDTH] = v
        for g in range(N_KV):
            hs = slice(g * HEAD_DIM, (g + 1) * HEAD_DIM)
            kvt_ref[br, 0, g] = k[:, hs]
            kvt_ref[br, 1, g] = v[:, hs]
    zg = jax.nn.sigmoid(proj(O_G, O_G + N_GATE))
    per_g = N_GATE // N_KV
    for g in range(N_KV):
        gt_ref[g] = zg[:, g * per_g:(g + 1) * per_g]


def _ab_proj_prompt(x, sc, sh, w_bf, cos2, sin2, lng, lnb, wtril, bst, seq):
    n, d = x.shape
    tm = 256
    tpb = seq // tm
    n_in = w_bf.shape[1]
    row = lambda i: (i, 0)
    mod = lambda i: (i // tpb, 0, 0)
    const2 = lambda i: (0, 0)
    pos = lambda i: (i % tpb, 0)
    out_shape = (
        jax.ShapeDtypeStruct((n, A_WIDTH), F32),
        jax.ShapeDtypeStruct((n, B_WIDTH), F32),
        jax.ShapeDtypeStruct((n, KV_WIDTH), F32),
        jax.ShapeDtypeStruct((n, KV_WIDTH), F32),
        jax.ShapeDtypeStruct((n, KV_WIDTH), F32),
        jax.ShapeDtypeStruct((3, 2, N_KV, n, HEAD_DIM), F32),
        jax.ShapeDtypeStruct((N_KV, n, N_GATE // N_KV), F32),
    )
    return pl.pallas_call(
        functools.partial(_ab_proj_kernel, tm=tm),
        grid=(n // tm,),
        in_specs=[
            pl.BlockSpec((tm, d), row),
            pl.BlockSpec((None, 1, d), mod),
            pl.BlockSpec((None, 1, d), mod),
            pl.BlockSpec((d, n_in), const2, pipeline_mode=pl.Buffered(1)),
            pl.BlockSpec((tm, 128), pos),
            pl.BlockSpec((tm, 128), pos),
            pl.BlockSpec((1, A_WIDTH), const2),
            pl.BlockSpec((1, A_WIDTH), const2),
            pl.BlockSpec((A_GROUPS, CHUNK, CHUNK), lambda i: (0, 0, 0)),
            pl.BlockSpec((CHUNK, A_GROUPS), const2),
        ],
        out_specs=(
            pl.BlockSpec((tm, A_WIDTH), row),
            pl.BlockSpec((tm, B_WIDTH), row),
            pl.BlockSpec((tm, KV_WIDTH), row),
            pl.BlockSpec((tm, KV_WIDTH), row),
            pl.BlockSpec((tm, KV_WIDTH), row),
            pl.BlockSpec((3, 2, N_KV, tm, HEAD_DIM), lambda i: (0, 0, 0, i, 0)),
            pl.BlockSpec((N_KV, tm, N_GATE // N_KV), lambda i: (0, i, 0)),
        ),
        out_shape=out_shape,
        compiler_params=_cparams(("parallel",), VMEM_LIMIT_BYTES),
        name="ab_proj_prompt",
    )(x, sc, sh, w_bf, cos2, sin2, lng, lnb, wtril, bst)


def _ab_post_sample(z, zg, cos2, sin2, lng, lnb, ws0, bs0):
    zu = z[:, O_U:O_V]
    zv = z[:, O_V:O_Q]
    vs = []
    for g in range(A_GROUPS):
        sl = slice(g * A_GROUP_DIM, (g + 1) * A_GROUP_DIM)
        vs.append(_ln(zv[:, sl]) * lng[:, sl] + lnb[:, sl])
    v = jnp.concatenate(vs, axis=1)
    a = zu * (ws0 * v + bs0)
    q = _rope(z[:, O_Q:O_KV], cos2, sin2)
    half = KV_WIDTH // 2
    kvs = []
    for br in range(3):
        lo = O_KV + br * KV_WIDTH
        k = _rope(z[:, lo:lo + half], cos2, sin2)
        kvs.append(jnp.concatenate([k, z[:, lo + half:lo + KV_WIDTH]], axis=1))
    return a, v, q, kvs[0], kvs[1], kvs[2], jax.nn.sigmoid(zg)


def _compress_mlp(lhs_lo, lhs_hi, w1_ref, w2_ref, kv, n):
    lo = _dot(lhs_lo, w1_ref[kv, 0])
    hi = _dot(lhs_hi, w1_ref[kv, 1])
    hsum = lo + pltpu.roll(hi, n - 1, 0)
    return _dot(_gelu(hsum), w2_ref[kv])


def _compress_prompt_kernel(x_ref, pe_ref, w1_ref, w2_ref, o_ref, *, n16):
    for kv in range(2):
        for gp in range(N_KV // 2):
            base = kv * (KV_WIDTH // 2) + gp * 128
            xg = jnp.concatenate(
                [x_ref[:, s * KV_WIDTH + base:s * KV_WIDTH + base + 128] for s in range(CMP_STRIDE)], axis=1)
            r = _compress_mlp(xg + pe_ref[kv, 0], xg + pe_ref[kv, 1], w1_ref, w2_ref, kv, n16)
            o_ref[kv, 2 * gp] = r[:, 0:HEAD_DIM]
            o_ref[kv, 2 * gp + 1] = r[:, HEAD_DIM:2 * HEAD_DIM]


def _compress_prompt(kvc3, pe_rows, w1bd, w2bd):
    b, n16, wid = kvc3.shape
    return pl.pallas_call(
        functools.partial(_compress_prompt_kernel, n16=n16),
        grid=(b,),
        in_specs=[
            pl.BlockSpec((None, n16, wid), lambda i: (i, 0, 0)),
            pl.BlockSpec(pe_rows.shape, lambda i: (0, 0, 0, 0)),
            pl.BlockSpec(w1bd.shape, lambda i: (0, 0, 0, 0)),
            pl.BlockSpec(w2bd.shape, lambda i: (0, 0, 0)),
        ],
        out_specs=pl.BlockSpec((None, 2, N_KV, n16, HEAD_DIM), lambda i: (i, 0, 0, 0, 0)),
        out_shape=jax.ShapeDtypeStruct((b, 2, N_KV, n16, HEAD_DIM), F32),
        compiler_params=_cparams(("parallel",), VMEM_LIMIT_BYTES),
        name="nsa_compress_prompt",
    )(kvc3, pe_rows, w1bd, w2bd)


def _compress_weights(pe, w1, w2):
    eye2 = jnp.eye(2, dtype=F32)
    w1r = w1.reshape(2, 2, CMP_STRIDE, HEAD_DIM, CMP_HID)
    w1bd = jnp.einsum('khsdc,gG->khsgdGc', w1r, eye2).reshape(2, 2, CMP_STRIDE * 128, 2 * CMP_HID).astype(BF16)
    pe_rows = jnp.broadcast_to(pe.reshape(2, 2, CMP_STRIDE, 1, HEAD_DIM), (2, 2, CMP_STRIDE, 2, HEAD_DIM))
    pe_rows = pe_rows.reshape(2, 2, 1, CMP_STRIDE * 128)
    w2bd = jnp.einsum('kcd,gG->kgcGd', w2, eye2).reshape(2, 2 * CMP_HID, 2 * HEAD_DIM).astype(BF16)
    return pe_rows, w1bd, w2bd


def _cmp_slc_map(n_rows, n_cmp, n_slc, n_cols):
    cs = np.arange(n_rows)[:, None] * CMP_STRIDE
    ss = np.arange(n_cols)[None, :] * SLC_LEN
    m = (cs < ss + SLC_LEN) & (cs + CMP_LEN > ss)
    m &= (np.arange(n_rows)[:, None] < n_cmp) & (np.arange(n_cols)[None, :] < n_slc)
    return m.astype(np.float32)


def _nsa_prompt_kernel(q_ref, gt_ref, kc_ref, vc_ref, ks_ref, vs_ref, kw_ref, vw_ref, mapt_ref, e_ref,
                       o_ref, selb_ref, *, n_cmp, n_slc, n_sel, kt_len):
    i = pl.program_id(2)
    q0 = i * Q_BLOCK
    rows = HPG * Q_BLOCK
    tile4 = lambda x: jnp.concatenate([x] * HPG, axis=0)
    qb = q_ref[...]
    q4 = jnp.concatenate([qb[:, h * HEAD_DIM:(h + 1) * HEAD_DIM] for h in range(HPG)], axis=0)
    q4 = (q4 * HEAD_DIM ** -0.5).astype(BF16)
    t_pos = q0 + lax.broadcasted_iota(I32, (Q_BLOCK, 1), 0)

    n16 = kc_ref.shape[0]
    n_idx = lax.broadcasted_iota(I32, (1, n16), 1)
    bias_c = jnp.where((n_idx * CMP_STRIDE + CMP_LEN - 1 <= t_pos) & (n_idx < n_cmp), 0.0, NEG)
    s = _dot_nt(q4, kc_ref[...]) + tile4(bias_c)
    p = jnp.exp(s - jnp.max(s, -1, keepdims=True))
    l = jnp.sum(p, -1, keepdims=True)
    pn = p * jnp.where((tile4(t_pos) >= CMP_LEN - 1) & (n_cmp > 0), 1.0 / l, 0.0)
    o_c = _dot(pn, vc_ref[...])
    psum = pn[0:Q_BLOCK]
    for h in range(1, HPG):
        psum = psum + pn[h * Q_BLOCK:(h + 1) * Q_BLOCK]

    imp_t = _dot_nt(mapt_ref[...], psum)
    blk_t = lax.broadcasted_iota(I32, (n_slc, Q_BLOCK), 0)
    cur_t = (q0 + lax.broadcasted_iota(I32, (n_slc, Q_BLOCK), 1)) // SLC_LEN
    forced = (blk_t == 0) | (blk_t == cur_t) | (blk_t == cur_t - 1)
    score = jnp.where(blk_t <= cur_t, imp_t + jnp.where(forced, FORCE_BONUS, 0.0), -jnp.inf)
    rank = jnp.zeros((n_slc, Q_BLOCK), I32)
    for j in range(n_slc):
        r = score[j:j + 1, :]
        rank = rank + ((r > score) | ((r == score) & (blk_t > j))).astype(I32)
    sel_t = ((rank < n_sel) & (score > -jnp.inf)).astype(F32)
    selb_ref[...] = (_dot(sel_t.T, e_ref[...]) - 1.0) * (-NEG)

    def sel_tile(kt, carry, causal):
        m, l, acc = carry
        k0 = pl.multiple_of(kt * kt_len, kt_len)
        bias = selb_ref[:, pl.ds(k0, kt_len)]
        if causal:
            kp = k0 + lax.broadcasted_iota(I32, (1, kt_len), 1)
            bias = bias + jnp.where(kp <= t_pos, 0.0, NEG)
        s = _dot_nt(q4, ks_ref[pl.ds(k0, kt_len), :]) + tile4(bias)
        m_new = jnp.maximum(m, jnp.max(s, -1, keepdims=True))
        alpha = jnp.exp(m - m_new)
        p = jnp.exp(s - m_new)
        l = alpha * l + jnp.sum(p, -1, keepdims=True)
        acc = alpha * acc + _dot(p, vs_ref[pl.ds(k0, kt_len), :])
        return m_new, l, acc

    last_kt = (q0 + Q_BLOCK - 1) // kt_len
    init = (jnp.full((rows, 1), NEG, F32), jnp.zeros((rows, 1), F32), jnp.zeros((rows, HEAD_DIM), F32))
    carry = lax.fori_loop(0, last_kt, functools.partial(sel_tile, causal=False), init)
    _, l, acc = sel_tile(last_kt, carry, True)
    o_s = acc * (1.0 / l)

    n_wt = WINDOW // Q_BLOCK + 1
    c_idx = lax.broadcasted_iota(I32, (Q_BLOCK, Q_BLOCK), 1)
    r_idx = lax.broadcasted_iota(I32, (Q_BLOCK, Q_BLOCK), 0)
    s_parts, v_parts = [], []
    for j in range(n_wt):
        ks_j = q0 - WINDOW + j * Q_BLOCK
        ld = pl.multiple_of(jnp.maximum(ks_j, 0), Q_BLOCK)
        off = jnp.where(ks_j >= 0, 0.0, NEG)
        sj = _dot_nt(q4, kw_ref[pl.ds(ld, Q_BLOCK), :])
        if j == 0:
            sj = sj + tile4(jnp.where(c_idx > r_idx, 0.0, NEG) + off)
        elif j == n_wt - 1:
            sj = sj + tile4(jnp.where(c_idx <= r_idx, 0.0, NEG))
        else:
            sj = sj + off
        s_parts.append(sj)
        v_parts.append(vw_ref[pl.ds(ld, Q_BLOCK), :])
    s = jnp.concatenate(s_parts, axis=1)
    p = jnp.exp(s - jnp.max(s, -1, keepdims=True))
    o_w = _dot(p, jnp.concatenate(v_parts, axis=0)) * (1.0 / jnp.sum(p, -1, keepdims=True))

    gt = gt_ref[...]

    def gcol(br):
        return jnp.concatenate([gt[:, br * HPG + h:br * HPG + h + 1] for h in range(HPG)], axis=0)

    o = gcol(0) * o_c + gcol(1) * o_s + gcol(2) * o_w
    o_ref[...] = jnp.concatenate([o[h * Q_BLOCK:(h + 1) * Q_BLOCK] for h in range(HPG)], axis=1)


def _nsa_prompt(q, gates, kcv, kvt, batch, seq):
    n = q.shape[0]
    nq = seq // Q_BLOCK
    n16 = kcv.shape[3]
    n_cmp = n16 - 1
    n_slc = -(-seq // SLC_LEN)
    n_sel = min(N_SEL, n_slc)
    kt_len = min(512, seq)
    mapt = jnp.asarray(_cmp_slc_map(n16, n_cmp, n_slc, n_slc).T)
    expand = jnp.asarray((np.arange(n_slc)[:, None] == np.arange(seq)[None, :] // SLC_LEN).astype(np.float32), BF16)
    per_g = N_GATE // N_KV
    qrow = lambda b, g, i: (b * nq + i, g)
    kvspec = lambda br, kv: pl.BlockSpec((None, None, None, seq, HEAD_DIM), lambda b, g, i: (br, kv, g, b, 0))
    return pl.pallas_call(
        functools.partial(_nsa_prompt_kernel, n_cmp=n_cmp, n_slc=n_slc, n_sel=n_sel, kt_len=kt_len),
        grid=(batch, N_KV, nq),
        in_specs=[
            pl.BlockSpec((Q_BLOCK, HPG * HEAD_DIM), qrow),
            pl.BlockSpec((None, Q_BLOCK, per_g), lambda b, g, i: (g, b * nq + i, 0)),
            pl.BlockSpec((None, None, None, n16, HEAD_DIM), lambda b, g, i: (b, 0, g, 0, 0)),
            pl.BlockSpec((None, None, None, n16, HEAD_DIM), lambda b, g, i: (b, 1, g, 0, 0)),
            kvspec(1, 0), kvspec(1, 1), kvspec(2, 0), kvspec(2, 1),
            pl.BlockSpec(mapt.shape, lambda b, g, i: (0, 0)),
            pl.BlockSpec(expand.shape, lambda b, g, i: (0, 0)),
        ],
        out_specs=pl.BlockSpec((Q_BLOCK, HPG * HEAD_DIM), qrow),
        out_shape=jax.ShapeDtypeStruct((n, B_WIDTH), F32),
        scratch_shapes=[pltpu.VMEM((Q_BLOCK, seq), F32)],
        compiler_params=_cparams(("parallel", "parallel", "arbitrary"), VMEM_LIMIT_BYTES),
        name="nsa_attn_prompt",
    )(q, gates, kcv, kcv, kvt, kvt, kvt, kvt, mapt, expand)


def _page_copy(cache_hbm, pt_ref, xbuf, sem, b, slot, p, n_pages):
    return pltpu.make_async_copy(cache_hbm.at[pt_ref[b * n_pages + p]], xbuf.at[slot, p], sem.at[slot])


def _nsa_sample_cmp_kernel(pt_ref, q_ref, cache_hbm, pe_ref, w1_ref, w2_ref, map_ref,
                           oc_ref, idx_ref, xbuf, xrow_a, xrow_b, lhs_a, lhs_b, sem, *, n_pages, n_cmp, n_slc, n_sel, pos):
    b = pl.program_id(0)
    nb = pl.num_programs(0)
    slot = b % 2
    n16 = lhs_a.shape[0]
    cpp = n16 // n_pages

    def start_all(bb, sl):
        def body(p, c):
            _page_copy(cache_hbm, pt_ref, xbuf, sem, bb, sl, p, n_pages).start()
            return c
        lax.fori_loop(0, n_pages, body, 0)

    @pl.when(b == 0)
    def _():
        start_all(0, 0)

    @pl.when(b + 1 < nb)
    def _():
        start_all(b + 1, 1 - slot)

    def wait_body(p, c):
        _page_copy(cache_hbm, pt_ref, xbuf, sem, b, slot, p, n_pages).wait()
        return c
    lax.fori_loop(0, n_pages, wait_body, 0)

    kc, vc = [], []
    wkv = N_KV * HEAD_DIM
    for kv, dst in ((0, kc), (1, vc)):
        def fill(p, c):
            r0 = pl.multiple_of(p * cpp, cpp)
            for gp, (xrow, lhs) in enumerate(((xrow_a, lhs_a), (xrow_b, lhs_b))):
                xrow[...] = xbuf[slot, p, kv * wkv + gp * 128:kv * wkv + (gp + 1) * 128, :].T
                for s in range(CMP_STRIDE):
                    lhs[pl.ds(r0, cpp), s * 128:(s + 1) * 128] = xrow[pl.ds(s, cpp, stride=CMP_STRIDE), :]
            return c
        lax.fori_loop(0, n_pages, fill, 0)
        for lhs in (lhs_a, lhs_b):
            xg = lhs[...]
            r = _compress_mlp(xg + pe_ref[kv, 0], xg + pe_ref[kv, 1], w1_ref, w2_ref, kv, n16)
            dst.append(r[:, 0:HEAD_DIM])
            dst.append(r[:, HEAD_DIM:2 * HEAD_DIM])

    qrow = q_ref[...] * HEAD_DIM ** -0.5
    n_idx = lax.broadcasted_iota(I32, (1, n16), 1)
    valid = (n_idx * CMP_STRIDE + CMP_LEN - 1 <= pos) & (n_idx < n_cmp)
    head_row = lax.broadcasted_iota(I32, (8, 1), 0) < HPG
    ncol = map_ref.shape[1]
    blk_r = lax.broadcasted_iota(I32, (1, ncol), 1)
    blk_c = lax.broadcasted_iota(I32, (ncol, 1), 0)
    cur = pos // SLC_LEN
    forced = (blk_r == 0) | (blk_r == cur) | (blk_r == cur - 1)
    oc_parts = []
    for g in range(N_KV):
        q8 = jnp.concatenate(
            [qrow[:, (g * HPG + h) * HEAD_DIM:(g * HPG + h + 1) * HEAD_DIM] for h in range(HPG)]
            + [jnp.zeros((8 - HPG, HEAD_DIM), F32)], axis=0)
        s = _dot_nt(q8, kc[g])
        p, l = _softmax_rows(s, valid)
        pn = jnp.where(head_row, p / jnp.maximum(l, 1e-30), 0.0)
        o8 = _dot(pn, vc[g])
        oc_parts += [o8[h:h + 1, :] for h in range(HPG)]
        psum = jnp.broadcast_to(jnp.sum(pn, axis=0, keepdims=True), pn.shape)
        imp = _dot(psum, map_ref[...])[0:1, :]
        score = jnp.where((blk_r <= cur) & (blk_r < n_slc), imp + jnp.where(forced, FORCE_BONUS, 0.0), -jnp.inf)
        score_c = jnp.broadcast_to(score, (8, ncol)).T[:, 0:1]
        beats = (score_c > score) | ((score_c == score) & (blk_c < blk_r))
        rank = jnp.sum(beats.astype(F32), axis=0, keepdims=True)
        r_iota = lax.broadcasted_iota(I32, (n_sel, ncol), 0).astype(F32)
        hit = (rank == r_iota) & (score > -jnp.inf)
        idx = jnp.sum(jnp.where(hit, blk_r.astype(F32), 0.0), axis=1, keepdims=True)
        idx_ref[:, g:g + 1] = idx.astype(I32)
    oc_ref[...] = jnp.concatenate(oc_parts, axis=1)


def _nsa_sample_cmp(page_table, q3, cache_t, pe_rows, w1bd, w2bd, past_len):
    db, n_pages = page_table.shape
    page = cache_t.shape[2]
    cpp = page // CMP_STRIDE
    n16 = n_pages * cpp
    n_cmp = (past_len + 1) // CMP_STRIDE - 1
    n_slc = -(-(past_len + 1) // SLC_LEN)
    n_sel = min(N_SEL, n_slc)
    ncol = -(-n_slc // 128) * 128
    cmap = jnp.asarray(_cmp_slc_map(n16, n_cmp, n_slc, ncol))
    grid_spec = pltpu.PrefetchScalarGridSpec(
        num_scalar_prefetch=1,
        grid=(db,),
        in_specs=[
            pl.BlockSpec((None, 1, B_WIDTH), lambda b, pt: (b, 0, 0)),
            pl.BlockSpec(memory_space=pl.ANY),
            pl.BlockSpec(pe_rows.shape, lambda b, pt: (0, 0, 0, 0)),
            pl.BlockSpec(w1bd.shape, lambda b, pt: (0, 0, 0, 0)),
            pl.BlockSpec(w2bd.shape, lambda b, pt: (0, 0, 0)),
            pl.BlockSpec(cmap.shape, lambda b, pt: (0, 0)),
        ],
        out_specs=(
            pl.BlockSpec((None, 1, B_WIDTH), lambda b, pt: (b, 0, 0)),
            pl.BlockSpec((None, n_sel, N_KV), lambda b, pt: (b, 0, 0)),
        ),
        scratch_shapes=[pltpu.VMEM((2, n_pages) + cache_t.shape[1:], F32),
                        pltpu.VMEM((page, 128), F32), pltpu.VMEM((page, 128), F32),
                        pltpu.VMEM((n16, CMP_STRIDE * 128), F32), pltpu.VMEM((n16, CMP_STRIDE * 128), F32),
                        pltpu.SemaphoreType.DMA((2,))],
    )
    return pl.pallas_call(
        functools.partial(_nsa_sample_cmp_kernel, n_pages=n_pages, n_cmp=n_cmp, n_slc=n_slc, n_sel=n_sel, pos=past_len),
        grid_spec=grid_spec,
        out_shape=(jax.ShapeDtypeStruct((db, 1, B_WIDTH), F32), jax.ShapeDtypeStruct((db, n_sel, N_KV), I32)),
        compiler_params=_cparams(("arbitrary",), VMEM_LIMIT_BYTES),
        name="nsa_cmp_sample",
    )(page_table.reshape(-1), q3, cache_t, pe_rows, w1bd, w2bd, cmap)


def _sel_copy(cache_hbm, pt_ref, idx_ref, kbuf, sem, b, slot, j2, n_pages, n_sel, last_real):
    j = j2 // 2
    kv = j2 % 2
    g = j // n_sel
    blk = jnp.minimum(idx_ref[b * (N_KV * n_sel) + j], last_real)
    per_page = cache_hbm.shape[2] // SLC_LEN
    page = pt_ref[b * n_pages + blk // per_page]
    row0 = pl.multiple_of((kv * N_KV + g) * HEAD_DIM, HEAD_DIM)
    return pltpu.make_async_copy(cache_hbm.at[page, pl.ds(row0, HEAD_DIM), :], kbuf.at[slot, j2], sem.at[slot])


def _nsa_sample_attn_kernel(pt_ref, idx_ref, q_ref, gt_ref, oc_ref, ks_new_ref, kw_new_ref, win_ref, cache_hbm,
                            o_ref, kbuf, sem, *, n_pages, n_sel, pos):
    b = pl.program_id(0)
    nb = pl.num_programs(0)
    slot = b % 2
    n_copies = 2 * N_KV * n_sel
    page = kbuf.shape[3]
    per_page = page // SLC_LEN
    new_blk = pos // SLC_LEN
    last_real = new_blk - 1

    def start_all(bb, sl):
        def body(j, c):
            _sel_copy(cache_hbm, pt_ref, idx_ref, kbuf, sem, bb, sl, j, n_pages, n_sel, last_real).start()
            return c
        lax.fori_loop(0, n_copies, body, 0)

    @pl.when(b == 0)
    def _():
        start_all(0, 0)

    @pl.when(b + 1 < nb)
    def _():
        start_all(b + 1, 1 - slot)

    def wait_body(j, c):
        _sel_copy(cache_hbm, pt_ref, idx_ref, kbuf, sem, b, slot, j, n_pages, n_sel, last_real).wait()
        return c
    lax.fori_loop(0, n_copies, wait_body, 0)

    qrow = q_ref[...] * HEAD_DIM ** -0.5
    gt = gt_ref[...]
    oc = oc_ref[...]
    ks_new = ks_new_ref[...]
    kw_new = kw_new_ref[...]
    half = KV_WIDTH // 2
    wlen = win_ref.shape[0]
    kp_w = pos - wlen + lax.broadcasted_iota(I32, (1, wlen), 1)
    mask_w = (kp_w >= 0) & (kp_w <= pos) & (kp_w > pos - WINDOW)
    out_parts = []
    for g in range(N_KV):
        q8 = jnp.concatenate(
            [qrow[:, (g * HPG + h) * HEAD_DIM:(g * HPG + h + 1) * HEAD_DIM] for h in range(HPG)]
            + [jnp.zeros((8 - HPG, HEAD_DIM), F32)], axis=0)
        ksl = slice(g * HEAD_DIM, (g + 1) * HEAD_DIM)
        vsl = slice(half + g * HEAD_DIM, half + (g + 1) * HEAD_DIM)

        def attend(s, mask, pv, k_new, v_new, has_new):
            rb = lambda t: t.astype(BF16).astype(F32)
            s_new = jnp.sum(rb(q8) * rb(k_new), axis=-1, keepdims=True)
            sm = jnp.where(mask, s, NEG)
            m = jnp.maximum(jnp.max(sm, -1, keepdims=True), jnp.where(has_new, s_new, NEG))
            p = jnp.where(mask, jnp.exp(sm - m), 0.0)
            p_new = jnp.where(has_new, jnp.exp(s_new - m), 0.0)
            l = jnp.maximum(jnp.sum(p, -1, keepdims=True) + p_new, 1e-30)
            return pv(p / l) + rb(p_new / l) * rb(v_new)

        kt_sel = jnp.concatenate([kbuf[slot, (g * n_sel + r) * 2] for r in range(n_sel)], axis=1)
        vt_sel = jnp.concatenate([kbuf[slot, (g * n_sel + r) * 2 + 1] for r in range(n_sel)], axis=1)
        blk_ids = [idx_ref[b * (N_KV * n_sel) + g * n_sel + r] for r in range(n_sel)]
        row_of = lambda vals: jnp.concatenate([jnp.full((1, page), v, I32) for v in vals], axis=1)
        lane_blk = (lax.broadcasted_iota(I32, (1, n_sel * page), 1) & (page - 1)) // SLC_LEN
        picked = (row_of(blk_ids) <= last_real) & (lane_blk == row_of([bid % per_page for bid in blk_ids]))
        has_new = functools.reduce(jnp.logical_or, [bid == new_blk for bid in blk_ids])
        o_s = attend(_dot(q8, kt_sel), picked, lambda p: _dot_nt(p, vt_sel), ks_new[:, ksl], ks_new[:, vsl], has_new)
        o_w = attend(_dot_nt(q8, win_ref[:, ksl]), mask_w, lambda p: _dot(p, win_ref[:, vsl]),
                     kw_new[:, ksl], kw_new[:, vsl], True)
        for h in range(HPG):
            hh = g * HPG + h
            c0 = g * (N_BRANCH * HPG)
            g0 = gt[:, c0 + h:c0 + h + 1]
            g1 = gt[:, c0 + HPG + h:c0 + HPG + h + 1]
            g2 = gt[:, c0 + 2 * HPG + h:c0 + 2 * HPG + h + 1]
            out_parts.append(g0 * oc[:, hh * HEAD_DIM:(hh + 1) * HEAD_DIM] + g1 * o_s[h:h + 1, :] + g2 * o_w[h:h + 1, :])
    o_ref[...] = jnp.concatenate(out_parts, axis=1)


def _nsa_sample_attn(page_table, sel_idx, q3, gates3, oc3, ks_new3, kw_new3, win_state, cache_s, past_len):
    db, n_pages = page_table.shape
    n_sel = sel_idx.shape[-1]
    wlen = win_state.shape[1]
    row = lambda wdt: pl.BlockSpec((None, 1, wdt), lambda b, pt, ix: (b, 0, 0))
    grid_spec = pltpu.PrefetchScalarGridSpec(
        num_scalar_prefetch=2,
        grid=(db,),
        in_specs=[
            row(B_WIDTH), row(N_GATE), row(B_WIDTH), row(KV_WIDTH), row(KV_WIDTH),
            pl.BlockSpec((None, wlen, KV_WIDTH), lambda b, pt, ix: (b, 0, 0)),
            pl.BlockSpec(memory_space=pl.ANY),
        ],
        out_specs=row(B_WIDTH),
        scratch_shapes=[pltpu.VMEM((2, 2 * N_KV * n_sel, HEAD_DIM, cache_s.shape[2]), F32),
                        pltpu.SemaphoreType.DMA((2,))],
    )
    return pl.pallas_call(
        functools.partial(_nsa_sample_attn_kernel, n_pages=n_pages, n_sel=n_sel, pos=past_len),
        grid_spec=grid_spec,
        out_shape=jax.ShapeDtypeStruct((db, 1, B_WIDTH), F32),
        compiler_params=_cparams(("arbitrary",), VMEM_LIMIT_BYTES),
        name="nsa_attn_sample",
    )(page_table.reshape(-1), sel_idx.reshape(-1), q3, gates3, oc3, ks_new3, kw_new3, win_state, cache_s)


def _post_ln_router(acc, x, gate, lng, lnb, sc2, sh2, rw_t):
    x1 = _ln(ALPHA * x + gate * acc) * lng + lnb
    h2 = _modulate(x1, sc2, sh2)
    return x1, h2, _dot_nt(rw_t, h2)


def _mix_out_kernel(l0_ref, l1_ref, w_ref, x_ref, gate_ref, lng_ref, lnb_ref, sc2_ref, sh2_ref, rw_ref,
                    x1_ref, h2_ref, lg_ref, *, concat):
    if concat:
        k0 = l0_ref.shape[1]
        acc = _dot(l0_ref[...], w_ref[0:k0, :]) + _dot(l1_ref[...], w_ref[k0:, :])
    else:
        acc = _dot(l0_ref[...] * l1_ref[...], w_ref[...])
    x1, h2, lg = _post_ln_router(acc, x_ref[...], gate_ref[...], lng_ref[...], lnb_ref[...],
                                 sc2_ref[...], sh2_ref[...], rw_ref[...])
    x1_ref[...] = x1
    h2_ref[...] = h2
    lg_ref[...] = lg


def _mix_out_prompt(l0, l1, w_bf, x, gate, lng, lnb, sc2, sh2, rw, seq, concat, name):
    n, d = x.shape
    tm = 256
    tpb = seq // tm
    row = lambda i: (i, 0)
    mod = lambda i: (i // tpb, 0, 0)
    const2 = lambda i: (0, 0)
    return pl.pallas_call(
        functools.partial(_mix_out_kernel, concat=concat),
        grid=(n // tm,),
        in_specs=[
            pl.BlockSpec((tm, l0.shape[1]), row),
            pl.BlockSpec((tm, l1.shape[1]), row),
            pl.BlockSpec(w_bf.shape, const2),
            pl.BlockSpec((tm, d), row),
            pl.BlockSpec((None, 1, d), mod),
            pl.BlockSpec((1, d), const2),
            pl.BlockSpec((1, d), const2),
            pl.BlockSpec((None, 1, d), mod),
            pl.BlockSpec((None, 1, d), mod),
            pl.BlockSpec(rw.shape, const2),
        ],
        out_specs=(pl.BlockSpec((tm, d), row), pl.BlockSpec((tm, d), row),
                   pl.BlockSpec((N_EXPERTS, tm), lambda i: (0, i))),
        out_shape=(jax.ShapeDtypeStruct((n, d), F32), jax.ShapeDtypeStruct((n, d), F32),
                   jax.ShapeDtypeStruct((N_EXPERTS, n), F32)),
        compiler_params=_cparams(("parallel",), VMEM_LIMIT_BYTES),
        name=name,
    )(l0, l1, w_bf, x, gate, lng, lnb, sc2, sh2, rw)


def _top2_route(lg, rb):
    s = jax.nn.sigmoid(lg)
    sb = s + rb
    rows = [sb[e:e + 1, :] for e in range(N_EXPERTS)]
    gs = []
    for g in range(N_GROUPS):
        v = rows[g * EXPERTS_PER_GROUP:(g + 1) * EXPERTS_PER_GROUP]
        pair = [v[i] + v[j] for i in range(EXPERTS_PER_GROUP) for j in range(i + 1, EXPERTS_PER_GROUP)]
        gs.append(functools.reduce(jnp.maximum, pair))
    best, gi = gs[0], jnp.zeros(gs[0].shape, I32)
    for g in range(1, N_GROUPS):
        better = gs[g] > best
        gi = jnp.where(better, g, gi)
        best = jnp.where(better, gs[g], best)
    cand = [jnp.where(gi == e // EXPERTS_PER_GROUP, rows[e], -jnp.inf) for e in range(N_EXPERTS)]
    ids = []
    for k in range(TOP_K):
        vk = jnp.full(cand[0].shape, -jnp.inf, F32)
        ik = jnp.zeros(cand[0].shape, I32)
        for e in range(N_EXPERTS):
            c = cand[e]
            for prev in ids:
                c = jnp.where(prev == e, -jnp.inf, c)
            better = c > vk
            ik = jnp.where(better, e, ik)
            vk = jnp.where(better, c, vk)
        ids.append(ik)
    ws = [functools.reduce(jnp.add, [jnp.where(ik == e, s[e:e + 1, :], 0.0) for e in range(N_EXPERTS)]) for ik in ids]
    tot = functools.reduce(jnp.add, ws)
    return ids, [w / tot for w in ws]


def _route_kernel(lg_ref, rb_ref, w_ref, dest_ref, be_ref, nu_ref, tot_ref, run_ref, ps_ref, *, blk):
    ph = pl.program_id(0)
    i = pl.program_id(1)
    tm = lg_ref.shape[1]
    ids, ws = _top2_route(lg_ref[...], rb_ref[...])
    e_iota = lax.broadcasted_iota(I32, (N_EXPERTS, tm), 0)
    oh = [(e_iota == ik).astype(F32) for ik in ids]
    ohsum = functools.reduce(jnp.add, oh)
    tile_cnt = jnp.sum(ohsum, axis=1, keepdims=True)

    @pl.when((ph == 0) & (i == 0))
    def _():
        tot_ref[...] = jnp.zeros_like(tot_ref)

    @pl.when(ph == 0)
    def _():
        tot_ref[...] = tot_ref[...] + tile_cnt

    @pl.when((ph == 1) & (i == 0))
    def _():
        cnt = tot_ref[...]
        padded = jnp.floor((cnt + (blk - 1)) * (1.0 / blk)) * blk
        sub = lax.broadcasted_iota(I32, cnt.shape, 0)
        start = jnp.zeros_like(cnt)
        for e in range(N_EXPERTS):
            start = start + jnp.where(sub > e, padded[e:e + 1, :], 0.0)
        ps_ref[...] = start
        run_ref[...] = jnp.zeros_like(run_ref)
        pad_end = start[:, 0:1] + padded[:, 0:1]
        blk_lo = (lax.broadcasted_iota(I32, (N_EXPERTS, be_ref.shape[1]), 1) * blk).astype(F32)
        n_le = jnp.sum((pad_end <= blk_lo).astype(F32), axis=0, keepdims=True)
        be_ref[...] = jnp.minimum(n_le, N_EXPERTS - 1.0).astype(I32)
        nu_ref[...] = (jnp.max(pad_end, axis=0, keepdims=True) * (1.0 / blk) + jnp.zeros(nu_ref.shape, F32)).astype(I32)

    @pl.when(ph == 1)
    def _():
        t_r = lax.broadcasted_iota(I32, (tm, tm), 0)
        t_c = lax.broadcasted_iota(I32, (tm, tm), 1)
        before = _dot(ohsum, (t_r < t_c).astype(F32))
        base = before + run_ref[:, 0:1] + ps_ref[:, 0:1]
        for k in range(TOP_K):
            w_ref[k:k + 1, :] = ws[k]
            dest_ref[k:k + 1, :] = jnp.sum(oh[k] * base, axis=0, keepdims=True).astype(I32)
        run_ref[...] = run_ref[...] + tile_cnt


def _route_tables(logits_t, router_b, blk):
    n = logits_t.shape[1]
    tm = min(512, n)
    a = n * TOP_K
    n_blocks = -(-a // blk) + N_EXPERTS
    nb_pad = -(-n_blocks // 128) * 128
    tok_blk = lambda p, i: (0, i * p)
    const = lambda p, i: (0, 0)
    w, dest, block_e, n_used = pl.pallas_call(
        functools.partial(_route_kernel, blk=blk),
        grid=(2, n // tm),
        in_specs=[pl.BlockSpec((N_EXPERTS, tm), lambda p, i: (0, i)), pl.BlockSpec((N_EXPERTS, 1), const)],
        out_specs=(pl.BlockSpec((TOP_K, tm), tok_blk), pl.BlockSpec((TOP_K, tm), tok_blk),
                   pl.BlockSpec((1, nb_pad), const), pl.BlockSpec((1, 128), const)),
        out_shape=(jax.ShapeDtypeStruct((TOP_K, n), F32), jax.ShapeDtypeStruct((TOP_K, n), I32),
                   jax.ShapeDtypeStruct((1, nb_pad), I32), jax.ShapeDtypeStruct((1, 128), I32)),
        scratch_shapes=[pltpu.VMEM((N_EXPERTS, 128), F32)] * 3,
        compiler_params=_cparams(("arbitrary", "arbitrary")),
        name="moe_route",
    )(logits_t, router_b.reshape(N_EXPERTS, 1).astype(F32))
    slot_of = dest.T.reshape(-1)
    tok = jnp.repeat(jnp.arange(n, dtype=I32), TOP_K)
    slot_tok = jnp.zeros((n_blocks * blk,), I32).at[slot_of].set(tok)
    return slot_tok, slot_of, w.T, block_e[0, :n_blocks], n_used[0, :1], n_blocks


def _row_copy(src_hbm, row, buf, slot, r, sem):
    return pltpu.make_async_copy(src_hbm.at[pl.ds(row, 1), :], buf.at[slot, pl.ds(r, 1), :], sem.at[slot])


ROW_DMA_UNROLL = 8
MOE_BLK_PROMPT = 256


def _start_row_gather(src_hbm, row_of, buf, slot, n_rows, sem):
    unroll = math.gcd(ROW_DMA_UNROLL, n_rows)

    def body(j, c):
        for u in range(unroll):
            r = j * unroll + u
            _row_copy(src_hbm, row_of(r), buf, slot, r, sem).start(priority=u % 2)
        return c
    lax.fori_loop(0, n_rows // unroll, body, 0)


def _wait_row_gather(src_hbm, buf, slot, n_rows, sem):
    unroll = math.gcd(ROW_DMA_UNROLL, n_rows)

    def body(j, c):
        for u in range(unroll):
            _row_copy(src_hbm, 0, buf, slot, j * unroll + u, sem).wait()
        return c
    lax.fori_loop(0, n_rows // unroll, body, 0)


def _moe_kernel(be_ref, tok_ref, nu_ref, x_hbm, win_all, wout_all, y_ref,
                xs_buf, w1_scr, w2_scr, stage, gsem, wsem, *, blk, layer):
    win_hbm = win_all.at[layer]
    wout_hbm = wout_all.at[layer]
    i = pl.program_id(0)
    n_used = nu_ref[0]
    slot = i % 2
    e = be_ref[i]

    def start_gather(bi, sl):
        _start_row_gather(x_hbm, lambda r: tok_ref[bi * blk + r], xs_buf, sl, blk, gsem)

    @pl.when(i == 0)
    def _():
        start_gather(0, 0)

    @pl.when(i + 1 < n_used)
    def _():
        start_gather(i + 1, 1 - slot)

    changed = (i == 0) | (e != be_ref[jnp.maximum(i - 1, 0)])

    @pl.when(changed & (i < n_used))
    def _():
        rows = stage.shape[1]
        chunks = [(win_hbm, w1_scr, c) for c in range(w1_scr.shape[0] // rows)]
        chunks += [(wout_hbm, w2_scr, c) for c in range(w2_scr.shape[0] // rows)]

        def cp(j):
            src, _, c = chunks[j]
            return pltpu.make_async_copy(src.at[e, pl.ds(c * rows, rows), :], stage.at[j % 2], wsem.at[j % 2])

        cp(0).start()
        for j in range(len(chunks)):
            if j + 1 < len(chunks):
                cp(j + 1).start()
            cp(j).wait()
            _, dst, c = chunks[j]
            dst[pl.ds(c * rows, rows), :] = stage[j % 2].astype(BF16)

    @pl.when(i < n_used)
    def _():
        _wait_row_gather(x_hbm, xs_buf, slot, blk, gsem)
        z = _dot(xs_buf[slot], w1_scr[...])
        de = z.shape[1] // 2
        act = _silu(z[:, :de]) * z[:, de:]
        y_ref[...] = _dot(act, w2_scr[...])

    @pl.when(i >= n_used)
    def _():
        y_ref[...] = jnp.zeros_like(y_ref)


def _moe_ffn(h2, slot_tok, block_e, n_used, n_blocks, w_in, w_out, layer, blk, name):
    n, d = h2.shape
    de2 = w_in.shape[-1]
    wdt = BF16
    stage_rows = 256
    grid_spec = pltpu.PrefetchScalarGridSpec(
        num_scalar_prefetch=3,
        grid=(n_blocks,),
        in_specs=[pl.BlockSpec(memory_space=pl.ANY)] * 3,
        out_specs=pl.BlockSpec((blk, d), lambda i, be, tk, nu: (i, 0)),
        scratch_shapes=[
            pltpu.VMEM((2, blk, d), F32),
            pltpu.VMEM((d, de2), wdt),
            pltpu.VMEM((de2 // 2, d), wdt),
            pltpu.VMEM((2, stage_rows, max(d, de2)), F32),
            pltpu.SemaphoreType.DMA((2,)),
            pltpu.SemaphoreType.DMA((2,)),
        ],
    )
    return pl.pallas_call(
        functools.partial(_moe_kernel, blk=blk, layer=layer),
        grid_spec=grid_spec,
        out_shape=jax.ShapeDtypeStruct((n_blocks * blk, d), F32),
        compiler_params=_cparams(("arbitrary",), VMEM_LIMIT_BYTES),
        name=name,
    )(block_e, slot_tok, n_used, h2, w_in, w_out)


def _combine_kernel(so_ref, x_ref, wt_ref, gate_ref, lng_ref, lnb_ref, y_hbm, o_ref, ybuf, sem, *, tm):
    i = pl.program_id(0)
    nt = pl.num_programs(0)
    slot = i % 2

    def start_gather(ti, sl):
        for k in range(TOP_K):
            _start_row_gather(y_hbm, lambda r: so_ref[(ti * tm + r) * TOP_K + k], ybuf.at[k], sl, tm, sem.at[k])

    @pl.when(i == 0)
    def _():
        start_gather(0, 0)

    @pl.when(i + 1 < nt)
    def _():
        start_gather(i + 1, 1 - slot)

    for k in range(TOP_K):
        _wait_row_gather(y_hbm, ybuf.at[k], slot, tm, sem.at[k])

    wt = wt_ref[...]
    f = wt[:, 0:1] * ybuf[0, slot] + wt[:, 1:2] * ybuf[1, slot]
    o_ref[...] = _ln(ALPHA * x_ref[...] + gate_ref[...] * f) * lng_ref[...] + lnb_ref[...]


def _moe_combine(slot_of, x1, wts, gate, lng, lnb, y, tm, tpb, name):
    n, d = x1.shape
    grid_spec = pltpu.PrefetchScalarGridSpec(
        num_scalar_prefetch=1,
        grid=(n // tm,),
        in_specs=[
            pl.BlockSpec((tm, d), lambda i, so: (i, 0)),
            pl.BlockSpec((tm, TOP_K), lambda i, so: (i, 0)),
            pl.BlockSpec((None, gate.shape[1], d), lambda i, so: (i // tpb, 0, 0)),
            pl.BlockSpec((1, d), lambda i, so: (0, 0)),
            pl.BlockSpec((1, d), lambda i, so: (0, 0)),
            pl.BlockSpec(memory_space=pl.ANY),
        ],
        out_specs=pl.BlockSpec((tm, d), lambda i, so: (i, 0)),
        scratch_shapes=[pltpu.VMEM((TOP_K, 2, tm, d), F32), pltpu.SemaphoreType.DMA((TOP_K, 2))],
    )
    return pl.pallas_call(
        functools.partial(_combine_kernel, tm=tm),
        grid_spec=grid_spec,
        out_shape=jax.ShapeDtypeStruct((n, d), F32),
        compiler_params=_cparams(("arbitrary",)),
        name=name,
    )(slot_of, x1, wts, gate, lng, lnb, y)


def _channel_sublayer(x1, h2, logits, gate, lng, lnb, router_b, w_in, w_out, layer, blk, tm, tpb, tag):
    slot_tok, slot_of, wts, block_e, n_used, n_blocks = _route_tables(logits, router_b, blk)
    y = _moe_ffn(h2, slot_tok, block_e, n_used, n_blocks, w_in, w_out, layer, blk, "moe_ffn_" + tag)
    return _moe_combine(slot_of, x1, wts, gate, lng, lnb, y, tm, tpb, "moe_combine_" + tag)


def _rec_in_kernel(x_ref, sc_ref, sh_ref, w_ref, gg_ref, xbr_ref):
    hb = _modulate(x_ref[...], sc_ref[...], sh_ref[...]).astype(BF16)
    d = gg_ref.shape[1]
    gg_ref[...] = _gelu(jnp.dot(hb, w_ref[:, 0:d], preferred_element_type=F32))
    xbr_ref[...] = jnp.dot(hb, w_ref[:, d:], preferred_element_type=F32)


def _rec_in_prompt(x, sc, sh, w_bf, seq):
    n, d = x.shape
    dr = w_bf.shape[1] // 2
    tm = 256
    tpb = seq // tm
    row = lambda i: (i, 0)
    mod = lambda i: (i // tpb, 0, 0)
    return pl.pallas_call(
        _rec_in_kernel,
        grid=(n // tm,),
        in_specs=[
            pl.BlockSpec((tm, d), row),
            pl.BlockSpec((None, 1, d), mod),
            pl.BlockSpec((None, 1, d), mod),
            pl.BlockSpec(w_bf.shape, lambda i: (0, 0), pipeline_mode=pl.Buffered(1)),
        ],
        out_specs=(pl.BlockSpec((tm, dr), row), pl.BlockSpec((tm, dr), row)),
        out_shape=(jax.ShapeDtypeStruct((n, dr), F32), jax.ShapeDtypeStruct((n, dr), F32)),
        compiler_params=_cparams(("parallel",), VMEM_LIMIT_BYTES),
        name="rec_in_prompt",
    )(x, sc, sh, w_bf)


def _log1p(y):
    w = 1.0 + y
    return jnp.where(w == 1.0, y, jnp.log(w) * (y / jnp.where(w == 1.0, 1.0, w - 1.0)))


def _expm1(x):
    u = jnp.exp(x)
    safe = (u != 1.0) & (u > 0.0)
    return jnp.where(u == 1.0, x, jnp.where(u > 0.0, (u - 1.0) * (x / jnp.where(safe, jnp.log(u), 1.0)), -1.0))


def _softplus(x):
    return jnp.maximum(x, 0.0) + _log1p(jnp.exp(-jnp.abs(x)))


def _rg_terms(xb, z, ba, bx, lam):
    bs = xb.shape[1]
    r = jax.nn.sigmoid(z[:, :bs] + ba)
    gi = jax.nn.sigmoid(z[:, bs:] + bx)
    log_a = -RG_C * r * _softplus(-lam)
    a = jnp.exp(log_a)
    return a, jnp.sqrt(-_expm1(2.0 * log_a)) * (gi * xb)


def _rec_gate_kernel(x_ref, cw_ref, cb_ref, wab_ref, ba_ref, bx_ref, lam_ref, a_ref, bt_ref, carry_ref, *, tpb, tm):
    i = pl.program_id(0)

    @pl.when(i % tpb == 0)
    def _():
        carry_ref[...] = jnp.zeros_like(carry_ref)

    x = x_ref[...]
    carry = carry_ref[...]
    d = x.shape[1]
    row8 = lax.broadcasted_iota(I32, (8, d), 0)
    xc = cb_ref[...] + x * cw_ref[CONV_W - 1:CONV_W, :]
    for k in range(1, CONV_W):
        xr = pltpu.roll(x, k, 0)
        head = jnp.where(row8 < k, pltpu.roll(carry, k, 0), xr[0:8])
        xk = jnp.concatenate([head, xr[8:]], axis=0)
        xc = xc + xk * cw_ref[CONV_W - 1 - k:CONV_W - k, :]
    carry_ref[...] = x[tm - 8:tm]
    bs = d // RNN_BLOCKS
    for nb in range(RNN_BLOCKS):
        sl = slice(nb * bs, (nb + 1) * bs)
        xb = xc[:, sl]
        a, bt = _rg_terms(xb, _dot(xb, wab_ref[nb]), ba_ref[:, sl], bx_ref[:, sl], lam_ref[:, sl])
        a_ref[:, sl] = a
        bt_ref[:, sl] = bt


def _rec_gate_prompt(xbr, cw, cb, wab, ba, bx, lam, seq):
    n, d = xbr.shape
    tm = 256
    tpb = seq // tm
    row = lambda i: (i, 0)
    const2 = lambda i: (0, 0)
    return pl.pallas_call(
        functools.partial(_rec_gate_kernel, tpb=tpb, tm=tm),
        grid=(n // tm,),
        in_specs=[
            pl.BlockSpec((tm, d), row),
            pl.BlockSpec(cw.shape, const2),
            pl.BlockSpec((1, d), const2),
            pl.BlockSpec(wab.shape, lambda i: (0, 0, 0)),
            pl.BlockSpec((1, d), const2),
            pl.BlockSpec((1, d), const2),
            pl.BlockSpec((1, d), const2),
        ],
        out_specs=(pl.BlockSpec((tm, d), row), pl.BlockSpec((tm, d), row)),
        out_shape=(jax.ShapeDtypeStruct((n, d), F32), jax.ShapeDtypeStruct((n, d), F32)),
        scratch_shapes=[pltpu.VMEM((8, d), F32)],
        compiler_params=_cparams(("arbitrary",), VMEM_LIMIT_BYTES),
        name="rec_gate_prompt",
    )(xbr, cw, cb, wab, ba, bx, lam)


def _scan_kernel(a_ref, b_ref, hs_ref, ht_ref, h_ref, *, tt):
    t = pl.program_id(1)

    @pl.when(t == 0)
    def _():
        h_ref[...] = jnp.zeros_like(h_ref)

    def body(j, h):
        h = a_ref[j] * h + b_ref[j]
        hs_ref[j] = h
        return h

    h = lax.fori_loop(0, tt, body, h_ref[...], unroll=8)
    h_ref[...] = h
    ht_ref[...] = h


def _scan_prompt(a4, b4):
    bsz, seq, s8, c8 = a4.shape
    tt = min(512, seq)
    blk = pl.BlockSpec((None, tt, s8, c8), lambda b, t: (b, t, 0, 0))
    return pl.pallas_call(
        functools.partial(_scan_kernel, tt=tt),
        grid=(bsz, seq // tt),
        in_specs=[blk, blk],
        out_specs=(blk, pl.BlockSpec((None, s8, c8), lambda b, t: (b, 0, 0))),
        out_shape=(jax.ShapeDtypeStruct(a4.shape, F32), jax.ShapeDtypeStruct((bsz, s8, c8), F32)),
        scratch_shapes=[pltpu.VMEM((s8, c8), F32)],
        compiler_params=_cparams(("parallel", "arbitrary"), VMEM_LIMIT_BYTES),
        name="rglru_scan_prompt",
    )(a4, b4)


def _rec_step_sample(z, buf0, buf1, buf2, h0, cw, cb, wa, wx, ba, bx, lam):
    d = h0.shape[1]
    gate_br = z[:, :d]
    xbr = z[:, d:]
    xc = cb + buf0 * cw[0:1] + buf1 * cw[1:2] + buf2 * cw[2:3] + xbr * cw[3:4]
    bs = d // RNN_BLOCKS
    a_parts, b_parts = [], []
    for nb in range(RNN_BLOCKS):
        sl = slice(nb * bs, (nb + 1) * bs)
        xb = xc[:, sl]
        zz = jnp.concatenate([_dot(xb, wa[nb]), _dot(xb, wx[nb])], axis=1)
        a, bt = _rg_terms(xb, zz, ba[:, sl], bx[:, sl], lam[:, sl])
        a_parts.append(a)
        b_parts.append(bt)
    h = jnp.concatenate(a_parts, axis=1) * h0 + jnp.concatenate(b_parts, axis=1)
    return _gelu(gate_br) * h, h, xbr


def kernel(x_prompt, x_sample, c_prompt, c_sample, cache_nsa_cmp, cache_nsa_slc, state_nsa_win, state_rglru_conv, state_rglru_h, page_table, ada_w, ada_b, ln_g, ln_b, ab_w_in, ab_w_out, gmlp_ln_g, gmlp_ln_b, gmlp_ws, gmlp_bs, nsa_cmp_pe, nsa_cmp_w1, nsa_cmp_w2, rec_w_in, rec_conv_w, rec_conv_b, rg_wa, rg_ba, rg_wx, rg_bx, rg_lambda, rec_w_out, router_w, router_b, moe_w_in, moe_w_out):
    bp, seq, d = x_prompt.shape
    db = x_sample.shape[0]
    assert x_sample.shape[1] == 1
    n_pool, page = cache_nsa_cmp.shape[:2]
    n_pages = page_table.shape[1]
    past_len = n_pages * page
    assert seq % 256 == 0 and past_len % SLC_LEN == 0 and page % SLC_LEN == 0 and past_len >= WINDOW
    n_p = bp * seq
    d_rnn = rec_conv_b.shape[0]

    mods = _ada_all(jnp.concatenate([c_prompt, c_sample], axis=0), ada_w, ada_b)

    def mod_rows(layer, sub):
        m = mods[layer * 2 + sub]
        parts = [m[:, j * d:(j + 1) * d] for j in range(3)]
        return [p[:bp].reshape(bp, 1, d) for p in parts], [p[bp:] for p in parts]

    lnrow = lambda a: a.reshape(1, d)

    gperm = np.array([(g * HPG + h) * N_BRANCH + br for g in range(N_KV) for br in range(N_BRANCH) for h in range(HPG)])
    w_gate = ab_w_in[:, O_G:][:, gperm]
    w_ab_bf = jnp.concatenate([ab_w_in[:, :O_G], w_gate], axis=1).astype(BF16)
    tril = jnp.tril(jnp.ones((CHUNK, CHUNK), F32))
    wtril = (gmlp_ws * tril).astype(BF16)
    bst = gmlp_bs.T
    glng = gmlp_ln_g.reshape(1, A_WIDTH)
    glnb = gmlp_ln_b.reshape(1, A_WIDTH)
    pe_rows, w1bd, w2bd = _compress_weights(nsa_cmp_pe, nsa_cmp_w1, nsa_cmp_w2)
    rw_t = router_w.T
    xp = x_prompt.reshape(n_p, d)
    xs = x_sample.reshape(db, d)

    (sh_p, sc_p, g_p), (sh_s, sc_s, g_s) = mod_rows(0, 0)
    (sh2_p, sc2_p, g2_p), (sh2_s, sc2_s, g2_s) = mod_rows(0, 1)
    cos_p, sin_p = _rope_tables(np.arange(seq))
    a_p, q_p, kvc_p, kvs_p, kvw_p, kvt_p, gt_p = _ab_proj_prompt(
        xp, sc_p, sh_p, w_ab_bf, cos_p, sin_p, glng, glnb, wtril, bst, seq)
    kcv_p = _compress_prompt(kvc_p.reshape(bp, seq // CMP_STRIDE, CMP_STRIDE * KV_WIDTH), pe_rows, w1bd, w2bd)
    o_p = _nsa_prompt(q_p, gt_p, kcv_p, kvt_p, bp, seq)
    w_out_bf = ab_w_out.astype(BF16)
    x1_p, h2_p, lg_p = _mix_out_prompt(a_p, o_p, w_out_bf, xp, g_p, lnrow(ln_g[0, 0]), lnrow(ln_b[0, 0]),
                                       sc2_p, sh2_p, rw_t, seq, True, "mix_out_l0_prompt")

    z_s = _small_mm(_modulate, [xs, sc_s, sh_s], ab_w_in, O_G, 512, "ab_proj_sample")
    zg_s = _small_mm(_modulate, [xs, sc_s, sh_s], w_gate, N_GATE, N_GATE, "ab_gate_sample")
    cos_s, sin_s = _rope_tables(np.full((db,), past_len))
    ws0 = jnp.repeat(gmlp_ws[:, 0, 0], A_GROUP_DIM).reshape(1, A_WIDTH)
    bs0 = jnp.repeat(gmlp_bs[:, 0], A_GROUP_DIM).reshape(1, A_WIDTH)
    sds = lambda *s: jax.ShapeDtypeStruct(s, F32)
    a_s, v_s, q_s, kvc_s, kvs_s, kvw_s, gt_s = _vmem_call(
        _ab_post_sample,
        (sds(db, A_WIDTH), sds(db, A_WIDTH), sds(db, B_WIDTH), sds(db, KV_WIDTH), sds(db, KV_WIDTH),
         sds(db, KV_WIDTH), sds(db, N_GATE)),
        (z_s, zg_s, cos_s, sin_s, glng, glnb, ws0, bs0), "ab_post_sample")
    q3 = q_s.reshape(db, 1, B_WIDTH)
    page_t = lambda c: jnp.transpose(c, (0, 2, 3, 4, 1)).reshape(n_pool, KV_WIDTH, page)
    oc3, sel_idx = _nsa_sample_cmp(page_table, q3, page_t(cache_nsa_cmp), pe_rows, w1bd, w2bd, past_len)
    sel_idx = jnp.transpose(sel_idx, (0, 2, 1))
    o_s = _nsa_sample_attn(page_table, sel_idx, q3, gt_s.reshape(db, 1, N_GATE), oc3,
                           kvs_s.reshape(db, 1, KV_WIDTH), kvw_s.reshape(db, 1, KV_WIDTH),
                           state_nsa_win.reshape(db, -1, KV_WIDTH), page_t(cache_nsa_slc),
                           past_len).reshape(db, B_WIDTH)
    f_s = _small_mm(lambda a, o: jnp.concatenate([a, o], axis=1), [a_s, o_s], ab_w_out, d, 512, "mix_out_l0_sample")
    x1_s, h2_s, lg_s = _vmem_call(
        _post_ln_router, (sds(db, d), sds(db, d), sds(N_EXPERTS, db)),
        (f_s, xs, g_s, lnrow(ln_g[0, 0]), lnrow(ln_b[0, 0]), sc2_s, sh2_s, rw_t), "post_l0_sample")

    x2_p = _channel_sublayer(x1_p, h2_p, lg_p, g2_p, lnrow(ln_g[0, 1]), lnrow(ln_b[0, 1]), router_b,
                             moe_w_in, moe_w_out, 0, MOE_BLK_PROMPT, 128, seq // 128, "l0_prompt")
    x2_s = _channel_sublayer(x1_s, h2_s, lg_s, g2_s.reshape(1, db, d), lnrow(ln_g[0, 1]), lnrow(ln_b[0, 1]), router_b,
                             moe_w_in, moe_w_out, 0, 32, db, 1, "l0_sample")

    (sh_p, sc_p, g_p), (sh_s, sc_s, g_s) = mod_rows(1, 0)
    (sh2_p, sc2_p, g2_p), (sh2_s, sc2_s, g2_s) = mod_rows(1, 1)
    gg_p, xbr_p = _rec_in_prompt(x2_p, sc_p, sh_p, rec_w_in.astype(BF16), seq)
    wab = jnp.concatenate([rg_wa, rg_wx], axis=2).astype(BF16)
    row_r = lambda a: a.reshape(1, d_rnn)
    a_t, b_t = _rec_gate_prompt(xbr_p, rec_conv_w, row_r(rec_conv_b), wab, row_r(rg_ba), row_r(rg_bx),
                                row_r(rg_lambda), seq)
    hs4, ht = _scan_prompt(a_t.reshape(bp, seq, 8, d_rnn // 8), b_t.reshape(bp, seq, 8, d_rnn // 8))
    x3_p, h4_p, lg_p = _mix_out_prompt(gg_p, hs4.reshape(n_p, d_rnn), rec_w_out.astype(BF16), x2_p, g_p,
                                       lnrow(ln_g[1, 0]), lnrow(ln_b[1, 0]), sc2_p, sh2_p, rw_t, seq, False,
                                       "mix_out_l1_prompt")
    conv_p = xbr_p.reshape(bp, seq, d_rnn)[:, seq - (CONV_W - 1):]
    h_p = ht.reshape(bp, d_rnn)

    zr_s = _small_mm(_modulate, [x2_s, sc_s, sh_s], rec_w_in, 2 * d_rnn, 512, "rec_in_sample")
    y_s, h_s, xbr_s = _vmem_call(
        _rec_step_sample, (sds(db, d_rnn), sds(db, d_rnn), sds(db, d_rnn)),
        (zr_s, state_rglru_conv[:, 0], state_rglru_conv[:, 1], state_rglru_conv[:, 2], state_rglru_h,
         rec_conv_w, row_r(rec_conv_b), rg_wa, rg_wx, row_r(rg_ba), row_r(rg_bx), row_r(rg_lambda)),
        "rec_step_sample")
    f_s = _small_mm(lambda y: y, [y_s], rec_w_out, d, 512, "mix_out_l1_sample")
    x3_s, h4_s, lg_s = _vmem_call(
        _post_ln_router, (sds(db, d), sds(db, d), sds(N_EXPERTS, db)),
        (f_s, x2_s, g_s, lnrow(ln_g[1, 0]), lnrow(ln_b[1, 0]), sc2_s, sh2_s, rw_t), "post_l1_sample")
    conv_s = jnp.concatenate([state_rglru_conv[:, 1:], xbr_s[:, None, :]], axis=1)

    y_p = _channel_sublayer(x3_p, h4_p, lg_p, g2_p, lnrow(ln_g[1, 1]), lnrow(ln_b[1, 1]), router_b,
                            moe_w_in, moe_w_out, 1, MOE_BLK_PROMPT, 128, seq // 128, "l1_prompt")
    y_s = _channel_sublayer(x3_s, h4_s, lg_s, g2_s.reshape(1, db, d), lnrow(ln_g[1, 1]), lnrow(ln_b[1, 1]), router_b,
                            moe_w_in, moe_w_out, 1, 32, db, 1, "l1_sample")

    kv5 = lambda a, b_: a.reshape(b_, -1, 2, N_KV, HEAD_DIM)
    keep = min(WINDOW, seq)
    win_p = kv5(kvw_p, bp)[:, seq - keep:]
    kw_full = jnp.concatenate([state_nsa_win, kv5(kvw_s, db)], axis=1)
    win_s = kw_full[:, kw_full.shape[1] - min(WINDOW, kw_full.shape[1]):]
    return (y_p.reshape(bp, seq, d), y_s.reshape(db, 1, d), kv5(kvc_p, bp), kv5(kvc_s, db), kv5(kvs_p, bp),
            kv5(kvs_s, db), win_p, win_s, v_s.reshape(db, 1, A_WIDTH), conv_p, conv_s, h_p, h_s)
```

```python
import functools
import math

import numpy as np
import jax
import jax.numpy as jnp
from jax import lax
from jax.experimental import pallas as pl
from jax.experimental.pallas import tpu as pltpu

F32 = jnp.float32
BF16 = jnp.bfloat16
I32 = jnp.int32

A_GROUPS = 8
A_GROUP_DIM = 128
A_WIDTH = A_GROUPS * A_GROUP_DIM
CHUNK = 128
N_HEADS = 16
N_KV = 4
HEAD_DIM = 64
HPG = N_HEADS // N_KV
B_WIDTH = N_HEADS * HEAD_DIM
KV_WIDTH = 2 * N_KV * HEAD_DIM
N_BRANCH = 3
CMP_LEN = 32
CMP_STRIDE = 16
CMP_HID = 2 * HEAD_DIM
SLC_LEN = 64
N_SEL = 16
WINDOW = 512
Q_BLOCK = 128
FORCE_BONUS = 1e4
ROPE_THETA = 10000.0
RNN_BLOCKS = 16
CONV_W = 4
RG_C = 8.0
N_EXPERTS = 16
N_GROUPS = 4
EXPERTS_PER_GROUP = N_EXPERTS // N_GROUPS
TOP_K = 2
DEPTH = 2
ALPHA = (2 * DEPTH) ** 0.25
LN_EPS = 1e-5
NEG = -1e30

O_U = 0
O_V = A_WIDTH
O_Q = 2 * A_WIDTH
O_KV = O_Q + B_WIDTH
O_G = O_KV + 3 * KV_WIDTH
N_GATE = N_BRANCH * N_HEADS

VMEM_LIMIT_BYTES = 56 * 1024 * 1024


def _cparams(sem, vmem=None):
    return pltpu.CompilerParams(dimension_semantics=sem, vmem_limit_bytes=vmem)


def _dot(a, b):
    return jnp.dot(a.astype(BF16), b.astype(BF16), preferred_element_type=F32)


def _dot_nt(a, b):
    dn = (((1,), (1,)), ((), ()))
    return lax.dot_general(a.astype(BF16), b.astype(BF16), dn, preferred_element_type=F32)


def _ln(x):
    mu = jnp.mean(x, -1, keepdims=True)
    xc = x - mu
    var = jnp.mean(xc * xc, -1, keepdims=True)
    return xc * lax.rsqrt(var + LN_EPS)


def _silu(x):
    return x * jax.nn.sigmoid(x)


def _gelu(x):
    return jax.nn.gelu(x, approximate=True)


def _rope(x, cos2, sin2):
    w = x.shape[1]
    rep = w // 128
    cos = jnp.concatenate([cos2] * rep, axis=1) if rep > 1 else cos2
    sin = jnp.concatenate([sin2] * rep, axis=1) if rep > 1 else sin2
    lane = lax.broadcasted_iota(I32, x.shape, 1)
    first = (lane & (HEAD_DIM - 1)) < HEAD_DIM // 2
    rot = jnp.where(first, pltpu.roll(x, w - HEAD_DIM // 2, 1), pltpu.roll(x, HEAD_DIM // 2, 1))
    return x * cos + rot * sin


def _rope_tables(pos):
    half = HEAD_DIM // 2
    inv = ROPE_THETA ** (-np.arange(half, dtype=np.float64) / half)
    ang = np.asarray(pos, np.float64)[:, None] * inv[None, :]
    cos = np.tile(np.cos(ang), (1, 4))
    sin = np.tile(np.concatenate([-np.sin(ang), np.sin(ang)], axis=1), (1, 2))
    return jnp.asarray(cos, F32), jnp.asarray(sin, F32)


def _softmax_rows(s, mask):
    sm = jnp.where(mask, s, NEG)
    m = jnp.max(sm, -1, keepdims=True)
    p = jnp.where(mask, jnp.exp(sm - m), 0.0)
    return p, jnp.sum(p, -1, keepdims=True)


def _ada_kernel(c_ref, w_ref, b_ref, o_ref):
    o_ref[...] = _dot(_silu(c_ref[...]), w_ref[...]) + b_ref[...]


def _ada_all(c_all, ada_w, ada_b):
    r, d = c_all.shape
    n_mod = ada_w.shape[0] * ada_w.shape[1]
    d3 = ada_w.shape[-1]
    tn = 512
    return pl.pallas_call(
        _ada_kernel,
        grid=(n_mod, d3 // tn),
        in_specs=[
            pl.BlockSpec((r, d), lambda l, j: (0, 0)),
            pl.BlockSpec((None, d, tn), lambda l, j: (l, 0, j)),
            pl.BlockSpec((None, 1, tn), lambda l, j: (l, 0, j)),
        ],
        out_specs=pl.BlockSpec((None, r, tn), lambda l, j: (l, 0, j)),
        out_shape=jax.ShapeDtypeStruct((n_mod, r, d3), F32),
        compiler_params=_cparams(("parallel", "parallel")),
        name="ada_mod",
    )(c_all, ada_w.reshape(n_mod, d, d3), ada_b.reshape(n_mod, 1, d3))


def _small_mm_kernel(*refs, n_x, pre):
    xs = [r[...] for r in refs[:n_x]]
    w_ref, o_ref = refs[n_x], refs[n_x + 1]
    o_ref[...] = _dot(pre(*xs), w_ref[...])


def _small_mm(pre, xs, w, n_out, tn, name):
    m = xs[0].shape[0]
    k = w.shape[0]
    in_specs = [pl.BlockSpec(x.shape, lambda j, nd=x.ndim: (0,) * nd) for x in xs]
    in_specs.append(pl.BlockSpec((k, tn), lambda j: (0, j)))
    return pl.pallas_call(
        functools.partial(_small_mm_kernel, n_x=len(xs), pre=pre),
        grid=(n_out // tn,),
        in_specs=in_specs,
        out_specs=pl.BlockSpec((m, tn), lambda j: (0, j)),
        out_shape=jax.ShapeDtypeStruct((m, n_out), F32),
        compiler_params=_cparams(("parallel",)),
        name=name,
    )(*xs, w)


def _vmem_call(fn, out_shapes, args, name):
    n_in = len(args)

    def kern(*refs):
        res = fn(*[r[...] for r in refs[:n_in]])
        for o, v in zip(refs[n_in:], res):
            o[...] = v

    return pl.pallas_call(kern, out_shape=out_shapes, name=name)(*args)


def _modulate(x, sc, sh):
    return x * (1.0 + sc) + sh


def _ab_proj_kernel(x_ref, sc_ref, sh_ref, w_ref, cos_ref, sin_ref, lng_ref, lnb_ref, wtril_ref, bst_ref,
                    a_ref, q_ref, kvc_ref, kvs_ref, kvw_ref, kvt_ref, gt_ref, *, tm):
    hb = _modulate(x_ref[...], sc_ref[...], sh_ref[...]).astype(BF16)

    def proj(lo, hi):
        return jnp.dot(hb, w_ref[:, lo:hi], preferred_element_type=F32)

    cos2 = cos_ref[...]
    sin2 = sin_ref[...]
    zu = proj(O_U, O_V)
    zv = proj(O_V, O_Q)
    for g in range(A_GROUPS):
        sl = slice(g * A_GROUP_DIM, (g + 1) * A_GROUP_DIM)
        vg = _ln(zv[:, sl]) * lng_ref[:, sl] + lnb_ref[:, sl]
        for c in range(tm // CHUNK):
            rs = slice(c * CHUNK, (c + 1) * CHUNK)
            mix = jnp.dot(wtril_ref[g], vg[rs].astype(BF16), preferred_element_type=F32) + bst_ref[:, g:g + 1]
            a_ref[rs, sl] = zu[rs, sl] * mix
    q_ref[...] = _rope(proj(O_Q, O_KV), cos2, sin2)
    half = KV_WIDTH // 2
    for br, ref in enumerate((kvc_ref, kvs_ref, kvw_ref)):
        lo = O_KV + br * KV_WIDTH
        k = _rope(proj(lo, lo + half), cos2, sin2)
        v = proj(lo + half, lo + KV_WIDTH)
        ref[:, 0:half] = k
        ref[:, half:KV_WIDTH] = v
        for g in range(N_KV):
            hs = slice(g * HEAD_DIM, (g + 1) * HEAD_DIM)
            kvt_ref[br, 0, g] = k[:, hs]
            kvt_ref[br, 1, g] = v[:, hs]
    zg = jax.nn.sigmoid(proj(O_G, O_G + N_GATE))
    per_g = N_GATE // N_KV
    for g in range(N_KV):
        gt_ref[g] = zg[:, g * per_g:(g + 1) * per_g]


def _ab_proj_prompt(x, sc, sh, w_bf, cos2, sin2, lng, lnb, wtril, bst, seq):
    n, d = x.shape
    tm = 256
    tpb = seq // tm
    n_in = w_bf.shape[1]
    row = lambda i: (i, 0)
    mod = lambda i: (i // tpb, 0, 0)
    const2 = lambda i: (0, 0)
    pos = lambda i: (i % tpb, 0)
    out_shape = (
        jax.ShapeDtypeStruct((n, A_WIDTH), F32),
        jax.ShapeDtypeStruct((n, B_WIDTH), F32),
        jax.ShapeDtypeStruct((n, KV_WIDTH), F32),
        jax.ShapeDtypeStruct((n, KV_WIDTH), F32),
        jax.ShapeDtypeStruct((n, KV_WIDTH), F32),
        jax.ShapeDtypeStruct((3, 2, N_KV, n, HEAD_DIM), F32),
        jax.ShapeDtypeStruct((N_KV, n, N_GATE // N_KV), F32),
    )
    return pl.pallas_call(
        functools.partial(_ab_proj_kernel, tm=tm),
        grid=(n // tm,),
        in_specs=[
            pl.BlockSpec((tm, d), row),
            pl.BlockSpec((None, 1, d), mod),
            pl.BlockSpec((None, 1, d), mod),
            pl.BlockSpec((d, n_in), const2, pipeline_mode=pl.Buffered(1)),
            pl.BlockSpec((tm, 128), pos),
            pl.BlockSpec((tm, 128), pos),
            pl.BlockSpec((1, A_WIDTH), const2),
            pl.BlockSpec((1, A_WIDTH), const2),
            pl.BlockSpec((A_GROUPS, CHUNK, CHUNK), lambda i: (0, 0, 0)),
            pl.BlockSpec((CHUNK, A_GROUPS), const2),
        ],
        out_specs=(
            pl.BlockSpec((tm, A_WIDTH), row),
            pl.BlockSpec((tm, B_WIDTH), row),
            pl.BlockSpec((tm, KV_WIDTH), row),
            pl.BlockSpec((tm, KV_WIDTH), row),
            pl.BlockSpec((tm, KV_WIDTH), row),
            pl.BlockSpec((3, 2, N_KV, tm, HEAD_DIM), lambda i: (0, 0, 0, i, 0)),
            pl.BlockSpec((N_KV, tm, N_GATE // N_KV), lambda i: (0, i, 0)),
        ),
        out_shape=out_shape,
        compiler_params=_cparams(("parallel",), VMEM_LIMIT_BYTES),
        name="ab_proj_prompt",
    )(x, sc, sh, w_bf, cos2, sin2, lng, lnb, wtril, bst)


def _ab_post_sample(z, zg, cos2, sin2, lng, lnb, ws0, bs0):
    zu = z[:, O_U:O_V]
    zv = z[:, O_V:O_Q]
    vs = []
    for g in range(A_GROUPS):
        sl = slice(g * A_GROUP_DIM, (g + 1) * A_GROUP_DIM)
        vs.append(_ln(zv[:, sl]) * lng[:, sl] + lnb[:, sl])
    v = jnp.concatenate(vs, axis=1)
    a = zu * (ws0 * v + bs0)
    q = _rope(z[:, O_Q:O_KV], cos2, sin2)
    half = KV_WIDTH // 2
    kvs = []
    for br in range(3):
        lo = O_KV + br * KV_WIDTH
        k = _rope(z[:, lo:lo + half], cos2, sin2)
        kvs.append(jnp.concatenate([k, z[:, lo + half:lo + KV_WIDTH]], axis=1))
    return a, v, q, kvs[0], kvs[1], kvs[2], jax.nn.sigmoid(zg)


def _compress_mlp(lhs_lo, lhs_hi, w1_ref, w2_ref, kv, n):
    lo = _dot(lhs_lo, w1_ref[kv, 0])
    hi = _dot(lhs_hi, w1_ref[kv, 1])
    hsum = lo + pltpu.roll(hi, n - 1, 0)
    return _dot(_gelu(hsum), w2_ref[kv])


def _compress_prompt_kernel(x_ref, pe_ref, w1_ref, w2_ref, o_ref, *, n16):
    for kv in range(2):
        for gp in range(N_KV // 2):
            base = kv * (KV_WIDTH // 2) + gp * 128
            xg = jnp.concatenate(
                [x_ref[:, s * KV_WIDTH + base:s * KV_WIDTH + base + 128] for s in range(CMP_STRIDE)], axis=1)
            r = _compress_mlp(xg + pe_ref[kv, 0], xg + pe_ref[kv, 1], w1_ref, w2_ref, kv, n16)
            o_ref[kv, 2 * gp] = r[:, 0:HEAD_DIM]
            o_ref[kv, 2 * gp + 1] = r[:, HEAD_DIM:2 * HEAD_DIM]


def _compress_prompt(kvc3, pe_rows, w1bd, w2bd):
    b, n16, wid = kvc3.shape
    return pl.pallas_call(
        functools.partial(_compress_prompt_kernel, n16=n16),
        grid=(b,),
        in_specs=[
            pl.BlockSpec((None, n16, wid), lambda i: (i, 0, 0)),
            pl.BlockSpec(pe_rows.shape, lambda i: (0, 0, 0, 0)),
            pl.BlockSpec(w1bd.shape, lambda i: (0, 0, 0, 0)),
            pl.BlockSpec(w2bd.shape, lambda i: (0, 0, 0)),
        ],
        out_specs=pl.BlockSpec((None, 2, N_KV, n16, HEAD_DIM), lambda i: (i, 0, 0, 0, 0)),
        out_shape=jax.ShapeDtypeStruct((b, 2, N_KV, n16, HEAD_DIM), F32),
        compiler_params=_cparams(("parallel",), VMEM_LIMIT_BYTES),
        name="nsa_compress_prompt",
    )(kvc3, pe_rows, w1bd, w2bd)


def _compress_weights(pe, w1, w2):
    eye2 = jnp.eye(2, dtype=F32)
    w1r = w1.reshape(2, 2, CMP_STRIDE, HEAD_DIM, CMP_HID)
    w1bd = jnp.einsum('khsdc,gG->khsgdGc', w1r, eye2).reshape(2, 2, CMP_STRIDE * 128, 2 * CMP_HID).astype(BF16)
    pe_rows = jnp.broadcast_to(pe.reshape(2, 2, CMP_STRIDE, 1, HEAD_DIM), (2, 2, CMP_STRIDE, 2, HEAD_DIM))
    pe_rows = pe_rows.reshape(2, 2, 1, CMP_STRIDE * 128)
    w2bd = jnp.einsum('kcd,gG->kgcGd', w2, eye2).reshape(2, 2 * CMP_HID, 2 * HEAD_DIM).astype(BF16)
    return pe_rows, w1bd, w2bd


def _cmp_slc_map(n_rows, n_cmp, n_slc, n_cols):
    cs = np.arange(n_rows)[:, None] * CMP_STRIDE
    ss = np.arange(n_cols)[None, :] * SLC_LEN
    m = (cs < ss + SLC_LEN) & (cs + CMP_LEN > ss)
    m &= (np.arange(n_rows)[:, None] < n_cmp) & (np.arange(n_cols)[None, :] < n_slc)
    return m.astype(np.float32)


def _nsa_prompt_kernel(q_ref, gt_ref, kc_ref, vc_ref, ks_ref, vs_ref, kw_ref, vw_ref, mapt_ref, e_ref,
                       o_ref, selb_ref, *, n_cmp, n_slc, n_sel, kt_len):
    i = pl.program_id(2)
    q0 = i * Q_BLOCK
    rows = HPG * Q_BLOCK
    tile4 = lambda x: jnp.concatenate([x] * HPG, axis=0)
    qb = q_ref[...]
    q4 = jnp.concatenate([qb[:, h * HEAD_DIM:(h + 1) * HEAD_DIM] for h in range(HPG)], axis=0)
    q4 = (q4 * HEAD_DIM ** -0.5).astype(BF16)
    t_pos = q0 + lax.broadcasted_iota(I32, (Q_BLOCK, 1), 0)

    n16 = kc_ref.shape[0]
    n_idx = lax.broadcasted_iota(I32, (1, n16), 1)
    bias_c = jnp.where((n_idx * CMP_STRIDE + CMP_LEN - 1 <= t_pos) & (n_idx < n_cmp), 0.0, NEG)
    s = _dot_nt(q4, kc_ref[...]) + tile4(bias_c)
    p = jnp.exp(s - jnp.max(s, -1, keepdims=True))
    l = jnp.sum(p, -1, keepdims=True)
    pn = p * jnp.where((tile4(t_pos) >= CMP_LEN - 1) & (n_cmp > 0), 1.0 / l, 0.0)
    o_c = _dot(pn, vc_ref[...])
    psum = pn[0:Q_BLOCK]
    for h in range(1, HPG):
        psum = psum + pn[h * Q_BLOCK:(h + 1) * Q_BLOCK]

    imp_t = _dot_nt(mapt_ref[...], psum)
    blk_t = lax.broadcasted_iota(I32, (n_slc, Q_BLOCK), 0)
    cur_t = (q0 + lax.broadcasted_iota(I32, (n_slc, Q_BLOCK), 1)) // SLC_LEN
    forced = (blk_t == 0) | (blk_t == cur_t) | (blk_t == cur_t - 1)
    score = jnp.where(blk_t <= cur_t, imp_t + jnp.where(forced, FORCE_BONUS, 0.0), -jnp.inf)
    rank = jnp.zeros((n_slc, Q_BLOCK), I32)
    for j in range(n_slc):
        r = score[j:j + 1, :]
        rank = rank + ((r > score) | ((r == score) & (blk_t > j))).astype(I32)
    sel_t = ((rank < n_sel) & (score > -jnp.inf)).astype(F32)
    selb_ref[...] = (_dot(sel_t.T, e_ref[...]) - 1.0) * (-NEG)

    def sel_tile(kt, carry, causal):
        m, l, acc = carry
        k0 = pl.multiple_of(kt * kt_len, kt_len)
        bias = selb_ref[:, pl.ds(k0, kt_len)]
        if causal:
            kp = k0 + lax.broadcasted_iota(I32, (1, kt_len), 1)
            bias = bias + jnp.where(kp <= t_pos, 0.0, NEG)
        s = _dot_nt(q4, ks_ref[pl.ds(k0, kt_len), :]) + tile4(bias)
        m_new = jnp.maximum(m, jnp.max(s, -1, keepdims=True))
        alpha = jnp.exp(m - m_new)
        p = jnp.exp(s - m_new)
        l = alpha * l + jnp.sum(p, -1, keepdims=True)
        acc = alpha * acc + _dot(p, vs_ref[pl.ds(k0, kt_len), :])
        return m_new, l, acc

    last_kt = (q0 + Q_BLOCK - 1) // kt_len
    init = (jnp.full((rows, 1), NEG, F32), jnp.zeros((rows, 1), F32), jnp.zeros((rows, HEAD_DIM), F32))
    carry = lax.fori_loop(0, last_kt, functools.partial(sel_tile, causal=False), init)
    _, l, acc = sel_tile(last_kt, carry, True)
    o_s = acc * (1.0 / l)

    n_wt = WINDOW // Q_BLOCK + 1
    c_idx = lax.broadcasted_iota(I32, (Q_BLOCK, Q_BLOCK), 1)
    r_idx = lax.broadcasted_iota(I32, (Q_BLOCK, Q_BLOCK), 0)
    s_parts, v_parts = [], []
    for j in range(n_wt):
        ks_j = q0 - WINDOW + j * Q_BLOCK
        ld = pl.multiple_of(jnp.maximum(ks_j, 0), Q_BLOCK)
        off = jnp.where(ks_j >= 0, 0.0, NEG)
        sj = _dot_nt(q4, kw_ref[pl.ds(ld, Q_BLOCK), :])
        if j == 0:
            sj = sj + tile4(jnp.where(c_idx > r_idx, 0.0, NEG) + off)
        elif j == n_wt - 1:
            sj = sj + tile4(jnp.where(c_idx <= r_idx, 0.0, NEG))
        else:
            sj = sj + off
        s_parts.append(sj)
        v_parts.append(vw_ref[pl.ds(ld, Q_BLOCK), :])
    s = jnp.concatenate(s_parts, axis=1)
    p = jnp.exp(s - jnp.max(s, -1, keepdims=True))
    o_w = _dot(p, jnp.concatenate(v_parts, axis=0)) * (1.0 / jnp.sum(p, -1, keepdims=True))

    gt = gt_ref[...]

    def gcol(br):
        return jnp.concatenate([gt[:, br * HPG + h:br * HPG + h + 1] for h in range(HPG)], axis=0)

    o = gcol(0) * o_c + gcol(1) * o_s + gcol(2) * o_w
    o_ref[...] = jnp.concatenate([o[h * Q_BLOCK:(h + 1) * Q_BLOCK] for h in range(HPG)], axis=1)


def _nsa_prompt(q, gates, kcv, kvt, batch, seq):
    n = q.shape[0]
    nq = seq // Q_BLOCK
    n16 = kcv.shape[3]
    n_cmp = n16 - 1
    n_slc = -(-seq // SLC_LEN)
    n_sel = min(N_SEL, n_slc)
    kt_len = min(512, seq)
    mapt = jnp.asarray(_cmp_slc_map(n16, n_cmp, n_slc, n_slc).T)
    expand = jnp.asarray((np.arange(n_slc)[:, None] == np.arange(seq)[None, :] // SLC_LEN).astype(np.float32), BF16)
    per_g = N_GATE // N_KV
    qrow = lambda b, g, i: (b * nq + i, g)
    kvspec = lambda br, kv: pl.BlockSpec((None, None, None, seq, HEAD_DIM), lambda b, g, i: (br, kv, g, b, 0))
    return pl.pallas_call(
        functools.partial(_nsa_prompt_kernel, n_cmp=n_cmp, n_slc=n_slc, n_sel=n_sel, kt_len=kt_len),
        grid=(batch, N_KV, nq),
        in_specs=[
            pl.BlockSpec((Q_BLOCK, HPG * HEAD_DIM), qrow),
            pl.BlockSpec((None, Q_BLOCK, per_g), lambda b, g, i: (g, b * nq + i, 0)),
            pl.BlockSpec((None, None, None, n16, HEAD_DIM), lambda b, g, i: (b, 0, g, 0, 0)),
            pl.BlockSpec((None, None, None, n16, HEAD_DIM), lambda b, g, i: (b, 1, g, 0, 0)),
            kvspec(1, 0), kvspec(1, 1), kvspec(2, 0), kvspec(2, 1),
            pl.BlockSpec(mapt.shape, lambda b, g, i: (0, 0)),
            pl.BlockSpec(expand.shape, lambda b, g, i: (0, 0)),
        ],
        out_specs=pl.BlockSpec((Q_BLOCK, HPG * HEAD_DIM), qrow),
        out_shape=jax.ShapeDtypeStruct((n, B_WIDTH), F32),
        scratch_shapes=[pltpu.VMEM((Q_BLOCK, seq), F32)],
        compiler_params=_cparams(("parallel", "parallel", "arbitrary"), VMEM_LIMIT_BYTES),
        name="nsa_attn_prompt",
    )(q, gates, kcv, kcv, kvt, kvt, kvt, kvt, mapt, expand)


FILL_PAGES = 8


def _page_copy(cache_hbm, pt_ref, xbuf, sem, b, slot, p, n_pages):
    return pltpu.make_async_copy(cache_hbm.at[pt_ref[b * n_pages + p]], xbuf.at[slot, p], sem.at[slot])


def _nsa_sample_cmp_kernel(pt_ref, q_ref, cache_hbm, pe_ref, w1_ref, w2_ref, map_ref,
                           oc_ref, idx_ref, xbuf, xrow, lhs_a, lhs_b, sem, *, n_pages, n_cmp, n_slc, n_sel, pos):
    b = pl.program_id(0)
    nb = pl.num_programs(0)
    slot = b % 2
    n16 = lhs_a.shape[0]
    cpp = n16 // n_pages

    def start_all(bb, sl):
        def body(p, c):
            _page_copy(cache_hbm, pt_ref, xbuf, sem, bb, sl, p, n_pages).start()
            return c
        lax.fori_loop(0, n_pages, body, 0)

    @pl.when(b == 0)
    def _():
        start_all(0, 0)

    @pl.when(b + 1 < nb)
    def _():
        start_all(b + 1, 1 - slot)

    def wait_body(p, c):
        _page_copy(cache_hbm, pt_ref, xbuf, sem, b, slot, p, n_pages).wait()
        return c
    lax.fori_loop(0, n_pages, wait_body, 0)

    kc, vc = [], []
    wkv = N_KV * HEAD_DIM
    for kv, dst in ((0, kc), (1, vc)):
        def fill(j, c):
            for u in range(FILL_PAGES):
                p = j * FILL_PAGES + u
                r0 = pl.multiple_of(p * cpp, cpp)
                for gp, lhs in enumerate((lhs_a, lhs_b)):
                    xrow[u, gp] = xbuf[slot, p, kv * wkv + gp * 128:kv * wkv + (gp + 1) * 128, :].T
                    for s in range(CMP_STRIDE):
                        lhs[pl.ds(r0, cpp), s * 128:(s + 1) * 128] = xrow[u, gp, pl.ds(s, cpp, stride=CMP_STRIDE), :]
            return c
        lax.fori_loop(0, n_pages // FILL_PAGES, fill, 0)
        for lhs in (lhs_a, lhs_b):
            xg = lhs[...]
            r = _compress_mlp(xg + pe_ref[kv, 0], xg + pe_ref[kv, 1], w1_ref, w2_ref, kv, n16)
            dst.append(r[:, 0:HEAD_DIM])
            dst.append(r[:, HEAD_DIM:2 * HEAD_DIM])

    qrow = q_ref[...] * HEAD_DIM ** -0.5
    n_idx = lax.broadcasted_iota(I32, (1, n16), 1)
    valid = (n_idx * CMP_STRIDE + CMP_LEN - 1 <= pos) & (n_idx < n_cmp)
    head_row = lax.broadcasted_iota(I32, (8, 1), 0) < HPG
    ncol = map_ref.shape[1]
    blk_r = lax.broadcasted_iota(I32, (1, ncol), 1)
    blk_c = lax.broadcasted_iota(I32, (ncol, 1), 0)
    cur = pos // SLC_LEN
    forced = (blk_r == 0) | (blk_r == cur) | (blk_r == cur - 1)
    oc_parts = []
    for g in range(N_KV):
        q8 = jnp.concatenate(
            [qrow[:, (g * HPG + h) * HEAD_DIM:(g * HPG + h + 1) * HEAD_DIM] for h in range(HPG)]
            + [jnp.zeros((8 - HPG, HEAD_DIM), F32)], axis=0)
        s = _dot_nt(q8, kc[g])
        p, l = _softmax_rows(s, valid)
        pn = jnp.where(head_row, p / jnp.maximum(l, 1e-30), 0.0)
        o8 = _dot(pn, vc[g])
        oc_parts += [o8[h:h + 1, :] for h in range(HPG)]
        psum = jnp.broadcast_to(jnp.sum(pn, axis=0, keepdims=True), pn.shape)
        imp = _dot(psum, map_ref[...])[0:1, :]
        score = jnp.where((blk_r <= cur) & (blk_r < n_slc), imp + jnp.where(forced, FORCE_BONUS, 0.0), -jnp.inf)
        score_c = jnp.broadcast_to(score, (8, ncol)).T[:, 0:1]
        beats = (score_c > score) | ((score_c == score) & (blk_c < blk_r))
        rank = jnp.sum(beats.astype(F32), axis=0, keepdims=True)
        r_iota = lax.broadcasted_iota(I32, (n_sel, ncol), 0).astype(F32)
        hit = (rank == r_iota) & (score > -jnp.inf)
        idx = jnp.sum(jnp.where(hit, blk_r.astype(F32), 0.0), axis=1, keepdims=True)
        idx_ref[:, g:g + 1] = idx.astype(I32)
    oc_ref[...] = jnp.concatenate(oc_parts, axis=1)


def _nsa_sample_cmp(page_table, q3, cache_t, pe_rows, w1bd, w2bd, past_len):
    db, n_pages = page_table.shape
    page = cache_t.shape[2]
    cpp = page // CMP_STRIDE
    n16 = n_pages * cpp
    n_cmp = (past_len + 1) // CMP_STRIDE - 1
    n_slc = -(-(past_len + 1) // SLC_LEN)
    n_sel = min(N_SEL, n_slc)
    ncol = -(-n_slc // 128) * 128
    cmap = jnp.asarray(_cmp_slc_map(n16, n_cmp, n_slc, ncol))
    grid_spec = pltpu.PrefetchScalarGridSpec(
        num_scalar_prefetch=1,
        grid=(db,),
        in_specs=[
            pl.BlockSpec((None, 1, B_WIDTH), lambda b, pt: (b, 0, 0)),
            pl.BlockSpec(memory_space=pl.ANY),
            pl.BlockSpec(pe_rows.shape, lambda b, pt: (0, 0, 0, 0)),
            pl.BlockSpec(w1bd.shape, lambda b, pt: (0, 0, 0, 0)),
            pl.BlockSpec(w2bd.shape, lambda b, pt: (0, 0, 0)),
            pl.BlockSpec(cmap.shape, lambda b, pt: (0, 0)),
        ],
        out_specs=(
            pl.BlockSpec((None, 1, B_WIDTH), lambda b, pt: (b, 0, 0)),
            pl.BlockSpec((None, n_sel, N_KV), lambda b, pt: (b, 0, 0)),
        ),
        scratch_shapes=[pltpu.VMEM((2, n_pages) + cache_t.shape[1:], F32),
                        pltpu.VMEM((FILL_PAGES, 2, page, 128), F32),
                        pltpu.VMEM((n16, CMP_STRIDE * 128), F32), pltpu.VMEM((n16, CMP_STRIDE * 128), F32),
                        pltpu.SemaphoreType.DMA((2,))],
    )
    return pl.pallas_call(
        functools.partial(_nsa_sample_cmp_kernel, n_pages=n_pages, n_cmp=n_cmp, n_slc=n_slc, n_sel=n_sel, pos=past_len),
        grid_spec=grid_spec,
        out_shape=(jax.ShapeDtypeStruct((db, 1, B_WIDTH), F32), jax.ShapeDtypeStruct((db, n_sel, N_KV), I32)),
        compiler_params=_cparams(("arbitrary",), VMEM_LIMIT_BYTES),
        name="nsa_cmp_sample",
    )(page_table.reshape(-1), q3, cache_t, pe_rows, w1bd, w2bd, cmap)


def _sel_copy(cache_hbm, pt_ref, idx_ref, kbuf, sem, b, slot, j2, n_pages, n_sel, last_real):
    j = j2 // 2
    kv = j2 % 2
    g = j // n_sel
    blk = jnp.minimum(idx_ref[b * (N_KV * n_sel) + j], last_real)
    per_page = cache_hbm.shape[2] // SLC_LEN
    page = pt_ref[b * n_pages + blk // per_page]
    row0 = pl.multiple_of((kv * N_KV + g) * HEAD_DIM, HEAD_DIM)
    return pltpu.make_async_copy(cache_hbm.at[page, pl.ds(row0, HEAD_DIM), :], kbuf.at[slot, j2], sem.at[slot])


def _nsa_sample_attn_kernel(pt_ref, idx_ref, q_ref, gt_ref, oc_ref, ks_new_ref, kw_new_ref, win_ref, cache_hbm,
                            o_ref, kbuf, sem, *, n_pages, n_sel, pos):
    b = pl.program_id(0)
    nb = pl.num_programs(0)
    slot = b % 2
    n_copies = 2 * N_KV * n_sel
    page = kbuf.shape[3]
    per_page = page // SLC_LEN
    new_blk = pos // SLC_LEN
    last_real = new_blk - 1

    def start_all(bb, sl):
        def body(j, c):
            _sel_copy(cache_hbm, pt_ref, idx_ref, kbuf, sem, bb, sl, j, n_pages, n_sel, last_real).start()
            return c
        lax.fori_loop(0, n_copies, body, 0)

    @pl.when(b == 0)
    def _():
        start_all(0, 0)

    @pl.when(b + 1 < nb)
    def _():
        start_all(b + 1, 1 - slot)

    def wait_body(j, c):
        _sel_copy(cache_hbm, pt_ref, idx_ref, kbuf, sem, b, slot, j, n_pages, n_sel, last_real).wait()
        return c
    lax.fori_loop(0, n_copies, wait_body, 0)

    qrow = q_ref[...] * HEAD_DIM ** -0.5
    gt = gt_ref[...]
    oc = oc_ref[...]
    ks_new = ks_new_ref[...]
    kw_new = kw_new_ref[...]
    half = KV_WIDTH // 2
    wlen = win_ref.shape[1]
    kp_w = pos - wlen + lax.broadcasted_iota(I32, (1, wlen), 1)
    mask_w = (kp_w >= 0) & (kp_w <= pos) & (kp_w > pos - WINDOW)
    out_parts = []
    for g in range(N_KV):
        q8 = jnp.concatenate(
            [qrow[:, (g * HPG + h) * HEAD_DIM:(g * HPG + h + 1) * HEAD_DIM] for h in range(HPG)]
            + [jnp.zeros((8 - HPG, HEAD_DIM), F32)], axis=0)
        ksl = slice(g * HEAD_DIM, (g + 1) * HEAD_DIM)
        vsl = slice(half + g * HEAD_DIM, half + (g + 1) * HEAD_DIM)

        def attend(s, mask, pv, k_new, v_new, has_new):
            rb = lambda t: t.astype(BF16).astype(F32)
            s_new = jnp.sum(rb(q8) * rb(k_new), axis=-1, keepdims=True)
            sm = jnp.where(mask, s, NEG)
            m = jnp.maximum(jnp.max(sm, -1, keepdims=True), jnp.where(has_new, s_new, NEG))
            p = jnp.where(mask, jnp.exp(sm - m), 0.0)
            p_new = jnp.where(has_new, jnp.exp(s_new - m), 0.0)
            l = jnp.maximum(jnp.sum(p, -1, keepdims=True) + p_new, 1e-30)
            return pv(p / l) + rb(p_new / l) * rb(v_new)

        kt_sel = jnp.concatenate([kbuf[slot, (g * n_sel + r) * 2] for r in range(n_sel)], axis=1)
        vt_sel = jnp.concatenate([kbuf[slot, (g * n_sel + r) * 2 + 1] for r in range(n_sel)], axis=1)
        blk_ids = [idx_ref[b * (N_KV * n_sel) + g * n_sel + r] for r in range(n_sel)]
        row_of = lambda vals: jnp.concatenate([jnp.full((1, page), v, I32) for v in vals], axis=1)
        lane_blk = (lax.broadcasted_iota(I32, (1, n_sel * page), 1) & (page - 1)) // SLC_LEN
        picked = (row_of(blk_ids) <= last_real) & (lane_blk == row_of([bid % per_page for bid in blk_ids]))
        has_new = functools.reduce(jnp.logical_or, [bid == new_blk for bid in blk_ids])
        o_s = attend(_dot(q8, kt_sel), picked, lambda p: _dot_nt(p, vt_sel), ks_new[:, ksl], ks_new[:, vsl], has_new)
        o_w = attend(_dot(q8, win_ref[ksl, :]), mask_w, lambda p: _dot_nt(p, win_ref[vsl, :]),
                     kw_new[:, ksl], kw_new[:, vsl], True)
        for h in range(HPG):
            hh = g * HPG + h
            c0 = g * (N_BRANCH * HPG)
            g0 = gt[:, c0 + h:c0 + h + 1]
            g1 = gt[:, c0 + HPG + h:c0 + HPG + h + 1]
            g2 = gt[:, c0 + 2 * HPG + h:c0 + 2 * HPG + h + 1]
            out_parts.append(g0 * oc[:, hh * HEAD_DIM:(hh + 1) * HEAD_DIM] + g1 * o_s[h:h + 1, :] + g2 * o_w[h:h + 1, :])
    o_ref[...] = jnp.concatenate(out_parts, axis=1)


def _nsa_sample_attn(page_table, sel_idx, q3, gates3, oc3, ks_new3, kw_new3, win_state, cache_s, past_len):
    db, n_pages = page_table.shape
    n_sel = sel_idx.shape[-1]
    wlen = win_state.shape[2]
    row = lambda wdt: pl.BlockSpec((None, 1, wdt), lambda b, pt, ix: (b, 0, 0))
    grid_spec = pltpu.PrefetchScalarGridSpec(
        num_scalar_prefetch=2,
        grid=(db,),
        in_specs=[
            row(B_WIDTH), row(N_GATE), row(B_WIDTH), row(KV_WIDTH), row(KV_WIDTH),
            pl.BlockSpec((None, KV_WIDTH, wlen), lambda b, pt, ix: (b, 0, 0)),
            pl.BlockSpec(memory_space=pl.ANY),
        ],
        out_specs=row(B_WIDTH),
        scratch_shapes=[pltpu.VMEM((2, 2 * N_KV * n_sel, HEAD_DIM, cache_s.shape[2]), F32),
                        pltpu.SemaphoreType.DMA((2,))],
    )
    return pl.pallas_call(
        functools.partial(_nsa_sample_attn_kernel, n_pages=n_pages, n_sel=n_sel, pos=past_len),
        grid_spec=grid_spec,
        out_shape=jax.ShapeDtypeStruct((db, 1, B_WIDTH), F32),
        compiler_params=_cparams(("arbitrary",), VMEM_LIMIT_BYTES),
        name="nsa_attn_sample",
    )(page_table.reshape(-1), sel_idx.reshape(-1), q3, gates3, oc3, ks_new3, kw_new3, win_state, cache_s)


def _post_ln_router(acc, x, gate, lng, lnb, sc2, sh2, rw_t):
    x1 = _ln(ALPHA * x + gate * acc) * lng + lnb
    h2 = _modulate(x1, sc2, sh2)
    return x1, h2, _dot_nt(rw_t, h2)


def _mix_out_kernel(l0_ref, l1_ref, w_ref, x_ref, gate_ref, lng_ref, lnb_ref, sc2_ref, sh2_ref, rw_ref,
                    x1_ref, h2_ref, lg_ref, *, concat):
    if concat:
        k0 = l0_ref.shape[1]
        acc = _dot(l0_ref[...], w_ref[0:k0, :]) + _dot(l1_ref[...], w_ref[k0:, :])
    else:
        acc = _dot(l0_ref[...] * l1_ref[...], w_ref[...])
    x1, h2, lg = _post_ln_router(acc, x_ref[...], gate_ref[...], lng_ref[...], lnb_ref[...],
                                 sc2_ref[...], sh2_ref[...], rw_ref[...])
    x1_ref[...] = x1
    h2_ref[...] = h2
    lg_ref[...] = lg


def _mix_out_prompt(l0, l1, w_bf, x, gate, lng, lnb, sc2, sh2, rw, seq, concat, name):
    n, d = x.shape
    tm = 256
    tpb = seq // tm
    row = lambda i: (i, 0)
    mod = lambda i: (i // tpb, 0, 0)
    const2 = lambda i: (0, 0)
    return pl.pallas_call(
        functools.partial(_mix_out_kernel, concat=concat),
        grid=(n // tm,),
        in_specs=[
            pl.BlockSpec((tm, l0.shape[1]), row),
            pl.BlockSpec((tm, l1.shape[1]), row),
            pl.BlockSpec(w_bf.shape, const2),
            pl.BlockSpec((tm, d), row),
            pl.BlockSpec((None, 1, d), mod),
            pl.BlockSpec((1, d), const2),
            pl.BlockSpec((1, d), const2),
            pl.BlockSpec((None, 1, d), mod),
            pl.BlockSpec((None, 1, d), mod),
            pl.BlockSpec(rw.shape, const2),
        ],
        out_specs=(pl.BlockSpec((tm, d), row), pl.BlockSpec((tm, d), row),
                   pl.BlockSpec((N_EXPERTS, tm), lambda i: (0, i))),
        out_shape=(jax.ShapeDtypeStruct((n, d), F32), jax.ShapeDtypeStruct((n, d), F32),
                   jax.ShapeDtypeStruct((N_EXPERTS, n), F32)),
        compiler_params=_cparams(("parallel",), VMEM_LIMIT_BYTES),
        name=name,
    )(l0, l1, w_bf, x, gate, lng, lnb, sc2, sh2, rw)


def _top2_route(lg, rb):
    s = jax.nn.sigmoid(lg)
    sb = s + rb
    rows = [sb[e:e + 1, :] for e in range(N_EXPERTS)]
    gs = []
    for g in range(N_GROUPS):
        v = rows[g * EXPERTS_PER_GROUP:(g + 1) * EXPERTS_PER_GROUP]
        pair = [v[i] + v[j] for i in range(EXPERTS_PER_GROUP) for j in range(i + 1, EXPERTS_PER_GROUP)]
        gs.append(functools.reduce(jnp.maximum, pair))
    best, gi = gs[0], jnp.zeros(gs[0].shape, I32)
    for g in range(1, N_GROUPS):
        better = gs[g] > best
        gi = jnp.where(better, g, gi)
        best = jnp.where(better, gs[g], best)
    cand = [jnp.where(gi == e // EXPERTS_PER_GROUP, rows[e], -jnp.inf) for e in range(N_EXPERTS)]
    ids = []
    for k in range(TOP_K):
        vk = jnp.full(cand[0].shape, -jnp.inf, F32)
        ik = jnp.zeros(cand[0].shape, I32)
        for e in range(N_EXPERTS):
            c = cand[e]
            for prev in ids:
                c = jnp.where(prev == e, -jnp.inf, c)
            better = c > vk
            ik = jnp.where(better, e, ik)
            vk = jnp.where(better, c, vk)
        ids.append(ik)
    ws = [functools.reduce(jnp.add, [jnp.where(ik == e, s[e:e + 1, :], 0.0) for e in range(N_EXPERTS)]) for ik in ids]
    tot = functools.reduce(jnp.add, ws)
    return ids, [w / tot for w in ws]


def _route_kernel(lg_ref, rb_ref, w_ref, dest_ref, be_ref, nu_ref, tot_ref, run_ref, ps_ref, *, blk):
    ph = pl.program_id(0)
    i = pl.program_id(1)
    tm = lg_ref.shape[1]
    ids, ws = _top2_route(lg_ref[...], rb_ref[...])
    e_iota = lax.broadcasted_iota(I32, (N_EXPERTS, tm), 0)
    oh = [(e_iota == ik).astype(F32) for ik in ids]
    ohsum = functools.reduce(jnp.add, oh)
    tile_cnt = jnp.sum(ohsum, axis=1, keepdims=True)

    @pl.when((ph == 0) & (i == 0))
    def _():
        tot_ref[...] = jnp.zeros_like(tot_ref)

    @pl.when(ph == 0)
    def _():
        tot_ref[...] = tot_ref[...] + tile_cnt

    @pl.when((ph == 1) & (i == 0))
    def _():
        cnt = tot_ref[...]
        padded = jnp.floor((cnt + (blk - 1)) * (1.0 / blk)) * blk
        sub = lax.broadcasted_iota(I32, cnt.shape, 0)
        start = jnp.zeros_like(cnt)
        for e in range(N_EXPERTS):
            start = start + jnp.where(sub > e, padded[e:e + 1, :], 0.0)
        ps_ref[...] = start
        run_ref[...] = jnp.zeros_like(run_ref)
        pad_end = start[:, 0:1] + padded[:, 0:1]
        blk_lo = (lax.broadcasted_iota(I32, (N_EXPERTS, be_ref.shape[1]), 1) * blk).astype(F32)
        n_le = jnp.sum((pad_end <= blk_lo).astype(F32), axis=0, keepdims=True)
        be_ref[...] = jnp.minimum(n_le, N_EXPERTS - 1.0).astype(I32)
        nu_ref[...] = (jnp.max(pad_end, axis=0, keepdims=True) * (1.0 / blk) + jnp.zeros(nu_ref.shape, F32)).astype(I32)

    @pl.when(ph == 1)
    def _():
        t_r = lax.broadcasted_iota(I32, (tm, tm), 0)
        t_c = lax.broadcasted_iota(I32, (tm, tm), 1)
        before = _dot(ohsum, (t_r < t_c).astype(F32))
        base = before + run_ref[:, 0:1] + ps_ref[:, 0:1]
        for k in range(TOP_K):
            w_ref[k:k + 1, :] = ws[k]
            dest_ref[k:k + 1, :] = jnp.sum(oh[k] * base, axis=0, keepdims=True).astype(I32)
        run_ref[...] = run_ref[...] + tile_cnt


def _route_tables(logits_t, router_b, blk):
    n = logits_t.shape[1]
    tm = min(512, n)
    a = n * TOP_K
    n_blocks = -(-a // blk) + N_EXPERTS
    nb_pad = -(-n_blocks // 128) * 128
    tok_blk = lambda p, i: (0, i * p)
    const = lambda p, i: (0, 0)
    w, dest, block_e, n_used = pl.pallas_call(
        functools.partial(_route_kernel, blk=blk),
        grid=(2, n // tm),
        in_specs=[pl.BlockSpec((N_EXPERTS, tm), lambda p, i: (0, i)), pl.BlockSpec((N_EXPERTS, 1), const)],
        out_specs=(pl.BlockSpec((TOP_K, tm), tok_blk), pl.BlockSpec((TOP_K, tm), tok_blk),
                   pl.BlockSpec((1, nb_pad), const), pl.BlockSpec((1, 128), const)),
        out_shape=(jax.ShapeDtypeStruct((TOP_K, n), F32), jax.ShapeDtypeStruct((TOP_K, n), I32),
                   jax.ShapeDtypeStruct((1, nb_pad), I32), jax.ShapeDtypeStruct((1, 128), I32)),
        scratch_shapes=[pltpu.VMEM((N_EXPERTS, 128), F32)] * 3,
        compiler_params=_cparams(("arbitrary", "arbitrary")),
        name="moe_route",
    )(logits_t, router_b.reshape(N_EXPERTS, 1).astype(F32))
    slot_of = dest.T.reshape(-1)
    tok = jnp.repeat(jnp.arange(n, dtype=I32), TOP_K)
    slot_tok = jnp.zeros((n_blocks * blk,), I32).at[slot_of].set(tok)
    return slot_tok, slot_of, w.T, block_e[0, :n_blocks], n_used[0, :1], n_blocks


def _row_copy(src_hbm, row, buf, slot, r, sem):
    return pltpu.make_async_copy(src_hbm.at[pl.ds(row, 1), :], buf.at[slot, pl.ds(r, 1), :], sem.at[slot])


ROW_DMA_UNROLL = 8
MOE_BLK_PROMPT = 256


def _start_row_gather(src_hbm, row_of, buf, slot, n_rows, sem):
    unroll = math.gcd(ROW_DMA_UNROLL, n_rows)

    def body(j, c):
        for u in range(unroll):
            r = j * unroll + u
            _row_copy(src_hbm, row_of(r), buf, slot, r, sem).start(priority=u % 2)
        return c
    lax.fori_loop(0, n_rows // unroll, body, 0)


def _wait_row_gather(src_hbm, buf, slot, n_rows, sem):
    unroll = math.gcd(ROW_DMA_UNROLL, n_rows)

    def body(j, c):
        for u in range(unroll):
            _row_copy(src_hbm, 0, buf, slot, j * unroll + u, sem).wait()
        return c
    lax.fori_loop(0, n_rows // unroll, body, 0)


def _moe_kernel(be_ref, tok_ref, nu_ref, x_hbm, w1_ref, w2_ref, y_ref, xs_buf, gsem, *, blk):
    i = pl.program_id(0)
    n_used = nu_ref[0]
    slot = i % 2

    def start_gather(bi, sl):
        _start_row_gather(x_hbm, lambda r: tok_ref[bi * blk + r], xs_buf, sl, blk, gsem)

    @pl.when(i == 0)
    def _():
        start_gather(0, 0)

    @pl.when(i + 1 < n_used)
    def _():
        start_gather(i + 1, 1 - slot)

    @pl.when(i < n_used)
    def _():
        _wait_row_gather(x_hbm, xs_buf, slot, blk, gsem)
        z = _dot(xs_buf[slot], w1_ref[...])
        de = z.shape[1] // 2
        act = _silu(z[:, :de]) * z[:, de:]
        y_ref[...] = _dot(act, w2_ref[...])

    @pl.when(i >= n_used)
    def _():
        y_ref[...] = jnp.zeros_like(y_ref)


def _moe_ffn(h2, slot_tok, block_e, n_used, n_blocks, w_in_bf, w_out_bf, layer, blk, name):
    n, d = h2.shape
    de2 = w_in_bf.shape[-1]
    grid_spec = pltpu.PrefetchScalarGridSpec(
        num_scalar_prefetch=3,
        grid=(n_blocks,),
        in_specs=[
            pl.BlockSpec(memory_space=pl.ANY),
            pl.BlockSpec((None, None, d, de2), lambda i, be, tk, nu: (layer, be[i], 0, 0)),
            pl.BlockSpec((None, None, de2 // 2, d), lambda i, be, tk, nu: (layer, be[i], 0, 0)),
        ],
        out_specs=pl.BlockSpec((blk, d), lambda i, be, tk, nu: (i, 0)),
        scratch_shapes=[pltpu.VMEM((2, blk, d), F32), pltpu.SemaphoreType.DMA((2,))],
    )
    return pl.pallas_call(
        functools.partial(_moe_kernel, blk=blk),
        grid_spec=grid_spec,
        out_shape=jax.ShapeDtypeStruct((n_blocks * blk, d), F32),
        compiler_params=_cparams(("arbitrary",), VMEM_LIMIT_BYTES),
        name=name,
    )(block_e, slot_tok, n_used, h2, w_in_bf, w_out_bf)


def _combine_kernel(so_ref, x_ref, wt_ref, gate_ref, lng_ref, lnb_ref, y_hbm, o_ref, ybuf, sem, *, tm):
    i = pl.program_id(0)
    nt = pl.num_programs(0)
    slot = i % 2

    def start_gather(ti, sl):
        for k in range(TOP_K):
            _start_row_gather(y_hbm, lambda r: so_ref[(ti * tm + r) * TOP_K + k], ybuf.at[k], sl, tm, sem.at[k])

    @pl.when(i == 0)
    def _():
        start_gather(0, 0)

    @pl.when(i + 1 < nt)
    def _():
        start_gather(i + 1, 1 - slot)

    for k in range(TOP_K):
        _wait_row_gather(y_hbm, ybuf.at[k], slot, tm, sem.at[k])

    wt = wt_ref[...]
    f = wt[:, 0:1] * ybuf[0, slot] + wt[:, 1:2] * ybuf[1, slot]
    o_ref[...] = _ln(ALPHA * x_ref[...] + gate_ref[...] * f) * lng_ref[...] + lnb_ref[...]


def _moe_combine(slot_of, x1, wts, gate, lng, lnb, y, tm, tpb, name):
    n, d = x1.shape
    grid_spec = pltpu.PrefetchScalarGridSpec(
        num_scalar_prefetch=1,
        grid=(n // tm,),
        in_specs=[
            pl.BlockSpec((tm, d), lambda i, so: (i, 0)),
            pl.BlockSpec((tm, TOP_K), lambda i, so: (i, 0)),
            pl.BlockSpec((None, gate.shape[1], d), lambda i, so: (i // tpb, 0, 0)),
            pl.BlockSpec((1, d), lambda i, so: (0, 0)),
            pl.BlockSpec((1, d), lambda i, so: (0, 0)),
            pl.BlockSpec(memory_space=pl.ANY),
        ],
        out_specs=pl.BlockSpec((tm, d), lambda i, so: (i, 0)),
        scratch_shapes=[pltpu.VMEM((TOP_K, 2, tm, d), F32), pltpu.SemaphoreType.DMA((TOP_K, 2))],
    )
    return pl.pallas_call(
        functools.partial(_combine_kernel, tm=tm),
        grid_spec=grid_spec,
        out_shape=jax.ShapeDtypeStruct((n, d), F32),
        compiler_params=_cparams(("arbitrary",)),
        name=name,
    )(slot_of, x1, wts, gate, lng, lnb, y)


def _channel_sublayer(x1, h2, logits, gate, lng, lnb, router_b, w_in, w_out, layer, blk, tm, tpb, tag):
    slot_tok, slot_of, wts, block_e, n_used, n_blocks = _route_tables(logits, router_b, blk)
    y = _moe_ffn(h2, slot_tok, block_e, n_used, n_blocks, w_in, w_out, layer, blk, "moe_ffn_" + tag)
    return _moe_combine(slot_of, x1, wts, gate, lng, lnb, y, tm, tpb, "moe_combine_" + tag)


def _rec_in_kernel(x_ref, sc_ref, sh_ref, w_ref, gg_ref, xbr_ref):
    hb = _modulate(x_ref[...], sc_ref[...], sh_ref[...]).astype(BF16)
    d = gg_ref.shape[1]
    gg_ref[...] = _gelu(jnp.dot(hb, w_ref[:, 0:d], preferred_element_type=F32))
    xbr_ref[...] = jnp.dot(hb, w_ref[:, d:], preferred_element_type=F32)


def _rec_in_prompt(x, sc, sh, w_bf, seq):
    n, d = x.shape
    dr = w_bf.shape[1] // 2
    tm = 256
    tpb = seq // tm
    row = lambda i: (i, 0)
    mod = lambda i: (i // tpb, 0, 0)
    return pl.pallas_call(
        _rec_in_kernel,
        grid=(n // tm,),
        in_specs=[
            pl.BlockSpec((tm, d), row),
            pl.BlockSpec((None, 1, d), mod),
            pl.BlockSpec((None, 1, d), mod),
            pl.BlockSpec(w_bf.shape, lambda i: (0, 0), pipeline_mode=pl.Buffered(1)),
        ],
        out_specs=(pl.BlockSpec((tm, dr), row), pl.BlockSpec((tm, dr), row)),
        out_shape=(jax.ShapeDtypeStruct((n, dr), F32), jax.ShapeDtypeStruct((n, dr), F32)),
        compiler_params=_cparams(("parallel",), VMEM_LIMIT_BYTES),
        name="rec_in_prompt",
    )(x, sc, sh, w_bf)


def _log1p(y):
    w = 1.0 + y
    return jnp.where(w == 1.0, y, jnp.log(w) * (y / jnp.where(w == 1.0, 1.0, w - 1.0)))


def _expm1(x):
    u = jnp.exp(x)
    safe = (u != 1.0) & (u > 0.0)
    return jnp.where(u == 1.0, x, jnp.where(u > 0.0, (u - 1.0) * (x / jnp.where(safe, jnp.log(u), 1.0)), -1.0))


def _softplus(x):
    return jnp.maximum(x, 0.0) + _log1p(jnp.exp(-jnp.abs(x)))


def _rg_terms(xb, z, ba, bx, lam):
    bs = xb.shape[1]
    r = jax.nn.sigmoid(z[:, :bs] + ba)
    gi = jax.nn.sigmoid(z[:, bs:] + bx)
    log_a = -RG_C * r * _softplus(-lam)
    a = jnp.exp(log_a)
    return a, jnp.sqrt(-_expm1(2.0 * log_a)) * (gi * xb)


def _rec_gate_kernel(x_ref, cw_ref, cb_ref, wab_ref, ba_ref, bx_ref, lam_ref, a_ref, bt_ref, carry_ref, *, tpb, tm):
    i = pl.program_id(0)

    @pl.when(i % tpb == 0)
    def _():
        carry_ref[...] = jnp.zeros_like(carry_ref)

    x = x_ref[...]
    carry = carry_ref[...]
    d = x.shape[1]
    row8 = lax.broadcasted_iota(I32, (8, d), 0)
    xc = cb_ref[...] + x * cw_ref[CONV_W - 1:CONV_W, :]
    for k in range(1, CONV_W):
        xr = pltpu.roll(x, k, 0)
        head = jnp.where(row8 < k, pltpu.roll(carry, k, 0), xr[0:8])
        xk = jnp.concatenate([head, xr[8:]], axis=0)
        xc = xc + xk * cw_ref[CONV_W - 1 - k:CONV_W - k, :]
    carry_ref[...] = x[tm - 8:tm]
    bs = d // RNN_BLOCKS
    for nb in range(RNN_BLOCKS):
        sl = slice(nb * bs, (nb + 1) * bs)
        xb = xc[:, sl]
        a, bt = _rg_terms(xb, _dot(xb, wab_ref[nb]), ba_ref[:, sl], bx_ref[:, sl], lam_ref[:, sl])
        a_ref[:, sl] = a
        bt_ref[:, sl] = bt


def _rec_gate_prompt(xbr, cw, cb, wab, ba, bx, lam, seq):
    n, d = xbr.shape
    tm = 256
    tpb = seq // tm
    row = lambda i: (i, 0)
    const2 = lambda i: (0, 0)
    return pl.pallas_call(
        functools.partial(_rec_gate_kernel, tpb=tpb, tm=tm),
        grid=(n // tm,),
        in_specs=[
            pl.BlockSpec((tm, d), row),
            pl.BlockSpec(cw.shape, const2),
            pl.BlockSpec((1, d), const2),
            pl.BlockSpec(wab.shape, lambda i: (0, 0, 0)),
            pl.BlockSpec((1, d), const2),
            pl.BlockSpec((1, d), const2),
            pl.BlockSpec((1, d), const2),
        ],
        out_specs=(pl.BlockSpec((tm, d), row), pl.BlockSpec((tm, d), row)),
        out_shape=(jax.ShapeDtypeStruct((n, d), F32), jax.ShapeDtypeStruct((n, d), F32)),
        scratch_shapes=[pltpu.VMEM((8, d), F32)],
        compiler_params=_cparams(("arbitrary",), VMEM_LIMIT_BYTES),
        name="rec_gate_prompt",
    )(xbr, cw, cb, wab, ba, bx, lam)


def _scan_kernel(a_ref, b_ref, hs_ref, ht_ref, h_ref, *, tt):
    t = pl.program_id(1)

    @pl.when(t == 0)
    def _():
        h_ref[...] = jnp.zeros_like(h_ref)

    def body(j, h):
        h = a_ref[j] * h + b_ref[j]
        hs_ref[j] = h
        return h

    h = lax.fori_loop(0, tt, body, h_ref[...], unroll=8)
    h_ref[...] = h
    ht_ref[...] = h


def _scan_prompt(a4, b4):
    bsz, seq, s8, c8 = a4.shape
    tt = min(512, seq)
    blk = pl.BlockSpec((None, tt, s8, c8), lambda b, t: (b, t, 0, 0))
    return pl.pallas_call(
        functools.partial(_scan_kernel, tt=tt),
        grid=(bsz, seq // tt),
        in_specs=[blk, blk],
        out_specs=(blk, pl.BlockSpec((None, s8, c8), lambda b, t: (b, 0, 0))),
        out_shape=(jax.ShapeDtypeStruct(a4.shape, F32), jax.ShapeDtypeStruct((bsz, s8, c8), F32)),
        scratch_shapes=[pltpu.VMEM((s8, c8), F32)],
        compiler_params=_cparams(("parallel", "arbitrary"), VMEM_LIMIT_BYTES),
        name="rglru_scan_prompt",
    )(a4, b4)


def _rec_step_sample(z, buf0, buf1, buf2, h0, cw, cb, wa, wx, ba, bx, lam):
    d = h0.shape[1]
    gate_br = z[:, :d]
    xbr = z[:, d:]
    xc = cb + buf0 * cw[0:1] + buf1 * cw[1:2] + buf2 * cw[2:3] + xbr * cw[3:4]
    bs = d // RNN_BLOCKS
    a_parts, b_parts = [], []
    for nb in range(RNN_BLOCKS):
        sl = slice(nb * bs, (nb + 1) * bs)
        xb = xc[:, sl]
        zz = jnp.concatenate([_dot(xb, wa[nb]), _dot(xb, wx[nb])], axis=1)
        a, bt = _rg_terms(xb, zz, ba[:, sl], bx[:, sl], lam[:, sl])
        a_parts.append(a)
        b_parts.append(bt)
    h = jnp.concatenate(a_parts, axis=1) * h0 + jnp.concatenate(b_parts, axis=1)
    return _gelu(gate_br) * h, h, xbr


def kernel(x_prompt, x_sample, c_prompt, c_sample, cache_nsa_cmp, cache_nsa_slc, state_nsa_win, state_rglru_conv, state_rglru_h, page_table, ada_w, ada_b, ln_g, ln_b, ab_w_in, ab_w_out, gmlp_ln_g, gmlp_ln_b, gmlp_ws, gmlp_bs, nsa_cmp_pe, nsa_cmp_w1, nsa_cmp_w2, rec_w_in, rec_conv_w, rec_conv_b, rg_wa, rg_ba, rg_wx, rg_bx, rg_lambda, rec_w_out, router_w, router_b, moe_w_in, moe_w_out):
    bp, seq, d = x_prompt.shape
    db = x_sample.shape[0]
    assert x_sample.shape[1] == 1
    n_pool, page = cache_nsa_cmp.shape[:2]
    n_pages = page_table.shape[1]
    past_len = n_pages * page
    assert seq % 256 == 0 and past_len % SLC_LEN == 0 and page % SLC_LEN == 0 and past_len >= WINDOW
    n_p = bp * seq
    d_rnn = rec_conv_b.shape[0]

    mods = _ada_all(jnp.concatenate([c_prompt, c_sample], axis=0), ada_w, ada_b)

    def mod_rows(layer, sub):
        m = mods[layer * 2 + sub]
        parts = [m[:, j * d:(j + 1) * d] for j in range(3)]
        return [p[:bp].reshape(bp, 1, d) for p in parts], [p[bp:] for p in parts]

    lnrow = lambda a: a.reshape(1, d)

    gperm = np.array([(g * HPG + h) * N_BRANCH + br for g in range(N_KV) for br in range(N_BRANCH) for h in range(HPG)])
    w_gate = ab_w_in[:, O_G:][:, gperm]
    w_ab_bf = jnp.concatenate([ab_w_in[:, :O_G], w_gate], axis=1).astype(BF16)
    tril = jnp.tril(jnp.ones((CHUNK, CHUNK), F32))
    wtril = (gmlp_ws * tril).astype(BF16)
    bst = gmlp_bs.T
    glng = gmlp_ln_g.reshape(1, A_WIDTH)
    glnb = gmlp_ln_b.reshape(1, A_WIDTH)
    pe_rows, w1bd, w2bd = _compress_weights(nsa_cmp_pe, nsa_cmp_w1, nsa_cmp_w2)
    rw_t = router_w.T
    moe_in_bf = moe_w_in.astype(BF16)
    moe_out_bf = moe_w_out.astype(BF16)
    xp = x_prompt.reshape(n_p, d)
    xs = x_sample.reshape(db, d)

    (sh_p, sc_p, g_p), (sh_s, sc_s, g_s) = mod_rows(0, 0)
    (sh2_p, sc2_p, g2_p), (sh2_s, sc2_s, g2_s) = mod_rows(0, 1)
    cos_p, sin_p = _rope_tables(np.arange(seq))
    a_p, q_p, kvc_p, kvs_p, kvw_p, kvt_p, gt_p = _ab_proj_prompt(
        xp, sc_p, sh_p, w_ab_bf, cos_p, sin_p, glng, glnb, wtril, bst, seq)
    kcv_p = _compress_prompt(kvc_p.reshape(bp, seq // CMP_STRIDE, CMP_STRIDE * KV_WIDTH), pe_rows, w1bd, w2bd)
    o_p = _nsa_prompt(q_p, gt_p, kcv_p, kvt_p, bp, seq)
    w_out_bf = ab_w_out.astype(BF16)
    x1_p, h2_p, lg_p = _mix_out_prompt(a_p, o_p, w_out_bf, xp, g_p, lnrow(ln_g[0, 0]), lnrow(ln_b[0, 0]),
                                       sc2_p, sh2_p, rw_t, seq, True, "mix_out_l0_prompt")

    z_s = _small_mm(_modulate, [xs, sc_s, sh_s], w_ab_bf, O_G, 512, "ab_proj_sample")
    zg_s = _small_mm(_modulate, [xs, sc_s, sh_s], w_gate, N_GATE, N_GATE, "ab_gate_sample")
    cos_s, sin_s = _rope_tables(np.full((db,), past_len))
    ws0 = jnp.repeat(gmlp_ws[:, 0, 0], A_GROUP_DIM).reshape(1, A_WIDTH)
    bs0 = jnp.repeat(gmlp_bs[:, 0], A_GROUP_DIM).reshape(1, A_WIDTH)
    sds = lambda *s: jax.ShapeDtypeStruct(s, F32)
    a_s, v_s, q_s, kvc_s, kvs_s, kvw_s, gt_s = _vmem_call(
        _ab_post_sample,
        (sds(db, A_WIDTH), sds(db, A_WIDTH), sds(db, B_WIDTH), sds(db, KV_WIDTH), sds(db, KV_WIDTH),
         sds(db, KV_WIDTH), sds(db, N_GATE)),
        (z_s, zg_s, cos_s, sin_s, glng, glnb, ws0, bs0), "ab_post_sample")
    q3 = q_s.reshape(db, 1, B_WIDTH)
    page_t = lambda c: jnp.transpose(c, (0, 2, 3, 4, 1)).reshape(n_pool, KV_WIDTH, page)
    oc3, sel_idx = _nsa_sample_cmp(page_table, q3, page_t(cache_nsa_cmp), pe_rows, w1bd, w2bd, past_len)
    sel_idx = jnp.transpose(sel_idx, (0, 2, 1))
    o_s = _nsa_sample_attn(page_table, sel_idx, q3, gt_s.reshape(db, 1, N_GATE), oc3,
                           kvs_s.reshape(db, 1, KV_WIDTH), kvw_s.reshape(db, 1, KV_WIDTH),
                           jnp.transpose(state_nsa_win, (0, 2, 3, 4, 1)).reshape(db, KV_WIDTH, -1),
                           page_t(cache_nsa_slc),
                           past_len).reshape(db, B_WIDTH)
    f_s = _small_mm(lambda a, o: jnp.concatenate([a, o], axis=1), [a_s, o_s], w_out_bf, d, 512, "mix_out_l0_sample")
    x1_s, h2_s, lg_s = _vmem_call(
        _post_ln_router, (sds(db, d), sds(db, d), sds(N_EXPERTS, db)),
        (f_s, xs, g_s, lnrow(ln_g[0, 0]), lnrow(ln_b[0, 0]), sc2_s, sh2_s, rw_t), "post_l0_sample")

    x2_p = _channel_sublayer(x1_p, h2_p, lg_p, g2_p, lnrow(ln_g[0, 1]), lnrow(ln_b[0, 1]), router_b,
                             moe_in_bf, moe_out_bf, 0,MOE_BLK_PROMPT, 128, seq // 128, "l0_prompt")
    x2_s = _channel_sublayer(x1_s, h2_s, lg_s, g2_s.reshape(1, db, d), lnrow(ln_g[0, 1]), lnrow(ln_b[0, 1]), router_b,
                             moe_in_bf, moe_out_bf, 0,32, db, 1, "l0_sample")

    (sh_p, sc_p, g_p), (sh_s, sc_s, g_s) = mod_rows(1, 0)
    (sh2_p, sc2_p, g2_p), (sh2_s, sc2_s, g2_s) = mod_rows(1, 1)
    rec_in_bf = rec_w_in.astype(BF16)
    rec_out_bf = rec_w_out.astype(BF16)
    gg_p, xbr_p = _rec_in_prompt(x2_p, sc_p, sh_p, rec_in_bf, seq)
    wab = jnp.concatenate([rg_wa, rg_wx], axis=2).astype(BF16)
    row_r = lambda a: a.reshape(1, d_rnn)
    a_t, b_t = _rec_gate_prompt(xbr_p, rec_conv_w, row_r(rec_conv_b), wab, row_r(rg_ba), row_r(rg_bx),
                                row_r(rg_lambda), seq)
    hs4, ht = _scan_prompt(a_t.reshape(bp, seq, 8, d_rnn // 8), b_t.reshape(bp, seq, 8, d_rnn // 8))
    x3_p, h4_p, lg_p = _mix_out_prompt(gg_p, hs4.reshape(n_p, d_rnn), rec_out_bf, x2_p, g_p,
                                       lnrow(ln_g[1, 0]), lnrow(ln_b[1, 0]), sc2_p, sh2_p, rw_t, seq, False,
                                       "mix_out_l1_prompt")
    conv_p = xbr_p.reshape(bp, seq, d_rnn)[:, seq - (CONV_W - 1):]
    h_p = ht.reshape(bp, d_rnn)

    zr_s = _small_mm(_modulate, [x2_s, sc_s, sh_s], rec_in_bf, 2 * d_rnn, 512, "rec_in_sample")
    y_s, h_s, xbr_s = _vmem_call(
        _rec_step_sample, (sds(db, d_rnn), sds(db, d_rnn), sds(db, d_rnn)),
        (zr_s, state_rglru_conv[:, 0], state_rglru_conv[:, 1], state_rglru_conv[:, 2], state_rglru_h,
         rec_conv_w, row_r(rec_conv_b), rg_wa, rg_wx, row_r(rg_ba), row_r(rg_bx), row_r(rg_lambda)),
        "rec_step_sample")
    f_s = _small_mm(lambda y: y, [y_s], rec_out_bf, d, 512, "mix_out_l1_sample")
    x3_s, h4_s, lg_s = _vmem_call(
        _post_ln_router, (sds(db, d), sds(db, d), sds(N_EXPERTS, db)),
        (f_s, x2_s, g_s, lnrow(ln_g[1, 0]), lnrow(ln_b[1, 0]), sc2_s, sh2_s, rw_t), "post_l1_sample")
    conv_s = jnp.concatenate([state_rglru_conv[:, 1:], xbr_s[:, None, :]], axis=1)

    y_p = _channel_sublayer(x3_p, h4_p, lg_p, g2_p, lnrow(ln_g[1, 1]), lnrow(ln_b[1, 1]), router_b,
                            moe_in_bf, moe_out_bf, 1,MOE_BLK_PROMPT, 128, seq // 128, "l1_prompt")
    y_s = _channel_sublayer(x3_s, h4_s, lg_s, g2_s.reshape(1, db, d), lnrow(ln_g[1, 1]), lnrow(ln_b[1, 1]), router_b,
                            moe_in_bf, moe_out_bf, 1,32, db, 1, "l1_sample")

    kv5 = lambda a, b_: a.reshape(b_, -1, 2, N_KV, HEAD_DIM)
    keep = min(WINDOW, seq)
    win_p = kv5(kvw_p, bp)[:, seq - keep:]
    kw_full = jnp.concatenate([state_nsa_win, kv5(kvw_s, db)], axis=1)
    win_s = kw_full[:, kw_full.shape[1] - min(WINDOW, kw_full.shape[1]):]
    return (y_p.reshape(bp, seq, d), y_s.reshape(db, 1, d), kv5(kvc_p, bp), kv5(kvc_s, db), kv5(kvs_p, bp),
            kv5(kvs_s, db), win_p, win_s, v_s.reshape(db, 1, A_WIDTH), conv_p, conv_s, h_p, h_s)
```

```python
import functools
import math

import numpy as np
import jax
import jax.numpy as jnp
from jax import lax
from jax.experimental import pallas as pl
from jax.experimental.pallas import tpu as pltpu

F32 = jnp.float32
BF16 = jnp.bfloat16
I32 = jnp.int32

A_GROUPS = 8
A_GROUP_DIM = 128
A_WIDTH = A_GROUPS * A_GROUP_DIM
CHUNK = 128
N_HEADS = 16
N_KV = 4
HEAD_DIM = 64
HPG = N_HEADS // N_KV
B_WIDTH = N_HEADS * HEAD_DIM
KV_WIDTH = 2 * N_KV * HEAD_DIM
N_BRANCH = 3
CMP_LEN = 32
CMP_STRIDE = 16
CMP_HID = 2 * HEAD_DIM
SLC_LEN = 64
N_SEL = 16
WINDOW = 512
Q_BLOCK = 128
FORCE_BONUS = 1e4
ROPE_THETA = 10000.0
RNN_BLOCKS = 16
CONV_W = 4
RG_C = 8.0
N_EXPERTS = 16
N_GROUPS = 4
EXPERTS_PER_GROUP = N_EXPERTS // N_GROUPS
TOP_K = 2
DEPTH = 2
ALPHA = (2 * DEPTH) ** 0.25
LN_EPS = 1e-5
NEG = -1e30

O_U = 0
O_V = A_WIDTH
O_Q = 2 * A_WIDTH
O_KV = O_Q + B_WIDTH
O_G = O_KV + 3 * KV_WIDTH
N_GATE = N_BRANCH * N_HEADS

VMEM_LIMIT_BYTES = 56 * 1024 * 1024


def _cparams(sem, vmem=None):
    return pltpu.CompilerParams(dimension_semantics=sem, vmem_limit_bytes=vmem)


def _dot(a, b):
    return jnp.dot(a.astype(BF16), b.astype(BF16), preferred_element_type=F32)


def _dot_nt(a, b):
    dn = (((1,), (1,)), ((), ()))
    return lax.dot_general(a.astype(BF16), b.astype(BF16), dn, preferred_element_type=F32)


def _ln(x):
    mu = jnp.mean(x, -1, keepdims=True)
    xc = x - mu
    var = jnp.mean(xc * xc, -1, keepdims=True)
    return xc * lax.rsqrt(var + LN_EPS)


def _silu(x):
    return x * jax.nn.sigmoid(x)


def _gelu(x):
    return jax.nn.gelu(x, approximate=True)


def _rope(x, cos2, sin2):
    w = x.shape[1]
    rep = w // 128
    cos = jnp.concatenate([cos2] * rep, axis=1) if rep > 1 else cos2
    sin = jnp.concatenate([sin2] * rep, axis=1) if rep > 1 else sin2
    lane = lax.broadcasted_iota(I32, x.shape, 1)
    first = (lane & (HEAD_DIM - 1)) < HEAD_DIM // 2
    rot = jnp.where(first, pltpu.roll(x, w - HEAD_DIM // 2, 1), pltpu.roll(x, HEAD_DIM // 2, 1))
    return x * cos + rot * sin


def _rope_tables(pos):
    half = HEAD_DIM // 2
    inv = ROPE_THETA ** (-np.arange(half, dtype=np.float64) / half)
    ang = np.asarray(pos, np.float64)[:, None] * inv[None, :]
    cos = np.tile(np.cos(ang), (1, 4))
    sin = np.tile(np.concatenate([-np.sin(ang), np.sin(ang)], axis=1), (1, 2))
    return jnp.asarray(cos, F32), jnp.asarray(sin, F32)


def _softmax_rows(s, mask):
    sm = jnp.where(mask, s, NEG)
    m = jnp.max(sm, -1, keepdims=True)
    p = jnp.where(mask, jnp.exp(sm - m), 0.0)
    return p, jnp.sum(p, -1, keepdims=True)


def _ada_kernel(c_ref, w_ref, b_ref, o_ref):
    o_ref[...] = _dot(_silu(c_ref[...]), w_ref[...]) + b_ref[...]


def _ada_all(c_all, ada_w, ada_b):
    r, d = c_all.shape
    n_mod = ada_w.shape[0] * ada_w.shape[1]
    d3 = ada_w.shape[-1]
    tn = 512
    return pl.pallas_call(
        _ada_kernel,
        grid=(n_mod, d3 // tn),
        in_specs=[
            pl.BlockSpec((r, d), lambda l, j: (0, 0)),
            pl.BlockSpec((None, d, tn), lambda l, j: (l, 0, j)),
            pl.BlockSpec((None, 1, tn), lambda l, j: (l, 0, j)),
        ],
        out_specs=pl.BlockSpec((None, r, tn), lambda l, j: (l, 0, j)),
        out_shape=jax.ShapeDtypeStruct((n_mod, r, d3), F32),
        compiler_params=_cparams(("parallel", "parallel")),
        name="ada_mod",
    )(c_all, ada_w.reshape(n_mod, d, d3), ada_b.reshape(n_mod, 1, d3))


def _small_mm_kernel(*refs, n_x, pre):
    xs = [r[...] for r in refs[:n_x]]
    w_ref, o_ref = refs[n_x], refs[n_x + 1]
    o_ref[...] = _dot(pre(*xs), w_ref[...])


def _small_mm(pre, xs, w, n_out, tn, name):
    m = xs[0].shape[0]
    k = w.shape[0]
    in_specs = [pl.BlockSpec(x.shape, lambda j, nd=x.ndim: (0,) * nd) for x in xs]
    in_specs.append(pl.BlockSpec((k, tn), lambda j: (0, j)))
    return pl.pallas_call(
        functools.partial(_small_mm_kernel, n_x=len(xs), pre=pre),
        grid=(n_out // tn,),
        in_specs=in_specs,
        out_specs=pl.BlockSpec((m, tn), lambda j: (0, j)),
        out_shape=jax.ShapeDtypeStruct((m, n_out), F32),
        compiler_params=_cparams(("parallel",)),
        name=name,
    )(*xs, w)


def _vmem_call(fn, out_shapes, args, name):
    n_in = len(args)

    def kern(*refs):
        res = fn(*[r[...] for r in refs[:n_in]])
        for o, v in zip(refs[n_in:], res):
            o[...] = v

    return pl.pallas_call(kern, out_shape=out_shapes, name=name)(*args)


def _modulate(x, sc, sh):
    return x * (1.0 + sc) + sh


def _ab_proj_kernel(x_ref, sc_ref, sh_ref, w_ref, cos_ref, sin_ref, lng_ref, lnb_ref, wtril_ref, bst_ref,
                    a_ref, q_ref, kvc_ref, kvs_ref, kvw_ref, kvt_ref, gt_ref, *, tm):
    hb = _modulate(x_ref[...], sc_ref[...], sh_ref[...]).astype(BF16)

    def proj(lo, hi):
        return jnp.dot(hb, w_ref[:, lo:hi], preferred_element_type=F32)

    cos2 = cos_ref[...]
    sin2 = sin_ref[...]
    zu = proj(O_U, O_V)
    zv = proj(O_V, O_Q)
    for g in range(A_GROUPS):
        sl = slice(g * A_GROUP_DIM, (g + 1) * A_GROUP_DIM)
        vg = _ln(zv[:, sl]) * lng_ref[:, sl] + lnb_ref[:, sl]
        for c in range(tm // CHUNK):
            rs = slice(c * CHUNK, (c + 1) * CHUNK)
            mix = jnp.dot(wtril_ref[g], vg[rs].astype(BF16), preferred_element_type=F32) + bst_ref[:, g:g + 1]
            a_ref[rs, sl] = zu[rs, sl] * mix
    q_ref[...] = _rope(proj(O_Q, O_KV), cos2, sin2)
    half = KV_WIDTH // 2
    for br, ref in enumerate((kvc_ref, kvs_ref, kvw_ref)):
        lo = O_KV + br * KV_WIDTH
        k = _rope(proj(lo, lo + half), cos2, sin2)
        v = proj(lo + half, lo + KV_WIDTH)
        ref[:, 0:half] = k
        ref[:, half:KV_WIDTH] = v
        for g in range(N_KV):
            hs = slice(g * HEAD_DIM, (g + 1) * HEAD_DIM)
            kvt_ref[br, 0, g] = k[:, hs]
            kvt_ref[br, 1, g] = v[:, hs]
    zg = jax.nn.sigmoid(proj(O_G, O_G + N_GATE))
    per_g = N_GATE // N_KV
    for g in range(N_KV):
        gt_ref[g] = zg[:, g * per_g:(g + 1) * per_g]


def _ab_proj_prompt(x, sc, sh, w_bf, cos2, sin2, lng, lnb, wtril, bst, seq):
    n, d = x.shape
    tm = 256
    tpb = seq // tm
    n_in = w_bf.shape[1]
    row = lambda i: (i, 0)
    mod = lambda i: (i // tpb, 0, 0)
    const2 = lambda i: (0, 0)
    pos = lambda i: (i % tpb, 0)
    out_shape = (
        jax.ShapeDtypeStruct((n, A_WIDTH), F32),
        jax.ShapeDtypeStruct((n, B_WIDTH), F32),
        jax.ShapeDtypeStruct((n, KV_WIDTH), F32),
        jax.ShapeDtypeStruct((n, KV_WIDTH), F32),
        jax.ShapeDtypeStruct((n, KV_WIDTH), F32),
        jax.ShapeDtypeStruct((3, 2, N_KV, n, HEAD_DIM), F32),
        jax.ShapeDtypeStruct((N_KV, n, N_GATE // N_KV), F32),
    )
    return pl.pallas_call(
        functools.partial(_ab_proj_kernel, tm=tm),
        grid=(n // tm,),
        in_specs=[
            pl.BlockSpec((tm, d), row),
            pl.BlockSpec((None, 1, d), mod),
            pl.BlockSpec((None, 1, d), mod),
            pl.BlockSpec((d, n_in), const2, pipeline_mode=pl.Buffered(1)),
            pl.BlockSpec((tm, 128), pos),
            pl.BlockSpec((tm, 128), pos),
            pl.BlockSpec((1, A_WIDTH), const2),
            pl.BlockSpec((1, A_WIDTH), const2),
            pl.BlockSpec((A_GROUPS, CHUNK, CHUNK), lambda i: (0, 0, 0)),
            pl.BlockSpec((CHUNK, A_GROUPS), const2),
        ],
        out_specs=(
            pl.BlockSpec((tm, A_WIDTH), row),
            pl.BlockSpec((tm, B_WIDTH), row),
            pl.BlockSpec((tm, KV_WIDTH), row),
            pl.BlockSpec((tm, KV_WIDTH), row),
            pl.BlockSpec((tm, KV_WIDTH), row),
            pl.BlockSpec((3, 2, N_KV, tm, HEAD_DIM), lambda i: (0, 0, 0, i, 0)),
            pl.BlockSpec((N_KV, tm, N_GATE // N_KV), lambda i: (0, i, 0)),
        ),
        out_shape=out_shape,
        compiler_params=_cparams(("parallel",), VMEM_LIMIT_BYTES),
        name="ab_proj_prompt",
    )(x, sc, sh, w_bf, cos2, sin2, lng, lnb, wtril, bst)


def _ab_post_sample(z, zg, cos2, sin2, lng, lnb, ws0, bs0):
    zu = z[:, O_U:O_V]
    zv = z[:, O_V:O_Q]
    vs = []
    for g in range(A_GROUPS):
        sl = slice(g * A_GROUP_DIM, (g + 1) * A_GROUP_DIM)
        vs.append(_ln(zv[:, sl]) * lng[:, sl] + lnb[:, sl])
    v = jnp.concatenate(vs, axis=1)
    a = zu * (ws0 * v + bs0)
    q = _rope(z[:, O_Q:O_KV], cos2, sin2)
    half = KV_WIDTH // 2
    kvs = []
    for br in range(3):
        lo = O_KV + br * KV_WIDTH
        k = _rope(z[:, lo:lo + half], cos2, sin2)
        kvs.append(jnp.concatenate([k, z[:, lo + half:lo + KV_WIDTH]], axis=1))
    return a, v, q, kvs[0], kvs[1], kvs[2], jax.nn.sigmoid(zg)


def _compress_mlp(lhs_lo, lhs_hi, w1_ref, w2_ref, kv, n):
    lo = _dot(lhs_lo, w1_ref[kv, 0])
    hi = _dot(lhs_hi, w1_ref[kv, 1])
    hsum = lo + pltpu.roll(hi, n - 1, 0)
    return _dot(_gelu(hsum), w2_ref[kv])


def _compress_prompt_kernel(x_ref, pe_ref, w1_ref, w2_ref, o_ref, *, n16):
    for kv in range(2):
        for gp in range(N_KV // 2):
            base = kv * (KV_WIDTH // 2) + gp * 128
            xg = jnp.concatenate(
                [x_ref[:, s * KV_WIDTH + base:s * KV_WIDTH + base + 128] for s in range(CMP_STRIDE)], axis=1)
            r = _compress_mlp(xg + pe_ref[kv, 0], xg + pe_ref[kv, 1], w1_ref, w2_ref, kv, n16)
            o_ref[kv, 2 * gp] = r[:, 0:HEAD_DIM]
            o_ref[kv, 2 * gp + 1] = r[:, HEAD_DIM:2 * HEAD_DIM]


def _compress_prompt(kvc3, pe_rows, w1bd, w2bd):
    b, n16, wid = kvc3.shape
    return pl.pallas_call(
        functools.partial(_compress_prompt_kernel, n16=n16),
        grid=(b,),
        in_specs=[
            pl.BlockSpec((None, n16, wid), lambda i: (i, 0, 0)),
            pl.BlockSpec(pe_rows.shape, lambda i: (0, 0, 0, 0)),
            pl.BlockSpec(w1bd.shape, lambda i: (0, 0, 0, 0)),
            pl.BlockSpec(w2bd.shape, lambda i: (0, 0, 0)),
        ],
        out_specs=pl.BlockSpec((None, 2, N_KV, n16, HEAD_DIM), lambda i: (i, 0, 0, 0, 0)),
        out_shape=jax.ShapeDtypeStruct((b, 2, N_KV, n16, HEAD_DIM), F32),
        compiler_params=_cparams(("parallel",), VMEM_LIMIT_BYTES),
        name="nsa_compress_prompt",
    )(kvc3, pe_rows, w1bd, w2bd)


def _compress_weights(pe, w1, w2):
    eye2 = jnp.eye(2, dtype=F32)
    w1r = w1.reshape(2, 2, CMP_STRIDE, HEAD_DIM, CMP_HID)
    w1bd = jnp.einsum('khsdc,gG->khsgdGc', w1r, eye2).reshape(2, 2, CMP_STRIDE * 128, 2 * CMP_HID).astype(BF16)
    pe_rows = jnp.broadcast_to(pe.reshape(2, 2, CMP_STRIDE, 1, HEAD_DIM), (2, 2, CMP_STRIDE, 2, HEAD_DIM))
    pe_rows = pe_rows.reshape(2, 2, 1, CMP_STRIDE * 128)
    w2bd = jnp.einsum('kcd,gG->kgcGd', w2, eye2).reshape(2, 2 * CMP_HID, 2 * HEAD_DIM).astype(BF16)
    return pe_rows, w1bd, w2bd


def _cmp_slc_map(n_rows, n_cmp, n_slc, n_cols):
    cs = np.arange(n_rows)[:, None] * CMP_STRIDE
    ss = np.arange(n_cols)[None, :] * SLC_LEN
    m = (cs < ss + SLC_LEN) & (cs + CMP_LEN > ss)
    m &= (np.arange(n_rows)[:, None] < n_cmp) & (np.arange(n_cols)[None, :] < n_slc)
    return m.astype(np.float32)


def _nsa_prompt_kernel(q_ref, gt_ref, kc_ref, vc_ref, ks_ref, vs_ref, kw_ref, vw_ref, mapt_ref, e_ref,
                       *rest, n_cmp, n_slc, n_sel, kt_len, n_cast):
    o_ref, selb_ref = rest[n_cast], rest[-1]
    for src, dst in zip(rest[:n_cast], rest[n_cast + 1:2 * n_cast + 1]):
        dst[...] = src[...].astype(BF16)
    i = pl.program_id(2)
    q0 = i * Q_BLOCK
    rows = HPG * Q_BLOCK
    tile4 = lambda x: jnp.concatenate([x] * HPG, axis=0)
    qb = q_ref[...]
    q4 = jnp.concatenate([qb[:, h * HEAD_DIM:(h + 1) * HEAD_DIM] for h in range(HPG)], axis=0)
    q4 = (q4 * HEAD_DIM ** -0.5).astype(BF16)
    t_pos = q0 + lax.broadcasted_iota(I32, (Q_BLOCK, 1), 0)

    n16 = kc_ref.shape[0]
    n_idx = lax.broadcasted_iota(I32, (1, n16), 1)
    bias_c = jnp.where((n_idx * CMP_STRIDE + CMP_LEN - 1 <= t_pos) & (n_idx < n_cmp), 0.0, NEG)
    s = _dot_nt(q4, kc_ref[...]) + tile4(bias_c)
    p = jnp.exp(s - jnp.max(s, -1, keepdims=True))
    l = jnp.sum(p, -1, keepdims=True)
    pn = p * jnp.where((tile4(t_pos) >= CMP_LEN - 1) & (n_cmp > 0), 1.0 / l, 0.0)
    o_c = _dot(pn, vc_ref[...])
    psum = pn[0:Q_BLOCK]
    for h in range(1, HPG):
        psum = psum + pn[h * Q_BLOCK:(h + 1) * Q_BLOCK]

    imp_t = _dot_nt(mapt_ref[...], psum)
    blk_t = lax.broadcasted_iota(I32, (n_slc, Q_BLOCK), 0)
    cur_t = (q0 + lax.broadcasted_iota(I32, (n_slc, Q_BLOCK), 1)) // SLC_LEN
    forced = (blk_t == 0) | (blk_t == cur_t) | (blk_t == cur_t - 1)
    score = jnp.where(blk_t <= cur_t, imp_t + jnp.where(forced, FORCE_BONUS, 0.0), -jnp.inf)
    rank = jnp.zeros((n_slc, Q_BLOCK), I32)
    for j in range(n_slc):
        r = score[j:j + 1, :]
        rank = rank + ((r > score) | ((r == score) & (blk_t > j))).astype(I32)
    sel_t = ((rank < n_sel) & (score > -jnp.inf)).astype(F32)
    selb_ref[...] = (_dot(sel_t.T, e_ref[...]) - 1.0) * (-NEG)

    def sel_tile(kt, carry, causal):
        m, l, acc = carry
        k0 = pl.multiple_of(kt * kt_len, kt_len)
        bias = selb_ref[:, pl.ds(k0, kt_len)]
        if causal:
            kp = k0 + lax.broadcasted_iota(I32, (1, kt_len), 1)
            bias = bias + jnp.where(kp <= t_pos, 0.0, NEG)
        s = _dot_nt(q4, ks_ref[pl.ds(k0, kt_len), :]) + tile4(bias)
        m_new = jnp.maximum(m, jnp.max(s, -1, keepdims=True))
        alpha = jnp.exp(m - m_new)
        p = jnp.exp(s - m_new)
        l = alpha * l + jnp.sum(p, -1, keepdims=True)
        acc = alpha * acc + _dot(p, vs_ref[pl.ds(k0, kt_len), :])
        return m_new, l, acc

    last_kt = (q0 + Q_BLOCK - 1) // kt_len
    init = (jnp.full((rows, 1), NEG, F32), jnp.zeros((rows, 1), F32), jnp.zeros((rows, HEAD_DIM), F32))
    carry = lax.fori_loop(0, last_kt, functools.partial(sel_tile, causal=False), init)
    _, l, acc = sel_tile(last_kt, carry, True)
    o_s = acc * (1.0 / l)

    n_wt = WINDOW // Q_BLOCK + 1
    c_idx = lax.broadcasted_iota(I32, (Q_BLOCK, Q_BLOCK), 1)
    r_idx = lax.broadcasted_iota(I32, (Q_BLOCK, Q_BLOCK), 0)
    s_parts, v_parts = [], []
    for j in range(n_wt):
        ks_j = q0 - WINDOW + j * Q_BLOCK
        ld = pl.multiple_of(jnp.maximum(ks_j, 0), Q_BLOCK)
        off = jnp.where(ks_j >= 0, 0.0, NEG)
        sj = _dot_nt(q4, kw_ref[pl.ds(ld, Q_BLOCK), :])
        if j == 0:
            sj = sj + tile4(jnp.where(c_idx > r_idx, 0.0, NEG) + off)
        elif j == n_wt - 1:
            sj = sj + tile4(jnp.where(c_idx <= r_idx, 0.0, NEG))
        else:
            sj = sj + off
        s_parts.append(sj)
        v_parts.append(vw_ref[pl.ds(ld, Q_BLOCK), :])
    s = jnp.concatenate(s_parts, axis=1)
    p = jnp.exp(s - jnp.max(s, -1, keepdims=True))
    o_w = _dot(p, jnp.concatenate(v_parts, axis=0)) * (1.0 / jnp.sum(p, -1, keepdims=True))

    gt = gt_ref[...]

    def gcol(br):
        return jnp.concatenate([gt[:, br * HPG + h:br * HPG + h + 1] for h in range(HPG)], axis=0)

    o = gcol(0) * o_c + gcol(1) * o_s + gcol(2) * o_w
    o_ref[...] = jnp.concatenate([o[h * Q_BLOCK:(h + 1) * Q_BLOCK] for h in range(HPG)], axis=1)


def _nsa_prompt(q, gates, kcv, kvt, batch, seq, to_bf16):
    n = q.shape[0]
    nq = seq // Q_BLOCK
    n_steps = batch * N_KV * nq
    step = lambda b, g, i: ((b * N_KV + g) * nq + i, 0)
    cast_specs = [pl.BlockSpec((w.shape[0] // n_steps, w.shape[1]), step) for w in to_bf16]
    n16 = kcv.shape[3]
    n_cmp = n16 - 1
    n_slc = -(-seq // SLC_LEN)
    n_sel = min(N_SEL, n_slc)
    kt_len = min(512, seq)
    mapt = jnp.asarray(_cmp_slc_map(n16, n_cmp, n_slc, n_slc).T)
    expand = jnp.asarray((np.arange(n_slc)[:, None] == np.arange(seq)[None, :] // SLC_LEN).astype(np.float32), BF16)
    per_g = N_GATE // N_KV
    qrow = lambda b, g, i: (b * nq + i, g)
    kvspec = lambda br, kv: pl.BlockSpec((None, None, None, seq, HEAD_DIM), lambda b, g, i: (br, kv, g, b, 0))
    return pl.pallas_call(
        functools.partial(_nsa_prompt_kernel, n_cmp=n_cmp, n_slc=n_slc, n_sel=n_sel, kt_len=kt_len,
                          n_cast=len(to_bf16)),
        grid=(batch, N_KV, nq),
        in_specs=[
            pl.BlockSpec((Q_BLOCK, HPG * HEAD_DIM), qrow),
            pl.BlockSpec((None, Q_BLOCK, per_g), lambda b, g, i: (g, b * nq + i, 0)),
            pl.BlockSpec((None, None, None, n16, HEAD_DIM), lambda b, g, i: (b, 0, g, 0, 0)),
            pl.BlockSpec((None, None, None, n16, HEAD_DIM), lambda b, g, i: (b, 1, g, 0, 0)),
            kvspec(1, 0), kvspec(1, 1), kvspec(2, 0), kvspec(2, 1),
            pl.BlockSpec(mapt.shape, lambda b, g, i: (0, 0)),
            pl.BlockSpec(expand.shape, lambda b, g, i: (0, 0)),
        ] + cast_specs,
        out_specs=[pl.BlockSpec((Q_BLOCK, HPG * HEAD_DIM), qrow)] + cast_specs,
        out_shape=[jax.ShapeDtypeStruct((n, B_WIDTH), F32)] + [jax.ShapeDtypeStruct(w.shape, BF16) for w in to_bf16],
        scratch_shapes=[pltpu.VMEM((Q_BLOCK, seq), F32)],
        compiler_params=_cparams(("parallel", "parallel", "arbitrary"), VMEM_LIMIT_BYTES),
        name="nsa_attn_prompt",
    )(q, gates, kcv, kcv, kvt, kvt, kvt, kvt, mapt, expand, *to_bf16)


FILL_PAGES = 8


def _page_copy(cache_hbm, pt_ref, xbuf, sem, b, slot, p, n_pages):
    return pltpu.make_async_copy(cache_hbm.at[pt_ref[b * n_pages + p]], xbuf.at[slot, p], sem.at[slot])


def _nsa_sample_cmp_kernel(pt_ref, q_ref, cache_hbm, pe_ref, w1_ref, w2_ref, map_ref,
                           oc_ref, idx_ref, xbuf, xrow, lhs_a, lhs_b, sem, *, n_pages, n_cmp, n_slc, n_sel, pos):
    b = pl.program_id(0)
    nb = pl.num_programs(0)
    slot = b % 2
    n16 = lhs_a.shape[0]
    cpp = n16 // n_pages

    def start_all(bb, sl):
        def body(p, c):
            _page_copy(cache_hbm, pt_ref, xbuf, sem, bb, sl, p, n_pages).start()
            return c
        lax.fori_loop(0, n_pages, body, 0)

    @pl.when(b == 0)
    def _():
        start_all(0, 0)

    @pl.when(b + 1 < nb)
    def _():
        start_all(b + 1, 1 - slot)

    def wait_body(p, c):
        _page_copy(cache_hbm, pt_ref, xbuf, sem, b, slot, p, n_pages).wait()
        return c
    lax.fori_loop(0, n_pages, wait_body, 0)

    kc, vc = [], []
    wkv = N_KV * HEAD_DIM
    for kv, dst in ((0, kc), (1, vc)):
        def fill(j, c):
            for u in range(FILL_PAGES):
                p = j * FILL_PAGES + u
                r0 = pl.multiple_of(p * cpp, cpp)
                for gp, lhs in enumerate((lhs_a, lhs_b)):
                    xrow[u, gp] = xbuf[slot, p, kv * wkv + gp * 128:kv * wkv + (gp + 1) * 128, :].T
                    for s in range(CMP_STRIDE):
                        lhs[pl.ds(r0, cpp), s * 128:(s + 1) * 128] = xrow[u, gp, pl.ds(s, cpp, stride=CMP_STRIDE), :]
            return c
        lax.fori_loop(0, n_pages // FILL_PAGES, fill, 0)
        for lhs in (lhs_a, lhs_b):
            xg = lhs[...]
            r = _compress_mlp(xg + pe_ref[kv, 0], xg + pe_ref[kv, 1], w1_ref, w2_ref, kv, n16)
            dst.append(r[:, 0:HEAD_DIM])
            dst.append(r[:, HEAD_DIM:2 * HEAD_DIM])

    qrow = q_ref[...] * HEAD_DIM ** -0.5
    n_idx = lax.broadcasted_iota(I32, (1, n16), 1)
    valid = (n_idx * CMP_STRIDE + CMP_LEN - 1 <= pos) & (n_idx < n_cmp)
    head_row = lax.broadcasted_iota(I32, (8, 1), 0) < HPG
    ncol = map_ref.shape[1]
    blk_r = lax.broadcasted_iota(I32, (1, ncol), 1)
    blk_c = lax.broadcasted_iota(I32, (ncol, 1), 0)
    cur = pos // SLC_LEN
    forced = (blk_r == 0) | (blk_r == cur) | (blk_r == cur - 1)
    oc_parts = []
    for g in range(N_KV):
        q8 = jnp.concatenate(
            [qrow[:, (g * HPG + h) * HEAD_DIM:(g * HPG + h + 1) * HEAD_DIM] for h in range(HPG)]
            + [jnp.zeros((8 - HPG, HEAD_DIM), F32)], axis=0)
        s = _dot_nt(q8, kc[g])
        p, l = _softmax_rows(s, valid)
        pn = jnp.where(head_row, p / jnp.maximum(l, 1e-30), 0.0)
        o8 = _dot(pn, vc[g])
        oc_parts += [o8[h:h + 1, :] for h in range(HPG)]
        psum = jnp.broadcast_to(jnp.sum(pn, axis=0, keepdims=True), pn.shape)
        imp = _dot(psum, map_ref[...])[0:1, :]
        score = jnp.where((blk_r <= cur) & (blk_r < n_slc), imp + jnp.where(forced, FORCE_BONUS, 0.0), -jnp.inf)
        score_c = jnp.broadcast_to(score, (8, ncol)).T[:, 0:1]
        beats = (score_c > score) | ((score_c == score) & (blk_c < blk_r))
        rank = jnp.sum(beats.astype(F32), axis=0, keepdims=True)
        r_iota = lax.broadcasted_iota(I32, (n_sel, ncol), 0).astype(F32)
        hit = (rank == r_iota) & (score > -jnp.inf)
        idx = jnp.sum(jnp.where(hit, blk_r.astype(F32), 0.0), axis=1, keepdims=True)
        idx_ref[:, g:g + 1] = idx.astype(I32)
    oc_ref[...] = jnp.concatenate(oc_parts, axis=1)


def _nsa_sample_cmp(page_table, q3, cache_t, pe_rows, w1bd, w2bd, past_len):
    db, n_pages = page_table.shape
    page = cache_t.shape[2]
    cpp = page // CMP_STRIDE
    n16 = n_pages * cpp
    n_cmp = (past_len + 1) // CMP_STRIDE - 1
    n_slc = -(-(past_len + 1) // SLC_LEN)
    n_sel = min(N_SEL, n_slc)
    ncol = -(-n_slc // 128) * 128
    cmap = jnp.asarray(_cmp_slc_map(n16, n_cmp, n_slc, ncol))
    grid_spec = pltpu.PrefetchScalarGridSpec(
        num_scalar_prefetch=1,
        grid=(db,),
        in_specs=[
            pl.BlockSpec((None, 1, B_WIDTH), lambda b, pt: (b, 0, 0)),
            pl.BlockSpec(memory_space=pl.ANY),
            pl.BlockSpec(pe_rows.shape, lambda b, pt: (0, 0, 0, 0)),
            pl.BlockSpec(w1bd.shape, lambda b, pt: (0, 0, 0, 0)),
            pl.BlockSpec(w2bd.shape, lambda b, pt: (0, 0, 0)),
            pl.BlockSpec(cmap.shape, lambda b, pt: (0, 0)),
        ],
        out_specs=(
            pl.BlockSpec((None, 1, B_WIDTH), lambda b, pt: (b, 0, 0)),
            pl.BlockSpec((None, n_sel, N_KV), lambda b, pt: (b, 0, 0)),
        ),
        scratch_shapes=[pltpu.VMEM((2, n_pages) + cache_t.shape[1:], F32),
                        pltpu.VMEM((FILL_PAGES, 2, page, 128), F32),
                        pltpu.VMEM((n16, CMP_STRIDE * 128), F32), pltpu.VMEM((n16, CMP_STRIDE * 128), F32),
                        pltpu.SemaphoreType.DMA((2,))],
    )
    return pl.pallas_call(
        functools.partial(_nsa_sample_cmp_kernel, n_pages=n_pages, n_cmp=n_cmp, n_slc=n_slc, n_sel=n_sel, pos=past_len),
        grid_spec=grid_spec,
        out_shape=(jax.ShapeDtypeStruct((db, 1, B_WIDTH), F32), jax.ShapeDtypeStruct((db, n_sel, N_KV), I32)),
        compiler_params=_cparams(("arbitrary",), VMEM_LIMIT_BYTES),
        name="nsa_cmp_sample",
    )(page_table.reshape(-1), q3, cache_t, pe_rows, w1bd, w2bd, cmap)


def _sel_copy(cache_hbm, pt_ref, idx_ref, kbuf, sem, b, slot, j2, n_pages, n_sel, last_real):
    j = j2 // 2
    kv = j2 % 2
    g = j // n_sel
    blk = jnp.minimum(idx_ref[b * (N_KV * n_sel) + j], last_real)
    per_page = cache_hbm.shape[2] // SLC_LEN
    page = pt_ref[b * n_pages + blk // per_page]
    row0 = pl.multiple_of((kv * N_KV + g) * HEAD_DIM, HEAD_DIM)
    return pltpu.make_async_copy(cache_hbm.at[page, pl.ds(row0, HEAD_DIM), :], kbuf.at[slot, j2], sem.at[slot])


def _nsa_sample_attn_kernel(pt_ref, idx_ref, q_ref, gt_ref, oc_ref, ks_new_ref, kw_new_ref, win_ref, cache_hbm,
                            o_ref, kbuf, sem, *, n_pages, n_sel, pos):
    b = pl.program_id(0)
    nb = pl.num_programs(0)
    slot = b % 2
    n_copies = 2 * N_KV * n_sel
    page = kbuf.shape[3]
    per_page = page // SLC_LEN
    new_blk = pos // SLC_LEN
    last_real = new_blk - 1

    def start_all(bb, sl):
        def body(j, c):
            _sel_copy(cache_hbm, pt_ref, idx_ref, kbuf, sem, bb, sl, j, n_pages, n_sel, last_real).start()
            return c
        lax.fori_loop(0, n_copies, body, 0)

    @pl.when(b == 0)
    def _():
        start_all(0, 0)

    @pl.when(b + 1 < nb)
    def _():
        start_all(b + 1, 1 - slot)

    def wait_body(j, c):
        _sel_copy(cache_hbm, pt_ref, idx_ref, kbuf, sem, b, slot, j, n_pages, n_sel, last_real).wait()
        return c
    lax.fori_loop(0, n_copies, wait_body, 0)

    qrow = q_ref[...] * HEAD_DIM ** -0.5
    gt = gt_ref[...]
    oc = oc_ref[...]
    ks_new = ks_new_ref[...]
    kw_new = kw_new_ref[...]
    half = KV_WIDTH // 2
    wlen = win_ref.shape[1]
    kp_w = pos - wlen + lax.broadcasted_iota(I32, (1, wlen), 1)
    mask_w = (kp_w >= 0) & (kp_w <= pos) & (kp_w > pos - WINDOW)
    out_parts = []
    for g in range(N_KV):
        q8 = jnp.concatenate(
            [qrow[:, (g * HPG + h) * HEAD_DIM:(g * HPG + h + 1) * HEAD_DIM] for h in range(HPG)]
            + [jnp.zeros((8 - HPG, HEAD_DIM), F32)], axis=0)
        ksl = slice(g * HEAD_DIM, (g + 1) * HEAD_DIM)
        vsl = slice(half + g * HEAD_DIM, half + (g + 1) * HEAD_DIM)

        def attend(s, mask, pv, k_new, v_new, has_new):
            rb = lambda t: t.astype(BF16).astype(F32)
            s_new = jnp.sum(rb(q8) * rb(k_new), axis=-1, keepdims=True)
            sm = jnp.where(mask, s, NEG)
            m = jnp.maximum(jnp.max(sm, -1, keepdims=True), jnp.where(has_new, s_new, NEG))
            p = jnp.where(mask, jnp.exp(sm - m), 0.0)
            p_new = jnp.where(has_new, jnp.exp(s_new - m), 0.0)
            l = jnp.maximum(jnp.sum(p, -1, keepdims=True) + p_new, 1e-30)
            return pv(p / l) + rb(p_new / l) * rb(v_new)

        kt_sel = jnp.concatenate([kbuf[slot, (g * n_sel + r) * 2] for r in range(n_sel)], axis=1)
        vt_sel = jnp.concatenate([kbuf[slot, (g * n_sel + r) * 2 + 1] for r in range(n_sel)], axis=1)
        blk_ids = [idx_ref[b * (N_KV * n_sel) + g * n_sel + r] for r in range(n_sel)]
        row_of = lambda vals: jnp.concatenate([jnp.full((1, page), v, I32) for v in vals], axis=1)
        lane_blk = (lax.broadcasted_iota(I32, (1, n_sel * page), 1) & (page - 1)) // SLC_LEN
        picked = (row_of(blk_ids) <= last_real) & (lane_blk == row_of([bid % per_page for bid in blk_ids]))
        has_new = functools.reduce(jnp.logical_or, [bid == new_blk for bid in blk_ids])
        o_s = attend(_dot(q8, kt_sel), picked, lambda p: _dot_nt(p, vt_sel), ks_new[:, ksl], ks_new[:, vsl], has_new)
        o_w = attend(_dot(q8, win_ref[ksl, :]), mask_w, lambda p: _dot_nt(p, win_ref[vsl, :]),
                     kw_new[:, ksl], kw_new[:, vsl], True)
        for h in range(HPG):
            hh = g * HPG + h
            c0 = g * (N_BRANCH * HPG)
            g0 = gt[:, c0 + h:c0 + h + 1]
            g1 = gt[:, c0 + HPG + h:c0 + HPG + h + 1]
            g2 = gt[:, c0 + 2 * HPG + h:c0 + 2 * HPG + h + 1]
            out_parts.append(g0 * oc[:, hh * HEAD_DIM:(hh + 1) * HEAD_DIM] + g1 * o_s[h:h + 1, :] + g2 * o_w[h:h + 1, :])
    o_ref[...] = jnp.concatenate(out_parts, axis=1)


def _nsa_sample_attn(page_table, sel_idx, q3, gates3, oc3, ks_new3, kw_new3, win_state, cache_s, past_len):
    db, n_pages = page_table.shape
    n_sel = sel_idx.shape[-1]
    wlen = win_state.shape[2]
    row = lambda wdt: pl.BlockSpec((None, 1, wdt), lambda b, pt, ix: (b, 0, 0))
    grid_spec = pltpu.PrefetchScalarGridSpec(
        num_scalar_prefetch=2,
        grid=(db,),
        in_specs=[
            row(B_WIDTH), row(N_GATE), row(B_WIDTH), row(KV_WIDTH), row(KV_WIDTH),
            pl.BlockSpec((None, KV_WIDTH, wlen), lambda b, pt, ix: (b, 0, 0)),
            pl.BlockSpec(memory_space=pl.ANY),
        ],
        out_specs=row(B_WIDTH),
        scratch_shapes=[pltpu.VMEM((2, 2 * N_KV * n_sel, HEAD_DIM, cache_s.shape[2]), F32),
                        pltpu.SemaphoreType.DMA((2,))],
    )
    return pl.pallas_call(
        functools.partial(_nsa_sample_attn_kernel, n_pages=n_pages, n_sel=n_sel, pos=past_len),
        grid_spec=grid_spec,
        out_shape=jax.ShapeDtypeStruct((db, 1, B_WIDTH), F32),
        compiler_params=_cparams(("arbitrary",), VMEM_LIMIT_BYTES),
        name="nsa_attn_sample",
    )(page_table.reshape(-1), sel_idx.reshape(-1), q3, gates3, oc3, ks_new3, kw_new3, win_state, cache_s)


def _post_ln_router(acc, x, gate, lng, lnb, sc2, sh2, rw_t):
    x1 = _ln(ALPHA * x + gate * acc) * lng + lnb
    h2 = _modulate(x1, sc2, sh2)
    return x1, h2, _dot_nt(rw_t, h2)


def _mix_out_kernel(l0_ref, l1_ref, w_ref, x_ref, gate_ref, lng_ref, lnb_ref, sc2_ref, sh2_ref, rw_ref,
                    x1_ref, h2_ref, lg_ref, *, concat):
    if concat:
        k0 = l0_ref.shape[1]
        acc = _dot(l0_ref[...], w_ref[0:k0, :]) + _dot(l1_ref[...], w_ref[k0:, :])
    else:
        acc = _dot(l0_ref[...] * l1_ref[...], w_ref[...])
    x1, h2, lg = _post_ln_router(acc, x_ref[...], gate_ref[...], lng_ref[...], lnb_ref[...],
                                 sc2_ref[...], sh2_ref[...], rw_ref[...])
    x1_ref[...] = x1
    h2_ref[...] = h2
    lg_ref[...] = lg


def _mix_out_prompt(l0, l1, w_bf, x, gate, lng, lnb, sc2, sh2, rw, seq, concat, name):
    n, d = x.shape
    tm = 512
    tpb = seq // tm
    row = lambda i: (i, 0)
    mod = lambda i: (i // tpb, 0, 0)
    const2 = lambda i: (0, 0)
    return pl.pallas_call(
        functools.partial(_mix_out_kernel, concat=concat),
        grid=(n // tm,),
        in_specs=[
            pl.BlockSpec((tm, l0.shape[1]), row),
            pl.BlockSpec((tm, l1.shape[1]), row),
            pl.BlockSpec(w_bf.shape, const2, pipeline_mode=pl.Buffered(1)),
            pl.BlockSpec((tm, d), row),
            pl.BlockSpec((None, 1, d), mod),
            pl.BlockSpec((1, d), const2),
            pl.BlockSpec((1, d), const2),
            pl.BlockSpec((None, 1, d), mod),
            pl.BlockSpec((None, 1, d), mod),
            pl.BlockSpec(rw.shape, const2),
        ],
        out_specs=(pl.BlockSpec((tm, d), row), pl.BlockSpec((tm, d), row),
                   pl.BlockSpec((N_EXPERTS, tm), lambda i: (0, i))),
        out_shape=(jax.ShapeDtypeStruct((n, d), F32), jax.ShapeDtypeStruct((n, d), F32),
                   jax.ShapeDtypeStruct((N_EXPERTS, n), F32)),
        compiler_params=_cparams(("parallel",), VMEM_LIMIT_BYTES),
        name=name,
    )(l0, l1, w_bf, x, gate, lng, lnb, sc2, sh2, rw)


def _top2_route(lg, rb):
    s = jax.nn.sigmoid(lg)
    sb = s + rb
    rows = [sb[e:e + 1, :] for e in range(N_EXPERTS)]
    gs = []
    for g in range(N_GROUPS):
        v = rows[g * EXPERTS_PER_GROUP:(g + 1) * EXPERTS_PER_GROUP]
        pair = [v[i] + v[j] for i in range(EXPERTS_PER_GROUP) for j in range(i + 1, EXPERTS_PER_GROUP)]
        gs.append(functools.reduce(jnp.maximum, pair))
    best, gi = gs[0], jnp.zeros(gs[0].shape, I32)
    for g in range(1, N_GROUPS):
        better = gs[g] > best
        gi = jnp.where(better, g, gi)
        best = jnp.where(better, gs[g], best)
    cand = [jnp.where(gi == e // EXPERTS_PER_GROUP, rows[e], -jnp.inf) for e in range(N_EXPERTS)]
    ids = []
    for k in range(TOP_K):
        vk = jnp.full(cand[0].shape, -jnp.inf, F32)
        ik = jnp.zeros(cand[0].shape, I32)
        for e in range(N_EXPERTS):
            c = cand[e]
            for prev in ids:
                c = jnp.where(prev == e, -jnp.inf, c)
            better = c > vk
            ik = jnp.where(better, e, ik)
            vk = jnp.where(better, c, vk)
        ids.append(ik)
    ws = [functools.reduce(jnp.add, [jnp.where(ik == e, s[e:e + 1, :], 0.0) for e in range(N_EXPERTS)]) for ik in ids]
    tot = functools.reduce(jnp.add, ws)
    return ids, [w / tot for w in ws]


def _route_kernel(lg_ref, rb_ref, w_ref, dest_ref, be_ref, nu_ref, tot_ref, run_ref, ps_ref, *, blk):
    ph = pl.program_id(0)
    i = pl.program_id(1)
    tm = lg_ref.shape[1]
    ids, ws = _top2_route(lg_ref[...], rb_ref[...])
    e_iota = lax.broadcasted_iota(I32, (N_EXPERTS, tm), 0)
    oh = [(e_iota == ik).astype(F32) for ik in ids]
    ohsum = functools.reduce(jnp.add, oh)
    tile_cnt = jnp.sum(ohsum, axis=1, keepdims=True)

    @pl.when((ph == 0) & (i == 0))
    def _():
        tot_ref[...] = jnp.zeros_like(tot_ref)

    @pl.when(ph == 0)
    def _():
        tot_ref[...] = tot_ref[...] + tile_cnt

    @pl.when((ph == 1) & (i == 0))
    def _():
        cnt = tot_ref[...]
        padded = jnp.floor((cnt + (blk - 1)) * (1.0 / blk)) * blk
        sub = lax.broadcasted_iota(I32, cnt.shape, 0)
        start = jnp.zeros_like(cnt)
        for e in range(N_EXPERTS):
            start = start + jnp.where(sub > e, padded[e:e + 1, :], 0.0)
        ps_ref[...] = start
        run_ref[...] = jnp.zeros_like(run_ref)
        pad_end = start[:, 0:1] + padded[:, 0:1]
        blk_lo = (lax.broadcasted_iota(I32, (N_EXPERTS, be_ref.shape[1]), 1) * blk).astype(F32)
        n_le = jnp.sum((pad_end <= blk_lo).astype(F32), axis=0, keepdims=True)
        be_ref[...] = jnp.minimum(n_le, N_EXPERTS - 1.0).astype(I32)
        nu_ref[...] = (jnp.max(pad_end, axis=0, keepdims=True) * (1.0 / blk) + jnp.zeros(nu_ref.shape, F32)).astype(I32)

    @pl.when(ph == 1)
    def _():
        t_r = lax.broadcasted_iota(I32, (tm, tm), 0)
        t_c = lax.broadcasted_iota(I32, (tm, tm), 1)
        before = _dot(ohsum, (t_r < t_c).astype(F32))
        base = before + run_ref[:, 0:1] + ps_ref[:, 0:1]
        for k in range(TOP_K):
            w_ref[k:k + 1, :] = ws[k]
            dest_ref[k:k + 1, :] = jnp.sum(oh[k] * base, axis=0, keepdims=True).astype(I32)
        run_ref[...] = run_ref[...] + tile_cnt


def _route_tables(logits_t, router_b, blk):
    n = logits_t.shape[1]
    tm = min(512, n)
    a = n * TOP_K
    n_blocks = -(-a // blk) + N_EXPERTS
    nb_pad = -(-n_blocks // 128) * 128
    tok_blk = lambda p, i: (0, i * p)
    const = lambda p, i: (0, 0)
    w, dest, block_e, n_used = pl.pallas_call(
        functools.partial(_route_kernel, blk=blk),
        grid=(2, n // tm),
        in_specs=[pl.BlockSpec((N_EXPERTS, tm), lambda p, i: (0, i)), pl.BlockSpec((N_EXPERTS, 1), const)],
        out_specs=(pl.BlockSpec((TOP_K, tm), tok_blk), pl.BlockSpec((TOP_K, tm), tok_blk),
                   pl.BlockSpec((1, nb_pad), const), pl.BlockSpec((1, 128), const)),
        out_shape=(jax.ShapeDtypeStruct((TOP_K, n), F32), jax.ShapeDtypeStruct((TOP_K, n), I32),
                   jax.ShapeDtypeStruct((1, nb_pad), I32), jax.ShapeDtypeStruct((1, 128), I32)),
        scratch_shapes=[pltpu.VMEM((N_EXPERTS, 128), F32)] * 3,
        compiler_params=_cparams(("arbitrary", "arbitrary")),
        name="moe_route",
    )(logits_t, router_b.reshape(N_EXPERTS, 1).astype(F32))
    slot_of = dest.T.reshape(-1)
    tok = jnp.repeat(jnp.arange(n, dtype=I32), TOP_K)
    slot_tok = jnp.zeros((n_blocks * blk,), I32).at[slot_of].set(tok)
    return slot_tok, slot_of, w.T, block_e[0, :n_blocks], n_used[0, :1], n_blocks


def _row_copy(src_hbm, row, buf, slot, r, sem):
    return pltpu.make_async_copy(src_hbm.at[pl.ds(row, 1), :], buf.at[slot, pl.ds(r, 1), :], sem.at[slot])


ROW_DMA_UNROLL = 8
MOE_BLK_PROMPT = 256


def _start_row_gather(src_hbm, row_of, buf, slot, n_rows, sem):
    unroll = math.gcd(ROW_DMA_UNROLL, n_rows)

    def body(j, c):
        for u in range(unroll):
            r = j * unroll + u
            _row_copy(src_hbm, row_of(r), buf, slot, r, sem).start(priority=u % 2)
        return c
    lax.fori_loop(0, n_rows // unroll, body, 0)


def _wait_row_gather(src_hbm, buf, slot, n_rows, sem):
    unroll = math.gcd(ROW_DMA_UNROLL, n_rows)

    def body(j, c):
        for u in range(unroll):
            _row_copy(src_hbm, 0, buf, slot, j * unroll + u, sem).wait()
        return c
    lax.fori_loop(0, n_rows // unroll, body, 0)


def _moe_kernel(be_ref, tok_ref, nu_ref, x_hbm, w1_ref, w2_ref, y_ref, xs_buf, gsem, *, blk):
    i = pl.program_id(0)
    n_used = nu_ref[0]
    slot = i % 2

    def start_gather(bi, sl):
        _start_row_gather(x_hbm, lambda r: tok_ref[bi * blk + r], xs_buf, sl, blk, gsem)

    @pl.when(i == 0)
    def _():
        start_gather(0, 0)

    @pl.when(i + 1 < n_used)
    def _():
        start_gather(i + 1, 1 - slot)

    @pl.when(i < n_used)
    def _():
        _wait_row_gather(x_hbm, xs_buf, slot, blk, gsem)
        z = _dot(xs_buf[slot], w1_ref[...])
        de = z.shape[1] // 2
        act = _silu(z[:, :de]) * z[:, de:]
        y_ref[...] = _dot(act, w2_ref[...])

    @pl.when(i >= n_used)
    def _():
        y_ref[...] = jnp.zeros_like(y_ref)


def _moe_ffn(h2, slot_tok, block_e, n_used, n_blocks, w_in_bf, w_out_bf, layer, blk, name):
    n, d = h2.shape
    de2 = w_in_bf.shape[-1]
    grid_spec = pltpu.PrefetchScalarGridSpec(
        num_scalar_prefetch=3,
        grid=(n_blocks,),
        in_specs=[
            pl.BlockSpec(memory_space=pl.ANY),
            pl.BlockSpec((None, None, d, de2), lambda i, be, tk, nu: (layer, be[i], 0, 0)),
            pl.BlockSpec((None, None, de2 // 2, d), lambda i, be, tk, nu: (layer, be[i], 0, 0)),
        ],
        out_specs=pl.BlockSpec((blk, d), lambda i, be, tk, nu: (i, 0)),
        scratch_shapes=[pltpu.VMEM((2, blk, d), F32), pltpu.SemaphoreType.DMA((2,))],
    )
    return pl.pallas_call(
        functools.partial(_moe_kernel, blk=blk),
        grid_spec=grid_spec,
        out_shape=jax.ShapeDtypeStruct((n_blocks * blk, d), F32),
        compiler_params=_cparams(("arbitrary",), VMEM_LIMIT_BYTES),
        name=name,
    )(block_e, slot_tok, n_used, h2, w_in_bf, w_out_bf)


def _combine_kernel(so_ref, x_ref, wt_ref, gate_ref, lng_ref, lnb_ref, y_hbm, o_ref, ybuf, sem, *, tm):
    i = pl.program_id(0)
    nt = pl.num_programs(0)
    slot = i % 2

    def start_gather(ti, sl):
        for k in range(TOP_K):
            _start_row_gather(y_hbm, lambda r: so_ref[(ti * tm + r) * TOP_K + k], ybuf.at[k], sl, tm, sem.at[k])

    @pl.when(i == 0)
    def _():
        start_gather(0, 0)

    @pl.when(i + 1 < nt)
    def _():
        start_gather(i + 1, 1 - slot)

    for k in range(TOP_K):
        _wait_row_gather(y_hbm, ybuf.at[k], slot, tm, sem.at[k])

    wt = wt_ref[...]
    f = wt[:, 0:1] * ybuf[0, slot] + wt[:, 1:2] * ybuf[1, slot]
    o_ref[...] = _ln(ALPHA * x_ref[...] + gate_ref[...] * f) * lng_ref[...] + lnb_ref[...]


def _moe_combine(slot_of, x1, wts, gate, lng, lnb, y, tm, tpb, name):
    n, d = x1.shape
    grid_spec = pltpu.PrefetchScalarGridSpec(
        num_scalar_prefetch=1,
        grid=(n // tm,),
        in_specs=[
            pl.BlockSpec((tm, d), lambda i, so: (i, 0)),
            pl.BlockSpec((tm, TOP_K), lambda i, so: (i, 0)),
            pl.BlockSpec((None, gate.shape[1], d), lambda i, so: (i // tpb, 0, 0)),
            pl.BlockSpec((1, d), lambda i, so: (0, 0)),
            pl.BlockSpec((1, d), lambda i, so: (0, 0)),
            pl.BlockSpec(memory_space=pl.ANY),
        ],
        out_specs=pl.BlockSpec((tm, d), lambda i, so: (i, 0)),
        scratch_shapes=[pltpu.VMEM((TOP_K, 2, tm, d), F32), pltpu.SemaphoreType.DMA((TOP_K, 2))],
    )
    return pl.pallas_call(
        functools.partial(_combine_kernel, tm=tm),
        grid_spec=grid_spec,
        out_shape=jax.ShapeDtypeStruct((n, d), F32),
        compiler_params=_cparams(("arbitrary",)),
        name=name,
    )(slot_of, x1, wts, gate, lng, lnb, y)


def _channel_sublayer(x1, h2, logits, gate, lng, lnb, router_b, w_in, w_out, layer, blk, tm, tpb, tag):
    slot_tok, slot_of, wts, block_e, n_used, n_blocks = _route_tables(logits, router_b, blk)
    y = _moe_ffn(h2, slot_tok, block_e, n_used, n_blocks, w_in, w_out, layer, blk, "moe_ffn_" + tag)
    return _moe_combine(slot_of, x1, wts, gate, lng, lnb, y, tm, tpb, "moe_combine_" + tag)


def _rec_in_kernel(x_ref, sc_ref, sh_ref, w_ref, gg_ref, xbr_ref):
    hb = _modulate(x_ref[...], sc_ref[...], sh_ref[...]).astype(BF16)
    d = gg_ref.shape[1]
    gg_ref[...] = _gelu(jnp.dot(hb, w_ref[:, 0:d], preferred_element_type=F32))
    xbr_ref[...] = jnp.dot(hb, w_ref[:, d:], preferred_element_type=F32)


def _rec_in_prompt(x, sc, sh, w_bf, seq):
    n, d = x.shape
    dr = w_bf.shape[1] // 2
    tm = 512
    tpb = seq // tm
    row = lambda i: (i, 0)
    mod = lambda i: (i // tpb, 0, 0)
    return pl.pallas_call(
        _rec_in_kernel,
        grid=(n // tm,),
        in_specs=[
            pl.BlockSpec((tm, d), row),
            pl.BlockSpec((None, 1, d), mod),
            pl.BlockSpec((None, 1, d), mod),
            pl.BlockSpec(w_bf.shape, lambda i: (0, 0), pipeline_mode=pl.Buffered(1)),
        ],
        out_specs=(pl.BlockSpec((tm, dr), row), pl.BlockSpec((tm, dr), row)),
        out_shape=(jax.ShapeDtypeStruct((n, dr), F32), jax.ShapeDtypeStruct((n, dr), F32)),
        compiler_params=_cparams(("parallel",), VMEM_LIMIT_BYTES),
        name="rec_in_prompt",
    )(x, sc, sh, w_bf)


def _log1p(y):
    w = 1.0 + y
    return jnp.where(w == 1.0, y, jnp.log(w) * (y / jnp.where(w == 1.0, 1.0, w - 1.0)))


def _expm1(x):
    u = jnp.exp(x)
    safe = (u != 1.0) & (u > 0.0)
    return jnp.where(u == 1.0, x, jnp.where(u > 0.0, (u - 1.0) * (x / jnp.where(safe, jnp.log(u), 1.0)), -1.0))


def _softplus(x):
    return jnp.maximum(x, 0.0) + _log1p(jnp.exp(-jnp.abs(x)))


def _rg_terms(xb, z, ba, bx, lam):
    bs = xb.shape[1]
    r = jax.nn.sigmoid(z[:, :bs] + ba)
    gi = jax.nn.sigmoid(z[:, bs:] + bx)
    log_a = -RG_C * r * _softplus(-lam)
    a = jnp.exp(log_a)
    return a, jnp.sqrt(-_expm1(2.0 * log_a)) * (gi * xb)


def _rec_gate_kernel(x_ref, cw_ref, cb_ref, wab_ref, ba_ref, bx_ref, lam_ref, a_ref, bt_ref, carry_ref, *, tpb, tm):
    i = pl.program_id(0)

    @pl.when(i % tpb == 0)
    def _():
        carry_ref[...] = jnp.zeros_like(carry_ref)

    x = x_ref[...]
    carry = carry_ref[...]
    d = x.shape[1]
    row8 = lax.broadcasted_iota(I32, (8, d), 0)
    xc = cb_ref[...] + x * cw_ref[CONV_W - 1:CONV_W, :]
    for k in range(1, CONV_W):
        xr = pltpu.roll(x, k, 0)
        head = jnp.where(row8 < k, pltpu.roll(carry, k, 0), xr[0:8])
        xk = jnp.concatenate([head, xr[8:]], axis=0)
        xc = xc + xk * cw_ref[CONV_W - 1 - k:CONV_W - k, :]
    carry_ref[...] = x[tm - 8:tm]
    bs = d // RNN_BLOCKS
    for nb in range(RNN_BLOCKS):
        sl = slice(nb * bs, (nb + 1) * bs)
        xb = xc[:, sl]
        a, bt = _rg_terms(xb, _dot(xb, wab_ref[nb]), ba_ref[:, sl], bx_ref[:, sl], lam_ref[:, sl])
        a_ref[:, sl] = a
        bt_ref[:, sl] = bt


def _rec_gate_prompt(xbr, cw, cb, wab, ba, bx, lam, seq):
    n, d = xbr.shape
    tm = 256
    tpb = seq // tm
    row = lambda i: (i, 0)
    const2 = lambda i: (0, 0)
    return pl.pallas_call(
        functools.partial(_rec_gate_kernel, tpb=tpb, tm=tm),
        grid=(n // tm,),
        in_specs=[
            pl.BlockSpec((tm, d), row),
            pl.BlockSpec(cw.shape, const2),
            pl.BlockSpec((1, d), const2),
            pl.BlockSpec(wab.shape, lambda i: (0, 0, 0)),
            pl.BlockSpec((1, d), const2),
            pl.BlockSpec((1, d), const2),
            pl.BlockSpec((1, d), const2),
        ],
        out_specs=(pl.BlockSpec((tm, d), row), pl.BlockSpec((tm, d), row)),
        out_shape=(jax.ShapeDtypeStruct((n, d), F32), jax.ShapeDtypeStruct((n, d), F32)),
        scratch_shapes=[pltpu.VMEM((8, d), F32)],
        compiler_params=_cparams(("arbitrary",), VMEM_LIMIT_BYTES),
        name="rec_gate_prompt",
    )(xbr, cw, cb, wab, ba, bx, lam)


def _scan_kernel(a_ref, b_ref, hs_ref, ht_ref, h_ref, *, tt):
    t = pl.program_id(1)

    @pl.when(t == 0)
    def _():
        h_ref[...] = jnp.zeros_like(h_ref)

    def body(j, h):
        h = a_ref[j] * h + b_ref[j]
        hs_ref[j] = h
        return h

    h = lax.fori_loop(0, tt, body, h_ref[...], unroll=8)
    h_ref[...] = h
    ht_ref[...] = h


def _scan_prompt(a4, b4):
    bsz, seq, s8, c8 = a4.shape
    tt = min(512, seq)
    blk = pl.BlockSpec((None, tt, s8, c8), lambda b, t: (b, t, 0, 0))
    return pl.pallas_call(
        functools.partial(_scan_kernel, tt=tt),
        grid=(bsz, seq // tt),
        in_specs=[blk, blk],
        out_specs=(blk, pl.BlockSpec((None, s8, c8), lambda b, t: (b, 0, 0))),
        out_shape=(jax.ShapeDtypeStruct(a4.shape, F32), jax.ShapeDtypeStruct((bsz, s8, c8), F32)),
        scratch_shapes=[pltpu.VMEM((s8, c8), F32)],
        compiler_params=_cparams(("parallel", "arbitrary"), VMEM_LIMIT_BYTES),
        name="rglru_scan_prompt",
    )(a4, b4)


def _rec_step_sample(z, buf0, buf1, buf2, h0, cw, cb, wa, wx, ba, bx, lam):
    d = h0.shape[1]
    gate_br = z[:, :d]
    xbr = z[:, d:]
    xc = cb + buf0 * cw[0:1] + buf1 * cw[1:2] + buf2 * cw[2:3] + xbr * cw[3:4]
    bs = d // RNN_BLOCKS
    a_parts, b_parts = [], []
    for nb in range(RNN_BLOCKS):
        sl = slice(nb * bs, (nb + 1) * bs)
        xb = xc[:, sl]
        zz = jnp.concatenate([_dot(xb, wa[nb]), _dot(xb, wx[nb])], axis=1)
        a, bt = _rg_terms(xb, zz, ba[:, sl], bx[:, sl], lam[:, sl])
        a_parts.append(a)
        b_parts.append(bt)
    h = jnp.concatenate(a_parts, axis=1) * h0 + jnp.concatenate(b_parts, axis=1)
    return _gelu(gate_br) * h, h, xbr


def kernel(x_prompt, x_sample, c_prompt, c_sample, cache_nsa_cmp, cache_nsa_slc, state_nsa_win, state_rglru_conv, state_rglru_h, page_table, ada_w, ada_b, ln_g, ln_b, ab_w_in, ab_w_out, gmlp_ln_g, gmlp_ln_b, gmlp_ws, gmlp_bs, nsa_cmp_pe, nsa_cmp_w1, nsa_cmp_w2, rec_w_in, rec_conv_w, rec_conv_b, rg_wa, rg_ba, rg_wx, rg_bx, rg_lambda, rec_w_out, router_w, router_b, moe_w_in, moe_w_out):
    bp, seq, d = x_prompt.shape
    db = x_sample.shape[0]
    assert x_sample.shape[1] == 1
    n_pool, page = cache_nsa_cmp.shape[:2]
    n_pages = page_table.shape[1]
    past_len = n_pages * page
    assert seq % 256 == 0 and past_len % SLC_LEN == 0 and page % SLC_LEN == 0 and past_len >= WINDOW
    n_p = bp * seq
    d_rnn = rec_conv_b.shape[0]

    mods = _ada_all(jnp.concatenate([c_prompt, c_sample], axis=0), ada_w, ada_b)

    def mod_rows(layer, sub):
        m = mods[layer * 2 + sub]
        parts = [m[:, j * d:(j + 1) * d] for j in range(3)]
        return [p[:bp].reshape(bp, 1, d) for p in parts], [p[bp:] for p in parts]

    lnrow = lambda a: a.reshape(1, d)

    gperm = np.array([(g * HPG + h) * N_BRANCH + br for g in range(N_KV) for br in range(N_BRANCH) for h in range(HPG)])
    w_gate = ab_w_in[:, O_G:][:, gperm]
    w_ab_bf = jnp.concatenate([ab_w_in[:, :O_G], w_gate], axis=1).astype(BF16)
    tril = jnp.tril(jnp.ones((CHUNK, CHUNK), F32))
    wtril = (gmlp_ws * tril).astype(BF16)
    bst = gmlp_bs.T
    glng = gmlp_ln_g.reshape(1, A_WIDTH)
    glnb = gmlp_ln_b.reshape(1, A_WIDTH)
    pe_rows, w1bd, w2bd = _compress_weights(nsa_cmp_pe, nsa_cmp_w1, nsa_cmp_w2)
    rw_t = router_w.T
    xp = x_prompt.reshape(n_p, d)
    xs = x_sample.reshape(db, d)

    (sh_p, sc_p, g_p), (sh_s, sc_s, g_s) = mod_rows(0, 0)
    (sh2_p, sc2_p, g2_p), (sh2_s, sc2_s, g2_s) = mod_rows(0, 1)
    cos_p, sin_p = _rope_tables(np.arange(seq))
    a_p, q_p, kvc_p, kvs_p, kvw_p, kvt_p, gt_p = _ab_proj_prompt(
        xp, sc_p, sh_p, w_ab_bf, cos_p, sin_p, glng, glnb, wtril, bst, seq)
    kcv_p = _compress_prompt(kvc_p.reshape(bp, seq // CMP_STRIDE, CMP_STRIDE * KV_WIDTH), pe_rows, w1bd, w2bd)
    o_p, moe_in_bf, moe_out_bf = _nsa_prompt(
        q_p, gt_p, kcv_p, kvt_p, bp, seq,
        [moe_w_in.reshape(-1, moe_w_in.shape[-1]), moe_w_out.reshape(-1, moe_w_out.shape[-1])])
    moe_in_bf = moe_in_bf.reshape(moe_w_in.shape)
    moe_out_bf = moe_out_bf.reshape(moe_w_out.shape)
    w_out_bf = ab_w_out.astype(BF16)
    x1_p, h2_p, lg_p = _mix_out_prompt(a_p, o_p, w_out_bf, xp, g_p, lnrow(ln_g[0, 0]), lnrow(ln_b[0, 0]),
                                       sc2_p, sh2_p, rw_t, seq, True, "mix_out_l0_prompt")

    z_s = _small_mm(_modulate, [xs, sc_s, sh_s], w_ab_bf, O_G, 512, "ab_proj_sample")
    zg_s = _small_mm(_modulate, [xs, sc_s, sh_s], w_gate, N_GATE, N_GATE, "ab_gate_sample")
    cos_s, sin_s = _rope_tables(np.full((db,), past_len))
    ws0 = jnp.repeat(gmlp_ws[:, 0, 0], A_GROUP_DIM).reshape(1, A_WIDTH)
    bs0 = jnp.repeat(gmlp_bs[:, 0], A_GROUP_DIM).reshape(1, A_WIDTH)
    sds = lambda *s: jax.ShapeDtypeStruct(s, F32)
    a_s, v_s, q_s, kvc_s, kvs_s, kvw_s, gt_s = _vmem_call(
        _ab_post_sample,
        (sds(db, A_WIDTH), sds(db, A_WIDTH), sds(db, B_WIDTH), sds(db, KV_WIDTH), sds(db, KV_WIDTH),
         sds(db, KV_WIDTH), sds(db, N_GATE)),
        (z_s, zg_s, cos_s, sin_s, glng, glnb, ws0, bs0), "ab_post_sample")
    q3 = q_s.reshape(db, 1, B_WIDTH)
    page_t = lambda c: jnp.transpose(c, (0, 2, 3, 4, 1)).reshape(n_pool, KV_WIDTH, page)
    oc3, sel_idx = _nsa_sample_cmp(page_table, q3, page_t(cache_nsa_cmp), pe_rows, w1bd, w2bd, past_len)
    sel_idx = jnp.transpose(sel_idx, (0, 2, 1))
    o_s = _nsa_sample_attn(page_table, sel_idx, q3, gt_s.reshape(db, 1, N_GATE), oc3,
                           kvs_s.reshape(db, 1, KV_WIDTH), kvw_s.reshape(db, 1, KV_WIDTH),
                           jnp.transpose(state_nsa_win, (0, 2, 3, 4, 1)).reshape(db, KV_WIDTH, -1),
                           page_t(cache_nsa_slc),
                           past_len).reshape(db, B_WIDTH)
    f_s = _small_mm(lambda a, o: jnp.concatenate([a, o], axis=1), [a_s, o_s], w_out_bf, d, 512, "mix_out_l0_sample")
    x1_s, h2_s, lg_s = _vmem_call(
        _post_ln_router, (sds(db, d), sds(db, d), sds(N_EXPERTS, db)),
        (f_s, xs, g_s, lnrow(ln_g[0, 0]), lnrow(ln_b[0, 0]), sc2_s, sh2_s, rw_t), "post_l0_sample")

    x2_p = _channel_sublayer(x1_p, h2_p, lg_p, g2_p, lnrow(ln_g[0, 1]), lnrow(ln_b[0, 1]), router_b,
                             moe_in_bf, moe_out_bf, 0,MOE_BLK_PROMPT, 128, seq // 128, "l0_prompt")
    x2_s = _channel_sublayer(x1_s, h2_s, lg_s, g2_s.reshape(1, db, d), lnrow(ln_g[0, 1]), lnrow(ln_b[0, 1]), router_b,
                             moe_in_bf, moe_out_bf, 0,32, db, 1, "l0_sample")

    (sh_p, sc_p, g_p), (sh_s, sc_s, g_s) = mod_rows(1, 0)
    (sh2_p, sc2_p, g2_p), (sh2_s, sc2_s, g2_s) = mod_rows(1, 1)
    rec_in_bf = rec_w_in.astype(BF16)
    rec_out_bf = rec_w_out.astype(BF16)
    gg_p, xbr_p = _rec_in_prompt(x2_p, sc_p, sh_p, rec_in_bf, seq)
    wab = jnp.concatenate([rg_wa, rg_wx], axis=2).astype(BF16)
    row_r = lambda a: a.reshape(1, d_rnn)
    a_t, b_t = _rec_gate_prompt(xbr_p, rec_conv_w, row_r(rec_conv_b), wab, row_r(rg_ba), row_r(rg_bx),
                                row_r(rg_lambda), seq)
    hs4, ht = _scan_prompt(a_t.reshape(bp, seq, 8, d_rnn // 8), b_t.reshape(bp, seq, 8, d_rnn // 8))
    x3_p, h4_p, lg_p = _mix_out_prompt(gg_p, hs4.reshape(n_p, d_rnn), rec_out_bf, x2_p, g_p,
                                       lnrow(ln_g[1, 0]), lnrow(ln_b[1, 0]), sc2_p, sh2_p, rw_t, seq, False,
                                       "mix_out_l1_prompt")
    conv_p = xbr_p.reshape(bp, seq, d_rnn)[:, seq - (CONV_W - 1):]
    h_p = ht.reshape(bp, d_rnn)

    zr_s = _small_mm(_modulate, [x2_s, sc_s, sh_s], rec_in_bf, 2 * d_rnn, 512, "rec_in_sample")
    y_s, h_s, xbr_s = _vmem_call(
        _rec_step_sample, (sds(db, d_rnn), sds(db, d_rnn), sds(db, d_rnn)),
        (zr_s, state_rglru_conv[:, 0], state_rglru_conv[:, 1], state_rglru_conv[:, 2], state_rglru_h,
         rec_conv_w, row_r(rec_conv_b), rg_wa, rg_wx, row_r(rg_ba), row_r(rg_bx), row_r(rg_lambda)),
        "rec_step_sample")
    f_s = _small_mm(lambda y: y, [y_s], rec_out_bf, d, 512, "mix_out_l1_sample")
    x3_s, h4_s, lg_s = _vmem_call(
        _post_ln_router, (sds(db, d), sds(db, d), sds(N_EXPERTS, db)),
        (f_s, x2_s, g_s, lnrow(ln_g[1, 0]), lnrow(ln_b[1, 0]), sc2_s, sh2_s, rw_t), "post_l1_sample")
    conv_s = jnp.concatenate([state_rglru_conv[:, 1:], xbr_s[:, None, :]], axis=1)

    y_p = _channel_sublayer(x3_p, h4_p, lg_p, g2_p, lnrow(ln_g[1, 1]), lnrow(ln_b[1, 1]), router_b,
                            moe_in_bf, moe_out_bf, 1,MOE_BLK_PROMPT, 128, seq // 128, "l1_prompt")
    y_s = _channel_sublayer(x3_s, h4_s, lg_s, g2_s.reshape(1, db, d), lnrow(ln_g[1, 1]), lnrow(ln_b[1, 1]), router_b,
                            moe_in_bf, moe_out_bf, 1,32, db, 1, "l1_sample")

    kv5 = lambda a, b_: a.reshape(b_, -1, 2, N_KV, HEAD_DIM)
    keep = min(WINDOW, seq)
    win_p = kv5(kvw_p, bp)[:, seq - keep:]
    kw_full = jnp.concatenate([state_nsa_win, kv5(kvw_s, db)], axis=1)
    win_s = kw_full[:, kw_full.shape[1] - min(WINDOW, kw_full.shape[1]):]
    return (y_p.reshape(bp, seq, d), y_s.reshape(db, 1, d), kv5(kvc_p, bp), kv5(kvc_s, db), kv5(kvs_p, bp),
            kv5(kvs_s, db), win_p, win_s, v_s.reshape(db, 1, A_WIDTH), conv_p, conv_s, h_p, h_s)
```

```python
import functools
import math

import numpy as np
import jax
import jax.numpy as jnp
from jax import lax
from jax.experimental import pallas as pl
from jax.experimental.pallas import tpu as pltpu

F32 = jnp.float32
BF16 = jnp.bfloat16
I32 = jnp.int32

A_GROUPS = 8
A_GROUP_DIM = 128
A_WIDTH = A_GROUPS * A_GROUP_DIM
CHUNK = 128
N_HEADS = 16
N_KV = 4
HEAD_DIM = 64
HPG = N_HEADS // N_KV
B_WIDTH = N_HEADS * HEAD_DIM
KV_WIDTH = 2 * N_KV * HEAD_DIM
N_BRANCH = 3
CMP_LEN = 32
CMP_STRIDE = 16
CMP_HID = 2 * HEAD_DIM
SLC_LEN = 64
N_SEL = 16
WINDOW = 512
Q_BLOCK = 128
FORCE_BONUS = 1e4
ROPE_THETA = 10000.0
RNN_BLOCKS = 16
CONV_W = 4
RG_C = 8.0
N_EXPERTS = 16
N_GROUPS = 4
EXPERTS_PER_GROUP = N_EXPERTS // N_GROUPS
TOP_K = 2
DEPTH = 2
ALPHA = (2 * DEPTH) ** 0.25
LN_EPS = 1e-5
NEG = -1e30

O_U = 0
O_V = A_WIDTH
O_Q = 2 * A_WIDTH
O_KV = O_Q + B_WIDTH
O_G = O_KV + 3 * KV_WIDTH
N_GATE = N_BRANCH * N_HEADS

VMEM_LIMIT_BYTES = 56 * 1024 * 1024


def _cparams(sem, vmem=None):
    return pltpu.CompilerParams(dimension_semantics=sem, vmem_limit_bytes=vmem)


def _dot(a, b):
    return jnp.dot(a.astype(BF16), b.astype(BF16), preferred_element_type=F32)


def _dot_nt(a, b):
    dn = (((1,), (1,)), ((), ()))
    return lax.dot_general(a.astype(BF16), b.astype(BF16), dn, preferred_element_type=F32)


def _ln(x):
    mu = jnp.mean(x, -1, keepdims=True)
    xc = x - mu
    var = jnp.mean(xc * xc, -1, keepdims=True)
    return xc * lax.rsqrt(var + LN_EPS)


def _silu(x):
    return x * jax.nn.sigmoid(x)


def _gelu(x):
    return jax.nn.gelu(x, approximate=True)


def _rope(x, cos2, sin2):
    w = x.shape[1]
    rep = w // 128
    cos = jnp.concatenate([cos2] * rep, axis=1) if rep > 1 else cos2
    sin = jnp.concatenate([sin2] * rep, axis=1) if rep > 1 else sin2
    lane = lax.broadcasted_iota(I32, x.shape, 1)
    first = (lane & (HEAD_DIM - 1)) < HEAD_DIM // 2
    rot = jnp.where(first, pltpu.roll(x, w - HEAD_DIM // 2, 1), pltpu.roll(x, HEAD_DIM // 2, 1))
    return x * cos + rot * sin


def _rope_tables(pos):
    half = HEAD_DIM // 2
    inv = ROPE_THETA ** (-np.arange(half, dtype=np.float64) / half)
    ang = np.asarray(pos, np.float64)[:, None] * inv[None, :]
    cos = np.tile(np.cos(ang), (1, 4))
    sin = np.tile(np.concatenate([-np.sin(ang), np.sin(ang)], axis=1), (1, 2))
    return jnp.asarray(cos, F32), jnp.asarray(sin, F32)


def _softmax_rows(s, mask):
    sm = jnp.where(mask, s, NEG)
    m = jnp.max(sm, -1, keepdims=True)
    p = jnp.where(mask, jnp.exp(sm - m), 0.0)
    return p, jnp.sum(p, -1, keepdims=True)


def _ada_kernel(c_ref, w_ref, b_ref, o_ref):
    o_ref[...] = _dot(_silu(c_ref[...]), w_ref[...]) + b_ref[...]


def _ada_all(c_all, ada_w, ada_b):
    r, d = c_all.shape
    n_mod = ada_w.shape[0] * ada_w.shape[1]
    d3 = ada_w.shape[-1]
    tn = 512
    return pl.pallas_call(
        _ada_kernel,
        grid=(n_mod, d3 // tn),
        in_specs=[
            pl.BlockSpec((r, d), lambda l, j: (0, 0)),
            pl.BlockSpec((None, d, tn), lambda l, j: (l, 0, j)),
            pl.BlockSpec((None, 1, tn), lambda l, j: (l, 0, j)),
        ],
        out_specs=pl.BlockSpec((None, r, tn), lambda l, j: (l, 0, j)),
        out_shape=jax.ShapeDtypeStruct((n_mod, r, d3), F32),
        compiler_params=_cparams(("parallel", "parallel")),
        name="ada_mod",
    )(c_all, ada_w.reshape(n_mod, d, d3), ada_b.reshape(n_mod, 1, d3))


def _small_mm_kernel(*refs, n_x, pre):
    xs = [r[...] for r in refs[:n_x]]
    w_ref, o_ref = refs[n_x], refs[n_x + 1]
    o_ref[...] = _dot(pre(*xs), w_ref[...])


def _small_mm(pre, xs, w, n_out, tn, name):
    m = xs[0].shape[0]
    k = w.shape[0]
    in_specs = [pl.BlockSpec(x.shape, lambda j, nd=x.ndim: (0,) * nd) for x in xs]
    in_specs.append(pl.BlockSpec((k, tn), lambda j: (0, j)))
    return pl.pallas_call(
        functools.partial(_small_mm_kernel, n_x=len(xs), pre=pre),
        grid=(n_out // tn,),
        in_specs=in_specs,
        out_specs=pl.BlockSpec((m, tn), lambda j: (0, j)),
        out_shape=jax.ShapeDtypeStruct((m, n_out), F32),
        compiler_params=_cparams(("parallel",)),
        name=name,
    )(*xs, w)


def _vmem_call(fn, out_shapes, args, name):
    n_in = len(args)

    def kern(*refs):
        res = fn(*[r[...] for r in refs[:n_in]])
        for o, v in zip(refs[n_in:], res):
            o[...] = v

    return pl.pallas_call(kern, out_shape=out_shapes, name=name)(*args)


def _modulate(x, sc, sh):
    return x * (1.0 + sc) + sh


def _ab_proj_kernel(x_ref, sc_ref, sh_ref, w_ref, cos_ref, sin_ref, lng_ref, lnb_ref, wtril_ref, bst_ref,
                    a_ref, q_ref, kvc_ref, kvs_ref, kvw_ref, kvt_ref, gt_ref, *, tm):
    hb = _modulate(x_ref[...], sc_ref[...], sh_ref[...]).astype(BF16)

    def proj(lo, hi):
        return jnp.dot(hb, w_ref[:, lo:hi], preferred_element_type=F32)

    cos2 = cos_ref[...]
    sin2 = sin_ref[...]
    zu = proj(O_U, O_V)
    zv = proj(O_V, O_Q)
    for g in range(A_GROUPS):
        sl = slice(g * A_GROUP_DIM, (g + 1) * A_GROUP_DIM)
        vg = _ln(zv[:, sl]) * lng_ref[:, sl] + lnb_ref[:, sl]
        for c in range(tm // CHUNK):
            rs = slice(c * CHUNK, (c + 1) * CHUNK)
            mix = jnp.dot(wtril_ref[g], vg[rs].astype(BF16), preferred_element_type=F32) + bst_ref[:, g:g + 1]
            a_ref[rs, sl] = zu[rs, sl] * mix
    q_ref[...] = _rope(proj(O_Q, O_KV), cos2, sin2)
    half = KV_WIDTH // 2
    for br, ref in enumerate((kvc_ref, kvs_ref, kvw_ref)):
        lo = O_KV + br * KV_WIDTH
        k = _rope(proj(lo, lo + half), cos2, sin2)
        v = proj(lo + half, lo + KV_WIDTH)
        ref[:, 0:half] = k
        ref[:, half:KV_WIDTH] = v
        for g in range(N_KV):
            hs = slice(g * HEAD_DIM, (g + 1) * HEAD_DIM)
            kvt_ref[br, 0, g] = k[:, hs]
            kvt_ref[br, 1, g] = v[:, hs]
    zg = jax.nn.sigmoid(proj(O_G, O_G + N_GATE))
    per_g = N_GATE // N_KV
    for g in range(N_KV):
        gt_ref[g] = zg[:, g * per_g:(g + 1) * per_g]


def _ab_proj_prompt(x, sc, sh, w_bf, cos2, sin2, lng, lnb, wtril, bst, seq):
    n, d = x.shape
    tm = 256
    tpb = seq // tm
    n_in = w_bf.shape[1]
    row = lambda i: (i, 0)
    mod = lambda i: (i // tpb, 0, 0)
    const2 = lambda i: (0, 0)
    pos = lambda i: (i % tpb, 0)
    out_shape = (
        jax.ShapeDtypeStruct((n, A_WIDTH), F32),
        jax.ShapeDtypeStruct((n, B_WIDTH), F32),
        jax.ShapeDtypeStruct((n, KV_WIDTH), F32),
        jax.ShapeDtypeStruct((n, KV_WIDTH), F32),
        jax.ShapeDtypeStruct((n, KV_WIDTH), F32),
        jax.ShapeDtypeStruct((3, 2, N_KV, n, HEAD_DIM), F32),
        jax.ShapeDtypeStruct((N_KV, n, N_GATE // N_KV), F32),
    )
    return pl.pallas_call(
        functools.partial(_ab_proj_kernel, tm=tm),
        grid=(n // tm,),
        in_specs=[
            pl.BlockSpec((tm, d), row),
            pl.BlockSpec((None, 1, d), mod),
            pl.BlockSpec((None, 1, d), mod),
            pl.BlockSpec((d, n_in), const2, pipeline_mode=pl.Buffered(1)),
            pl.BlockSpec((tm, 128), pos),
            pl.BlockSpec((tm, 128), pos),
            pl.BlockSpec((1, A_WIDTH), const2),
            pl.BlockSpec((1, A_WIDTH), const2),
            pl.BlockSpec((A_GROUPS, CHUNK, CHUNK), lambda i: (0, 0, 0)),
            pl.BlockSpec((CHUNK, A_GROUPS), const2),
        ],
        out_specs=(
            pl.BlockSpec((tm, A_WIDTH), row),
            pl.BlockSpec((tm, B_WIDTH), row),
            pl.BlockSpec((tm, KV_WIDTH), row),
            pl.BlockSpec((tm, KV_WIDTH), row),
            pl.BlockSpec((tm, KV_WIDTH), row),
            pl.BlockSpec((3, 2, N_KV, tm, HEAD_DIM), lambda i: (0, 0, 0, i, 0)),
            pl.BlockSpec((N_KV, tm, N_GATE // N_KV), lambda i: (0, i, 0)),
        ),
        out_shape=out_shape,
        compiler_params=_cparams(("parallel",), VMEM_LIMIT_BYTES),
        name="ab_proj_prompt",
    )(x, sc, sh, w_bf, cos2, sin2, lng, lnb, wtril, bst)


def _ab_post_sample(z, zg, cos2, sin2, lng, lnb, ws0, bs0):
    zu = z[:, O_U:O_V]
    zv = z[:, O_V:O_Q]
    vs = []
    for g in range(A_GROUPS):
        sl = slice(g * A_GROUP_DIM, (g + 1) * A_GROUP_DIM)
        vs.append(_ln(zv[:, sl]) * lng[:, sl] + lnb[:, sl])
    v = jnp.concatenate(vs, axis=1)
    a = zu * (ws0 * v + bs0)
    q = _rope(z[:, O_Q:O_KV], cos2, sin2)
    half = KV_WIDTH // 2
    kvs = []
    for br in range(3):
        lo = O_KV + br * KV_WIDTH
        k = _rope(z[:, lo:lo + half], cos2, sin2)
        kvs.append(jnp.concatenate([k, z[:, lo + half:lo + KV_WIDTH]], axis=1))
    return a, v, q, kvs[0], kvs[1], kvs[2], jax.nn.sigmoid(zg)


def _compress_mlp(lhs_lo, lhs_hi, w1_ref, w2_ref, kv, n):
    lo = _dot(lhs_lo, w1_ref[kv, 0])
    hi = _dot(lhs_hi, w1_ref[kv, 1])
    hsum = lo + pltpu.roll(hi, n - 1, 0)
    return _dot(_gelu(hsum), w2_ref[kv])


def _compress_prompt_kernel(x_ref, pe_ref, w1_ref, w2_ref, o_ref, *, n16):
    for kv in range(2):
        for gp in range(N_KV // 2):
            base = kv * (KV_WIDTH // 2) + gp * 128
            xg = jnp.concatenate(
                [x_ref[:, s * KV_WIDTH + base:s * KV_WIDTH + base + 128] for s in range(CMP_STRIDE)], axis=1)
            r = _compress_mlp(xg + pe_ref[kv, 0], xg + pe_ref[kv, 1], w1_ref, w2_ref, kv, n16)
            o_ref[kv, 2 * gp] = r[:, 0:HEAD_DIM]
            o_ref[kv, 2 * gp + 1] = r[:, HEAD_DIM:2 * HEAD_DIM]


def _compress_prompt(kvc3, pe_rows, w1bd, w2bd):
    b, n16, wid = kvc3.shape
    return pl.pallas_call(
        functools.partial(_compress_prompt_kernel, n16=n16),
        grid=(b,),
        in_specs=[
            pl.BlockSpec((None, n16, wid), lambda i: (i, 0, 0)),
            pl.BlockSpec(pe_rows.shape, lambda i: (0, 0, 0, 0)),
            pl.BlockSpec(w1bd.shape, lambda i: (0, 0, 0, 0)),
            pl.BlockSpec(w2bd.shape, lambda i: (0, 0, 0)),
        ],
        out_specs=pl.BlockSpec((None, 2, N_KV, n16, HEAD_DIM), lambda i: (i, 0, 0, 0, 0)),
        out_shape=jax.ShapeDtypeStruct((b, 2, N_KV, n16, HEAD_DIM), F32),
        compiler_params=_cparams(("parallel",), VMEM_LIMIT_BYTES),
        name="nsa_compress_prompt",
    )(kvc3, pe_rows, w1bd, w2bd)


def _compress_weights(pe, w1, w2):
    eye2 = jnp.eye(2, dtype=F32)
    w1r = w1.reshape(2, 2, CMP_STRIDE, HEAD_DIM, CMP_HID)
    w1bd = jnp.einsum('khsdc,gG->khsgdGc', w1r, eye2).reshape(2, 2, CMP_STRIDE * 128, 2 * CMP_HID).astype(BF16)
    pe_rows = jnp.broadcast_to(pe.reshape(2, 2, CMP_STRIDE, 1, HEAD_DIM), (2, 2, CMP_STRIDE, 2, HEAD_DIM))
    pe_rows = pe_rows.reshape(2, 2, 1, CMP_STRIDE * 128)
    w2bd = jnp.einsum('kcd,gG->kgcGd', w2, eye2).reshape(2, 2 * CMP_HID, 2 * HEAD_DIM).astype(BF16)
    return pe_rows, w1bd, w2bd


def _cmp_slc_map(n_rows, n_cmp, n_slc, n_cols):
    cs = np.arange(n_rows)[:, None] * CMP_STRIDE
    ss = np.arange(n_cols)[None, :] * SLC_LEN
    m = (cs < ss + SLC_LEN) & (cs + CMP_LEN > ss)
    m &= (np.arange(n_rows)[:, None] < n_cmp) & (np.arange(n_cols)[None, :] < n_slc)
    return m.astype(np.float32)


def _nsa_prompt_kernel(q_ref, gt_ref, kc_ref, vc_ref, ks_ref, vs_ref, kw_ref, vw_ref, mapt_ref, e_ref,
                       *rest, n_cmp, n_slc, n_sel, kt_len, n_cast):
    o_ref, selb_ref = rest[n_cast], rest[-1]
    for src, dst in zip(rest[:n_cast], rest[n_cast + 1:2 * n_cast + 1]):
        dst[...] = src[...].astype(BF16)
    i = pl.program_id(2)
    q0 = i * Q_BLOCK
    rows = HPG * Q_BLOCK
    tile4 = lambda x: jnp.concatenate([x] * HPG, axis=0)
    qb = q_ref[...]
    q4 = jnp.concatenate([qb[:, h * HEAD_DIM:(h + 1) * HEAD_DIM] for h in range(HPG)], axis=0)
    q4 = (q4 * HEAD_DIM ** -0.5).astype(BF16)
    t_pos = q0 + lax.broadcasted_iota(I32, (Q_BLOCK, 1), 0)

    n16 = kc_ref.shape[0]
    n_idx = lax.broadcasted_iota(I32, (1, n16), 1)
    bias_c = jnp.where((n_idx * CMP_STRIDE + CMP_LEN - 1 <= t_pos) & (n_idx < n_cmp), 0.0, NEG)
    s = _dot_nt(q4, kc_ref[...]) + tile4(bias_c)
    p = jnp.exp(s - jnp.max(s, -1, keepdims=True))
    l = jnp.sum(p, -1, keepdims=True)
    pn = p * jnp.where((tile4(t_pos) >= CMP_LEN - 1) & (n_cmp > 0), 1.0 / l, 0.0)
    o_c = _dot(pn, vc_ref[...])
    psum = pn[0:Q_BLOCK]
    for h in range(1, HPG):
        psum = psum + pn[h * Q_BLOCK:(h + 1) * Q_BLOCK]

    imp_t = _dot_nt(mapt_ref[...], psum)
    blk_t = lax.broadcasted_iota(I32, (n_slc, Q_BLOCK), 0)
    cur_t = (q0 + lax.broadcasted_iota(I32, (n_slc, Q_BLOCK), 1)) // SLC_LEN
    forced = (blk_t == 0) | (blk_t == cur_t) | (blk_t == cur_t - 1)
    score = jnp.where(blk_t <= cur_t, imp_t + jnp.where(forced, FORCE_BONUS, 0.0), -jnp.inf)
    rank = jnp.zeros((n_slc, Q_BLOCK), I32)
    for j in range(n_slc):
        r = score[j:j + 1, :]
        rank = rank + ((r > score) | ((r == score) & (blk_t > j))).astype(I32)
    sel_t = ((rank < n_sel) & (score > -jnp.inf)).astype(F32)
    selb_ref[...] = (_dot(sel_t.T, e_ref[...]) - 1.0) * (-NEG)

    def sel_tile(kt, carry, causal):
        m, l, acc = carry
        k0 = pl.multiple_of(kt * kt_len, kt_len)
        bias = selb_ref[:, pl.ds(k0, kt_len)]
        if causal:
            kp = k0 + lax.broadcasted_iota(I32, (1, kt_len), 1)
            bias = bias + jnp.where(kp <= t_pos, 0.0, NEG)
        s = _dot_nt(q4, ks_ref[pl.ds(k0, kt_len), :]) + tile4(bias)
        m_new = jnp.maximum(m, jnp.max(s, -1, keepdims=True))
        alpha = jnp.exp(m - m_new)
        p = jnp.exp(s - m_new)
        l = alpha * l + jnp.sum(p, -1, keepdims=True)
        acc = alpha * acc + _dot(p, vs_ref[pl.ds(k0, kt_len), :])
        return m_new, l, acc

    last_kt = (q0 + Q_BLOCK - 1) // kt_len
    init = (jnp.full((rows, 1), NEG, F32), jnp.zeros((rows, 1), F32), jnp.zeros((rows, HEAD_DIM), F32))
    carry = lax.fori_loop(0, last_kt, functools.partial(sel_tile, causal=False), init)
    _, l, acc = sel_tile(last_kt, carry, True)
    o_s = acc * (1.0 / l)

    n_wt = WINDOW // Q_BLOCK + 1
    c_idx = lax.broadcasted_iota(I32, (Q_BLOCK, Q_BLOCK), 1)
    r_idx = lax.broadcasted_iota(I32, (Q_BLOCK, Q_BLOCK), 0)
    s_parts, v_parts = [], []
    for j in range(n_wt):
        ks_j = q0 - WINDOW + j * Q_BLOCK
        ld = pl.multiple_of(jnp.maximum(ks_j, 0), Q_BLOCK)
        off = jnp.where(ks_j >= 0, 0.0, NEG)
        sj = _dot_nt(q4, kw_ref[pl.ds(ld, Q_BLOCK), :])
        if j == 0:
            sj = sj + tile4(jnp.where(c_idx > r_idx, 0.0, NEG) + off)
        elif j == n_wt - 1:
            sj = sj + tile4(jnp.where(c_idx <= r_idx, 0.0, NEG))
        else:
            sj = sj + off
        s_parts.append(sj)
        v_parts.append(vw_ref[pl.ds(ld, Q_BLOCK), :])
    s = jnp.concatenate(s_parts, axis=1)
    p = jnp.exp(s - jnp.max(s, -1, keepdims=True))
    o_w = _dot(p, jnp.concatenate(v_parts, axis=0)) * (1.0 / jnp.sum(p, -1, keepdims=True))

    gt = gt_ref[...]

    def gcol(br):
        return jnp.concatenate([gt[:, br * HPG + h:br * HPG + h + 1] for h in range(HPG)], axis=0)

    o = gcol(0) * o_c + gcol(1) * o_s + gcol(2) * o_w
    o_ref[...] = jnp.concatenate([o[h * Q_BLOCK:(h + 1) * Q_BLOCK] for h in range(HPG)], axis=1)


def _nsa_prompt(q, gates, kcv, kvt, batch, seq, to_bf16):
    n = q.shape[0]
    nq = seq // Q_BLOCK
    n_steps = batch * N_KV * nq
    step = lambda b, g, i: ((b * N_KV + g) * nq + i, 0)
    cast_specs = [pl.BlockSpec((w.shape[0] // n_steps, w.shape[1]), step) for w in to_bf16]
    n16 = kcv.shape[3]
    n_cmp = n16 - 1
    n_slc = -(-seq // SLC_LEN)
    n_sel = min(N_SEL, n_slc)
    kt_len = min(512, seq)
    mapt = jnp.asarray(_cmp_slc_map(n16, n_cmp, n_slc, n_slc).T)
    expand = jnp.asarray((np.arange(n_slc)[:, None] == np.arange(seq)[None, :] // SLC_LEN).astype(np.float32), BF16)
    per_g = N_GATE // N_KV
    qrow = lambda b, g, i: (b * nq + i, g)
    kvspec = lambda br, kv: pl.BlockSpec((None, None, None, seq, HEAD_DIM), lambda b, g, i: (br, kv, g, b, 0))
    return pl.pallas_call(
        functools.partial(_nsa_prompt_kernel, n_cmp=n_cmp, n_slc=n_slc, n_sel=n_sel, kt_len=kt_len,
                          n_cast=len(to_bf16)),
        grid=(batch, N_KV, nq),
        in_specs=[
            pl.BlockSpec((Q_BLOCK, HPG * HEAD_DIM), qrow),
            pl.BlockSpec((None, Q_BLOCK, per_g), lambda b, g, i: (g, b * nq + i, 0)),
            pl.BlockSpec((None, None, None, n16, HEAD_DIM), lambda b, g, i: (b, 0, g, 0, 0)),
            pl.BlockSpec((None, None, None, n16, HEAD_DIM), lambda b, g, i: (b, 1, g, 0, 0)),
            kvspec(1, 0), kvspec(1, 1), kvspec(2, 0), kvspec(2, 1),
            pl.BlockSpec(mapt.shape, lambda b, g, i: (0, 0)),
            pl.BlockSpec(expand.shape, lambda b, g, i: (0, 0)),
        ] + cast_specs,
        out_specs=[pl.BlockSpec((Q_BLOCK, HPG * HEAD_DIM), qrow)] + cast_specs,
        out_shape=[jax.ShapeDtypeStruct((n, B_WIDTH), F32)] + [jax.ShapeDtypeStruct(w.shape, BF16) for w in to_bf16],
        scratch_shapes=[pltpu.VMEM((Q_BLOCK, seq), F32)],
        compiler_params=_cparams(("parallel", "parallel", "arbitrary"), VMEM_LIMIT_BYTES),
        name="nsa_attn_prompt",
    )(q, gates, kcv, kcv, kvt, kvt, kvt, kvt, mapt, expand, *to_bf16)


FILL_PAGES = 8


def _page_copy(cache_hbm, pt_ref, xbuf, sem, b, slot, p, n_pages):
    return pltpu.make_async_copy(cache_hbm.at[pt_ref[b * n_pages + p]], xbuf.at[slot, p], sem.at[slot])


def _nsa_sample_cmp_kernel(pt_ref, q_ref, cache_hbm, pe_ref, w1_ref, w2_ref, map_ref,
                           oc_ref, idx_ref, xbuf, xrow, lhs_a, lhs_b, sem, *, n_pages, n_cmp, n_slc, n_sel, pos):
    b = pl.program_id(0)
    nb = pl.num_programs(0)
    slot = b % 2
    n16 = lhs_a.shape[0]
    cpp = n16 // n_pages

    def start_all(bb, sl):
        def body(p, c):
            _page_copy(cache_hbm, pt_ref, xbuf, sem, bb, sl, p, n_pages).start()
            return c
        lax.fori_loop(0, n_pages, body, 0)

    @pl.when(b == 0)
    def _():
        start_all(0, 0)

    @pl.when(b + 1 < nb)
    def _():
        start_all(b + 1, 1 - slot)

    def wait_body(p, c):
        _page_copy(cache_hbm, pt_ref, xbuf, sem, b, slot, p, n_pages).wait()
        return c
    lax.fori_loop(0, n_pages, wait_body, 0)

    kc, vc = [], []
    wkv = N_KV * HEAD_DIM
    for kv, dst in ((0, kc), (1, vc)):
        def fill(j, c):
            for u in range(FILL_PAGES):
                p = j * FILL_PAGES + u
                r0 = pl.multiple_of(p * cpp, cpp)
                for gp, lhs in enumerate((lhs_a, lhs_b)):
                    xrow[u, gp] = xbuf[slot, p, kv * wkv + gp * 128:kv * wkv + (gp + 1) * 128, :].T
                    for s in range(CMP_STRIDE):
                        lhs[pl.ds(r0, cpp), s * 128:(s + 1) * 128] = xrow[u, gp, pl.ds(s, cpp, stride=CMP_STRIDE), :]
            return c
        lax.fori_loop(0, n_pages // FILL_PAGES, fill, 0)
        for lhs in (lhs_a, lhs_b):
            xg = lhs[...]
            r = _compress_mlp(xg + pe_ref[kv, 0], xg + pe_ref[kv, 1], w1_ref, w2_ref, kv, n16)
            dst.append(r[:, 0:HEAD_DIM])
            dst.append(r[:, HEAD_DIM:2 * HEAD_DIM])

    qrow = q_ref[...] * HEAD_DIM ** -0.5
    n_idx = lax.broadcasted_iota(I32, (1, n16), 1)
    valid = (n_idx * CMP_STRIDE + CMP_LEN - 1 <= pos) & (n_idx < n_cmp)
    head_row = lax.broadcasted_iota(I32, (8, 1), 0) < HPG
    ncol = map_ref.shape[1]
    blk_r = lax.broadcasted_iota(I32, (1, ncol), 1)
    blk_c = lax.broadcasted_iota(I32, (ncol, 1), 0)
    cur = pos // SLC_LEN
    forced = (blk_r == 0) | (blk_r == cur) | (blk_r == cur - 1)
    oc_parts = []
    for g in range(N_KV):
        q8 = jnp.concatenate(
            [qrow[:, (g * HPG + h) * HEAD_DIM:(g * HPG + h + 1) * HEAD_DIM] for h in range(HPG)]
            + [jnp.zeros((8 - HPG, HEAD_DIM), F32)], axis=0)
        s = _dot_nt(q8, kc[g])
        p, l = _softmax_rows(s, valid)
        pn = jnp.where(head_row, p / jnp.maximum(l, 1e-30), 0.0)
        o8 = _dot(pn, vc[g])
        oc_parts += [o8[h:h + 1, :] for h in range(HPG)]
        psum = jnp.broadcast_to(jnp.sum(pn, axis=0, keepdims=True), pn.shape)
        imp = _dot(psum, map_ref[...])[0:1, :]
        score = jnp.where((blk_r <= cur) & (blk_r < n_slc), imp + jnp.where(forced, FORCE_BONUS, 0.0), -jnp.inf)
        score_c = jnp.broadcast_to(score, (8, ncol)).T[:, 0:1]
        beats = (score_c > score) | ((score_c == score) & (blk_c < blk_r))
        rank = jnp.sum(beats.astype(F32), axis=0, keepdims=True)
        r_iota = lax.broadcasted_iota(I32, (n_sel, ncol), 0).astype(F32)
        hit = (rank == r_iota) & (score > -jnp.inf)
        idx = jnp.sum(jnp.where(hit, blk_r.astype(F32), 0.0), axis=1, keepdims=True)
        idx_ref[:, g:g + 1] = idx.astype(I32)
    oc_ref[...] = jnp.concatenate(oc_parts, axis=1)


def _nsa_sample_cmp(page_table, q3, cache_t, pe_rows, w1bd, w2bd, past_len):
    db, n_pages = page_table.shape
    page = cache_t.shape[2]
    cpp = page // CMP_STRIDE
    n16 = n_pages * cpp
    n_cmp = (past_len + 1) // CMP_STRIDE - 1
    n_slc = -(-(past_len + 1) // SLC_LEN)
    n_sel = min(N_SEL, n_slc)
    ncol = -(-n_slc // 128) * 128
    cmap = jnp.asarray(_cmp_slc_map(n16, n_cmp, n_slc, ncol))
    grid_spec = pltpu.PrefetchScalarGridSpec(
        num_scalar_prefetch=1,
        grid=(db,),
        in_specs=[
            pl.BlockSpec((None, 1, B_WIDTH), lambda b, pt: (b, 0, 0)),
            pl.BlockSpec(memory_space=pl.ANY),
            pl.BlockSpec(pe_rows.shape, lambda b, pt: (0, 0, 0, 0)),
            pl.BlockSpec(w1bd.shape, lambda b, pt: (0, 0, 0, 0)),
            pl.BlockSpec(w2bd.shape, lambda b, pt: (0, 0, 0)),
            pl.BlockSpec(cmap.shape, lambda b, pt: (0, 0)),
        ],
        out_specs=(
            pl.BlockSpec((None, 1, B_WIDTH), lambda b, pt: (b, 0, 0)),
            pl.BlockSpec((None, n_sel, N_KV), lambda b, pt: (b, 0, 0)),
        ),
        scratch_shapes=[pltpu.VMEM((2, n_pages) + cache_t.shape[1:], F32),
                        pltpu.VMEM((FILL_PAGES, 2, page, 128), F32),
                        pltpu.VMEM((n16, CMP_STRIDE * 128), F32), pltpu.VMEM((n16, CMP_STRIDE * 128), F32),
                        pltpu.SemaphoreType.DMA((2,))],
    )
    return pl.pallas_call(
        functools.partial(_nsa_sample_cmp_kernel, n_pages=n_pages, n_cmp=n_cmp, n_slc=n_slc, n_sel=n_sel, pos=past_len),
        grid_spec=grid_spec,
        out_shape=(jax.ShapeDtypeStruct((db, 1, B_WIDTH), F32), jax.ShapeDtypeStruct((db, n_sel, N_KV), I32)),
        compiler_params=_cparams(("arbitrary",), VMEM_LIMIT_BYTES),
        name="nsa_cmp_sample",
    )(page_table.reshape(-1), q3, cache_t, pe_rows, w1bd, w2bd, cmap)


def _sel_copy(cache_hbm, pt_ref, idx_ref, kbuf, sem, b, slot, j2, n_pages, n_sel, last_real):
    j = j2 // 2
    kv = j2 % 2
    g = j // n_sel
    blk = jnp.minimum(idx_ref[b * (N_KV * n_sel) + j], last_real)
    per_page = cache_hbm.shape[2] // SLC_LEN
    page = pt_ref[b * n_pages + blk // per_page]
    row0 = pl.multiple_of((kv * N_KV + g) * HEAD_DIM, HEAD_DIM)
    return pltpu.make_async_copy(cache_hbm.at[page, pl.ds(row0, HEAD_DIM), :], kbuf.at[slot, j2], sem.at[slot])


def _nsa_sample_attn_kernel(pt_ref, idx_ref, q_ref, gt_ref, oc_ref, ks_new_ref, kw_new_ref, win_ref, cache_hbm,
                            o_ref, kbuf, sem, *, n_pages, n_sel, pos):
    b = pl.program_id(0)
    nb = pl.num_programs(0)
    slot = b % 2
    n_copies = 2 * N_KV * n_sel
    page = kbuf.shape[3]
    per_page = page // SLC_LEN
    new_blk = pos // SLC_LEN
    last_real = new_blk - 1

    def start_all(bb, sl):
        def body(j, c):
            _sel_copy(cache_hbm, pt_ref, idx_ref, kbuf, sem, bb, sl, j, n_pages, n_sel, last_real).start()
            return c
        lax.fori_loop(0, n_copies, body, 0)

    @pl.when(b == 0)
    def _():
        start_all(0, 0)

    @pl.when(b + 1 < nb)
    def _():
        start_all(b + 1, 1 - slot)

    def wait_body(j, c):
        _sel_copy(cache_hbm, pt_ref, idx_ref, kbuf, sem, b, slot, j, n_pages, n_sel, last_real).wait()
        return c
    lax.fori_loop(0, n_copies, wait_body, 0)

    qrow = q_ref[...] * HEAD_DIM ** -0.5
    gt = gt_ref[...]
    oc = oc_ref[...]
    ks_new = ks_new_ref[...]
    kw_new = kw_new_ref[...]
    half = KV_WIDTH // 2
    wlen = win_ref.shape[1]
    kp_w = pos - wlen + lax.broadcasted_iota(I32, (1, wlen), 1)
    mask_w = (kp_w >= 0) & (kp_w <= pos) & (kp_w > pos - WINDOW)
    out_parts = []
    for g in range(N_KV):
        q8 = jnp.concatenate(
            [qrow[:, (g * HPG + h) * HEAD_DIM:(g * HPG + h + 1) * HEAD_DIM] for h in range(HPG)]
            + [jnp.zeros((8 - HPG, HEAD_DIM), F32)], axis=0)
        ksl = slice(g * HEAD_DIM, (g + 1) * HEAD_DIM)
        vsl = slice(half + g * HEAD_DIM, half + (g + 1) * HEAD_DIM)

        def attend(s, mask, pv, k_new, v_new, has_new):
            rb = lambda t: t.astype(BF16).astype(F32)
            s_new = jnp.sum(rb(q8) * rb(k_new), axis=-1, keepdims=True)
            sm = jnp.where(mask, s, NEG)
            m = jnp.maximum(jnp.max(sm, -1, keepdims=True), jnp.where(has_new, s_new, NEG))
            p = jnp.where(mask, jnp.exp(sm - m), 0.0)
            p_new = jnp.where(has_new, jnp.exp(s_new - m), 0.0)
            l = jnp.maximum(jnp.sum(p, -1, keepdims=True) + p_new, 1e-30)
            return pv(p / l) + rb(p_new / l) * rb(v_new)

        kt_sel = jnp.concatenate([kbuf[slot, (g * n_sel + r) * 2] for r in range(n_sel)], axis=1)
        vt_sel = jnp.concatenate([kbuf[slot, (g * n_sel + r) * 2 + 1] for r in range(n_sel)], axis=1)
        blk_ids = [idx_ref[b * (N_KV * n_sel) + g * n_sel + r] for r in range(n_sel)]
        row_of = lambda vals: jnp.concatenate([jnp.full((1, page), v, I32) for v in vals], axis=1)
        lane_blk = (lax.broadcasted_iota(I32, (1, n_sel * page), 1) & (page - 1)) // SLC_LEN
        picked = (row_of(blk_ids) <= last_real) & (lane_blk == row_of([bid % per_page for bid in blk_ids]))
        has_new = functools.reduce(jnp.logical_or, [bid == new_blk for bid in blk_ids])
        o_s = attend(_dot(q8, kt_sel), picked, lambda p: _dot_nt(p, vt_sel), ks_new[:, ksl], ks_new[:, vsl], has_new)
        o_w = attend(_dot(q8, win_ref[ksl, :]), mask_w, lambda p: _dot_nt(p, win_ref[vsl, :]),
                     kw_new[:, ksl], kw_new[:, vsl], True)
        for h in range(HPG):
            hh = g * HPG + h
            c0 = g * (N_BRANCH * HPG)
            g0 = gt[:, c0 + h:c0 + h + 1]
            g1 = gt[:, c0 + HPG + h:c0 + HPG + h + 1]
            g2 = gt[:, c0 + 2 * HPG + h:c0 + 2 * HPG + h + 1]
            out_parts.append(g0 * oc[:, hh * HEAD_DIM:(hh + 1) * HEAD_DIM] + g1 * o_s[h:h + 1, :] + g2 * o_w[h:h + 1, :])
    o_ref[...] = jnp.concatenate(out_parts, axis=1)


def _nsa_sample_attn(page_table, sel_idx, q3, gates3, oc3, ks_new3, kw_new3, win_state, cache_s, past_len):
    db, n_pages = page_table.shape
    n_sel = sel_idx.shape[-1]
    wlen = win_state.shape[2]
    row = lambda wdt: pl.BlockSpec((None, 1, wdt), lambda b, pt, ix: (b, 0, 0))
    grid_spec = pltpu.PrefetchScalarGridSpec(
        num_scalar_prefetch=2,
        grid=(db,),
        in_specs=[
            row(B_WIDTH), row(N_GATE), row(B_WIDTH), row(KV_WIDTH), row(KV_WIDTH),
            pl.BlockSpec((None, KV_WIDTH, wlen), lambda b, pt, ix: (b, 0, 0)),
            pl.BlockSpec(memory_space=pl.ANY),
        ],
        out_specs=row(B_WIDTH),
        scratch_shapes=[pltpu.VMEM((2, 2 * N_KV * n_sel, HEAD_DIM, cache_s.shape[2]), F32),
                        pltpu.SemaphoreType.DMA((2,))],
    )
    return pl.pallas_call(
        functools.partial(_nsa_sample_attn_kernel, n_pages=n_pages, n_sel=n_sel, pos=past_len),
        grid_spec=grid_spec,
        out_shape=jax.ShapeDtypeStruct((db, 1, B_WIDTH), F32),
        compiler_params=_cparams(("arbitrary",), VMEM_LIMIT_BYTES),
        name="nsa_attn_sample",
    )(page_table.reshape(-1), sel_idx.reshape(-1), q3, gates3, oc3, ks_new3, kw_new3, win_state, cache_s)


def _post_ln_router(acc, x, gate, lng, lnb, sc2, sh2, rw_t):
    x1 = _ln(ALPHA * x + gate * acc) * lng + lnb
    h2 = _modulate(x1, sc2, sh2)
    return x1, h2, _dot_nt(rw_t, h2)


def _mix_out_kernel(l0_ref, l1_ref, w_ref, x_ref, gate_ref, lng_ref, lnb_ref, sc2_ref, sh2_ref, rw_ref,
                    x1_ref, h2_ref, lg_ref, *, concat):
    if concat:
        k0 = l0_ref.shape[1]
        acc = _dot(l0_ref[...], w_ref[0:k0, :]) + _dot(l1_ref[...], w_ref[k0:, :])
    else:
        acc = _dot(l0_ref[...] * l1_ref[...], w_ref[...])
    x1, h2, lg = _post_ln_router(acc, x_ref[...], gate_ref[...], lng_ref[...], lnb_ref[...],
                                 sc2_ref[...], sh2_ref[...], rw_ref[...])
    x1_ref[...] = x1
    h2_ref[...] = h2
    lg_ref[...] = lg


def _mix_out_prompt(l0, l1, w_bf, x, gate, lng, lnb, sc2, sh2, rw, seq, concat, name):
    n, d = x.shape
    tm = 512
    tpb = seq // tm
    row = lambda i: (i, 0)
    mod = lambda i: (i // tpb, 0, 0)
    const2 = lambda i: (0, 0)
    return pl.pallas_call(
        functools.partial(_mix_out_kernel, concat=concat),
        grid=(n // tm,),
        in_specs=[
            pl.BlockSpec((tm, l0.shape[1]), row),
            pl.BlockSpec((tm, l1.shape[1]), row),
            pl.BlockSpec(w_bf.shape, const2, pipeline_mode=pl.Buffered(1)),
            pl.BlockSpec((tm, d), row),
            pl.BlockSpec((None, 1, d), mod),
            pl.BlockSpec((1, d), const2),
            pl.BlockSpec((1, d), const2),
            pl.BlockSpec((None, 1, d), mod),
            pl.BlockSpec((None, 1, d), mod),
            pl.BlockSpec(rw.shape, const2),
        ],
        out_specs=(pl.BlockSpec((tm, d), row), pl.BlockSpec((tm, d), row),
                   pl.BlockSpec((N_EXPERTS, tm), lambda i: (0, i))),
        out_shape=(jax.ShapeDtypeStruct((n, d), F32), jax.ShapeDtypeStruct((n, d), F32),
                   jax.ShapeDtypeStruct((N_EXPERTS, n), F32)),
        compiler_params=_cparams(("parallel",), VMEM_LIMIT_BYTES),
        name=name,
    )(l0, l1, w_bf, x, gate, lng, lnb, sc2, sh2, rw)


def _top2_route(lg, rb):
    s = jax.nn.sigmoid(lg)
    sb = s + rb
    rows = [sb[e:e + 1, :] for e in range(N_EXPERTS)]
    gs = []
    for g in range(N_GROUPS):
        v = rows[g * EXPERTS_PER_GROUP:(g + 1) * EXPERTS_PER_GROUP]
        pair = [v[i] + v[j] for i in range(EXPERTS_PER_GROUP) for j in range(i + 1, EXPERTS_PER_GROUP)]
        gs.append(functools.reduce(jnp.maximum, pair))
    best, gi = gs[0], jnp.zeros(gs[0].shape, I32)
    for g in range(1, N_GROUPS):
        better = gs[g] > best
        gi = jnp.where(better, g, gi)
        best = jnp.where(better, gs[g], best)
    cand = [jnp.where(gi == e // EXPERTS_PER_GROUP, rows[e], -jnp.inf) for e in range(N_EXPERTS)]
    ids = []
    for k in range(TOP_K):
        vk = jnp.full(cand[0].shape, -jnp.inf, F32)
        ik = jnp.zeros(cand[0].shape, I32)
        for e in range(N_EXPERTS):
            c = cand[e]
            for prev in ids:
                c = jnp.where(prev == e, -jnp.inf, c)
            better = c > vk
            ik = jnp.where(better, e, ik)
            vk = jnp.where(better, c, vk)
        ids.append(ik)
    ws = [functools.reduce(jnp.add, [jnp.where(ik == e, s[e:e + 1, :], 0.0) for e in range(N_EXPERTS)]) for ik in ids]
    tot = functools.reduce(jnp.add, ws)
    return ids, [w / tot for w in ws]


def _route_kernel(lg_ref, rb_ref, w_ref, dest_ref, be_ref, nu_ref, tot_ref, run_ref, ps_ref, *, blk):
    ph = pl.program_id(0)
    i = pl.program_id(1)
    tm = lg_ref.shape[1]
    ids, ws = _top2_route(lg_ref[...], rb_ref[...])
    e_iota = lax.broadcasted_iota(I32, (N_EXPERTS, tm), 0)
    oh = [(e_iota == ik).astype(F32) for ik in ids]
    ohsum = functools.reduce(jnp.add, oh)
    tile_cnt = jnp.sum(ohsum, axis=1, keepdims=True)

    @pl.when((ph == 0) & (i == 0))
    def _():
        tot_ref[...] = jnp.zeros_like(tot_ref)

    @pl.when(ph == 0)
    def _():
        tot_ref[...] = tot_ref[...] + tile_cnt

    @pl.when((ph == 1) & (i == 0))
    def _():
        cnt = tot_ref[...]
        padded = jnp.floor((cnt + (blk - 1)) * (1.0 / blk)) * blk
        sub = lax.broadcasted_iota(I32, cnt.shape, 0)
        start = jnp.zeros_like(cnt)
        for e in range(N_EXPERTS):
            start = start + jnp.where(sub > e, padded[e:e + 1, :], 0.0)
        ps_ref[...] = start
        run_ref[...] = jnp.zeros_like(run_ref)
        pad_end = start[:, 0:1] + padded[:, 0:1]
        blk_lo = (lax.broadcasted_iota(I32, (N_EXPERTS, be_ref.shape[1]), 1) * blk).astype(F32)
        n_le = jnp.sum((pad_end <= blk_lo).astype(F32), axis=0, keepdims=True)
        be_ref[...] = jnp.minimum(n_le, N_EXPERTS - 1.0).astype(I32)
        nu_ref[...] = (jnp.max(pad_end, axis=0, keepdims=True) * (1.0 / blk) + jnp.zeros(nu_ref.shape, F32)).astype(I32)

    @pl.when(ph == 1)
    def _():
        t_r = lax.broadcasted_iota(I32, (tm, tm), 0)
        t_c = lax.broadcasted_iota(I32, (tm, tm), 1)
        before = _dot(ohsum, (t_r < t_c).astype(F32))
        base = before + run_ref[:, 0:1] + ps_ref[:, 0:1]
        for k in range(TOP_K):
            w_ref[k:k + 1, :] = ws[k]
            dest_ref[k:k + 1, :] = jnp.sum(oh[k] * base, axis=0, keepdims=True).astype(I32)
        run_ref[...] = run_ref[...] + tile_cnt


def _route_tables(logits_t, router_b, blk):
    n = logits_t.shape[1]
    tm = min(512, n)
    a = n * TOP_K
    n_blocks = -(-a // blk) + N_EXPERTS
    nb_pad = -(-n_blocks // 128) * 128
    tok_blk = lambda p, i: (0, i * p)
    const = lambda p, i: (0, 0)
    w, dest, block_e, n_used = pl.pallas_call(
        functools.partial(_route_kernel, blk=blk),
        grid=(2, n // tm),
        in_specs=[pl.BlockSpec((N_EXPERTS, tm), lambda p, i: (0, i)), pl.BlockSpec((N_EXPERTS, 1), const)],
        out_specs=(pl.BlockSpec((TOP_K, tm), tok_blk), pl.BlockSpec((TOP_K, tm), tok_blk),
                   pl.BlockSpec((1, nb_pad), const), pl.BlockSpec((1, 128), const)),
        out_shape=(jax.ShapeDtypeStruct((TOP_K, n), F32), jax.ShapeDtypeStruct((TOP_K, n), I32),
                   jax.ShapeDtypeStruct((1, nb_pad), I32), jax.ShapeDtypeStruct((1, 128), I32)),
        scratch_shapes=[pltpu.VMEM((N_EXPERTS, 128), F32)] * 3,
        compiler_params=_cparams(("arbitrary", "arbitrary")),
        name="moe_route",
    )(logits_t, router_b.reshape(N_EXPERTS, 1).astype(F32))
    slot_of = dest.T.reshape(-1)
    tok = jnp.repeat(jnp.arange(n, dtype=I32), TOP_K)
    slot_tok = jnp.zeros((n_blocks * blk,), I32).at[slot_of].set(tok)
    return slot_tok, slot_of, w.T, block_e[0, :n_blocks], n_used[0, :1], n_blocks


def _row_copy(src_hbm, row, buf, slot, r, sem):
    return pltpu.make_async_copy(src_hbm.at[pl.ds(row, 1), :], buf.at[slot, pl.ds(r, 1), :], sem.at[slot])


ROW_DMA_UNROLL = 8
MOE_BLK_PROMPT = 256


def _start_row_gather(src_hbm, row_of, buf, slot, n_rows, sem):
    unroll = math.gcd(ROW_DMA_UNROLL, n_rows)

    def body(j, c):
        for u in range(unroll):
            r = j * unroll + u
            _row_copy(src_hbm, row_of(r), buf, slot, r, sem).start(priority=u % 2)
        return c
    lax.fori_loop(0, n_rows // unroll, body, 0)


def _wait_row_gather(src_hbm, buf, slot, n_rows, sem):
    unroll = math.gcd(ROW_DMA_UNROLL, n_rows)

    def body(j, c):
        for u in range(unroll):
            _row_copy(src_hbm, 0, buf, slot, j * unroll + u, sem).wait()
        return c
    lax.fori_loop(0, n_rows // unroll, body, 0)


def _moe_kernel(be_ref, tok_ref, nu_ref, x_hbm, w1_ref, w2_ref, y_ref, xs_buf, gsem, *, blk):
    i = pl.program_id(0)
    n_used = nu_ref[0]
    slot = i % 2

    def start_gather(bi, sl):
        _start_row_gather(x_hbm, lambda r: tok_ref[bi * blk + r], xs_buf, sl, blk, gsem)

    @pl.when(i == 0)
    def _():
        start_gather(0, 0)

    @pl.when(i + 1 < n_used)
    def _():
        start_gather(i + 1, 1 - slot)

    @pl.when(i < n_used)
    def _():
        _wait_row_gather(x_hbm, xs_buf, slot, blk, gsem)
        z = _dot(xs_buf[slot], w1_ref[...])
        de = z.shape[1] // 2
        act = _silu(z[:, :de]) * z[:, de:]
        y_ref[...] = _dot(act, w2_ref[...])

    @pl.when(i >= n_used)
    def _():
        y_ref[...] = jnp.zeros_like(y_ref)


def _moe_ffn(h2, slot_tok, block_e, n_used, n_blocks, w_in_bf, w_out_bf, layer, blk, name):
    n, d = h2.shape
    de2 = w_in_bf.shape[-1]
    grid_spec = pltpu.PrefetchScalarGridSpec(
        num_scalar_prefetch=3,
        grid=(n_blocks,),
        in_specs=[
            pl.BlockSpec(memory_space=pl.ANY),
            pl.BlockSpec((None, None, d, de2), lambda i, be, tk, nu: (layer, be[i], 0, 0)),
            pl.BlockSpec((None, None, de2 // 2, d), lambda i, be, tk, nu: (layer, be[i], 0, 0)),
        ],
        out_specs=pl.BlockSpec((blk, d), lambda i, be, tk, nu: (i, 0)),
        scratch_shapes=[pltpu.VMEM((2, blk, d), F32), pltpu.SemaphoreType.DMA((2,))],
    )
    return pl.pallas_call(
        functools.partial(_moe_kernel, blk=blk),
        grid_spec=grid_spec,
        out_shape=jax.ShapeDtypeStruct((n_blocks * blk, d), F32),
        compiler_params=_cparams(("arbitrary",), VMEM_LIMIT_BYTES),
        name=name,
    )(block_e, slot_tok, n_used, h2, w_in_bf, w_out_bf)


def _combine_kernel(so_ref, x_ref, wt_ref, gate_ref, lng_ref, lnb_ref, y_hbm, o_ref, ybuf, sem, *, tm):
    i = pl.program_id(0)
    nt = pl.num_programs(0)
    slot = i % 2

    def start_gather(ti, sl):
        for k in range(TOP_K):
            _start_row_gather(y_hbm, lambda r: so_ref[(ti * tm + r) * TOP_K + k], ybuf.at[k], sl, tm, sem.at[k])

    @pl.when(i == 0)
    def _():
        start_gather(0, 0)

    @pl.when(i + 1 < nt)
    def _():
        start_gather(i + 1, 1 - slot)

    for k in range(TOP_K):
        _wait_row_gather(y_hbm, ybuf.at[k], slot, tm, sem.at[k])

    wt = wt_ref[...]
    f = wt[:, 0:1] * ybuf[0, slot] + wt[:, 1:2] * ybuf[1, slot]
    o_ref[...] = _ln(ALPHA * x_ref[...] + gate_ref[...] * f) * lng_ref[...] + lnb_ref[...]


def _moe_combine(slot_of, x1, wts, gate, lng, lnb, y, tm, tpb, name):
    n, d = x1.shape
    grid_spec = pltpu.PrefetchScalarGridSpec(
        num_scalar_prefetch=1,
        grid=(n // tm,),
        in_specs=[
            pl.BlockSpec((tm, d), lambda i, so: (i, 0)),
            pl.BlockSpec((tm, TOP_K), lambda i, so: (i, 0)),
            pl.BlockSpec((None, gate.shape[1], d), lambda i, so: (i // tpb, 0, 0)),
            pl.BlockSpec((1, d), lambda i, so: (0, 0)),
            pl.BlockSpec((1, d), lambda i, so: (0, 0)),
            pl.BlockSpec(memory_space=pl.ANY),
        ],
        out_specs=pl.BlockSpec((tm, d), lambda i, so: (i, 0)),
        scratch_shapes=[pltpu.VMEM((TOP_K, 2, tm, d), F32), pltpu.SemaphoreType.DMA((TOP_K, 2))],
    )
    return pl.pallas_call(
        functools.partial(_combine_kernel, tm=tm),
        grid_spec=grid_spec,
        out_shape=jax.ShapeDtypeStruct((n, d), F32),
        compiler_params=_cparams(("arbitrary",)),
        name=name,
    )(slot_of, x1, wts, gate, lng, lnb, y)


def _channel_sublayer(x1, h2, logits, gate, lng, lnb, router_b, w_in, w_out, layer, blk, tm, tpb, tag):
    slot_tok, slot_of, wts, block_e, n_used, n_blocks = _route_tables(logits, router_b, blk)
    y = _moe_ffn(h2, slot_tok, block_e, n_used, n_blocks, w_in, w_out, layer, blk, "moe_ffn_" + tag)
    return _moe_combine(slot_of, x1, wts, gate, lng, lnb, y, tm, tpb, "moe_combine_" + tag)


def _rec_in_kernel(x_ref, sc_ref, sh_ref, w_ref, gg_ref, xbr_ref):
    hb = _modulate(x_ref[...], sc_ref[...], sh_ref[...]).astype(BF16)
    d = gg_ref.shape[1]
    gg_ref[...] = _gelu(jnp.dot(hb, w_ref[:, 0:d], preferred_element_type=F32))
    xbr_ref[...] = jnp.dot(hb, w_ref[:, d:], preferred_element_type=F32)


def _rec_in_prompt(x, sc, sh, w_bf, seq):
    n, d = x.shape
    dr = w_bf.shape[1] // 2
    tm = 512
    tpb = seq // tm
    row = lambda i: (i, 0)
    mod = lambda i: (i // tpb, 0, 0)
    return pl.pallas_call(
        _rec_in_kernel,
        grid=(n // tm,),
        in_specs=[
            pl.BlockSpec((tm, d), row),
            pl.BlockSpec((None, 1, d), mod),
            pl.BlockSpec((None, 1, d), mod),
            pl.BlockSpec(w_bf.shape, lambda i: (0, 0), pipeline_mode=pl.Buffered(1)),
        ],
        out_specs=(pl.BlockSpec((tm, dr), row), pl.BlockSpec((tm, dr), row)),
        out_shape=(jax.ShapeDtypeStruct((n, dr), F32), jax.ShapeDtypeStruct((n, dr), F32)),
        compiler_params=_cparams(("parallel",), VMEM_LIMIT_BYTES),
        name="rec_in_prompt",
    )(x, sc, sh, w_bf)


def _log1p(y):
    w = 1.0 + y
    return jnp.where(w == 1.0, y, jnp.log(w) * (y / jnp.where(w == 1.0, 1.0, w - 1.0)))


def _expm1(x):
    u = jnp.exp(x)
    safe = (u != 1.0) & (u > 0.0)
    return jnp.where(u == 1.0, x, jnp.where(u > 0.0, (u - 1.0) * (x / jnp.where(safe, jnp.log(u), 1.0)), -1.0))


def _softplus(x):
    return jnp.maximum(x, 0.0) + _log1p(jnp.exp(-jnp.abs(x)))


def _rg_terms(xb, z, ba, bx, lam):
    bs = xb.shape[1]
    r = jax.nn.sigmoid(z[:, :bs] + ba)
    gi = jax.nn.sigmoid(z[:, bs:] + bx)
    log_a = -RG_C * r * _softplus(-lam)
    a = jnp.exp(log_a)
    return a, jnp.sqrt(-_expm1(2.0 * log_a)) * (gi * xb)


def _scan_rows(a, b, h_in):
    tm = a.shape[0]
    sub = lax.broadcasted_iota(I32, a.shape, 0) & 7
    for step in (1, 2, 4):
        keep = sub >= step
        b = jnp.where(keep, a * pltpu.roll(b, step, 0) + b, b)
        a = jnp.where(keep, a * pltpu.roll(a, step, 0), a)
    outs = []
    h = h_in
    for j in range(tm // 8):
        hj = a[8 * j:8 * j + 8] * h + b[8 * j:8 * j + 8]
        outs.append(hj)
        h = hj[7:8]
    return jnp.concatenate(outs, axis=0), h


def _rec_gate_kernel(x_ref, cw_ref, cb_ref, wab_ref, ba_ref, bx_ref, lam_ref, hs_ref, ht_ref, carry_ref, h_ref,
                     *, tpb, tm):
    i = pl.program_id(0)

    @pl.when(i % tpb == 0)
    def _():
        carry_ref[...] = jnp.zeros_like(carry_ref)
        h_ref[...] = jnp.zeros_like(h_ref)

    x = x_ref[...]
    carry = carry_ref[...]
    d = x.shape[1]
    row8 = lax.broadcasted_iota(I32, (8, d), 0)
    xc = cb_ref[...] + x * cw_ref[CONV_W - 1:CONV_W, :]
    for k in range(1, CONV_W):
        xr = pltpu.roll(x, k, 0)
        head = jnp.where(row8 < k, pltpu.roll(carry, k, 0), xr[0:8])
        xk = jnp.concatenate([head, xr[8:]], axis=0)
        xc = xc + xk * cw_ref[CONV_W - 1 - k:CONV_W - k, :]
    carry_ref[...] = x[tm - 8:tm]
    bs = d // RNN_BLOCKS
    for nb in range(RNN_BLOCKS):
        sl = slice(nb * bs, (nb + 1) * bs)
        xb = xc[:, sl]
        a, bt = _rg_terms(xb, _dot(xb, wab_ref[nb]), ba_ref[:, sl], bx_ref[:, sl], lam_ref[:, sl])
        hs, h_last = _scan_rows(a, bt, h_ref[0:1, sl])
        hs_ref[:, sl] = hs
        h_ref[:, sl] = jnp.broadcast_to(h_last, (8, bs))
    ht_ref[...] = h_ref[...]


def _rec_scan_prompt(xbr, cw, cb, wab, ba, bx, lam, seq):
    n, d = xbr.shape
    tm = 256
    tpb = seq // tm
    row = lambda i: (i, 0)
    const2 = lambda i: (0, 0)
    return pl.pallas_call(
        functools.partial(_rec_gate_kernel, tpb=tpb, tm=tm),
        grid=(n // tm,),
        in_specs=[
            pl.BlockSpec((tm, d), row),
            pl.BlockSpec(cw.shape, const2),
            pl.BlockSpec((1, d), const2),
            pl.BlockSpec(wab.shape, lambda i: (0, 0, 0)),
            pl.BlockSpec((1, d), const2),
            pl.BlockSpec((1, d), const2),
            pl.BlockSpec((1, d), const2),
        ],
        out_specs=(pl.BlockSpec((tm, d), row), pl.BlockSpec((None, 8, d), lambda i: (i // tpb, 0, 0))),
        out_shape=(jax.ShapeDtypeStruct((n, d), F32), jax.ShapeDtypeStruct((n // seq, 8, d), F32)),
        scratch_shapes=[pltpu.VMEM((8, d), F32), pltpu.VMEM((8, d), F32)],
        compiler_params=_cparams(("arbitrary",), VMEM_LIMIT_BYTES),
        name="rec_scan_prompt",
    )(xbr, cw, cb, wab, ba, bx, lam)


def _rec_step_sample(z, buf0, buf1, buf2, h0, cw, cb, wa, wx, ba, bx, lam):
    d = h0.shape[1]
    gate_br = z[:, :d]
    xbr = z[:, d:]
    xc = cb + buf0 * cw[0:1] + buf1 * cw[1:2] + buf2 * cw[2:3] + xbr * cw[3:4]
    bs = d // RNN_BLOCKS
    a_parts, b_parts = [], []
    for nb in range(RNN_BLOCKS):
        sl = slice(nb * bs, (nb + 1) * bs)
        xb = xc[:, sl]
        zz = jnp.concatenate([_dot(xb, wa[nb]), _dot(xb, wx[nb])], axis=1)
        a, bt = _rg_terms(xb, zz, ba[:, sl], bx[:, sl], lam[:, sl])
        a_parts.append(a)
        b_parts.append(bt)
    h = jnp.concatenate(a_parts, axis=1) * h0 + jnp.concatenate(b_parts, axis=1)
    return _gelu(gate_br) * h, h, xbr


def kernel(x_prompt, x_sample, c_prompt, c_sample, cache_nsa_cmp, cache_nsa_slc, state_nsa_win, state_rglru_conv, state_rglru_h, page_table, ada_w, ada_b, ln_g, ln_b, ab_w_in, ab_w_out, gmlp_ln_g, gmlp_ln_b, gmlp_ws, gmlp_bs, nsa_cmp_pe, nsa_cmp_w1, nsa_cmp_w2, rec_w_in, rec_conv_w, rec_conv_b, rg_wa, rg_ba, rg_wx, rg_bx, rg_lambda, rec_w_out, router_w, router_b, moe_w_in, moe_w_out):
    bp, seq, d = x_prompt.shape
    db = x_sample.shape[0]
    assert x_sample.shape[1] == 1
    n_pool, page = cache_nsa_cmp.shape[:2]
    n_pages = page_table.shape[1]
    past_len = n_pages * page
    assert seq % 256 == 0 and past_len % SLC_LEN == 0 and page % SLC_LEN == 0 and past_len >= WINDOW
    n_p = bp * seq
    d_rnn = rec_conv_b.shape[0]

    mods = _ada_all(jnp.concatenate([c_prompt, c_sample], axis=0), ada_w, ada_b)

    def mod_rows(layer, sub):
        m = mods[layer * 2 + sub]
        parts = [m[:, j * d:(j + 1) * d] for j in range(3)]
        return [p[:bp].reshape(bp, 1, d) for p in parts], [p[bp:] for p in parts]

    lnrow = lambda a: a.reshape(1, d)

    gperm = np.array([(g * HPG + h) * N_BRANCH + br for g in range(N_KV) for br in range(N_BRANCH) for h in range(HPG)])
    w_gate = ab_w_in[:, O_G:][:, gperm]
    w_ab_bf = jnp.concatenate([ab_w_in[:, :O_G], w_gate], axis=1).astype(BF16)
    tril = jnp.tril(jnp.ones((CHUNK, CHUNK), F32))
    wtril = (gmlp_ws * tril).astype(BF16)
    bst = gmlp_bs.T
    glng = gmlp_ln_g.reshape(1, A_WIDTH)
    glnb = gmlp_ln_b.reshape(1, A_WIDTH)
    pe_rows, w1bd, w2bd = _compress_weights(nsa_cmp_pe, nsa_cmp_w1, nsa_cmp_w2)
    rw_t = router_w.T
    xp = x_prompt.reshape(n_p, d)
    xs = x_sample.reshape(db, d)

    (sh_p, sc_p, g_p), (sh_s, sc_s, g_s) = mod_rows(0, 0)
    (sh2_p, sc2_p, g2_p), (sh2_s, sc2_s, g2_s) = mod_rows(0, 1)
    cos_p, sin_p = _rope_tables(np.arange(seq))
    a_p, q_p, kvc_p, kvs_p, kvw_p, kvt_p, gt_p = _ab_proj_prompt(
        xp, sc_p, sh_p, w_ab_bf, cos_p, sin_p, glng, glnb, wtril, bst, seq)
    kcv_p = _compress_prompt(kvc_p.reshape(bp, seq // CMP_STRIDE, CMP_STRIDE * KV_WIDTH), pe_rows, w1bd, w2bd)
    o_p, moe_in_bf, moe_out_bf = _nsa_prompt(
        q_p, gt_p, kcv_p, kvt_p, bp, seq,
        [moe_w_in.reshape(-1, moe_w_in.shape[-1]), moe_w_out.reshape(-1, moe_w_out.shape[-1])])
    moe_in_bf = moe_in_bf.reshape(moe_w_in.shape)
    moe_out_bf = moe_out_bf.reshape(moe_w_out.shape)
    w_out_bf = ab_w_out.astype(BF16)
    x1_p, h2_p, lg_p = _mix_out_prompt(a_p, o_p, w_out_bf, xp, g_p, lnrow(ln_g[0, 0]), lnrow(ln_b[0, 0]),
                                       sc2_p, sh2_p, rw_t, seq, True, "mix_out_l0_prompt")

    z_s = _small_mm(_modulate, [xs, sc_s, sh_s], w_ab_bf, O_G, 512, "ab_proj_sample")
    zg_s = _small_mm(_modulate, [xs, sc_s, sh_s], w_gate, N_GATE, N_GATE, "ab_gate_sample")
    cos_s, sin_s = _rope_tables(np.full((db,), past_len))
    ws0 = jnp.repeat(gmlp_ws[:, 0, 0], A_GROUP_DIM).reshape(1, A_WIDTH)
    bs0 = jnp.repeat(gmlp_bs[:, 0], A_GROUP_DIM).reshape(1, A_WIDTH)
    sds = lambda *s: jax.ShapeDtypeStruct(s, F32)
    a_s, v_s, q_s, kvc_s, kvs_s, kvw_s, gt_s = _vmem_call(
        _ab_post_sample,
        (sds(db, A_WIDTH), sds(db, A_WIDTH), sds(db, B_WIDTH), sds(db, KV_WIDTH), sds(db, KV_WIDTH),
         sds(db, KV_WIDTH), sds(db, N_GATE)),
        (z_s, zg_s, cos_s, sin_s, glng, glnb, ws0, bs0), "ab_post_sample")
    q3 = q_s.reshape(db, 1, B_WIDTH)
    page_t = lambda c: jnp.transpose(c, (0, 2, 3, 4, 1)).reshape(n_pool, KV_WIDTH, page)
    oc3, sel_idx = _nsa_sample_cmp(page_table, q3, page_t(cache_nsa_cmp), pe_rows, w1bd, w2bd, past_len)
    sel_idx = jnp.transpose(sel_idx, (0, 2, 1))
    o_s = _nsa_sample_attn(page_table, sel_idx, q3, gt_s.reshape(db, 1, N_GATE), oc3,
                           kvs_s.reshape(db, 1, KV_WIDTH), kvw_s.reshape(db, 1, KV_WIDTH),
                           jnp.transpose(state_nsa_win, (0, 2, 3, 4, 1)).reshape(db, KV_WIDTH, -1),
                           page_t(cache_nsa_slc),
                           past_len).reshape(db, B_WIDTH)
    f_s = _small_mm(lambda a, o: jnp.concatenate([a, o], axis=1), [a_s, o_s], w_out_bf, d, 512, "mix_out_l0_sample")
    x1_s, h2_s, lg_s = _vmem_call(
        _post_ln_router, (sds(db, d), sds(db, d), sds(N_EXPERTS, db)),
        (f_s, xs, g_s, lnrow(ln_g[0, 0]), lnrow(ln_b[0, 0]), sc2_s, sh2_s, rw_t), "post_l0_sample")

    x2_p = _channel_sublayer(x1_p, h2_p, lg_p, g2_p, lnrow(ln_g[0, 1]), lnrow(ln_b[0, 1]), router_b,
                             moe_in_bf, moe_out_bf, 0,MOE_BLK_PROMPT, 128, seq // 128, "l0_prompt")
    x2_s = _channel_sublayer(x1_s, h2_s, lg_s, g2_s.reshape(1, db, d), lnrow(ln_g[0, 1]), lnrow(ln_b[0, 1]), router_b,
                             moe_in_bf, moe_out_bf, 0,32, db, 1, "l0_sample")

    (sh_p, sc_p, g_p), (sh_s, sc_s, g_s) = mod_rows(1, 0)
    (sh2_p, sc2_p, g2_p), (sh2_s, sc2_s, g2_s) = mod_rows(1, 1)
    rec_in_bf = rec_w_in.astype(BF16)
    rec_out_bf = rec_w_out.astype(BF16)
    gg_p, xbr_p = _rec_in_prompt(x2_p, sc_p, sh_p, rec_in_bf, seq)
    wab = jnp.concatenate([rg_wa, rg_wx], axis=2).astype(BF16)
    row_r = lambda a: a.reshape(1, d_rnn)
    hs_p, ht = _rec_scan_prompt(xbr_p, rec_conv_w, row_r(rec_conv_b), wab, row_r(rg_ba), row_r(rg_bx),
                                row_r(rg_lambda), seq)
    x3_p, h4_p, lg_p = _mix_out_prompt(gg_p, hs_p, rec_out_bf, x2_p, g_p,
                                       lnrow(ln_g[1, 0]), lnrow(ln_b[1, 0]), sc2_p, sh2_p, rw_t, seq, False,
                                       "mix_out_l1_prompt")
    conv_p = xbr_p.reshape(bp, seq, d_rnn)[:, seq - (CONV_W - 1):]
    h_p = ht[:, 0, :]

    zr_s = _small_mm(_modulate, [x2_s, sc_s, sh_s], rec_in_bf, 2 * d_rnn, 512, "rec_in_sample")
    y_s, h_s, xbr_s = _vmem_call(
        _rec_step_sample, (sds(db, d_rnn), sds(db, d_rnn), sds(db, d_rnn)),
        (zr_s, state_rglru_conv[:, 0], state_rglru_conv[:, 1], state_rglru_conv[:, 2], state_rglru_h,
         rec_conv_w, row_r(rec_conv_b), rg_wa, rg_wx, row_r(rg_ba), row_r(rg_bx), row_r(rg_lambda)),
        "rec_step_sample")
    f_s = _small_mm(lambda y: y, [y_s], rec_out_bf, d, 512, "mix_out_l1_sample")
    x3_s, h4_s, lg_s = _vmem_call(
        _post_ln_router, (sds(db, d), sds(db, d), sds(N_EXPERTS, db)),
        (f_s, x2_s, g_s, lnrow(ln_g[1, 0]), lnrow(ln_b[1, 0]), sc2_s, sh2_s, rw_t), "post_l1_sample")
    conv_s = jnp.concatenate([state_rglru_conv[:, 1:], xbr_s[:, None, :]], axis=1)

    y_p = _channel_sublayer(x3_p, h4_p, lg_p, g2_p, lnrow(ln_g[1, 1]), lnrow(ln_b[1, 1]), router_b,
                            moe_in_bf, moe_out_bf, 1,MOE_BLK_PROMPT, 128, seq // 128, "l1_prompt")
    y_s = _channel_sublayer(x3_s, h4_s, lg_s, g2_s.reshape(1, db, d), lnrow(ln_g[1, 1]), lnrow(ln_b[1, 1]), router_b,
                            moe_in_bf, moe_out_bf, 1,32, db, 1, "l1_sample")

    kv5 = lambda a, b_: a.reshape(b_, -1, 2, N_KV, HEAD_DIM)
    keep = min(WINDOW, seq)
    win_p = kv5(kvw_p, bp)[:, seq - keep:]
    kw_full = jnp.concatenate([state_nsa_win, kv5(kvw_s, db)], axis=1)
    win_s = kw_full[:, kw_full.shape[1] - min(WINDOW, kw_full.shape[1]):]
    return (y_p.reshape(bp, seq, d), y_s.reshape(db, 1, d), kv5(kvc_p, bp), kv5(kvc_s, db), kv5(kvs_p, bp),
            kv5(kvs_s, db), win_p, win_s, v_s.reshape(db, 1, A_WIDTH), conv_p, conv_s, h_p, h_s)
```

```python
import functools
import math

import numpy as np
import jax
import jax.numpy as jnp
from jax import lax
from jax.experimental import pallas as pl
from jax.experimental.pallas import tpu as pltpu

F32 = jnp.float32
BF16 = jnp.bfloat16
I32 = jnp.int32

A_GROUPS = 8
A_GROUP_DIM = 128
A_WIDTH = A_GROUPS * A_GROUP_DIM
CHUNK = 128
N_HEADS = 16
N_KV = 4
HEAD_DIM = 64
HPG = N_HEADS // N_KV
B_WIDTH = N_HEADS * HEAD_DIM
KV_WIDTH = 2 * N_KV * HEAD_DIM
N_BRANCH = 3
CMP_LEN = 32
CMP_STRIDE = 16
CMP_HID = 2 * HEAD_DIM
SLC_LEN = 64
N_SEL = 16
WINDOW = 512
Q_BLOCK = 128
FORCE_BONUS = 1e4
ROPE_THETA = 10000.0
RNN_BLOCKS = 16
CONV_W = 4
RG_C = 8.0
N_EXPERTS = 16
N_GROUPS = 4
EXPERTS_PER_GROUP = N_EXPERTS // N_GROUPS
TOP_K = 2
DEPTH = 2
ALPHA = (2 * DEPTH) ** 0.25
LN_EPS = 1e-5
NEG = -1e30

O_U = 0
O_V = A_WIDTH
O_Q = 2 * A_WIDTH
O_KV = O_Q + B_WIDTH
O_G = O_KV + 3 * KV_WIDTH
N_GATE = N_BRANCH * N_HEADS

VMEM_LIMIT_BYTES = 56 * 1024 * 1024


def _cparams(sem, vmem=None):
    return pltpu.CompilerParams(dimension_semantics=sem, vmem_limit_bytes=vmem)


def _dot(a, b):
    return jnp.dot(a.astype(BF16), b.astype(BF16), preferred_element_type=F32)


def _dot_nt(a, b):
    dn = (((1,), (1,)), ((), ()))
    return lax.dot_general(a.astype(BF16), b.astype(BF16), dn, preferred_element_type=F32)


def _ln(x):
    mu = jnp.mean(x, -1, keepdims=True)
    xc = x - mu
    var = jnp.mean(xc * xc, -1, keepdims=True)
    return xc * lax.rsqrt(var + LN_EPS)


def _silu(x):
    return x * jax.nn.sigmoid(x)


def _gelu(x):
    return jax.nn.gelu(x, approximate=True)


def _rope(x, cos2, sin2):
    w = x.shape[1]
    rep = w // 128
    cos = jnp.concatenate([cos2] * rep, axis=1) if rep > 1 else cos2
    sin = jnp.concatenate([sin2] * rep, axis=1) if rep > 1 else sin2
    lane = lax.broadcasted_iota(I32, x.shape, 1)
    first = (lane & (HEAD_DIM - 1)) < HEAD_DIM // 2
    rot = jnp.where(first, pltpu.roll(x, w - HEAD_DIM // 2, 1), pltpu.roll(x, HEAD_DIM // 2, 1))
    return x * cos + rot * sin


def _rope_tables(pos):
    half = HEAD_DIM // 2
    inv = ROPE_THETA ** (-np.arange(half, dtype=np.float64) / half)
    ang = np.asarray(pos, np.float64)[:, None] * inv[None, :]
    cos = np.tile(np.cos(ang), (1, 4))
    sin = np.tile(np.concatenate([-np.sin(ang), np.sin(ang)], axis=1), (1, 2))
    return jnp.asarray(cos, F32), jnp.asarray(sin, F32)


def _softmax_rows(s, mask):
    sm = jnp.where(mask, s, NEG)
    m = jnp.max(sm, -1, keepdims=True)
    p = jnp.where(mask, jnp.exp(sm - m), 0.0)
    return p, jnp.sum(p, -1, keepdims=True)


def _ada_kernel(c_ref, w_ref, b_ref, o_ref):
    o_ref[...] = _dot(_silu(c_ref[...]), w_ref[...]) + b_ref[...]


def _ada_all(c_all, ada_w, ada_b):
    r, d = c_all.shape
    n_mod = ada_w.shape[0] * ada_w.shape[1]
    d3 = ada_w.shape[-1]
    tn = 512
    return pl.pallas_call(
        _ada_kernel,
        grid=(n_mod, d3 // tn),
        in_specs=[
            pl.BlockSpec((r, d), lambda l, j: (0, 0)),
            pl.BlockSpec((None, d, tn), lambda l, j: (l, 0, j)),
            pl.BlockSpec((None, 1, tn), lambda l, j: (l, 0, j)),
        ],
        out_specs=pl.BlockSpec((None, r, tn), lambda l, j: (l, 0, j)),
        out_shape=jax.ShapeDtypeStruct((n_mod, r, d3), F32),
        compiler_params=_cparams(("parallel", "parallel")),
        name="ada_mod",
    )(c_all, ada_w.reshape(n_mod, d, d3), ada_b.reshape(n_mod, 1, d3))


def _small_mm_kernel(*refs, n_x, pre):
    xs = [r[...] for r in refs[:n_x]]
    w_ref, o_ref = refs[n_x], refs[n_x + 1]
    o_ref[...] = _dot(pre(*xs), w_ref[...])


def _small_mm(pre, xs, w, n_out, tn, name):
    m = xs[0].shape[0]
    k = w.shape[0]
    in_specs = [pl.BlockSpec(x.shape, lambda j, nd=x.ndim: (0,) * nd) for x in xs]
    in_specs.append(pl.BlockSpec((k, tn), lambda j: (0, j)))
    return pl.pallas_call(
        functools.partial(_small_mm_kernel, n_x=len(xs), pre=pre),
        grid=(n_out // tn,),
        in_specs=in_specs,
        out_specs=pl.BlockSpec((m, tn), lambda j: (0, j)),
        out_shape=jax.ShapeDtypeStruct((m, n_out), F32),
        compiler_params=_cparams(("parallel",)),
        name=name,
    )(*xs, w)


def _vmem_call(fn, out_shapes, args, name):
    n_in = len(args)

    def kern(*refs):
        res = fn(*[r[...] for r in refs[:n_in]])
        for o, v in zip(refs[n_in:], res):
            o[...] = v

    return pl.pallas_call(kern, out_shape=out_shapes, name=name)(*args)


def _modulate(x, sc, sh):
    return x * (1.0 + sc) + sh


def _ab_proj_kernel(x_ref, sc_ref, sh_ref, w_ref, cos_ref, sin_ref, lng_ref, lnb_ref, wtril_ref, bst_ref,
                    a_ref, q_ref, kvc_ref, kvs_ref, kvw_ref, kvt_ref, gt_ref, *, tm):
    hb = _modulate(x_ref[...], sc_ref[...], sh_ref[...]).astype(BF16)

    def proj(lo, hi):
        return jnp.dot(hb, w_ref[:, lo:hi], preferred_element_type=F32)

    cos2 = cos_ref[...]
    sin2 = sin_ref[...]
    zu = proj(O_U, O_V)
    zv = proj(O_V, O_Q)
    for g in range(A_GROUPS):
        sl = slice(g * A_GROUP_DIM, (g + 1) * A_GROUP_DIM)
        vg = _ln(zv[:, sl]) * lng_ref[:, sl] + lnb_ref[:, sl]
        for c in range(tm // CHUNK):
            rs = slice(c * CHUNK, (c + 1) * CHUNK)
            mix = jnp.dot(wtril_ref[g], vg[rs].astype(BF16), preferred_element_type=F32) + bst_ref[:, g:g + 1]
            a_ref[rs, sl] = zu[rs, sl] * mix
    q_ref[...] = _rope(proj(O_Q, O_KV), cos2, sin2)
    half = KV_WIDTH // 2
    for br, ref in enumerate((kvc_ref, kvs_ref, kvw_ref)):
        lo = O_KV + br * KV_WIDTH
        k = _rope(proj(lo, lo + half), cos2, sin2)
        v = proj(lo + half, lo + KV_WIDTH)
        ref[:, 0:half] = k
        ref[:, half:KV_WIDTH] = v
        for g in range(N_KV):
            hs = slice(g * HEAD_DIM, (g + 1) * HEAD_DIM)
            kvt_ref[br, 0, g] = k[:, hs]
            kvt_ref[br, 1, g] = v[:, hs]
    zg = jax.nn.sigmoid(proj(O_G, O_G + N_GATE))
    per_g = N_GATE // N_KV
    for g in range(N_KV):
        gt_ref[g] = zg[:, g * per_g:(g + 1) * per_g]


def _ab_proj_prompt(x, sc, sh, w_bf, cos2, sin2, lng, lnb, wtril, bst, seq):
    n, d = x.shape
    tm = 256
    tpb = seq // tm
    n_in = w_bf.shape[1]
    row = lambda i: (i, 0)
    mod = lambda i: (i // tpb, 0, 0)
    const2 = lambda i: (0, 0)
    pos = lambda i: (i % tpb, 0)
    out_shape = (
        jax.ShapeDtypeStruct((n, A_WIDTH), F32),
        jax.ShapeDtypeStruct((n, B_WIDTH), F32),
        jax.ShapeDtypeStruct((n, KV_WIDTH), F32),
        jax.ShapeDtypeStruct((n, KV_WIDTH), F32),
        jax.ShapeDtypeStruct((n, KV_WIDTH), F32),
        jax.ShapeDtypeStruct((3, 2, N_KV, n, HEAD_DIM), F32),
        jax.ShapeDtypeStruct((N_KV, n, N_GATE // N_KV), F32),
    )
    return pl.pallas_call(
        functools.partial(_ab_proj_kernel, tm=tm),
        grid=(n // tm,),
        in_specs=[
            pl.BlockSpec((tm, d), row),
            pl.BlockSpec((None, 1, d), mod),
            pl.BlockSpec((None, 1, d), mod),
            pl.BlockSpec((d, n_in), const2, pipeline_mode=pl.Buffered(1)),
            pl.BlockSpec((tm, 128), pos),
            pl.BlockSpec((tm, 128), pos),
            pl.BlockSpec((1, A_WIDTH), const2),
            pl.BlockSpec((1, A_WIDTH), const2),
            pl.BlockSpec((A_GROUPS, CHUNK, CHUNK), lambda i: (0, 0, 0)),
            pl.BlockSpec((CHUNK, A_GROUPS), const2),
        ],
        out_specs=(
            pl.BlockSpec((tm, A_WIDTH), row),
            pl.BlockSpec((tm, B_WIDTH), row),
            pl.BlockSpec((tm, KV_WIDTH), row),
            pl.BlockSpec((tm, KV_WIDTH), row),
            pl.BlockSpec((tm, KV_WIDTH), row),
            pl.BlockSpec((3, 2, N_KV, tm, HEAD_DIM), lambda i: (0, 0, 0, i, 0)),
            pl.BlockSpec((N_KV, tm, N_GATE // N_KV), lambda i: (0, i, 0)),
        ),
        out_shape=out_shape,
        compiler_params=_cparams(("parallel",), VMEM_LIMIT_BYTES),
        name="ab_proj_prompt",
    )(x, sc, sh, w_bf, cos2, sin2, lng, lnb, wtril, bst)


def _ab_post_sample(z, zg, cos2, sin2, lng, lnb, ws0, bs0):
    zu = z[:, O_U:O_V]
    zv = z[:, O_V:O_Q]
    vs = []
    for g in range(A_GROUPS):
        sl = slice(g * A_GROUP_DIM, (g + 1) * A_GROUP_DIM)
        vs.append(_ln(zv[:, sl]) * lng[:, sl] + lnb[:, sl])
    v = jnp.concatenate(vs, axis=1)
    a = zu * (ws0 * v + bs0)
    q = _rope(z[:, O_Q:O_KV], cos2, sin2)
    half = KV_WIDTH // 2
    kvs = []
    for br in range(3):
        lo = O_KV + br * KV_WIDTH
        k = _rope(z[:, lo:lo + half], cos2, sin2)
        kvs.append(jnp.concatenate([k, z[:, lo + half:lo + KV_WIDTH]], axis=1))
    return a, v, q, kvs[0], kvs[1], kvs[2], jax.nn.sigmoid(zg)


def _compress_mlp(lhs_lo, lhs_hi, w1_ref, w2_ref, kv, n):
    lo = _dot(lhs_lo, w1_ref[kv, 0])
    hi = _dot(lhs_hi, w1_ref[kv, 1])
    hsum = lo + pltpu.roll(hi, n - 1, 0)
    return _dot(_gelu(hsum), w2_ref[kv])


def _compress_prompt_kernel(x_ref, pe_ref, w1_ref, w2_ref, o_ref, *, n16):
    for kv in range(2):
        for gp in range(N_KV // 2):
            base = kv * (KV_WIDTH // 2) + gp * 128
            xg = jnp.concatenate(
                [x_ref[:, s * KV_WIDTH + base:s * KV_WIDTH + base + 128] for s in range(CMP_STRIDE)], axis=1)
            r = _compress_mlp(xg + pe_ref[kv, 0], xg + pe_ref[kv, 1], w1_ref, w2_ref, kv, n16)
            o_ref[kv, 2 * gp] = r[:, 0:HEAD_DIM]
            o_ref[kv, 2 * gp + 1] = r[:, HEAD_DIM:2 * HEAD_DIM]


def _compress_prompt(kvc3, pe_rows, w1bd, w2bd):
    b, n16, wid = kvc3.shape
    return pl.pallas_call(
        functools.partial(_compress_prompt_kernel, n16=n16),
        grid=(b,),
        in_specs=[
            pl.BlockSpec((None, n16, wid), lambda i: (i, 0, 0)),
            pl.BlockSpec(pe_rows.shape, lambda i: (0, 0, 0, 0)),
            pl.BlockSpec(w1bd.shape, lambda i: (0, 0, 0, 0)),
            pl.BlockSpec(w2bd.shape, lambda i: (0, 0, 0)),
        ],
        out_specs=pl.BlockSpec((None, 2, N_KV, n16, HEAD_DIM), lambda i: (i, 0, 0, 0, 0)),
        out_shape=jax.ShapeDtypeStruct((b, 2, N_KV, n16, HEAD_DIM), F32),
        compiler_params=_cparams(("parallel",), VMEM_LIMIT_BYTES),
        name="nsa_compress_prompt",
    )(kvc3, pe_rows, w1bd, w2bd)


def _compress_weights(pe, w1, w2):
    eye2 = jnp.eye(2, dtype=F32)
    w1r = w1.reshape(2, 2, CMP_STRIDE, HEAD_DIM, CMP_HID)
    w1bd = jnp.einsum('khsdc,gG->khsgdGc', w1r, eye2).reshape(2, 2, CMP_STRIDE * 128, 2 * CMP_HID).astype(BF16)
    pe_rows = jnp.broadcast_to(pe.reshape(2, 2, CMP_STRIDE, 1, HEAD_DIM), (2, 2, CMP_STRIDE, 2, HEAD_DIM))
    pe_rows = pe_rows.reshape(2, 2, 1, CMP_STRIDE * 128)
    w2bd = jnp.einsum('kcd,gG->kgcGd', w2, eye2).reshape(2, 2 * CMP_HID, 2 * HEAD_DIM).astype(BF16)
    return pe_rows, w1bd, w2bd


def _cmp_slc_map(n_rows, n_cmp, n_slc, n_cols):
    cs = np.arange(n_rows)[:, None] * CMP_STRIDE
    ss = np.arange(n_cols)[None, :] * SLC_LEN
    m = (cs < ss + SLC_LEN) & (cs + CMP_LEN > ss)
    m &= (np.arange(n_rows)[:, None] < n_cmp) & (np.arange(n_cols)[None, :] < n_slc)
    return m.astype(np.float32)


def _nsa_prompt_kernel(q_ref, gt_ref, kc_ref, vc_ref, ks_ref, vs_ref, kw_ref, vw_ref, mapt_ref, e_ref,
                       *rest, n_cmp, n_slc, n_sel, kt_len, n_cast):
    o_ref, selb_ref = rest[n_cast], rest[-1]
    for src, dst in zip(rest[:n_cast], rest[n_cast + 1:2 * n_cast + 1]):
        dst[...] = src[...].astype(BF16)
    i = pl.program_id(2)
    q0 = i * Q_BLOCK
    rows = HPG * Q_BLOCK
    tile4 = lambda x: jnp.concatenate([x] * HPG, axis=0)
    qb = q_ref[...]
    q4 = jnp.concatenate([qb[:, h * HEAD_DIM:(h + 1) * HEAD_DIM] for h in range(HPG)], axis=0)
    q4 = (q4 * HEAD_DIM ** -0.5).astype(BF16)
    t_pos = q0 + lax.broadcasted_iota(I32, (Q_BLOCK, 1), 0)

    n16 = kc_ref.shape[0]
    n_idx = lax.broadcasted_iota(I32, (1, n16), 1)
    bias_c = jnp.where((n_idx * CMP_STRIDE + CMP_LEN - 1 <= t_pos) & (n_idx < n_cmp), 0.0, NEG)
    s = _dot_nt(q4, kc_ref[...]) + tile4(bias_c)
    p = jnp.exp(s - jnp.max(s, -1, keepdims=True))
    l = jnp.sum(p, -1, keepdims=True)
    pn = p * jnp.where((tile4(t_pos) >= CMP_LEN - 1) & (n_cmp > 0), 1.0 / l, 0.0)
    o_c = _dot(pn, vc_ref[...])
    psum = pn[0:Q_BLOCK]
    for h in range(1, HPG):
        psum = psum + pn[h * Q_BLOCK:(h + 1) * Q_BLOCK]

    imp_t = _dot_nt(mapt_ref[...], psum)
    blk_t = lax.broadcasted_iota(I32, (n_slc, Q_BLOCK), 0)
    cur_t = (q0 + lax.broadcasted_iota(I32, (n_slc, Q_BLOCK), 1)) // SLC_LEN
    forced = (blk_t == 0) | (blk_t == cur_t) | (blk_t == cur_t - 1)
    score = jnp.where(blk_t <= cur_t, imp_t + jnp.where(forced, FORCE_BONUS, 0.0), -jnp.inf)
    rank = jnp.zeros((n_slc, Q_BLOCK), I32)
    for j in range(n_slc):
        r = score[j:j + 1, :]
        rank = rank + ((r > score) | ((r == score) & (blk_t > j))).astype(I32)
    sel_t = ((rank < n_sel) & (score > -jnp.inf)).astype(F32)
    selb_ref[...] = (_dot(sel_t.T, e_ref[...]) - 1.0) * (-NEG)

    def sel_tile(kt, carry, causal):
        m, l, acc = carry
        k0 = pl.multiple_of(kt * kt_len, kt_len)
        bias = selb_ref[:, pl.ds(k0, kt_len)]
        if causal:
            kp = k0 + lax.broadcasted_iota(I32, (1, kt_len), 1)
            bias = bias + jnp.where(kp <= t_pos, 0.0, NEG)
        s = _dot_nt(q4, ks_ref[pl.ds(k0, kt_len), :]) + tile4(bias)
        m_new = jnp.maximum(m, jnp.max(s, -1, keepdims=True))
        alpha = jnp.exp(m - m_new)
        p = jnp.exp(s - m_new)
        l = alpha * l + jnp.sum(p, -1, keepdims=True)
        acc = alpha * acc + _dot(p, vs_ref[pl.ds(k0, kt_len), :])
        return m_new, l, acc

    last_kt = (q0 + Q_BLOCK - 1) // kt_len
    init = (jnp.full((rows, 1), NEG, F32), jnp.zeros((rows, 1), F32), jnp.zeros((rows, HEAD_DIM), F32))
    carry = lax.fori_loop(0, last_kt, functools.partial(sel_tile, causal=False), init)
    _, l, acc = sel_tile(last_kt, carry, True)
    o_s = acc * (1.0 / l)

    n_wt = WINDOW // Q_BLOCK + 1
    c_idx = lax.broadcasted_iota(I32, (Q_BLOCK, Q_BLOCK), 1)
    r_idx = lax.broadcasted_iota(I32, (Q_BLOCK, Q_BLOCK), 0)
    s_parts, v_parts = [], []
    for j in range(n_wt):
        ks_j = q0 - WINDOW + j * Q_BLOCK
        ld = pl.multiple_of(jnp.maximum(ks_j, 0), Q_BLOCK)
        off = jnp.where(ks_j >= 0, 0.0, NEG)
        sj = _dot_nt(q4, kw_ref[pl.ds(ld, Q_BLOCK), :])
        if j == 0:
            sj = sj + tile4(jnp.where(c_idx > r_idx, 0.0, NEG) + off)
        elif j == n_wt - 1:
            sj = sj + tile4(jnp.where(c_idx <= r_idx, 0.0, NEG))
        else:
            sj = sj + off
        s_parts.append(sj)
        v_parts.append(vw_ref[pl.ds(ld, Q_BLOCK), :])
    s = jnp.concatenate(s_parts, axis=1)
    p = jnp.exp(s - jnp.max(s, -1, keepdims=True))
    o_w = _dot(p, jnp.concatenate(v_parts, axis=0)) * (1.0 / jnp.sum(p, -1, keepdims=True))

    gt = gt_ref[...]

    def gcol(br):
        return jnp.concatenate([gt[:, br * HPG + h:br * HPG + h + 1] for h in range(HPG)], axis=0)

    o = gcol(0) * o_c + gcol(1) * o_s + gcol(2) * o_w
    o_ref[...] = jnp.concatenate([o[h * Q_BLOCK:(h + 1) * Q_BLOCK] for h in range(HPG)], axis=1)


def _nsa_prompt(q, gates, kcv, kvt, batch, seq, to_bf16):
    n = q.shape[0]
    nq = seq // Q_BLOCK
    n_steps = batch * N_KV * nq
    step = lambda b, g, i: ((b * N_KV + g) * nq + i, 0)
    cast_specs = [pl.BlockSpec((w.shape[0] // n_steps, w.shape[1]), step) for w in to_bf16]
    n16 = kcv.shape[3]
    n_cmp = n16 - 1
    n_slc = -(-seq // SLC_LEN)
    n_sel = min(N_SEL, n_slc)
    kt_len = min(512, seq)
    mapt = jnp.asarray(_cmp_slc_map(n16, n_cmp, n_slc, n_slc).T)
    expand = jnp.asarray((np.arange(n_slc)[:, None] == np.arange(seq)[None, :] // SLC_LEN).astype(np.float32), BF16)
    per_g = N_GATE // N_KV
    qrow = lambda b, g, i: (b * nq + i, g)
    kvspec = lambda br, kv: pl.BlockSpec((None, None, None, seq, HEAD_DIM), lambda b, g, i: (br, kv, g, b, 0))
    return pl.pallas_call(
        functools.partial(_nsa_prompt_kernel, n_cmp=n_cmp, n_slc=n_slc, n_sel=n_sel, kt_len=kt_len,
                          n_cast=len(to_bf16)),
        grid=(batch, N_KV, nq),
        in_specs=[
            pl.BlockSpec((Q_BLOCK, HPG * HEAD_DIM), qrow),
            pl.BlockSpec((None, Q_BLOCK, per_g), lambda b, g, i: (g, b * nq + i, 0)),
            pl.BlockSpec((None, None, None, n16, HEAD_DIM), lambda b, g, i: (b, 0, g, 0, 0)),
            pl.BlockSpec((None, None, None, n16, HEAD_DIM), lambda b, g, i: (b, 1, g, 0, 0)),
            kvspec(1, 0), kvspec(1, 1), kvspec(2, 0), kvspec(2, 1),
            pl.BlockSpec(mapt.shape, lambda b, g, i: (0, 0)),
            pl.BlockSpec(expand.shape, lambda b, g, i: (0, 0)),
        ] + cast_specs,
        out_specs=[pl.BlockSpec((Q_BLOCK, HPG * HEAD_DIM), qrow)] + cast_specs,
        out_shape=[jax.ShapeDtypeStruct((n, B_WIDTH), F32)] + [jax.ShapeDtypeStruct(w.shape, BF16) for w in to_bf16],
        scratch_shapes=[pltpu.VMEM((Q_BLOCK, seq), F32)],
        compiler_params=_cparams(("parallel", "parallel", "arbitrary"), VMEM_LIMIT_BYTES),
        name="nsa_attn_prompt",
    )(q, gates, kcv, kcv, kvt, kvt, kvt, kvt, mapt, expand, *to_bf16)


FILL_PAGES = 8


def _page_copy(cache_hbm, pt_ref, xbuf, sem, b, slot, p, n_pages):
    return pltpu.make_async_copy(cache_hbm.at[pt_ref[b * n_pages + p]], xbuf.at[slot, p], sem.at[slot])


def _nsa_sample_cmp_kernel(pt_ref, q_ref, cache_hbm, pe_ref, w1_ref, w2_ref, map_ref,
                           oc_ref, idx_ref, xbuf, xrow, lhs_a, lhs_b, sem, *, n_pages, n_cmp, n_slc, n_sel, pos):
    b = pl.program_id(0)
    nb = pl.num_programs(0)
    slot = b % 2
    n16 = lhs_a.shape[0]
    cpp = n16 // n_pages

    def start_all(bb, sl):
        def body(p, c):
            _page_copy(cache_hbm, pt_ref, xbuf, sem, bb, sl, p, n_pages).start()
            return c
        lax.fori_loop(0, n_pages, body, 0)

    @pl.when(b == 0)
    def _():
        start_all(0, 0)

    @pl.when(b + 1 < nb)
    def _():
        start_all(b + 1, 1 - slot)

    def wait_body(p, c):
        _page_copy(cache_hbm, pt_ref, xbuf, sem, b, slot, p, n_pages).wait()
        return c
    lax.fori_loop(0, n_pages, wait_body, 0)

    kc, vc = [], []
    wkv = N_KV * HEAD_DIM
    for kv, dst in ((0, kc), (1, vc)):
        def fill(j, c):
            for u in range(FILL_PAGES):
                p = j * FILL_PAGES + u
                r0 = pl.multiple_of(p * cpp, cpp)
                for gp, lhs in enumerate((lhs_a, lhs_b)):
                    xrow[u, gp] = xbuf[slot, p, kv * wkv + gp * 128:kv * wkv + (gp + 1) * 128, :].T
                    for s in range(CMP_STRIDE):
                        lhs[pl.ds(r0, cpp), s * 128:(s + 1) * 128] = xrow[u, gp, pl.ds(s, cpp, stride=CMP_STRIDE), :]
            return c
        lax.fori_loop(0, n_pages // FILL_PAGES, fill, 0)
        for lhs in (lhs_a, lhs_b):
            xg = lhs[...]
            r = _compress_mlp(xg + pe_ref[kv, 0], xg + pe_ref[kv, 1], w1_ref, w2_ref, kv, n16)
            dst.append(r[:, 0:HEAD_DIM])
            dst.append(r[:, HEAD_DIM:2 * HEAD_DIM])

    qrow = q_ref[...] * HEAD_DIM ** -0.5
    n_idx = lax.broadcasted_iota(I32, (1, n16), 1)
    valid = (n_idx * CMP_STRIDE + CMP_LEN - 1 <= pos) & (n_idx < n_cmp)
    head_row = lax.broadcasted_iota(I32, (8, 1), 0) < HPG
    ncol = map_ref.shape[1]
    blk_r = lax.broadcasted_iota(I32, (1, ncol), 1)
    blk_c = lax.broadcasted_iota(I32, (ncol, 1), 0)
    cur = pos // SLC_LEN
    forced = (blk_r == 0) | (blk_r == cur) | (blk_r == cur - 1)
    oc_parts = []
    for g in range(N_KV):
        q8 = jnp.concatenate(
            [qrow[:, (g * HPG + h) * HEAD_DIM:(g * HPG + h + 1) * HEAD_DIM] for h in range(HPG)]
            + [jnp.zeros((8 - HPG, HEAD_DIM), F32)], axis=0)
        s = _dot_nt(q8, kc[g])
        p, l = _softmax_rows(s, valid)
        pn = jnp.where(head_row, p / jnp.maximum(l, 1e-30), 0.0)
        o8 = _dot(pn, vc[g])
        oc_parts += [o8[h:h + 1, :] for h in range(HPG)]
        psum = jnp.broadcast_to(jnp.sum(pn, axis=0, keepdims=True), pn.shape)
        imp = _dot(psum, map_ref[...])[0:1, :]
        score = jnp.where((blk_r <= cur) & (blk_r < n_slc), imp + jnp.where(forced, FORCE_BONUS, 0.0), -jnp.inf)
        score_c = jnp.broadcast_to(score, (8, ncol)).T[:, 0:1]
        beats = (score_c > score) | ((score_c == score) & (blk_c < blk_r))
        rank = jnp.sum(beats.astype(F32), axis=0, keepdims=True)
        r_iota = lax.broadcasted_iota(I32, (n_sel, ncol), 0).astype(F32)
        hit = (rank == r_iota) & (score > -jnp.inf)
        idx = jnp.sum(jnp.where(hit, blk_r.astype(F32), 0.0), axis=1, keepdims=True)
        idx_ref[:, g:g + 1] = idx.astype(I32)
    oc_ref[...] = jnp.concatenate(oc_parts, axis=1)


def _nsa_sample_cmp(page_table, q3, cache_t, pe_rows, w1bd, w2bd, past_len):
    db, n_pages = page_table.shape
    page = cache_t.shape[2]
    cpp = page // CMP_STRIDE
    n16 = n_pages * cpp
    n_cmp = (past_len + 1) // CMP_STRIDE - 1
    n_slc = -(-(past_len + 1) // SLC_LEN)
    n_sel = min(N_SEL, n_slc)
    ncol = -(-n_slc // 128) * 128
    cmap = jnp.asarray(_cmp_slc_map(n16, n_cmp, n_slc, ncol))
    grid_spec = pltpu.PrefetchScalarGridSpec(
        num_scalar_prefetch=1,
        grid=(db,),
        in_specs=[
            pl.BlockSpec((None, 1, B_WIDTH), lambda b, pt: (b, 0, 0)),
            pl.BlockSpec(memory_space=pl.ANY),
            pl.BlockSpec(pe_rows.shape, lambda b, pt: (0, 0, 0, 0)),
            pl.BlockSpec(w1bd.shape, lambda b, pt: (0, 0, 0, 0)),
            pl.BlockSpec(w2bd.shape, lambda b, pt: (0, 0, 0)),
            pl.BlockSpec(cmap.shape, lambda b, pt: (0, 0)),
        ],
        out_specs=(
            pl.BlockSpec((None, 1, B_WIDTH), lambda b, pt: (b, 0, 0)),
            pl.BlockSpec((None, n_sel, N_KV), lambda b, pt: (b, 0, 0)),
        ),
        scratch_shapes=[pltpu.VMEM((2, n_pages) + cache_t.shape[1:], F32),
                        pltpu.VMEM((FILL_PAGES, 2, page, 128), F32),
                        pltpu.VMEM((n16, CMP_STRIDE * 128), F32), pltpu.VMEM((n16, CMP_STRIDE * 128), F32),
                        pltpu.SemaphoreType.DMA((2,))],
    )
    return pl.pallas_call(
        functools.partial(_nsa_sample_cmp_kernel, n_pages=n_pages, n_cmp=n_cmp, n_slc=n_slc, n_sel=n_sel, pos=past_len),
        grid_spec=grid_spec,
        out_shape=(jax.ShapeDtypeStruct((db, 1, B_WIDTH), F32), jax.ShapeDtypeStruct((db, n_sel, N_KV), I32)),
        compiler_params=_cparams(("arbitrary",), VMEM_LIMIT_BYTES),
        name="nsa_cmp_sample",
    )(page_table.reshape(-1), q3, cache_t, pe_rows, w1bd, w2bd, cmap)


def _sel_copy(cache_hbm, pt_ref, idx_ref, kbuf, sem, b, slot, j2, n_pages, n_sel, last_real):
    j = j2 // 2
    kv = j2 % 2
    g = j // n_sel
    blk = jnp.minimum(idx_ref[b * (N_KV * n_sel) + j], last_real)
    per_page = cache_hbm.shape[2] // SLC_LEN
    page = pt_ref[b * n_pages + blk // per_page]
    row0 = pl.multiple_of((kv * N_KV + g) * HEAD_DIM, HEAD_DIM)
    return pltpu.make_async_copy(cache_hbm.at[page, pl.ds(row0, HEAD_DIM), :], kbuf.at[slot, j2], sem.at[slot])


def _nsa_sample_attn_kernel(pt_ref, idx_ref, q_ref, gt_ref, oc_ref, ks_new_ref, kw_new_ref, win_ref, cache_hbm,
                            o_ref, kbuf, sem, *, n_pages, n_sel, pos):
    b = pl.program_id(0)
    nb = pl.num_programs(0)
    slot = b % 2
    n_copies = 2 * N_KV * n_sel
    page = kbuf.shape[3]
    per_page = page // SLC_LEN
    new_blk = pos // SLC_LEN
    last_real = new_blk - 1

    def start_all(bb, sl):
        def body(j, c):
            _sel_copy(cache_hbm, pt_ref, idx_ref, kbuf, sem, bb, sl, j, n_pages, n_sel, last_real).start()
            return c
        lax.fori_loop(0, n_copies, body, 0)

    @pl.when(b == 0)
    def _():
        start_all(0, 0)

    @pl.when(b + 1 < nb)
    def _():
        start_all(b + 1, 1 - slot)

    def wait_body(j, c):
        _sel_copy(cache_hbm, pt_ref, idx_ref, kbuf, sem, b, slot, j, n_pages, n_sel, last_real).wait()
        return c
    lax.fori_loop(0, n_copies, wait_body, 0)

    qrow = q_ref[...] * HEAD_DIM ** -0.5
    gt = gt_ref[...]
    oc = oc_ref[...]
    ks_new = ks_new_ref[...]
    kw_new = kw_new_ref[...]
    half = KV_WIDTH // 2
    wlen = win_ref.shape[1]
    kp_w = pos - wlen + lax.broadcasted_iota(I32, (1, wlen), 1)
    mask_w = (kp_w >= 0) & (kp_w <= pos) & (kp_w > pos - WINDOW)
    out_parts = []
    for g in range(N_KV):
        q8 = jnp.concatenate(
            [qrow[:, (g * HPG + h) * HEAD_DIM:(g * HPG + h + 1) * HEAD_DIM] for h in range(HPG)]
            + [jnp.zeros((8 - HPG, HEAD_DIM), F32)], axis=0)
        ksl = slice(g * HEAD_DIM, (g + 1) * HEAD_DIM)
        vsl = slice(half + g * HEAD_DIM, half + (g + 1) * HEAD_DIM)

        def attend(s, mask, pv, k_new, v_new, has_new):
            rb = lambda t: t.astype(BF16).astype(F32)
            s_new = jnp.sum(rb(q8) * rb(k_new), axis=-1, keepdims=True)
            sm = jnp.where(mask, s, NEG)
            m = jnp.maximum(jnp.max(sm, -1, keepdims=True), jnp.where(has_new, s_new, NEG))
            p = jnp.where(mask, jnp.exp(sm - m), 0.0)
            p_new = jnp.where(has_new, jnp.exp(s_new - m), 0.0)
            l = jnp.maximum(jnp.sum(p, -1, keepdims=True) + p_new, 1e-30)
            return pv(p / l) + rb(p_new / l) * rb(v_new)

        kt_sel = jnp.concatenate([kbuf[slot, (g * n_sel + r) * 2] for r in range(n_sel)], axis=1)
        vt_sel = jnp.concatenate([kbuf[slot, (g * n_sel + r) * 2 + 1] for r in range(n_sel)], axis=1)
        blk_ids = [idx_ref[b * (N_KV * n_sel) + g * n_sel + r] for r in range(n_sel)]
        row_of = lambda vals: jnp.concatenate([jnp.full((1, page), v, I32) for v in vals], axis=1)
        lane_blk = (lax.broadcasted_iota(I32, (1, n_sel * page), 1) & (page - 1)) // SLC_LEN
        picked = (row_of(blk_ids) <= last_real) & (lane_blk == row_of([bid % per_page for bid in blk_ids]))
        has_new = functools.reduce(jnp.logical_or, [bid == new_blk for bid in blk_ids])
        o_s = attend(_dot(q8, kt_sel), picked, lambda p: _dot_nt(p, vt_sel), ks_new[:, ksl], ks_new[:, vsl], has_new)
        o_w = attend(_dot(q8, win_ref[ksl, :]), mask_w, lambda p: _dot_nt(p, win_ref[vsl, :]),
                     kw_new[:, ksl], kw_new[:, vsl], True)
        for h in range(HPG):
            hh = g * HPG + h
            c0 = g * (N_BRANCH * HPG)
            g0 = gt[:, c0 + h:c0 + h + 1]
            g1 = gt[:, c0 + HPG + h:c0 + HPG + h + 1]
            g2 = gt[:, c0 + 2 * HPG + h:c0 + 2 * HPG + h + 1]
            out_parts.append(g0 * oc[:, hh * HEAD_DIM:(hh + 1) * HEAD_DIM] + g1 * o_s[h:h + 1, :] + g2 * o_w[h:h + 1, :])
    o_ref[...] = jnp.concatenate(out_parts, axis=1)


def _nsa_sample_attn(page_table, sel_idx, q3, gates3, oc3, ks_new3, kw_new3, win_state, cache_s, past_len):
    db, n_pages = page_table.shape
    n_sel = sel_idx.shape[-1]
    wlen = win_state.shape[2]
    row = lambda wdt: pl.BlockSpec((None, 1, wdt), lambda b, pt, ix: (b, 0, 0))
    grid_spec = pltpu.PrefetchScalarGridSpec(
        num_scalar_prefetch=2,
        grid=(db,),
        in_specs=[
            row(B_WIDTH), row(N_GATE), row(B_WIDTH), row(KV_WIDTH), row(KV_WIDTH),
            pl.BlockSpec((None, KV_WIDTH, wlen), lambda b, pt, ix: (b, 0, 0)),
            pl.BlockSpec(memory_space=pl.ANY),
        ],
        out_specs=row(B_WIDTH),
        scratch_shapes=[pltpu.VMEM((2, 2 * N_KV * n_sel, HEAD_DIM, cache_s.shape[2]), F32),
                        pltpu.SemaphoreType.DMA((2,))],
    )
    return pl.pallas_call(
        functools.partial(_nsa_sample_attn_kernel, n_pages=n_pages, n_sel=n_sel, pos=past_len),
        grid_spec=grid_spec,
        out_shape=jax.ShapeDtypeStruct((db, 1, B_WIDTH), F32),
        compiler_params=_cparams(("arbitrary",), VMEM_LIMIT_BYTES),
        name="nsa_attn_sample",
    )(page_table.reshape(-1), sel_idx.reshape(-1), q3, gates3, oc3, ks_new3, kw_new3, win_state, cache_s)


def _post_ln_router(acc, x, gate, lng, lnb, sc2, sh2, rw_t):
    x1 = _ln(ALPHA * x + gate * acc) * lng + lnb
    h2 = _modulate(x1, sc2, sh2)
    return x1, h2, _dot_nt(rw_t, h2)


def _mix_out_kernel(l0_ref, l1_ref, w_ref, x_ref, gate_ref, lng_ref, lnb_ref, sc2_ref, sh2_ref, rw_ref, h2_init,
                    x1_ref, h2_ref, lg_ref, *, concat):
    del h2_init
    if concat:
        k0 = l0_ref.shape[1]
        acc = _dot(l0_ref[...], w_ref[0:k0, :]) + _dot(l1_ref[...], w_ref[k0:, :])
    else:
        acc = _dot(l0_ref[...] * l1_ref[...], w_ref[...])
    x1, h2, lg = _post_ln_router(acc, x_ref[...], gate_ref[...], lng_ref[...], lnb_ref[...],
                                 sc2_ref[...], sh2_ref[...], rw_ref[...])
    x1_ref[...] = x1
    h2_ref[...] = h2
    lg_ref[...] = lg


def _mix_out_prompt(l0, l1, w_bf, x, gate, lng, lnb, sc2, sh2, rw, seq, concat, tail_rows, name):
    n, d = x.shape
    tm = 512
    h2_init = jnp.zeros((n + tail_rows, d), F32)
    tpb = seq // tm
    row = lambda i: (i, 0)
    mod = lambda i: (i // tpb, 0, 0)
    const2 = lambda i: (0, 0)
    return pl.pallas_call(
        functools.partial(_mix_out_kernel, concat=concat),
        grid=(n // tm,),
        in_specs=[
            pl.BlockSpec((tm, l0.shape[1]), row),
            pl.BlockSpec((tm, l1.shape[1]), row),
            pl.BlockSpec(w_bf.shape, const2, pipeline_mode=pl.Buffered(1)),
            pl.BlockSpec((tm, d), row),
            pl.BlockSpec((None, 1, d), mod),
            pl.BlockSpec((1, d), const2),
            pl.BlockSpec((1, d), const2),
            pl.BlockSpec((None, 1, d), mod),
            pl.BlockSpec((None, 1, d), mod),
            pl.BlockSpec(rw.shape, const2),
            pl.BlockSpec(memory_space=pl.ANY),
        ],
        out_specs=(pl.BlockSpec((tm, d), row), pl.BlockSpec((tm, d), row),
                   pl.BlockSpec((N_EXPERTS, tm), lambda i: (0, i))),
        out_shape=(jax.ShapeDtypeStruct((n, d), F32), jax.ShapeDtypeStruct((n + tail_rows, d), F32),
                   jax.ShapeDtypeStruct((N_EXPERTS, n), F32)),
        input_output_aliases={10: 1},
        compiler_params=_cparams(("parallel",), VMEM_LIMIT_BYTES),
        name=name,
    )(l0, l1, w_bf, x, gate, lng, lnb, sc2, sh2, rw, h2_init)


def _top2_route(lg, rb):
    s = jax.nn.sigmoid(lg)
    sb = s + rb
    rows = [sb[e:e + 1, :] for e in range(N_EXPERTS)]
    gs = []
    for g in range(N_GROUPS):
        v = rows[g * EXPERTS_PER_GROUP:(g + 1) * EXPERTS_PER_GROUP]
        pair = [v[i] + v[j] for i in range(EXPERTS_PER_GROUP) for j in range(i + 1, EXPERTS_PER_GROUP)]
        gs.append(functools.reduce(jnp.maximum, pair))
    best, gi = gs[0], jnp.zeros(gs[0].shape, I32)
    for g in range(1, N_GROUPS):
        better = gs[g] > best
        gi = jnp.where(better, g, gi)
        best = jnp.where(better, gs[g], best)
    cand = [jnp.where(gi == e // EXPERTS_PER_GROUP, rows[e], -jnp.inf) for e in range(N_EXPERTS)]
    ids = []
    for k in range(TOP_K):
        vk = jnp.full(cand[0].shape, -jnp.inf, F32)
        ik = jnp.zeros(cand[0].shape, I32)
        for e in range(N_EXPERTS):
            c = cand[e]
            for prev in ids:
                c = jnp.where(prev == e, -jnp.inf, c)
            better = c > vk
            ik = jnp.where(better, e, ik)
            vk = jnp.where(better, c, vk)
        ids.append(ik)
    ws = [functools.reduce(jnp.add, [jnp.where(ik == e, s[e:e + 1, :], 0.0) for e in range(N_EXPERTS)]) for ik in ids]
    tot = functools.reduce(jnp.add, ws)
    return ids, [w / tot for w in ws]


def _route_kernel(lg_ref, rb_ref, w_ref, dest_ref, be_ref, nu_ref, tot_ref, run_ref, ps_ref, *, blk, n_valid, n_slots):
    ph = pl.program_id(0)
    i = pl.program_id(1)
    tm = lg_ref.shape[1]
    ids, ws = _top2_route(lg_ref[...], rb_ref[...])
    e_iota = lax.broadcasted_iota(I32, (N_EXPERTS, tm), 0)
    valid = i * tm + lax.broadcasted_iota(I32, (1, tm), 1) < n_valid
    oh = [((e_iota == ik) & valid).astype(F32) for ik in ids]
    ohsum = functools.reduce(jnp.add, oh)
    tile_cnt = jnp.sum(ohsum, axis=1, keepdims=True)

    @pl.when((ph == 0) & (i == 0))
    def _():
        tot_ref[...] = jnp.zeros_like(tot_ref)

    @pl.when(ph == 0)
    def _():
        tot_ref[...] = tot_ref[...] + tile_cnt

    @pl.when((ph == 1) & (i == 0))
    def _():
        cnt = tot_ref[...]
        padded = jnp.floor((cnt + (blk - 1)) * (1.0 / blk)) * blk
        sub = lax.broadcasted_iota(I32, cnt.shape, 0)
        start = jnp.zeros_like(cnt)
        for e in range(N_EXPERTS):
            start = start + jnp.where(sub > e, padded[e:e + 1, :], 0.0)
        ps_ref[...] = start
        run_ref[...] = jnp.zeros_like(run_ref)
        pad_end = start[:, 0:1] + padded[:, 0:1]
        blk_lo = (lax.broadcasted_iota(I32, (N_EXPERTS, be_ref.shape[1]), 1) * blk).astype(F32)
        n_le = jnp.sum((pad_end <= blk_lo).astype(F32), axis=0, keepdims=True)
        be_ref[...] = jnp.minimum(n_le, N_EXPERTS - 1.0).astype(I32)
        nu_ref[...] = (jnp.max(pad_end, axis=0, keepdims=True) * (1.0 / blk) + jnp.zeros(nu_ref.shape, F32)).astype(I32)

    @pl.when(ph == 1)
    def _():
        t_r = lax.broadcasted_iota(I32, (tm, tm), 0)
        t_c = lax.broadcasted_iota(I32, (tm, tm), 1)
        before = _dot(ohsum, (t_r < t_c).astype(F32))
        base = before + run_ref[:, 0:1] + ps_ref[:, 0:1]
        for k in range(TOP_K):
            w_ref[k:k + 1, :] = ws[k]
            slot = jnp.sum(oh[k] * base, axis=0, keepdims=True).astype(I32)
            dest_ref[k:k + 1, :] = jnp.where(valid, slot, n_slots)
        run_ref[...] = run_ref[...] + tile_cnt


def _route_tables(logits_t, router_b, blk, n_valid):
    n = logits_t.shape[1]
    tm = min(512, n)
    a = n_valid * TOP_K
    n_blocks = -(-a // blk) + N_EXPERTS
    nb_pad = -(-n_blocks // 128) * 128
    tok_blk = lambda p, i: (0, i * p)
    const = lambda p, i: (0, 0)
    w, dest, block_e, n_used = pl.pallas_call(
        functools.partial(_route_kernel, blk=blk, n_valid=n_valid, n_slots=n_blocks * blk),
        grid=(2, n // tm),
        in_specs=[pl.BlockSpec((N_EXPERTS, tm), lambda p, i: (0, i)), pl.BlockSpec((N_EXPERTS, 1), const)],
        out_specs=(pl.BlockSpec((TOP_K, tm), tok_blk), pl.BlockSpec((TOP_K, tm), tok_blk),
                   pl.BlockSpec((1, nb_pad), const), pl.BlockSpec((1, 128), const)),
        out_shape=(jax.ShapeDtypeStruct((TOP_K, n), F32), jax.ShapeDtypeStruct((TOP_K, n), I32),
                   jax.ShapeDtypeStruct((1, nb_pad), I32), jax.ShapeDtypeStruct((1, 128), I32)),
        scratch_shapes=[pltpu.VMEM((N_EXPERTS, 128), F32)] * 3,
        compiler_params=_cparams(("arbitrary", "arbitrary")),
        name="moe_route",
    )(logits_t, router_b.reshape(N_EXPERTS, 1).astype(F32))
    tok = jnp.repeat(jnp.arange(n, dtype=I32), TOP_K)
    slot_tok = jnp.zeros((n_blocks * blk,), I32).at[dest.T.reshape(-1)].set(tok, mode="drop")
    return slot_tok, dest, w, block_e[0, :n_blocks], n_used[0, :1], n_blocks


def _row_copy(src_hbm, row, buf, slot, r, sem):
    return pltpu.make_async_copy(src_hbm.at[pl.ds(row, 1), :], buf.at[slot, pl.ds(r, 1), :], sem.at[slot])


ROW_DMA_UNROLL = 8
MOE_BLK = 256


def _start_row_gather(src_hbm, row_of, buf, slot, n_rows, sem):
    unroll = math.gcd(ROW_DMA_UNROLL, n_rows)

    def body(j, c):
        for u in range(unroll):
            r = j * unroll + u
            _row_copy(src_hbm, row_of(r), buf, slot, r, sem).start(priority=u % 2)
        return c
    lax.fori_loop(0, n_rows // unroll, body, 0)


def _wait_row_gather(src_hbm, buf, slot, n_rows, sem):
    unroll = math.gcd(ROW_DMA_UNROLL, n_rows)

    def body(j, c):
        for u in range(unroll):
            _row_copy(src_hbm, 0, buf, slot, j * unroll + u, sem).wait()
        return c
    lax.fori_loop(0, n_rows // unroll, body, 0)


def _moe_kernel(be_ref, tok_ref, nu_ref, x_hbm, w1_ref, w2_ref, y_ref, xs_buf, gsem, *, blk):
    i = pl.program_id(0)
    n_used = nu_ref[0]
    slot = i % 2

    def start_gather(bi, sl):
        _start_row_gather(x_hbm, lambda r: tok_ref[bi * blk + r], xs_buf, sl, blk, gsem)

    @pl.when(i == 0)
    def _():
        start_gather(0, 0)

    @pl.when(i + 1 < n_used)
    def _():
        start_gather(i + 1, 1 - slot)

    @pl.when(i < n_used)
    def _():
        _wait_row_gather(x_hbm, xs_buf, slot, blk, gsem)
        z = _dot(xs_buf[slot], w1_ref[...])
        de = z.shape[1] // 2
        act = _silu(z[:, :de]) * z[:, de:]
        y_ref[...] = _dot(act, w2_ref[...])

    @pl.when(i >= n_used)
    def _():
        y_ref[...] = jnp.zeros_like(y_ref)


def _moe_ffn(h2, slot_tok, block_e, n_used, n_blocks, w_in_bf, w_out_bf, layer, blk, name):
    n, d = h2.shape
    de2 = w_in_bf.shape[-1]
    grid_spec = pltpu.PrefetchScalarGridSpec(
        num_scalar_prefetch=3,
        grid=(n_blocks,),
        in_specs=[
            pl.BlockSpec(memory_space=pl.ANY),
            pl.BlockSpec((None, None, d, de2), lambda i, be, tk, nu: (layer, be[i], 0, 0)),
            pl.BlockSpec((None, None, de2 // 2, d), lambda i, be, tk, nu: (layer, be[i], 0, 0)),
        ],
        out_specs=pl.BlockSpec((blk, d), lambda i, be, tk, nu: (i, 0)),
        scratch_shapes=[pltpu.VMEM((2, blk, d), F32), pltpu.SemaphoreType.DMA((2,))],
    )
    return pl.pallas_call(
        functools.partial(_moe_kernel, blk=blk),
        grid_spec=grid_spec,
        out_shape=jax.ShapeDtypeStruct((n_blocks * blk, d), F32),
        compiler_params=_cparams(("arbitrary",), VMEM_LIMIT_BYTES),
        name=name,
    )(block_e, slot_tok, n_used, h2, w_in_bf, w_out_bf)


def _combine_kernel(so_ref, x_ref, wt_ref, gate_ref, lng_ref, lnb_ref, y_hbm, o_ref, ybuf, sem, *, tm):
    i = pl.program_id(0)
    nt = pl.num_programs(0)
    slot = i % 2

    def start_gather(ti, sl):
        for k in range(TOP_K):
            _start_row_gather(y_hbm, lambda r: so_ref[(ti * tm + r) * TOP_K + k], ybuf.at[k], sl, tm, sem.at[k])

    @pl.when(i == 0)
    def _():
        start_gather(0, 0)

    @pl.when(i + 1 < nt)
    def _():
        start_gather(i + 1, 1 - slot)

    for k in range(TOP_K):
        _wait_row_gather(y_hbm, ybuf.at[k], slot, tm, sem.at[k])

    wt = wt_ref[...]
    f = wt[:, 0:1] * ybuf[0, slot] + wt[:, 1:2] * ybuf[1, slot]
    o_ref[...] = _ln(ALPHA * x_ref[...] + gate_ref[...] * f) * lng_ref[...] + lnb_ref[...]


def _moe_combine(slot_of, x1, wts, gate, lng, lnb, y, tm, tpb, name):
    n, d = x1.shape
    grid_spec = pltpu.PrefetchScalarGridSpec(
        num_scalar_prefetch=1,
        grid=(n // tm,),
        in_specs=[
            pl.BlockSpec((tm, d), lambda i, so: (i, 0)),
            pl.BlockSpec((tm, TOP_K), lambda i, so: (i, 0)),
            pl.BlockSpec((None, gate.shape[1], d), lambda i, so: (i // tpb, 0, 0)),
            pl.BlockSpec((1, d), lambda i, so: (0, 0)),
            pl.BlockSpec((1, d), lambda i, so: (0, 0)),
            pl.BlockSpec(memory_space=pl.ANY),
        ],
        out_specs=pl.BlockSpec((tm, d), lambda i, so: (i, 0)),
        scratch_shapes=[pltpu.VMEM((TOP_K, 2, tm, d), F32), pltpu.SemaphoreType.DMA((TOP_K, 2))],
    )
    return pl.pallas_call(
        functools.partial(_combine_kernel, tm=tm),
        grid_spec=grid_spec,
        out_shape=jax.ShapeDtypeStruct((n, d), F32),
        compiler_params=_cparams(("arbitrary",)),
        name=name,
    )(slot_of, x1, wts, gate, lng, lnb, y)


def _append_kernel(rows_ref, buf_hbm, out_hbm, sem, *, at):
    del buf_hbm
    cp = pltpu.make_async_copy(rows_ref, out_hbm.at[pl.ds(at, rows_ref.shape[0]), :], sem)
    cp.start()
    cp.wait()


def _append_rows(buf, rows, at):
    return pl.pallas_call(
        functools.partial(_append_kernel, at=at),
        in_specs=[pl.BlockSpec(memory_space=pltpu.VMEM), pl.BlockSpec(memory_space=pl.ANY)],
        out_specs=pl.BlockSpec(memory_space=pl.ANY),
        out_shape=jax.ShapeDtypeStruct(buf.shape, buf.dtype),
        input_output_aliases={1: 0},
        scratch_shapes=[pltpu.SemaphoreType.DMA(())],
        name="append_sample_rows",
    )(rows, buf)


def _channel_sublayer(x1_p, x1_s, h2_all, lg_p, lg_s, gate_p, gate_s, lng, lnb, router_b, w_in, w_out, layer,
                      seq, tag):
    n_p, d = x1_p.shape
    db = x1_s.shape[0]
    n_all = n_p + db
    n_pad = -(-n_all // 512) * 512
    lg_all = jnp.concatenate([lg_p, lg_s, jnp.zeros((N_EXPERTS, n_pad - n_all), F32)], axis=1)
    slot_tok, dest, wts, block_e, n_used, n_blocks = _route_tables(lg_all, router_b, MOE_BLK, n_all)
    y = _moe_ffn(h2_all, slot_tok, block_e, n_used, n_blocks, w_in, w_out, layer, MOE_BLK, "moe_ffn_" + tag)
    part = lambda a, lo, hi: a[:, lo:hi].T
    out_p = _moe_combine(part(dest, 0, n_p).reshape(-1), x1_p, part(wts, 0, n_p), gate_p, lng, lnb, y,
                         128, seq // 128, "moe_combine_" + tag + "_prompt")
    out_s = _moe_combine(part(dest, n_p, n_all).reshape(-1), x1_s, part(wts, n_p, n_all), gate_s.reshape(1, db, d),
                         lng, lnb, y, db, 1, "moe_combine_" + tag + "_sample")
    return out_p, out_s


def _rec_in_kernel(x_ref, sc_ref, sh_ref, w_ref, gg_ref, xbr_ref):
    hb = _modulate(x_ref[...], sc_ref[...], sh_ref[...]).astype(BF16)
    d = gg_ref.shape[1]
    gg_ref[...] = _gelu(jnp.dot(hb, w_ref[:, 0:d], preferred_element_type=F32))
    xbr_ref[...] = jnp.dot(hb, w_ref[:, d:], preferred_element_type=F32)


def _rec_in_prompt(x, sc, sh, w_bf, seq):
    n, d = x.shape
    dr = w_bf.shape[1] // 2
    tm = 512
    tpb = seq // tm
    row = lambda i: (i, 0)
    mod = lambda i: (i // tpb, 0, 0)
    return pl.pallas_call(
        _rec_in_kernel,
        grid=(n // tm,),
        in_specs=[
            pl.BlockSpec((tm, d), row),
            pl.BlockSpec((None, 1, d), mod),
            pl.BlockSpec((None, 1, d), mod),
            pl.BlockSpec(w_bf.shape, lambda i: (0, 0), pipeline_mode=pl.Buffered(1)),
        ],
        out_specs=(pl.BlockSpec((tm, dr), row), pl.BlockSpec((tm, dr), row)),
        out_shape=(jax.ShapeDtypeStruct((n, dr), F32), jax.ShapeDtypeStruct((n, dr), F32)),
        compiler_params=_cparams(("parallel",), VMEM_LIMIT_BYTES),
        name="rec_in_prompt",
    )(x, sc, sh, w_bf)


def _log1p(y):
    w = 1.0 + y
    return jnp.where(w == 1.0, y, jnp.log(w) * (y / jnp.where(w == 1.0, 1.0, w - 1.0)))


def _expm1(x):
    u = jnp.exp(x)
    safe = (u != 1.0) & (u > 0.0)
    return jnp.where(u == 1.0, x, jnp.where(u > 0.0, (u - 1.0) * (x / jnp.where(safe, jnp.log(u), 1.0)), -1.0))


def _softplus(x):
    return jnp.maximum(x, 0.0) + _log1p(jnp.exp(-jnp.abs(x)))


def _rg_terms(xb, z, ba, bx, lam):
    bs = xb.shape[1]
    r = jax.nn.sigmoid(z[:, :bs] + ba)
    gi = jax.nn.sigmoid(z[:, bs:] + bx)
    log_a = -RG_C * r * _softplus(-lam)
    a = jnp.exp(log_a)
    return a, jnp.sqrt(-_expm1(2.0 * log_a)) * (gi * xb)


def _scan_rows(a, b, h_in):
    tm = a.shape[0]
    sub = lax.broadcasted_iota(I32, a.shape, 0) & 7
    for step in (1, 2, 4):
        keep = sub >= step
        b = jnp.where(keep, a * pltpu.roll(b, step, 0) + b, b)
        a = jnp.where(keep, a * pltpu.roll(a, step, 0), a)
    outs = []
    h = h_in
    for j in range(tm // 8):
        hj = a[8 * j:8 * j + 8] * h + b[8 * j:8 * j + 8]
        outs.append(hj)
        h = hj[7:8]
    return jnp.concatenate(outs, axis=0), h


def _rec_gate_kernel(x_ref, cw_ref, cb_ref, wab_ref, ba_ref, bx_ref, lam_ref, hs_ref, ht_ref, carry_ref, h_ref,
                     *, tpb, tm):
    i = pl.program_id(0)

    @pl.when(i % tpb == 0)
    def _():
        carry_ref[...] = jnp.zeros_like(carry_ref)
        h_ref[...] = jnp.zeros_like(h_ref)

    x = x_ref[...]
    carry = carry_ref[...]
    d = x.shape[1]
    row8 = lax.broadcasted_iota(I32, (8, d), 0)
    xc = cb_ref[...] + x * cw_ref[CONV_W - 1:CONV_W, :]
    for k in range(1, CONV_W):
        xr = pltpu.roll(x, k, 0)
        head = jnp.where(row8 < k, pltpu.roll(carry, k, 0), xr[0:8])
        xk = jnp.concatenate([head, xr[8:]], axis=0)
        xc = xc + xk * cw_ref[CONV_W - 1 - k:CONV_W - k, :]
    carry_ref[...] = x[tm - 8:tm]
    bs = d // RNN_BLOCKS
    for nb in range(RNN_BLOCKS):
        sl = slice(nb * bs, (nb + 1) * bs)
        xb = xc[:, sl]
        a, bt = _rg_terms(xb, _dot(xb, wab_ref[nb]), ba_ref[:, sl], bx_ref[:, sl], lam_ref[:, sl])
        hs, h_last = _scan_rows(a, bt, h_ref[0:1, sl])
        hs_ref[:, sl] = hs
        h_ref[:, sl] = jnp.broadcast_to(h_last, (8, bs))
    ht_ref[...] = h_ref[...]


def _rec_scan_prompt(xbr, cw, cb, wab, ba, bx, lam, seq):
    n, d = xbr.shape
    tm = 256
    tpb = seq // tm
    row = lambda i: (i, 0)
    const2 = lambda i: (0, 0)
    return pl.pallas_call(
        functools.partial(_rec_gate_kernel, tpb=tpb, tm=tm),
        grid=(n // tm,),
        in_specs=[
            pl.BlockSpec((tm, d), row),
            pl.BlockSpec(cw.shape, const2),
            pl.BlockSpec((1, d), const2),
            pl.BlockSpec(wab.shape, lambda i: (0, 0, 0)),
            pl.BlockSpec((1, d), const2),
            pl.BlockSpec((1, d), const2),
            pl.BlockSpec((1, d), const2),
        ],
        out_specs=(pl.BlockSpec((tm, d), row), pl.BlockSpec((None, 8, d), lambda i: (i // tpb, 0, 0))),
        out_shape=(jax.ShapeDtypeStruct((n, d), F32), jax.ShapeDtypeStruct((n // seq, 8, d), F32)),
        scratch_shapes=[pltpu.VMEM((8, d), F32), pltpu.VMEM((8, d), F32)],
        compiler_params=_cparams(("arbitrary",), VMEM_LIMIT_BYTES),
        name="rec_scan_prompt",
    )(xbr, cw, cb, wab, ba, bx, lam)


def _rec_step_sample(z, buf0, buf1, buf2, h0, cw, cb, wa, wx, ba, bx, lam):
    d = h0.shape[1]
    gate_br = z[:, :d]
    xbr = z[:, d:]
    xc = cb + buf0 * cw[0:1] + buf1 * cw[1:2] + buf2 * cw[2:3] + xbr * cw[3:4]
    bs = d // RNN_BLOCKS
    a_parts, b_parts = [], []
    for nb in range(RNN_BLOCKS):
        sl = slice(nb * bs, (nb + 1) * bs)
        xb = xc[:, sl]
        zz = jnp.concatenate([_dot(xb, wa[nb]), _dot(xb, wx[nb])], axis=1)
        a, bt = _rg_terms(xb, zz, ba[:, sl], bx[:, sl], lam[:, sl])
        a_parts.append(a)
        b_parts.append(bt)
    h = jnp.concatenate(a_parts, axis=1) * h0 + jnp.concatenate(b_parts, axis=1)
    return _gelu(gate_br) * h, h, xbr


def kernel(x_prompt, x_sample, c_prompt, c_sample, cache_nsa_cmp, cache_nsa_slc, state_nsa_win, state_rglru_conv, state_rglru_h, page_table, ada_w, ada_b, ln_g, ln_b, ab_w_in, ab_w_out, gmlp_ln_g, gmlp_ln_b, gmlp_ws, gmlp_bs, nsa_cmp_pe, nsa_cmp_w1, nsa_cmp_w2, rec_w_in, rec_conv_w, rec_conv_b, rg_wa, rg_ba, rg_wx, rg_bx, rg_lambda, rec_w_out, router_w, router_b, moe_w_in, moe_w_out):
    bp, seq, d = x_prompt.shape
    db = x_sample.shape[0]
    assert x_sample.shape[1] == 1
    n_pool, page = cache_nsa_cmp.shape[:2]
    n_pages = page_table.shape[1]
    past_len = n_pages * page
    assert seq % 256 == 0 and past_len % SLC_LEN == 0 and page % SLC_LEN == 0 and past_len >= WINDOW
    n_p = bp * seq
    d_rnn = rec_conv_b.shape[0]

    mods = _ada_all(jnp.concatenate([c_prompt, c_sample], axis=0), ada_w, ada_b)

    def mod_rows(layer, sub):
        m = mods[layer * 2 + sub]
        parts = [m[:, j * d:(j + 1) * d] for j in range(3)]
        return [p[:bp].reshape(bp, 1, d) for p in parts], [p[bp:] for p in parts]

    lnrow = lambda a: a.reshape(1, d)

    gperm = np.array([(g * HPG + h) * N_BRANCH + br for g in range(N_KV) for br in range(N_BRANCH) for h in range(HPG)])
    w_gate = ab_w_in[:, O_G:][:, gperm]
    w_ab_bf = jnp.concatenate([ab_w_in[:, :O_G], w_gate], axis=1).astype(BF16)
    tril = jnp.tril(jnp.ones((CHUNK, CHUNK), F32))
    wtril = (gmlp_ws * tril).astype(BF16)
    bst = gmlp_bs.T
    glng = gmlp_ln_g.reshape(1, A_WIDTH)
    glnb = gmlp_ln_b.reshape(1, A_WIDTH)
    pe_rows, w1bd, w2bd = _compress_weights(nsa_cmp_pe, nsa_cmp_w1, nsa_cmp_w2)
    rw_t = router_w.T
    xp = x_prompt.reshape(n_p, d)
    xs = x_sample.reshape(db, d)

    (sh_p, sc_p, g_p), (sh_s, sc_s, g_s) = mod_rows(0, 0)
    (sh2_p, sc2_p, g2_p), (sh2_s, sc2_s, g2_s) = mod_rows(0, 1)
    cos_p, sin_p = _rope_tables(np.arange(seq))
    a_p, q_p, kvc_p, kvs_p, kvw_p, kvt_p, gt_p = _ab_proj_prompt(
        xp, sc_p, sh_p, w_ab_bf, cos_p, sin_p, glng, glnb, wtril, bst, seq)
    kcv_p = _compress_prompt(kvc_p.reshape(bp, seq // CMP_STRIDE, CMP_STRIDE * KV_WIDTH), pe_rows, w1bd, w2bd)
    o_p, moe_in_bf, moe_out_bf = _nsa_prompt(
        q_p, gt_p, kcv_p, kvt_p, bp, seq,
        [moe_w_in.reshape(-1, moe_w_in.shape[-1]), moe_w_out.reshape(-1, moe_w_out.shape[-1])])
    moe_in_bf = moe_in_bf.reshape(moe_w_in.shape)
    moe_out_bf = moe_out_bf.reshape(moe_w_out.shape)
    w_out_bf = ab_w_out.astype(BF16)
    x1_p, h2_p, lg_p = _mix_out_prompt(a_p, o_p, w_out_bf, xp, g_p, lnrow(ln_g[0, 0]), lnrow(ln_b[0, 0]),
                                       sc2_p, sh2_p, rw_t, seq, True, db, "mix_out_l0_prompt")

    z_s = _small_mm(_modulate, [xs, sc_s, sh_s], w_ab_bf, O_G, 512, "ab_proj_sample")
    zg_s = _small_mm(_modulate, [xs, sc_s, sh_s], w_gate, N_GATE, N_GATE, "ab_gate_sample")
    cos_s, sin_s = _rope_tables(np.full((db,), past_len))
    ws0 = jnp.repeat(gmlp_ws[:, 0, 0], A_GROUP_DIM).reshape(1, A_WIDTH)
    bs0 = jnp.repeat(gmlp_bs[:, 0], A_GROUP_DIM).reshape(1, A_WIDTH)
    sds = lambda *s: jax.ShapeDtypeStruct(s, F32)
    a_s, v_s, q_s, kvc_s, kvs_s, kvw_s, gt_s = _vmem_call(
        _ab_post_sample,
        (sds(db, A_WIDTH), sds(db, A_WIDTH), sds(db, B_WIDTH), sds(db, KV_WIDTH), sds(db, KV_WIDTH),
         sds(db, KV_WIDTH), sds(db, N_GATE)),
        (z_s, zg_s, cos_s, sin_s, glng, glnb, ws0, bs0), "ab_post_sample")
    q3 = q_s.reshape(db, 1, B_WIDTH)
    page_t = lambda c: jnp.transpose(c, (0, 2, 3, 4, 1)).reshape(n_pool, KV_WIDTH, page)
    oc3, sel_idx = _nsa_sample_cmp(page_table, q3, page_t(cache_nsa_cmp), pe_rows, w1bd, w2bd, past_len)
    sel_idx = jnp.transpose(sel_idx, (0, 2, 1))
    o_s = _nsa_sample_attn(page_table, sel_idx, q3, gt_s.reshape(db, 1, N_GATE), oc3,
                           kvs_s.reshape(db, 1, KV_WIDTH), kvw_s.reshape(db, 1, KV_WIDTH),
                           jnp.transpose(state_nsa_win, (0, 2, 3, 4, 1)).reshape(db, KV_WIDTH, -1),
                           page_t(cache_nsa_slc),
                           past_len).reshape(db, B_WIDTH)
    f_s = _small_mm(lambda a, o: jnp.concatenate([a, o], axis=1), [a_s, o_s], w_out_bf, d, 512, "mix_out_l0_sample")
    x1_s, h2_s, lg_s = _vmem_call(
        _post_ln_router, (sds(db, d), sds(db, d), sds(N_EXPERTS, db)),
        (f_s, xs, g_s, lnrow(ln_g[0, 0]), lnrow(ln_b[0, 0]), sc2_s, sh2_s, rw_t), "post_l0_sample")

    x2_p, x2_s = _channel_sublayer(x1_p, x1_s, _append_rows(h2_p, h2_s, n_p), lg_p, lg_s, g2_p, g2_s,
                                   lnrow(ln_g[0, 1]), lnrow(ln_b[0, 1]), router_b, moe_in_bf, moe_out_bf, 0, seq, "l0")

    (sh_p, sc_p, g_p), (sh_s, sc_s, g_s) = mod_rows(1, 0)
    (sh2_p, sc2_p, g2_p), (sh2_s, sc2_s, g2_s) = mod_rows(1, 1)
    rec_in_bf = rec_w_in.astype(BF16)
    rec_out_bf = rec_w_out.astype(BF16)
    gg_p, xbr_p = _rec_in_prompt(x2_p, sc_p, sh_p, rec_in_bf, seq)
    wab = jnp.concatenate([rg_wa, rg_wx], axis=2).astype(BF16)
    row_r = lambda a: a.reshape(1, d_rnn)
    hs_p, ht = _rec_scan_prompt(xbr_p, rec_conv_w, row_r(rec_conv_b), wab, row_r(rg_ba), row_r(rg_bx),
                                row_r(rg_lambda), seq)
    x3_p, h4_p, lg_p = _mix_out_prompt(gg_p, hs_p, rec_out_bf, x2_p, g_p,
                                       lnrow(ln_g[1, 0]), lnrow(ln_b[1, 0]), sc2_p, sh2_p, rw_t, seq, False, db,
                                       "mix_out_l1_prompt")
    conv_p = xbr_p.reshape(bp, seq, d_rnn)[:, seq - (CONV_W - 1):]
    h_p = ht[:, 0, :]

    zr_s = _small_mm(_modulate, [x2_s, sc_s, sh_s], rec_in_bf, 2 * d_rnn, 512, "rec_in_sample")
    y_s, h_s, xbr_s = _vmem_call(
        _rec_step_sample, (sds(db, d_rnn), sds(db, d_rnn), sds(db, d_rnn)),
        (zr_s, state_rglru_conv[:, 0], state_rglru_conv[:, 1], state_rglru_conv[:, 2], state_rglru_h,
         rec_conv_w, row_r(rec_conv_b), rg_wa, rg_wx, row_r(rg_ba), row_r(rg_bx), row_r(rg_lambda)),
        "rec_step_sample")
    f_s = _small_mm(lambda y: y, [y_s], rec_out_bf, d, 512, "mix_out_l1_sample")
    x3_s, h4_s, lg_s = _vmem_call(
        _post_ln_router, (sds(db, d), sds(db, d), sds(N_EXPERTS, db)),
        (f_s, x2_s, g_s, lnrow(ln_g[1, 0]), lnrow(ln_b[1, 0]), sc2_s, sh2_s, rw_t), "post_l1_sample")
    conv_s = jnp.concatenate([state_rglru_conv[:, 1:], xbr_s[:, None, :]], axis=1)

    y_p, y_s = _channel_sublayer(x3_p, x3_s, _append_rows(h4_p, h4_s, n_p), lg_p, lg_s, g2_p, g2_s,
                                 lnrow(ln_g[1, 1]), lnrow(ln_b[1, 1]), router_b, moe_in_bf, moe_out_bf, 1, seq, "l1")

    kv5 = lambda a, b_: a.reshape(b_, -1, 2, N_KV, HEAD_DIM)
    keep = min(WINDOW, seq)
    win_p = kv5(kvw_p, bp)[:, seq - keep:]
    kw_full = jnp.concatenate([state_nsa_win, kv5(kvw_s, db)], axis=1)
    win_s = kw_full[:, kw_full.shape[1] - min(WINDOW, kw_full.shape[1]):]
    return (y_p.reshape(bp, seq, d), y_s.reshape(db, 1, d), kv5(kvc_p, bp), kv5(kvc_s, db), kv5(kvs_p, bp),
            kv5(kvs_s, db), win_p, win_s, v_s.reshape(db, 1, A_WIDTH), conv_p, conv_s, h_p, h_s)
```

```python
import functools
import math

import numpy as np
import jax
import jax.numpy as jnp
from jax import lax
from jax.experimental import pallas as pl
from jax.experimental.pallas import tpu as pltpu

F32 = jnp.float32
BF16 = jnp.bfloat16
I32 = jnp.int32

A_GROUPS = 8
A_GROUP_DIM = 128
A_WIDTH = A_GROUPS * A_GROUP_DIM
CHUNK = 128
N_HEADS = 16
N_KV = 4
HEAD_DIM = 64
HPG = N_HEADS // N_KV
B_WIDTH = N_HEADS * HEAD_DIM
KV_WIDTH = 2 * N_KV * HEAD_DIM
N_BRANCH = 3
CMP_LEN = 32
CMP_STRIDE = 16
CMP_HID = 2 * HEAD_DIM
SLC_LEN = 64
N_SEL = 16
WINDOW = 512
Q_BLOCK = 128
FORCE_BONUS = 1e4
ROPE_THETA = 10000.0
RNN_BLOCKS = 16
CONV_W = 4
RG_C = 8.0
N_EXPERTS = 16
N_GROUPS = 4
EXPERTS_PER_GROUP = N_EXPERTS // N_GROUPS
TOP_K = 2
DEPTH = 2
ALPHA = (2 * DEPTH) ** 0.25
LN_EPS = 1e-5
NEG = -1e30

O_U = 0
O_V = A_WIDTH
O_Q = 2 * A_WIDTH
O_KV = O_Q + B_WIDTH
O_G = O_KV + 3 * KV_WIDTH
N_GATE = N_BRANCH * N_HEADS

VMEM_LIMIT_BYTES = 56 * 1024 * 1024


def _cparams(sem, vmem=None):
    return pltpu.CompilerParams(dimension_semantics=sem, vmem_limit_bytes=vmem)


def _dot(a, b):
    return jnp.dot(a.astype(BF16), b.astype(BF16), preferred_element_type=F32)


def _dot_nt(a, b):
    dn = (((1,), (1,)), ((), ()))
    return lax.dot_general(a.astype(BF16), b.astype(BF16), dn, preferred_element_type=F32)


def _ln(x):
    mu = jnp.mean(x, -1, keepdims=True)
    xc = x - mu
    var = jnp.mean(xc * xc, -1, keepdims=True)
    return xc * lax.rsqrt(var + LN_EPS)


def _silu(x):
    return x * jax.nn.sigmoid(x)


def _gelu(x):
    return jax.nn.gelu(x, approximate=True)


def _rope(x, cos2, sin2):
    w = x.shape[1]
    rep = w // 128
    cos = jnp.concatenate([cos2] * rep, axis=1) if rep > 1 else cos2
    sin = jnp.concatenate([sin2] * rep, axis=1) if rep > 1 else sin2
    lane = lax.broadcasted_iota(I32, x.shape, 1)
    first = (lane & (HEAD_DIM - 1)) < HEAD_DIM // 2
    rot = jnp.where(first, pltpu.roll(x, w - HEAD_DIM // 2, 1), pltpu.roll(x, HEAD_DIM // 2, 1))
    return x * cos + rot * sin


def _rope_tables(pos):
    half = HEAD_DIM // 2
    inv = ROPE_THETA ** (-np.arange(half, dtype=np.float64) / half)
    ang = np.asarray(pos, np.float64)[:, None] * inv[None, :]
    cos = np.tile(np.cos(ang), (1, 4))
    sin = np.tile(np.concatenate([-np.sin(ang), np.sin(ang)], axis=1), (1, 2))
    return jnp.asarray(cos, F32), jnp.asarray(sin, F32)


def _softmax_rows(s, mask):
    sm = jnp.where(mask, s, NEG)
    m = jnp.max(sm, -1, keepdims=True)
    p = jnp.where(mask, jnp.exp(sm - m), 0.0)
    return p, jnp.sum(p, -1, keepdims=True)


def _ada_kernel(c_ref, w_ref, b_ref, o_ref):
    o_ref[...] = _dot(_silu(c_ref[...]), w_ref[...]) + b_ref[...]


def _ada_all(c_all, ada_w, ada_b):
    r, d = c_all.shape
    n_mod = ada_w.shape[0] * ada_w.shape[1]
    d3 = ada_w.shape[-1]
    tn = 512
    return pl.pallas_call(
        _ada_kernel,
        grid=(n_mod, d3 // tn),
        in_specs=[
            pl.BlockSpec((r, d), lambda l, j: (0, 0)),
            pl.BlockSpec((None, d, tn), lambda l, j: (l, 0, j)),
            pl.BlockSpec((None, 1, tn), lambda l, j: (l, 0, j)),
        ],
        out_specs=pl.BlockSpec((None, r, tn), lambda l, j: (l, 0, j)),
        out_shape=jax.ShapeDtypeStruct((n_mod, r, d3), F32),
        compiler_params=_cparams(("parallel", "parallel")),
        name="ada_mod",
    )(c_all, ada_w.reshape(n_mod, d, d3), ada_b.reshape(n_mod, 1, d3))


def _small_mm_kernel(*refs, n_x, pre):
    xs = [r[...] for r in refs[:n_x]]
    w_ref, o_ref = refs[n_x], refs[n_x + 1]
    o_ref[...] = _dot(pre(*xs), w_ref[...])


def _small_mm(pre, xs, w, n_out, tn, name):
    m = xs[0].shape[0]
    k = w.shape[0]
    in_specs = [pl.BlockSpec(x.shape, lambda j, nd=x.ndim: (0,) * nd) for x in xs]
    in_specs.append(pl.BlockSpec((k, tn), lambda j: (0, j)))
    return pl.pallas_call(
        functools.partial(_small_mm_kernel, n_x=len(xs), pre=pre),
        grid=(n_out // tn,),
        in_specs=in_specs,
        out_specs=pl.BlockSpec((m, tn), lambda j: (0, j)),
        out_shape=jax.ShapeDtypeStruct((m, n_out), F32),
        compiler_params=_cparams(("parallel",)),
        name=name,
    )(*xs, w)


def _vmem_call(fn, out_shapes, args, name):
    n_in = len(args)

    def kern(*refs):
        res = fn(*[r[...] for r in refs[:n_in]])
        for o, v in zip(refs[n_in:], res):
            o[...] = v

    return pl.pallas_call(kern, out_shape=out_shapes, name=name)(*args)


def _modulate(x, sc, sh):
    return x * (1.0 + sc) + sh


def _ab_proj_kernel(x_ref, sc_ref, sh_ref, w_ref, cos_ref, sin_ref, lng_ref, lnb_ref, wtril_ref, bst_ref,
                    a_ref, q_ref, kvc_ref, kvs_ref, kvw_ref, kvt_ref, gt_ref, *, tm):
    hb = _modulate(x_ref[...], sc_ref[...], sh_ref[...]).astype(BF16)

    def proj(lo, hi):
        return jnp.dot(hb, w_ref[:, lo:hi], preferred_element_type=F32)

    cos2 = cos_ref[...]
    sin2 = sin_ref[...]
    zu = proj(O_U, O_V)
    zv = proj(O_V, O_Q)
    for g in range(A_GROUPS):
        sl = slice(g * A_GROUP_DIM, (g + 1) * A_GROUP_DIM)
        vg = _ln(zv[:, sl]) * lng_ref[:, sl] + lnb_ref[:, sl]
        for c in range(tm // CHUNK):
            rs = slice(c * CHUNK, (c + 1) * CHUNK)
            mix = jnp.dot(wtril_ref[g], vg[rs].astype(BF16), preferred_element_type=F32) + bst_ref[:, g:g + 1]
            a_ref[rs, sl] = zu[rs, sl] * mix
    q_ref[...] = _rope(proj(O_Q, O_KV), cos2, sin2)
    half = KV_WIDTH // 2
    for br, ref in enumerate((kvc_ref, kvs_ref, kvw_ref)):
        lo = O_KV + br * KV_WIDTH
        k = _rope(proj(lo, lo + half), cos2, sin2)
        v = proj(lo + half, lo + KV_WIDTH)
        ref[:, 0:half] = k
        ref[:, half:KV_WIDTH] = v
        for g in range(N_KV):
            hs = slice(g * HEAD_DIM, (g + 1) * HEAD_DIM)
            kvt_ref[br, 0, g] = k[:, hs]
            kvt_ref[br, 1, g] = v[:, hs]
    zg = jax.nn.sigmoid(proj(O_G, O_G + N_GATE))
    per_g = N_GATE // N_KV
    for g in range(N_KV):
        gt_ref[g] = zg[:, g * per_g:(g + 1) * per_g]


def _ab_proj_prompt(x, sc, sh, w_bf, cos2, sin2, lng, lnb, wtril, bst, seq):
    n, d = x.shape
    tm = 256
    tpb = seq // tm
    n_in = w_bf.shape[1]
    row = lambda i: (i, 0)
    mod = lambda i: (i // tpb, 0, 0)
    const2 = lambda i: (0, 0)
    pos = lambda i: (i % tpb, 0)
    out_shape = (
        jax.ShapeDtypeStruct((n, A_WIDTH), F32),
        jax.ShapeDtypeStruct((n, B_WIDTH), F32),
        jax.ShapeDtypeStruct((n, KV_WIDTH), F32),
        jax.ShapeDtypeStruct((n, KV_WIDTH), F32),
        jax.ShapeDtypeStruct((n, KV_WIDTH), F32),
        jax.ShapeDtypeStruct((3, 2, N_KV, n, HEAD_DIM), F32),
        jax.ShapeDtypeStruct((N_KV, n, N_GATE // N_KV), F32),
    )
    return pl.pallas_call(
        functools.partial(_ab_proj_kernel, tm=tm),
        grid=(n // tm,),
        in_specs=[
            pl.BlockSpec((tm, d), row),
            pl.BlockSpec((None, 1, d), mod),
            pl.BlockSpec((None, 1, d), mod),
            pl.BlockSpec((d, n_in), const2, pipeline_mode=pl.Buffered(1)),
            pl.BlockSpec((tm, 128), pos),
            pl.BlockSpec((tm, 128), pos),
            pl.BlockSpec((1, A_WIDTH), const2),
            pl.BlockSpec((1, A_WIDTH), const2),
            pl.BlockSpec((A_GROUPS, CHUNK, CHUNK), lambda i: (0, 0, 0)),
            pl.BlockSpec((CHUNK, A_GROUPS), const2),
        ],
        out_specs=(
            pl.BlockSpec((tm, A_WIDTH), row),
            pl.BlockSpec((tm, B_WIDTH), row),
            pl.BlockSpec((tm, KV_WIDTH), row),
            pl.BlockSpec((tm, KV_WIDTH), row),
            pl.BlockSpec((tm, KV_WIDTH), row),
            pl.BlockSpec((3, 2, N_KV, tm, HEAD_DIM), lambda i: (0, 0, 0, i, 0)),
            pl.BlockSpec((N_KV, tm, N_GATE // N_KV), lambda i: (0, i, 0)),
        ),
        out_shape=out_shape,
        compiler_params=_cparams(("parallel",), VMEM_LIMIT_BYTES),
        name="ab_proj_prompt",
    )(x, sc, sh, w_bf, cos2, sin2, lng, lnb, wtril, bst)


def _ab_post_sample(z, zg, cos2, sin2, lng, lnb, ws0, bs0):
    zu = z[:, O_U:O_V]
    zv = z[:, O_V:O_Q]
    vs = []
    for g in range(A_GROUPS):
        sl = slice(g * A_GROUP_DIM, (g + 1) * A_GROUP_DIM)
        vs.append(_ln(zv[:, sl]) * lng[:, sl] + lnb[:, sl])
    v = jnp.concatenate(vs, axis=1)
    a = zu * (ws0 * v + bs0)
    q = _rope(z[:, O_Q:O_KV], cos2, sin2)
    half = KV_WIDTH // 2
    kvs = []
    for br in range(3):
        lo = O_KV + br * KV_WIDTH
        k = _rope(z[:, lo:lo + half], cos2, sin2)
        kvs.append(jnp.concatenate([k, z[:, lo + half:lo + KV_WIDTH]], axis=1))
    return a, v, q, kvs[0], kvs[1], kvs[2], jax.nn.sigmoid(zg)


def _compress_mlp(lhs_lo, lhs_hi, w1_ref, w2_ref, kv, n):
    lo = _dot(lhs_lo, w1_ref[kv, 0])
    hi = _dot(lhs_hi, w1_ref[kv, 1])
    hsum = lo + pltpu.roll(hi, n - 1, 0)
    return _dot(_gelu(hsum), w2_ref[kv])


def _compress_prompt_kernel(x_ref, pe_ref, w1_ref, w2_ref, o_ref, *, n16):
    for kv in range(2):
        for gp in range(N_KV // 2):
            base = kv * (KV_WIDTH // 2) + gp * 128
            xg = jnp.concatenate(
                [x_ref[:, s * KV_WIDTH + base:s * KV_WIDTH + base + 128] for s in range(CMP_STRIDE)], axis=1)
            r = _compress_mlp(xg + pe_ref[kv, 0], xg + pe_ref[kv, 1], w1_ref, w2_ref, kv, n16)
            o_ref[kv, 2 * gp] = r[:, 0:HEAD_DIM]
            o_ref[kv, 2 * gp + 1] = r[:, HEAD_DIM:2 * HEAD_DIM]


def _compress_prompt(kvc3, pe_rows, w1bd, w2bd):
    b, n16, wid = kvc3.shape
    return pl.pallas_call(
        functools.partial(_compress_prompt_kernel, n16=n16),
        grid=(b,),
        in_specs=[
            pl.BlockSpec((None, n16, wid), lambda i: (i, 0, 0)),
            pl.BlockSpec(pe_rows.shape, lambda i: (0, 0, 0, 0)),
            pl.BlockSpec(w1bd.shape, lambda i: (0, 0, 0, 0)),
            pl.BlockSpec(w2bd.shape, lambda i: (0, 0, 0)),
        ],
        out_specs=pl.BlockSpec((None, 2, N_KV, n16, HEAD_DIM), lambda i: (i, 0, 0, 0, 0)),
        out_shape=jax.ShapeDtypeStruct((b, 2, N_KV, n16, HEAD_DIM), F32),
        compiler_params=_cparams(("parallel",), VMEM_LIMIT_BYTES),
        name="nsa_compress_prompt",
    )(kvc3, pe_rows, w1bd, w2bd)


def _compress_weights(pe, w1, w2):
    eye2 = jnp.eye(2, dtype=F32)
    w1r = w1.reshape(2, 2, CMP_STRIDE, HEAD_DIM, CMP_HID)
    w1bd = jnp.einsum('khsdc,gG->khsgdGc', w1r, eye2).reshape(2, 2, CMP_STRIDE * 128, 2 * CMP_HID).astype(BF16)
    pe_rows = jnp.broadcast_to(pe.reshape(2, 2, CMP_STRIDE, 1, HEAD_DIM), (2, 2, CMP_STRIDE, 2, HEAD_DIM))
    pe_rows = pe_rows.reshape(2, 2, 1, CMP_STRIDE * 128)
    w2bd = jnp.einsum('kcd,gG->kgcGd', w2, eye2).reshape(2, 2 * CMP_HID, 2 * HEAD_DIM).astype(BF16)
    return pe_rows, w1bd, w2bd


def _cmp_slc_map(n_rows, n_cmp, n_slc, n_cols):
    cs = np.arange(n_rows)[:, None] * CMP_STRIDE
    ss = np.arange(n_cols)[None, :] * SLC_LEN
    m = (cs < ss + SLC_LEN) & (cs + CMP_LEN > ss)
    m &= (np.arange(n_rows)[:, None] < n_cmp) & (np.arange(n_cols)[None, :] < n_slc)
    return m.astype(np.float32)


def _nsa_prompt_kernel(q_ref, gt_ref, kc_ref, vc_ref, ks_ref, vs_ref, kw_ref, vw_ref, mapt_ref, e_ref,
                       *rest, n_cmp, n_slc, n_sel, kt_len, n_cast):
    o_ref, selb_ref = rest[n_cast], rest[-1]
    for src, dst in zip(rest[:n_cast], rest[n_cast + 1:2 * n_cast + 1]):
        dst[...] = src[...].astype(BF16)
    i = pl.program_id(2)
    q0 = i * Q_BLOCK
    rows = HPG * Q_BLOCK
    tile4 = lambda x: jnp.concatenate([x] * HPG, axis=0)
    qb = q_ref[...]
    q4 = jnp.concatenate([qb[:, h * HEAD_DIM:(h + 1) * HEAD_DIM] for h in range(HPG)], axis=0)
    q4 = (q4 * HEAD_DIM ** -0.5).astype(BF16)
    t_pos = q0 + lax.broadcasted_iota(I32, (Q_BLOCK, 1), 0)

    n16 = kc_ref.shape[0]
    n_idx = lax.broadcasted_iota(I32, (1, n16), 1)
    bias_c = jnp.where((n_idx * CMP_STRIDE + CMP_LEN - 1 <= t_pos) & (n_idx < n_cmp), 0.0, NEG)
    s = _dot_nt(q4, kc_ref[...]) + tile4(bias_c)
    p = jnp.exp(s - jnp.max(s, -1, keepdims=True))
    l = jnp.sum(p, -1, keepdims=True)
    pn = p * jnp.where((tile4(t_pos) >= CMP_LEN - 1) & (n_cmp > 0), 1.0 / l, 0.0)
    o_c = _dot(pn, vc_ref[...])
    psum = pn[0:Q_BLOCK]
    for h in range(1, HPG):
        psum = psum + pn[h * Q_BLOCK:(h + 1) * Q_BLOCK]

    imp_t = _dot_nt(mapt_ref[...], psum)
    blk_t = lax.broadcasted_iota(I32, (n_slc, Q_BLOCK), 0)
    cur_t = (q0 + lax.broadcasted_iota(I32, (n_slc, Q_BLOCK), 1)) // SLC_LEN
    forced = (blk_t == 0) | (blk_t == cur_t) | (blk_t == cur_t - 1)
    score = jnp.where(blk_t <= cur_t, imp_t + jnp.where(forced, FORCE_BONUS, 0.0), -jnp.inf)
    rank = jnp.zeros((n_slc, Q_BLOCK), I32)
    for j in range(n_slc):
        r = score[j:j + 1, :]
        rank = rank + ((r > score) | ((r == score) & (blk_t > j))).astype(I32)
    sel_t = ((rank < n_sel) & (score > -jnp.inf)).astype(F32)
    selb_ref[...] = (_dot(sel_t.T, e_ref[...]) - 1.0) * (-NEG)

    def sel_tile(kt, carry, causal):
        m, l, acc = carry
        k0 = pl.multiple_of(kt * kt_len, kt_len)
        bias = selb_ref[:, pl.ds(k0, kt_len)]
        if causal:
            kp = k0 + lax.broadcasted_iota(I32, (1, kt_len), 1)
            bias = bias + jnp.where(kp <= t_pos, 0.0, NEG)
        s = _dot_nt(q4, ks_ref[pl.ds(k0, kt_len), :]) + tile4(bias)
        m_new = jnp.maximum(m, jnp.max(s, -1, keepdims=True))
        alpha = jnp.exp(m - m_new)
        p = jnp.exp(s - m_new)
        l = alpha * l + jnp.sum(p, -1, keepdims=True)
        acc = alpha * acc + _dot(p, vs_ref[pl.ds(k0, kt_len), :])
        return m_new, l, acc

    last_kt = (q0 + Q_BLOCK - 1) // kt_len
    init = (jnp.full((rows, 1), NEG, F32), jnp.zeros((rows, 1), F32), jnp.zeros((rows, HEAD_DIM), F32))
    carry = lax.fori_loop(0, last_kt, functools.partial(sel_tile, causal=False), init)
    _, l, acc = sel_tile(last_kt, carry, True)
    o_s = acc * (1.0 / l)

    n_wt = WINDOW // Q_BLOCK + 1
    c_idx = lax.broadcasted_iota(I32, (Q_BLOCK, Q_BLOCK), 1)
    r_idx = lax.broadcasted_iota(I32, (Q_BLOCK, Q_BLOCK), 0)
    s_parts, v_parts = [], []
    for j in range(n_wt):
        ks_j = q0 - WINDOW + j * Q_BLOCK
        ld = pl.multiple_of(jnp.maximum(ks_j, 0), Q_BLOCK)
        off = jnp.where(ks_j >= 0, 0.0, NEG)
        sj = _dot_nt(q4, kw_ref[pl.ds(ld, Q_BLOCK), :])
        if j == 0:
            sj = sj + tile4(jnp.where(c_idx > r_idx, 0.0, NEG) + off)
        elif j == n_wt - 1:
            sj = sj + tile4(jnp.where(c_idx <= r_idx, 0.0, NEG))
        else:
            sj = sj + off
        s_parts.append(sj)
        v_parts.append(vw_ref[pl.ds(ld, Q_BLOCK), :])
    s = jnp.concatenate(s_parts, axis=1)
    p = jnp.exp(s - jnp.max(s, -1, keepdims=True))
    o_w = _dot(p, jnp.concatenate(v_parts, axis=0)) * (1.0 / jnp.sum(p, -1, keepdims=True))

    gt = gt_ref[...]

    def gcol(br):
        return jnp.concatenate([gt[:, br * HPG + h:br * HPG + h + 1] for h in range(HPG)], axis=0)

    o = gcol(0) * o_c + gcol(1) * o_s + gcol(2) * o_w
    o_ref[...] = jnp.concatenate([o[h * Q_BLOCK:(h + 1) * Q_BLOCK] for h in range(HPG)], axis=1)


def _nsa_prompt(q, gates, kcv, kvt, batch, seq, to_bf16):
    n = q.shape[0]
    nq = seq // Q_BLOCK
    n_steps = batch * N_KV * nq
    step = lambda b, g, i: ((b * N_KV + g) * nq + i, 0)
    cast_specs = [pl.BlockSpec((w.shape[0] // n_steps, w.shape[1]), step) for w in to_bf16]
    n16 = kcv.shape[3]
    n_cmp = n16 - 1
    n_slc = -(-seq // SLC_LEN)
    n_sel = min(N_SEL, n_slc)
    kt_len = min(512, seq)
    mapt = jnp.asarray(_cmp_slc_map(n16, n_cmp, n_slc, n_slc).T)
    expand = jnp.asarray((np.arange(n_slc)[:, None] == np.arange(seq)[None, :] // SLC_LEN).astype(np.float32), BF16)
    per_g = N_GATE // N_KV
    qrow = lambda b, g, i: (b * nq + i, g)
    kvspec = lambda br, kv: pl.BlockSpec((None, None, None, seq, HEAD_DIM), lambda b, g, i: (br, kv, g, b, 0))
    return pl.pallas_call(
        functools.partial(_nsa_prompt_kernel, n_cmp=n_cmp, n_slc=n_slc, n_sel=n_sel, kt_len=kt_len,
                          n_cast=len(to_bf16)),
        grid=(batch, N_KV, nq),
        in_specs=[
            pl.BlockSpec((Q_BLOCK, HPG * HEAD_DIM), qrow),
            pl.BlockSpec((None, Q_BLOCK, per_g), lambda b, g, i: (g, b * nq + i, 0)),
            pl.BlockSpec((None, None, None, n16, HEAD_DIM), lambda b, g, i: (b, 0, g, 0, 0)),
            pl.BlockSpec((None, None, None, n16, HEAD_DIM), lambda b, g, i: (b, 1, g, 0, 0)),
            kvspec(1, 0), kvspec(1, 1), kvspec(2, 0), kvspec(2, 1),
            pl.BlockSpec(mapt.shape, lambda b, g, i: (0, 0)),
            pl.BlockSpec(expand.shape, lambda b, g, i: (0, 0)),
        ] + cast_specs,
        out_specs=[pl.BlockSpec((Q_BLOCK, HPG * HEAD_DIM), qrow)] + cast_specs,
        out_shape=[jax.ShapeDtypeStruct((n, B_WIDTH), F32)] + [jax.ShapeDtypeStruct(w.shape, BF16) for w in to_bf16],
        scratch_shapes=[pltpu.VMEM((Q_BLOCK, seq), F32)],
        compiler_params=_cparams(("parallel", "parallel", "arbitrary"), VMEM_LIMIT_BYTES),
        name="nsa_attn_prompt",
    )(q, gates, kcv, kcv, kvt, kvt, kvt, kvt, mapt, expand, *to_bf16)


FILL_PAGES = 8


def _page_copy(cache_hbm, pt_ref, xbuf, sem, b, slot, p, n_pages):
    return pltpu.make_async_copy(cache_hbm.at[pt_ref[b * n_pages + p]], xbuf.at[slot, p], sem.at[slot])


def _nsa_sample_cmp_kernel(pt_ref, q_ref, cache_hbm, pe_ref, w1_ref, w2_ref, map_ref,
                           oc_ref, idx_ref, xbuf, xrow, lhs_a, lhs_b, sem, *, n_pages, n_cmp, n_slc, n_sel, pos):
    b = pl.program_id(0)
    nb = pl.num_programs(0)
    slot = b % 2
    n16 = lhs_a.shape[0]
    cpp = n16 // n_pages

    def start_all(bb, sl):
        def body(p, c):
            _page_copy(cache_hbm, pt_ref, xbuf, sem, bb, sl, p, n_pages).start()
            return c
        lax.fori_loop(0, n_pages, body, 0)

    @pl.when(b == 0)
    def _():
        start_all(0, 0)

    @pl.when(b + 1 < nb)
    def _():
        start_all(b + 1, 1 - slot)

    def wait_body(p, c):
        _page_copy(cache_hbm, pt_ref, xbuf, sem, b, slot, p, n_pages).wait()
        return c
    lax.fori_loop(0, n_pages, wait_body, 0)

    kc, vc = [], []
    wkv = N_KV * HEAD_DIM
    for kv, dst in ((0, kc), (1, vc)):
        def fill(j, c):
            for u in range(FILL_PAGES):
                p = j * FILL_PAGES + u
                r0 = pl.multiple_of(p * cpp, cpp)
                for gp, lhs in enumerate((lhs_a, lhs_b)):
                    xrow[u, gp] = xbuf[slot, p, kv * wkv + gp * 128:kv * wkv + (gp + 1) * 128, :].T
                    for s in range(CMP_STRIDE):
                        lhs[pl.ds(r0, cpp), s * 128:(s + 1) * 128] = xrow[u, gp, pl.ds(s, cpp, stride=CMP_STRIDE), :]
            return c
        lax.fori_loop(0, n_pages // FILL_PAGES, fill, 0)
        for lhs in (lhs_a, lhs_b):
            xg = lhs[...]
            r = _compress_mlp(xg + pe_ref[kv, 0], xg + pe_ref[kv, 1], w1_ref, w2_ref, kv, n16)
            dst.append(r[:, 0:HEAD_DIM])
            dst.append(r[:, HEAD_DIM:2 * HEAD_DIM])

    qrow = q_ref[...] * HEAD_DIM ** -0.5
    n_idx = lax.broadcasted_iota(I32, (1, n16), 1)
    valid = (n_idx * CMP_STRIDE + CMP_LEN - 1 <= pos) & (n_idx < n_cmp)
    head_row = lax.broadcasted_iota(I32, (8, 1), 0) < HPG
    ncol = map_ref.shape[1]
    blk_r = lax.broadcasted_iota(I32, (1, ncol), 1)
    blk_c = lax.broadcasted_iota(I32, (ncol, 1), 0)
    cur = pos // SLC_LEN
    forced = (blk_r == 0) | (blk_r == cur) | (blk_r == cur - 1)
    oc_parts = []
    for g in range(N_KV):
        q8 = jnp.concatenate(
            [qrow[:, (g * HPG + h) * HEAD_DIM:(g * HPG + h + 1) * HEAD_DIM] for h in range(HPG)]
            + [jnp.zeros((8 - HPG, HEAD_DIM), F32)], axis=0)
        s = _dot_nt(q8, kc[g])
        p, l = _softmax_rows(s, valid)
        pn = jnp.where(head_row, p / jnp.maximum(l, 1e-30), 0.0)
        o8 = _dot(pn, vc[g])
        oc_parts += [o8[h:h + 1, :] for h in range(HPG)]
        psum = jnp.broadcast_to(jnp.sum(pn, axis=0, keepdims=True), pn.shape)
        imp = _dot(psum, map_ref[...])[0:1, :]
        score = jnp.where((blk_r <= cur) & (blk_r < n_slc), imp + jnp.where(forced, FORCE_BONUS, 0.0), -jnp.inf)
        score_c = jnp.broadcast_to(score, (8, ncol)).T[:, 0:1]
        beats = (score_c > score) | ((score_c == score) & (blk_c < blk_r))
        rank = jnp.sum(beats.astype(F32), axis=0, keepdims=True)
        r_iota = lax.broadcasted_iota(I32, (n_sel, ncol), 0).astype(F32)
        hit = (rank == r_iota) & (score > -jnp.inf)
        idx = jnp.sum(jnp.where(hit, blk_r.astype(F32), 0.0), axis=1, keepdims=True)
        idx_ref[:, g:g + 1] = idx.astype(I32)
    oc_ref[...] = jnp.concatenate(oc_parts, axis=1)


def _nsa_sample_cmp(page_table, q3, cache_t, pe_rows, w1bd, w2bd, past_len):
    db, n_pages = page_table.shape
    page = cache_t.shape[2]
    cpp = page // CMP_STRIDE
    n16 = n_pages * cpp
    n_cmp = (past_len + 1) // CMP_STRIDE - 1
    n_slc = -(-(past_len + 1) // SLC_LEN)
    n_sel = min(N_SEL, n_slc)
    ncol = -(-n_slc // 128) * 128
    cmap = jnp.asarray(_cmp_slc_map(n16, n_cmp, n_slc, ncol))
    grid_spec = pltpu.PrefetchScalarGridSpec(
        num_scalar_prefetch=1,
        grid=(db,),
        in_specs=[
            pl.BlockSpec((None, 1, B_WIDTH), lambda b, pt: (b, 0, 0)),
            pl.BlockSpec(memory_space=pl.ANY),
            pl.BlockSpec(pe_rows.shape, lambda b, pt: (0, 0, 0, 0)),
            pl.BlockSpec(w1bd.shape, lambda b, pt: (0, 0, 0, 0)),
            pl.BlockSpec(w2bd.shape, lambda b, pt: (0, 0, 0)),
            pl.BlockSpec(cmap.shape, lambda b, pt: (0, 0)),
        ],
        out_specs=(
            pl.BlockSpec((None, 1, B_WIDTH), lambda b, pt: (b, 0, 0)),
            pl.BlockSpec((None, n_sel, N_KV), lambda b, pt: (b, 0, 0)),
        ),
        scratch_shapes=[pltpu.VMEM((2, n_pages) + cache_t.shape[1:], F32),
                        pltpu.VMEM((FILL_PAGES, 2, page, 128), F32),
                        pltpu.VMEM((n16, CMP_STRIDE * 128), F32), pltpu.VMEM((n16, CMP_STRIDE * 128), F32),
                        pltpu.SemaphoreType.DMA((2,))],
    )
    return pl.pallas_call(
        functools.partial(_nsa_sample_cmp_kernel, n_pages=n_pages, n_cmp=n_cmp, n_slc=n_slc, n_sel=n_sel, pos=past_len),
        grid_spec=grid_spec,
        out_shape=(jax.ShapeDtypeStruct((db, 1, B_WIDTH), F32), jax.ShapeDtypeStruct((db, n_sel, N_KV), I32)),
        compiler_params=_cparams(("arbitrary",), VMEM_LIMIT_BYTES),
        name="nsa_cmp_sample",
    )(page_table.reshape(-1), q3, cache_t, pe_rows, w1bd, w2bd, cmap)


def _sel_copy(cache_hbm, page_ref, kbuf, sem, b, slot, g, r, n_sel):
    j = g * n_sel + r
    src = cache_hbm.at[page_ref[b * (N_KV * n_sel) + j], :, pl.ds(g * HEAD_DIM, HEAD_DIM), :]
    return pltpu.make_async_copy(src, kbuf.at[slot, j], sem.at[slot])


def _nsa_sample_attn_kernel(page_ref, idx_ref, q_ref, gt_ref, oc_ref, ks_new_ref, kw_new_ref, win_ref, cache_hbm,
                            o_ref, kbuf, sem, *, n_sel, pos):
    b = pl.program_id(0)
    nb = pl.num_programs(0)
    slot = b % 2
    page = kbuf.shape[4]
    per_page = page // SLC_LEN
    new_blk = pos // SLC_LEN
    last_real = new_blk - 1
    unroll = math.gcd(ROW_DMA_UNROLL, n_sel)

    def for_all(bb, sl, fn):
        for g in range(N_KV):
            def body(j, c):
                for u in range(unroll):
                    fn(_sel_copy(cache_hbm, page_ref, kbuf, sem, bb, sl, g, j * unroll + u, n_sel), u)
                return c
            lax.fori_loop(0, n_sel // unroll, body, 0)

    start = lambda cp, u: cp.start(priority=u % 2)

    @pl.when(b == 0)
    def _():
        for_all(0, 0, start)

    @pl.when(b + 1 < nb)
    def _():
        for_all(b + 1, 1 - slot, start)

    for_all(b, slot, lambda cp, u: cp.wait())

    qrow = q_ref[...] * HEAD_DIM ** -0.5
    gt = gt_ref[...]
    oc = oc_ref[...]
    ks_new = ks_new_ref[...]
    kw_new = kw_new_ref[...]
    half = KV_WIDTH // 2
    wlen = win_ref.shape[1]
    kp_w = pos - wlen + lax.broadcasted_iota(I32, (1, wlen), 1)
    mask_w = (kp_w >= 0) & (kp_w <= pos) & (kp_w > pos - WINDOW)
    out_parts = []
    for g in range(N_KV):
        q8 = jnp.concatenate(
            [qrow[:, (g * HPG + h) * HEAD_DIM:(g * HPG + h + 1) * HEAD_DIM] for h in range(HPG)]
            + [jnp.zeros((8 - HPG, HEAD_DIM), F32)], axis=0)
        ksl = slice(g * HEAD_DIM, (g + 1) * HEAD_DIM)
        vsl = slice(half + g * HEAD_DIM, half + (g + 1) * HEAD_DIM)

        def attend(s, mask, pv, k_new, v_new, has_new):
            rb = lambda t: t.astype(BF16).astype(F32)
            s_new = jnp.sum(rb(q8) * rb(k_new), axis=-1, keepdims=True)
            sm = jnp.where(mask, s, NEG)
            m = jnp.maximum(jnp.max(sm, -1, keepdims=True), jnp.where(has_new, s_new, NEG))
            p = jnp.where(mask, jnp.exp(sm - m), 0.0)
            p_new = jnp.where(has_new, jnp.exp(s_new - m), 0.0)
            l = jnp.maximum(jnp.sum(p, -1, keepdims=True) + p_new, 1e-30)
            return pv(p / l) + rb(p_new / l) * rb(v_new)

        kt_sel = jnp.concatenate([kbuf[slot, g * n_sel + r, 0] for r in range(n_sel)], axis=1)
        vt_sel = jnp.concatenate([kbuf[slot, g * n_sel + r, 1] for r in range(n_sel)], axis=1)
        blk_ids = [idx_ref[b * (N_KV * n_sel) + g * n_sel + r] for r in range(n_sel)]
        row_of = lambda vals: jnp.concatenate([jnp.full((1, page), v, I32) for v in vals], axis=1)
        lane_blk = (lax.broadcasted_iota(I32, (1, n_sel * page), 1) & (page - 1)) // SLC_LEN
        picked = (row_of(blk_ids) <= last_real) & (lane_blk == row_of([bid % per_page for bid in blk_ids]))
        has_new = functools.reduce(jnp.logical_or, [bid == new_blk for bid in blk_ids])
        o_s = attend(_dot(q8, kt_sel), picked, lambda p: _dot_nt(p, vt_sel), ks_new[:, ksl], ks_new[:, vsl], has_new)
        o_w = attend(_dot(q8, win_ref[ksl, :]), mask_w, lambda p: _dot_nt(p, win_ref[vsl, :]),
                     kw_new[:, ksl], kw_new[:, vsl], True)
        for h in range(HPG):
            hh = g * HPG + h
            c0 = g * (N_BRANCH * HPG)
            g0 = gt[:, c0 + h:c0 + h + 1]
            g1 = gt[:, c0 + HPG + h:c0 + HPG + h + 1]
            g2 = gt[:, c0 + 2 * HPG + h:c0 + 2 * HPG + h + 1]
            out_parts.append(g0 * oc[:, hh * HEAD_DIM:(hh + 1) * HEAD_DIM] + g1 * o_s[h:h + 1, :] + g2 * o_w[h:h + 1, :])
    o_ref[...] = jnp.concatenate(out_parts, axis=1)


def _nsa_sample_attn(page_table, sel_idx, q3, gates3, oc3, ks_new3, kw_new3, win_state, cache_s, past_len):
    db, n_pages = page_table.shape
    n_sel = sel_idx.shape[-1]
    wlen = win_state.shape[2]
    n_pool, _, page = cache_s.shape
    sel_blk = jnp.minimum(sel_idx.reshape(db, -1), past_len // SLC_LEN - 1)
    sel_page = jnp.take_along_axis(page_table, sel_blk // (page // SLC_LEN), axis=1)
    cache4 = cache_s.reshape(n_pool, 2, KV_WIDTH // 2, page)
    row = lambda wdt: pl.BlockSpec((None, 1, wdt), lambda b, pt, ix: (b, 0, 0))
    grid_spec = pltpu.PrefetchScalarGridSpec(
        num_scalar_prefetch=2,
        grid=(db,),
        in_specs=[
            row(B_WIDTH), row(N_GATE), row(B_WIDTH), row(KV_WIDTH), row(KV_WIDTH),
            pl.BlockSpec((None, KV_WIDTH, wlen), lambda b, pt, ix: (b, 0, 0)),
            pl.BlockSpec(memory_space=pl.ANY),
        ],
        out_specs=row(B_WIDTH),
        scratch_shapes=[pltpu.VMEM((2, N_KV * n_sel, 2, HEAD_DIM, page), F32), pltpu.SemaphoreType.DMA((2,))],
    )
    return pl.pallas_call(
        functools.partial(_nsa_sample_attn_kernel, n_sel=n_sel, pos=past_len),
        grid_spec=grid_spec,
        out_shape=jax.ShapeDtypeStruct((db, 1, B_WIDTH), F32),
        compiler_params=_cparams(("arbitrary",), VMEM_LIMIT_BYTES),
        name="nsa_attn_sample",
    )(sel_page.reshape(-1), sel_idx.reshape(-1), q3, gates3, oc3, ks_new3, kw_new3, win_state, cache4)


def _post_ln_router(acc, x, gate, lng, lnb, sc2, sh2, rw_t):
    x1 = _ln(ALPHA * x + gate * acc) * lng + lnb
    h2 = _modulate(x1, sc2, sh2)
    return x1, h2, _dot_nt(rw_t, h2)


def _mix_out_kernel(l0_ref, l1_ref, w_ref, x_ref, gate_ref, lng_ref, lnb_ref, sc2_ref, sh2_ref, rw_ref, h2_init,
                    x1_ref, h2_ref, lg_ref, *, concat):
    del h2_init
    if concat:
        k0 = l0_ref.shape[1]
        acc = _dot(l0_ref[...], w_ref[0:k0, :]) + _dot(l1_ref[...], w_ref[k0:, :])
    else:
        acc = _dot(l0_ref[...] * l1_ref[...], w_ref[...])
    x1, h2, lg = _post_ln_router(acc, x_ref[...], gate_ref[...], lng_ref[...], lnb_ref[...],
                                 sc2_ref[...], sh2_ref[...], rw_ref[...])
    x1_ref[...] = x1
    h2_ref[...] = h2
    lg_ref[...] = lg


def _mix_out_prompt(l0, l1, w_bf, x, gate, lng, lnb, sc2, sh2, rw, seq, concat, tail_rows, name):
    n, d = x.shape
    tm = 512
    h2_init = jnp.zeros((n + tail_rows, d), F32)
    tpb = seq // tm
    row = lambda i: (i, 0)
    mod = lambda i: (i // tpb, 0, 0)
    const2 = lambda i: (0, 0)
    return pl.pallas_call(
        functools.partial(_mix_out_kernel, concat=concat),
        grid=(n // tm,),
        in_specs=[
            pl.BlockSpec((tm, l0.shape[1]), row),
            pl.BlockSpec((tm, l1.shape[1]), row),
            pl.BlockSpec(w_bf.shape, const2, pipeline_mode=pl.Buffered(1)),
            pl.BlockSpec((tm, d), row),
            pl.BlockSpec((None, 1, d), mod),
            pl.BlockSpec((1, d), const2),
            pl.BlockSpec((1, d), const2),
            pl.BlockSpec((None, 1, d), mod),
            pl.BlockSpec((None, 1, d), mod),
            pl.BlockSpec(rw.shape, const2),
            pl.BlockSpec(memory_space=pl.ANY),
        ],
        out_specs=(pl.BlockSpec((tm, d), row), pl.BlockSpec((tm, d), row),
                   pl.BlockSpec((N_EXPERTS, tm), lambda i: (0, i))),
        out_shape=(jax.ShapeDtypeStruct((n, d), F32), jax.ShapeDtypeStruct((n + tail_rows, d), F32),
                   jax.ShapeDtypeStruct((N_EXPERTS, n), F32)),
        input_output_aliases={10: 1},
        compiler_params=_cparams(("parallel",), VMEM_LIMIT_BYTES),
        name=name,
    )(l0, l1, w_bf, x, gate, lng, lnb, sc2, sh2, rw, h2_init)


def _top2_route(lg, rb):
    s = jax.nn.sigmoid(lg)
    sb = s + rb
    rows = [sb[e:e + 1, :] for e in range(N_EXPERTS)]
    gs = []
    for g in range(N_GROUPS):
        v = rows[g * EXPERTS_PER_GROUP:(g + 1) * EXPERTS_PER_GROUP]
        pair = [v[i] + v[j] for i in range(EXPERTS_PER_GROUP) for j in range(i + 1, EXPERTS_PER_GROUP)]
        gs.append(functools.reduce(jnp.maximum, pair))
    best, gi = gs[0], jnp.zeros(gs[0].shape, I32)
    for g in range(1, N_GROUPS):
        better = gs[g] > best
        gi = jnp.where(better, g, gi)
        best = jnp.where(better, gs[g], best)
    cand = [jnp.where(gi == e // EXPERTS_PER_GROUP, rows[e], -jnp.inf) for e in range(N_EXPERTS)]
    ids = []
    for k in range(TOP_K):
        vk = jnp.full(cand[0].shape, -jnp.inf, F32)
        ik = jnp.zeros(cand[0].shape, I32)
        for e in range(N_EXPERTS):
            c = cand[e]
            for prev in ids:
                c = jnp.where(prev == e, -jnp.inf, c)
            better = c > vk
            ik = jnp.where(better, e, ik)
            vk = jnp.where(better, c, vk)
        ids.append(ik)
    ws = [functools.reduce(jnp.add, [jnp.where(ik == e, s[e:e + 1, :], 0.0) for e in range(N_EXPERTS)]) for ik in ids]
    tot = functools.reduce(jnp.add, ws)
    return ids, [w / tot for w in ws]


def _route_kernel(lg_ref, rb_ref, w_ref, dest_ref, be_ref, nu_ref, tot_ref, run_ref, ps_ref, *, blk, n_valid, n_slots):
    ph = pl.program_id(0)
    i = pl.program_id(1)
    tm = lg_ref.shape[1]
    ids, ws = _top2_route(lg_ref[...], rb_ref[...])
    e_iota = lax.broadcasted_iota(I32, (N_EXPERTS, tm), 0)
    valid = i * tm + lax.broadcasted_iota(I32, (1, tm), 1) < n_valid
    oh = [((e_iota == ik) & valid).astype(F32) for ik in ids]
    ohsum = functools.reduce(jnp.add, oh)
    tile_cnt = jnp.sum(ohsum, axis=1, keepdims=True)

    @pl.when((ph == 0) & (i == 0))
    def _():
        tot_ref[...] = jnp.zeros_like(tot_ref)

    @pl.when(ph == 0)
    def _():
        tot_ref[...] = tot_ref[...] + tile_cnt

    @pl.when((ph == 1) & (i == 0))
    def _():
        cnt = tot_ref[...]
        padded = jnp.floor((cnt + (blk - 1)) * (1.0 / blk)) * blk
        sub = lax.broadcasted_iota(I32, cnt.shape, 0)
        start = jnp.zeros_like(cnt)
        for e in range(N_EXPERTS):
            start = start + jnp.where(sub > e, padded[e:e + 1, :], 0.0)
        ps_ref[...] = start
        run_ref[...] = jnp.zeros_like(run_ref)
        pad_end = start[:, 0:1] + padded[:, 0:1]
        blk_lo = (lax.broadcasted_iota(I32, (N_EXPERTS, be_ref.shape[1]), 1) * blk).astype(F32)
        n_le = jnp.sum((pad_end <= blk_lo).astype(F32), axis=0, keepdims=True)
        be_ref[...] = jnp.minimum(n_le, N_EXPERTS - 1.0).astype(I32)
        nu_ref[...] = (jnp.max(pad_end, axis=0, keepdims=True) * (1.0 / blk) + jnp.zeros(nu_ref.shape, F32)).astype(I32)

    @pl.when(ph == 1)
    def _():
        t_r = lax.broadcasted_iota(I32, (tm, tm), 0)
        t_c = lax.broadcasted_iota(I32, (tm, tm), 1)
        before = _dot(ohsum, (t_r < t_c).astype(F32))
        base = before + run_ref[:, 0:1] + ps_ref[:, 0:1]
        for k in range(TOP_K):
            w_ref[k:k + 1, :] = ws[k]
            slot = jnp.sum(oh[k] * base, axis=0, keepdims=True).astype(I32)
            dest_ref[k:k + 1, :] = jnp.where(valid, slot, n_slots)
        run_ref[...] = run_ref[...] + tile_cnt


def _route_tables(logits_t, router_b, blk, n_valid):
    n = logits_t.shape[1]
    tm = min(512, n)
    a = n_valid * TOP_K
    n_blocks = -(-a // blk) + N_EXPERTS
    nb_pad = -(-n_blocks // 128) * 128
    tok_blk = lambda p, i: (0, i * p)
    const = lambda p, i: (0, 0)
    w, dest, block_e, n_used = pl.pallas_call(
        functools.partial(_route_kernel, blk=blk, n_valid=n_valid, n_slots=n_blocks * blk),
        grid=(2, n // tm),
        in_specs=[pl.BlockSpec((N_EXPERTS, tm), lambda p, i: (0, i)), pl.BlockSpec((N_EXPERTS, 1), const)],
        out_specs=(pl.BlockSpec((TOP_K, tm), tok_blk), pl.BlockSpec((TOP_K, tm), tok_blk),
                   pl.BlockSpec((1, nb_pad), const), pl.BlockSpec((1, 128), const)),
        out_shape=(jax.ShapeDtypeStruct((TOP_K, n), F32), jax.ShapeDtypeStruct((TOP_K, n), I32),
                   jax.ShapeDtypeStruct((1, nb_pad), I32), jax.ShapeDtypeStruct((1, 128), I32)),
        scratch_shapes=[pltpu.VMEM((N_EXPERTS, 128), F32)] * 3,
        compiler_params=_cparams(("arbitrary", "arbitrary")),
        name="moe_route",
    )(logits_t, router_b.reshape(N_EXPERTS, 1).astype(F32))
    tok = jnp.repeat(jnp.arange(n, dtype=I32), TOP_K)
    slot_tok = jnp.zeros((n_blocks * blk,), I32).at[dest.T.reshape(-1)].set(tok, mode="drop")
    return slot_tok, dest, w, block_e[0, :n_blocks], n_used[0, :1], n_blocks


def _row_copy(src_hbm, row, buf, slot, r, sem):
    return pltpu.make_async_copy(src_hbm.at[pl.ds(row, 1), :], buf.at[slot, pl.ds(r, 1), :], sem.at[slot])


ROW_DMA_UNROLL = 8
MOE_BLK = 256


def _start_row_gather(src_hbm, row_of, buf, slot, n_rows, sem):
    unroll = math.gcd(ROW_DMA_UNROLL, n_rows)

    def body(j, c):
        for u in range(unroll):
            r = j * unroll + u
            _row_copy(src_hbm, row_of(r), buf, slot, r, sem).start(priority=u % 2)
        return c
    lax.fori_loop(0, n_rows // unroll, body, 0)


def _wait_row_gather(src_hbm, buf, slot, n_rows, sem):
    unroll = math.gcd(ROW_DMA_UNROLL, n_rows)

    def body(j, c):
        for u in range(unroll):
            _row_copy(src_hbm, 0, buf, slot, j * unroll + u, sem).wait()
        return c
    lax.fori_loop(0, n_rows // unroll, body, 0)


def _moe_kernel(be_ref, tok_ref, nu_ref, x_hbm, w1_ref, w2_ref, y_ref, xs_buf, gsem, *, blk):
    i = pl.program_id(0)
    n_used = nu_ref[0]
    slot = i % 2

    def start_gather(bi, sl):
        _start_row_gather(x_hbm, lambda r: tok_ref[bi * blk + r], xs_buf, sl, blk, gsem)

    @pl.when(i == 0)
    def _():
        start_gather(0, 0)

    @pl.when(i + 1 < n_used)
    def _():
        start_gather(i + 1, 1 - slot)

    @pl.when(i < n_used)
    def _():
        _wait_row_gather(x_hbm, xs_buf, slot, blk, gsem)
        z = _dot(xs_buf[slot], w1_ref[...])
        de = z.shape[1] // 2
        act = _silu(z[:, :de]) * z[:, de:]
        y_ref[...] = _dot(act, w2_ref[...])

    @pl.when(i >= n_used)
    def _():
        y_ref[...] = jnp.zeros_like(y_ref)


def _moe_ffn(h2, slot_tok, block_e, n_used, n_blocks, w_in_bf, w_out_bf, layer, blk, name):
    n, d = h2.shape
    de2 = w_in_bf.shape[-1]
    grid_spec = pltpu.PrefetchScalarGridSpec(
        num_scalar_prefetch=3,
        grid=(n_blocks,),
        in_specs=[
            pl.BlockSpec(memory_space=pl.ANY),
            pl.BlockSpec((None, None, d, de2), lambda i, be, tk, nu: (layer, be[i], 0, 0)),
            pl.BlockSpec((None, None, de2 // 2, d), lambda i, be, tk, nu: (layer, be[i], 0, 0)),
        ],
        out_specs=pl.BlockSpec((blk, d), lambda i, be, tk, nu: (i, 0)),
        scratch_shapes=[pltpu.VMEM((2, blk, d), F32), pltpu.SemaphoreType.DMA((2,))],
    )
    return pl.pallas_call(
        functools.partial(_moe_kernel, blk=blk),
        grid_spec=grid_spec,
        out_shape=jax.ShapeDtypeStruct((n_blocks * blk, d), F32),
        compiler_params=_cparams(("arbitrary",), VMEM_LIMIT_BYTES),
        name=name,
    )(block_e, slot_tok, n_used, h2, w_in_bf, w_out_bf)


def _combine_kernel(so_ref, x_ref, wt_ref, gate_ref, lng_ref, lnb_ref, y_hbm, o_ref, ybuf, sem, *, tm):
    i = pl.program_id(0)
    nt = pl.num_programs(0)
    slot = i % 2

    def start_gather(ti, sl):
        for k in range(TOP_K):
            _start_row_gather(y_hbm, lambda r: so_ref[(ti * tm + r) * TOP_K + k], ybuf.at[k], sl, tm, sem.at[k])

    @pl.when(i == 0)
    def _():
        start_gather(0, 0)

    @pl.when(i + 1 < nt)
    def _():
        start_gather(i + 1, 1 - slot)

    for k in range(TOP_K):
        _wait_row_gather(y_hbm, ybuf.at[k], slot, tm, sem.at[k])

    wt = wt_ref[...]
    f = wt[:, 0:1] * ybuf[0, slot] + wt[:, 1:2] * ybuf[1, slot]
    o_ref[...] = _ln(ALPHA * x_ref[...] + gate_ref[...] * f) * lng_ref[...] + lnb_ref[...]


def _moe_combine(slot_of, x1, wts, gate, lng, lnb, y, tm, tpb, name):
    n, d = x1.shape
    grid_spec = pltpu.PrefetchScalarGridSpec(
        num_scalar_prefetch=1,
        grid=(n // tm,),
        in_specs=[
            pl.BlockSpec((tm, d), lambda i, so: (i, 0)),
            pl.BlockSpec((tm, TOP_K), lambda i, so: (i, 0)),
            pl.BlockSpec((None, gate.shape[1], d), lambda i, so: (i // tpb, 0, 0)),
            pl.BlockSpec((1, d), lambda i, so: (0, 0)),
            pl.BlockSpec((1, d), lambda i, so: (0, 0)),
            pl.BlockSpec(memory_space=pl.ANY),
        ],
        out_specs=pl.BlockSpec((tm, d), lambda i, so: (i, 0)),
        scratch_shapes=[pltpu.VMEM((TOP_K, 2, tm, d), F32), pltpu.SemaphoreType.DMA((TOP_K, 2))],
    )
    return pl.pallas_call(
        functools.partial(_combine_kernel, tm=tm),
        grid_spec=grid_spec,
        out_shape=jax.ShapeDtypeStruct((n, d), F32),
        compiler_params=_cparams(("arbitrary",)),
        name=name,
    )(slot_of, x1, wts, gate, lng, lnb, y)


def _append_kernel(rows_ref, buf_hbm, out_hbm, sem, *, at):
    del buf_hbm
    cp = pltpu.make_async_copy(rows_ref, out_hbm.at[pl.ds(at, rows_ref.shape[0]), :], sem)
    cp.start()
    cp.wait()


def _append_rows(buf, rows, at):
    return pl.pallas_call(
        functools.partial(_append_kernel, at=at),
        in_specs=[pl.BlockSpec(memory_space=pltpu.VMEM), pl.BlockSpec(memory_space=pl.ANY)],
        out_specs=pl.BlockSpec(memory_space=pl.ANY),
        out_shape=jax.ShapeDtypeStruct(buf.shape, buf.dtype),
        input_output_aliases={1: 0},
        scratch_shapes=[pltpu.SemaphoreType.DMA(())],
        name="append_sample_rows",
    )(rows, buf)


def _channel_sublayer(x1_p, x1_s, h2_all, lg_p, lg_s, gate_p, gate_s, lng, lnb, router_b, w_in, w_out, layer,
                      seq, tag):
    n_p, d = x1_p.shape
    db = x1_s.shape[0]
    n_all = n_p + db
    n_pad = -(-n_all // 512) * 512
    lg_all = jnp.concatenate([lg_p, lg_s, jnp.zeros((N_EXPERTS, n_pad - n_all), F32)], axis=1)
    slot_tok, dest, wts, block_e, n_used, n_blocks = _route_tables(lg_all, router_b, MOE_BLK, n_all)
    y = _moe_ffn(h2_all, slot_tok, block_e, n_used, n_blocks, w_in, w_out, layer, MOE_BLK, "moe_ffn_" + tag)
    part = lambda a, lo, hi: a[:, lo:hi].T
    out_p = _moe_combine(part(dest, 0, n_p).reshape(-1), x1_p, part(wts, 0, n_p), gate_p, lng, lnb, y,
                         128, seq // 128, "moe_combine_" + tag + "_prompt")
    out_s = _moe_combine(part(dest, n_p, n_all).reshape(-1), x1_s, part(wts, n_p, n_all), gate_s.reshape(1, db, d),
                         lng, lnb, y, db, 1, "moe_combine_" + tag + "_sample")
    return out_p, out_s


def _rec_in_kernel(x_ref, sc_ref, sh_ref, w_ref, gg_ref, xbr_ref):
    hb = _modulate(x_ref[...], sc_ref[...], sh_ref[...]).astype(BF16)
    d = gg_ref.shape[1]
    gg_ref[...] = _gelu(jnp.dot(hb, w_ref[:, 0:d], preferred_element_type=F32))
    xbr_ref[...] = jnp.dot(hb, w_ref[:, d:], preferred_element_type=F32)


def _rec_in_prompt(x, sc, sh, w_bf, seq):
    n, d = x.shape
    dr = w_bf.shape[1] // 2
    tm = 512
    tpb = seq // tm
    row = lambda i: (i, 0)
    mod = lambda i: (i // tpb, 0, 0)
    return pl.pallas_call(
        _rec_in_kernel,
        grid=(n // tm,),
        in_specs=[
            pl.BlockSpec((tm, d), row),
            pl.BlockSpec((None, 1, d), mod),
            pl.BlockSpec((None, 1, d), mod),
            pl.BlockSpec(w_bf.shape, lambda i: (0, 0), pipeline_mode=pl.Buffered(1)),
        ],
        out_specs=(pl.BlockSpec((tm, dr), row), pl.BlockSpec((tm, dr), row)),
        out_shape=(jax.ShapeDtypeStruct((n, dr), F32), jax.ShapeDtypeStruct((n, dr), F32)),
        compiler_params=_cparams(("parallel",), VMEM_LIMIT_BYTES),
        name="rec_in_prompt",
    )(x, sc, sh, w_bf)


def _log1p(y):
    w = 1.0 + y
    return jnp.where(w == 1.0, y, jnp.log(w) * (y / jnp.where(w == 1.0, 1.0, w - 1.0)))


def _expm1(x):
    u = jnp.exp(x)
    safe = (u != 1.0) & (u > 0.0)
    return jnp.where(u == 1.0, x, jnp.where(u > 0.0, (u - 1.0) * (x / jnp.where(safe, jnp.log(u), 1.0)), -1.0))


def _softplus(x):
    return jnp.maximum(x, 0.0) + _log1p(jnp.exp(-jnp.abs(x)))


def _rg_terms(xb, z, ba, bx, lam):
    bs = xb.shape[1]
    r = jax.nn.sigmoid(z[:, :bs] + ba)
    gi = jax.nn.sigmoid(z[:, bs:] + bx)
    log_a = -RG_C * r * _softplus(-lam)
    a = jnp.exp(log_a)
    return a, jnp.sqrt(-_expm1(2.0 * log_a)) * (gi * xb)


def _scan_rows(a, b, h_in):
    tm = a.shape[0]
    sub = lax.broadcasted_iota(I32, a.shape, 0) & 7
    for step in (1, 2, 4):
        keep = sub >= step
        b = jnp.where(keep, a * pltpu.roll(b, step, 0) + b, b)
        a = jnp.where(keep, a * pltpu.roll(a, step, 0), a)
    outs = []
    h = h_in
    for j in range(tm // 8):
        hj = a[8 * j:8 * j + 8] * h + b[8 * j:8 * j + 8]
        outs.append(hj)
        h = hj[7:8]
    return jnp.concatenate(outs, axis=0), h


def _rec_gate_kernel(x_ref, cw_ref, cb_ref, wab_ref, ba_ref, bx_ref, lam_ref, hs_ref, ht_ref, carry_ref, h_ref,
                     *, tpb, tm):
    i = pl.program_id(0)

    @pl.when(i % tpb == 0)
    def _():
        carry_ref[...] = jnp.zeros_like(carry_ref)
        h_ref[...] = jnp.zeros_like(h_ref)

    x = x_ref[...]
    carry = carry_ref[...]
    d = x.shape[1]
    row8 = lax.broadcasted_iota(I32, (8, d), 0)
    xc = cb_ref[...] + x * cw_ref[CONV_W - 1:CONV_W, :]
    for k in range(1, CONV_W):
        xr = pltpu.roll(x, k, 0)
        head = jnp.where(row8 < k, pltpu.roll(carry, k, 0), xr[0:8])
        xk = jnp.concatenate([head, xr[8:]], axis=0)
        xc = xc + xk * cw_ref[CONV_W - 1 - k:CONV_W - k, :]
    carry_ref[...] = x[tm - 8:tm]
    bs = d // RNN_BLOCKS
    for nb in range(RNN_BLOCKS):
        sl = slice(nb * bs, (nb + 1) * bs)
        xb = xc[:, sl]
        a, bt = _rg_terms(xb, _dot(xb, wab_ref[nb]), ba_ref[:, sl], bx_ref[:, sl], lam_ref[:, sl])
        hs, h_last = _scan_rows(a, bt, h_ref[0:1, sl])
        hs_ref[:, sl] = hs
        h_ref[:, sl] = jnp.broadcast_to(h_last, (8, bs))
    ht_ref[...] = h_ref[...]


def _rec_scan_prompt(xbr, cw, cb, wab, ba, bx, lam, seq):
    n, d = xbr.shape
    tm = 256
    tpb = seq // tm
    row = lambda i: (i, 0)
    const2 = lambda i: (0, 0)
    return pl.pallas_call(
        functools.partial(_rec_gate_kernel, tpb=tpb, tm=tm),
        grid=(n // tm,),
        in_specs=[
            pl.BlockSpec((tm, d), row),
            pl.BlockSpec(cw.shape, const2),
            pl.BlockSpec((1, d), const2),
            pl.BlockSpec(wab.shape, lambda i: (0, 0, 0)),
            pl.BlockSpec((1, d), const2),
            pl.BlockSpec((1, d), const2),
            pl.BlockSpec((1, d), const2),
        ],
        out_specs=(pl.BlockSpec((tm, d), row), pl.BlockSpec((None, 8, d), lambda i: (i // tpb, 0, 0))),
        out_shape=(jax.ShapeDtypeStruct((n, d), F32), jax.ShapeDtypeStruct((n // seq, 8, d), F32)),
        scratch_shapes=[pltpu.VMEM((8, d), F32), pltpu.VMEM((8, d), F32)],
        compiler_params=_cparams(("arbitrary",), VMEM_LIMIT_BYTES),
        name="rec_scan_prompt",
    )(xbr, cw, cb, wab, ba, bx, lam)


def _rec_step_sample(z, buf0, buf1, buf2, h0, cw, cb, wa, wx, ba, bx, lam):
    d = h0.shape[1]
    gate_br = z[:, :d]
    xbr = z[:, d:]
    xc = cb + buf0 * cw[0:1] + buf1 * cw[1:2] + buf2 * cw[2:3] + xbr * cw[3:4]
    bs = d // RNN_BLOCKS
    a_parts, b_parts = [], []
    for nb in range(RNN_BLOCKS):
        sl = slice(nb * bs, (nb + 1) * bs)
        xb = xc[:, sl]
        zz = jnp.concatenate([_dot(xb, wa[nb]), _dot(xb, wx[nb])], axis=1)
        a, bt = _rg_terms(xb, zz, ba[:, sl], bx[:, sl], lam[:, sl])
        a_parts.append(a)
        b_parts.append(bt)
    h = jnp.concatenate(a_parts, axis=1) * h0 + jnp.concatenate(b_parts, axis=1)
    return _gelu(gate_br) * h, h, xbr


def kernel(x_prompt, x_sample, c_prompt, c_sample, cache_nsa_cmp, cache_nsa_slc, state_nsa_win, state_rglru_conv, state_rglru_h, page_table, ada_w, ada_b, ln_g, ln_b, ab_w_in, ab_w_out, gmlp_ln_g, gmlp_ln_b, gmlp_ws, gmlp_bs, nsa_cmp_pe, nsa_cmp_w1, nsa_cmp_w2, rec_w_in, rec_conv_w, rec_conv_b, rg_wa, rg_ba, rg_wx, rg_bx, rg_lambda, rec_w_out, router_w, router_b, moe_w_in, moe_w_out):
    bp, seq, d = x_prompt.shape
    db = x_sample.shape[0]
    assert x_sample.shape[1] == 1
    n_pool, page = cache_nsa_cmp.shape[:2]
    n_pages = page_table.shape[1]
    past_len = n_pages * page
    assert seq % 256 == 0 and past_len % SLC_LEN == 0 and page % SLC_LEN == 0 and past_len >= WINDOW
    n_p = bp * seq
    d_rnn = rec_conv_b.shape[0]

    mods = _ada_all(jnp.concatenate([c_prompt, c_sample], axis=0), ada_w, ada_b)

    def mod_rows(layer, sub):
        m = mods[layer * 2 + sub]
        parts = [m[:, j * d:(j + 1) * d] for j in range(3)]
        return [p[:bp].reshape(bp, 1, d) for p in parts], [p[bp:] for p in parts]

    lnrow = lambda a: a.reshape(1, d)

    gperm = np.array([(g * HPG + h) * N_BRANCH + br for g in range(N_KV) for br in range(N_BRANCH) for h in range(HPG)])
    w_gate = ab_w_in[:, O_G:][:, gperm]
    w_ab_bf = jnp.concatenate([ab_w_in[:, :O_G], w_gate], axis=1).astype(BF16)
    tril = jnp.tril(jnp.ones((CHUNK, CHUNK), F32))
    wtril = (gmlp_ws * tril).astype(BF16)
    bst = gmlp_bs.T
    glng = gmlp_ln_g.reshape(1, A_WIDTH)
    glnb = gmlp_ln_b.reshape(1, A_WIDTH)
    pe_rows, w1bd, w2bd = _compress_weights(nsa_cmp_pe, nsa_cmp_w1, nsa_cmp_w2)
    rw_t = router_w.T
    xp = x_prompt.reshape(n_p, d)
    xs = x_sample.reshape(db, d)

    (sh_p, sc_p, g_p), (sh_s, sc_s, g_s) = mod_rows(0, 0)
    (sh2_p, sc2_p, g2_p), (sh2_s, sc2_s, g2_s) = mod_rows(0, 1)
    cos_p, sin_p = _rope_tables(np.arange(seq))
    a_p, q_p, kvc_p, kvs_p, kvw_p, kvt_p, gt_p = _ab_proj_prompt(
        xp, sc_p, sh_p, w_ab_bf, cos_p, sin_p, glng, glnb, wtril, bst, seq)
    kcv_p = _compress_prompt(kvc_p.reshape(bp, seq // CMP_STRIDE, CMP_STRIDE * KV_WIDTH), pe_rows, w1bd, w2bd)
    o_p, moe_in_bf, moe_out_bf = _nsa_prompt(
        q_p, gt_p, kcv_p, kvt_p, bp, seq,
        [moe_w_in.reshape(-1, moe_w_in.shape[-1]), moe_w_out.reshape(-1, moe_w_out.shape[-1])])
    moe_in_bf = moe_in_bf.reshape(moe_w_in.shape)
    moe_out_bf = moe_out_bf.reshape(moe_w_out.shape)
    w_out_bf = ab_w_out.astype(BF16)
    x1_p, h2_p, lg_p = _mix_out_prompt(a_p, o_p, w_out_bf, xp, g_p, lnrow(ln_g[0, 0]), lnrow(ln_b[0, 0]),
                                       sc2_p, sh2_p, rw_t, seq, True, db, "mix_out_l0_prompt")

    z_s = _small_mm(_modulate, [xs, sc_s, sh_s], w_ab_bf, O_G, 512, "ab_proj_sample")
    zg_s = _small_mm(_modulate, [xs, sc_s, sh_s], w_gate, N_GATE, N_GATE, "ab_gate_sample")
    cos_s, sin_s = _rope_tables(np.full((db,), past_len))
    ws0 = jnp.repeat(gmlp_ws[:, 0, 0], A_GROUP_DIM).reshape(1, A_WIDTH)
    bs0 = jnp.repeat(gmlp_bs[:, 0], A_GROUP_DIM).reshape(1, A_WIDTH)
    sds = lambda *s: jax.ShapeDtypeStruct(s, F32)
    a_s, v_s, q_s, kvc_s, kvs_s, kvw_s, gt_s = _vmem_call(
        _ab_post_sample,
        (sds(db, A_WIDTH), sds(db, A_WIDTH), sds(db, B_WIDTH), sds(db, KV_WIDTH), sds(db, KV_WIDTH),
         sds(db, KV_WIDTH), sds(db, N_GATE)),
        (z_s, zg_s, cos_s, sin_s, glng, glnb, ws0, bs0), "ab_post_sample")
    q3 = q_s.reshape(db, 1, B_WIDTH)
    page_t = lambda c: jnp.transpose(c, (0, 2, 3, 4, 1)).reshape(n_pool, KV_WIDTH, page)
    oc3, sel_idx = _nsa_sample_cmp(page_table, q3, page_t(cache_nsa_cmp), pe_rows, w1bd, w2bd, past_len)
    sel_idx = jnp.transpose(sel_idx, (0, 2, 1))
    o_s = _nsa_sample_attn(page_table, sel_idx, q3, gt_s.reshape(db, 1, N_GATE), oc3,
                           kvs_s.reshape(db, 1, KV_WIDTH), kvw_s.reshape(db, 1, KV_WIDTH),
                           jnp.transpose(state_nsa_win, (0, 2, 3, 4, 1)).reshape(db, KV_WIDTH, -1),
                           page_t(cache_nsa_slc),
                           past_len).reshape(db, B_WIDTH)
    f_s = _small_mm(lambda a, o: jnp.concatenate([a, o], axis=1), [a_s, o_s], w_out_bf, d, 512, "mix_out_l0_sample")
    x1_s, h2_s, lg_s = _vmem_call(
        _post_ln_router, (sds(db, d), sds(db, d), sds(N_EXPERTS, db)),
        (f_s, xs, g_s, lnrow(ln_g[0, 0]), lnrow(ln_b[0, 0]), sc2_s, sh2_s, rw_t), "post_l0_sample")

    x2_p, x2_s = _channel_sublayer(x1_p, x1_s, _append_rows(h2_p, h2_s, n_p), lg_p, lg_s, g2_p, g2_s,
                                   lnrow(ln_g[0, 1]), lnrow(ln_b[0, 1]), router_b, moe_in_bf, moe_out_bf, 0, seq, "l0")

    (sh_p, sc_p, g_p), (sh_s, sc_s, g_s) = mod_rows(1, 0)
    (sh2_p, sc2_p, g2_p), (sh2_s, sc2_s, g2_s) = mod_rows(1, 1)
    rec_in_bf = rec_w_in.astype(BF16)
    rec_out_bf = rec_w_out.astype(BF16)
    gg_p, xbr_p = _rec_in_prompt(x2_p, sc_p, sh_p, rec_in_bf, seq)
    wab = jnp.concatenate([rg_wa, rg_wx], axis=2).astype(BF16)
    row_r = lambda a: a.reshape(1, d_rnn)
    hs_p, ht = _rec_scan_prompt(xbr_p, rec_conv_w, row_r(rec_conv_b), wab, row_r(rg_ba), row_r(rg_bx),
                                row_r(rg_lambda), seq)
    x3_p, h4_p, lg_p = _mix_out_prompt(gg_p, hs_p, rec_out_bf, x2_p, g_p,
                                       lnrow(ln_g[1, 0]), lnrow(ln_b[1, 0]), sc2_p, sh2_p, rw_t, seq, False, db,
                                       "mix_out_l1_prompt")
    conv_p = xbr_p.reshape(bp, seq, d_rnn)[:, seq - (CONV_W - 1):]
    h_p = ht[:, 0, :]

    zr_s = _small_mm(_modulate, [x2_s, sc_s, sh_s], rec_in_bf, 2 * d_rnn, 512, "rec_in_sample")
    y_s, h_s, xbr_s = _vmem_call(
        _rec_step_sample, (sds(db, d_rnn), sds(db, d_rnn), sds(db, d_rnn)),
        (zr_s, state_rglru_conv[:, 0], state_rglru_conv[:, 1], state_rglru_conv[:, 2], state_rglru_h,
         rec_conv_w, row_r(rec_conv_b), rg_wa, rg_wx, row_r(rg_ba), row_r(rg_bx), row_r(rg_lambda)),
        "rec_step_sample")
    f_s = _small_mm(lambda y: y, [y_s], rec_out_bf, d, 512, "mix_out_l1_sample")
    x3_s, h4_s, lg_s = _vmem_call(
        _post_ln_router, (sds(db, d), sds(db, d), sds(N_EXPERTS, db)),
        (f_s, x2_s, g_s, lnrow(ln_g[1, 0]), lnrow(ln_b[1, 0]), sc2_s, sh2_s, rw_t), "post_l1_sample")
    conv_s = jnp.concatenate([state_rglru_conv[:, 1:], xbr_s[:, None, :]], axis=1)

    y_p, y_s = _channel_sublayer(x3_p, x3_s, _append_rows(h4_p, h4_s, n_p), lg_p, lg_s, g2_p, g2_s,
                                 lnrow(ln_g[1, 1]), lnrow(ln_b[1, 1]), router_b, moe_in_bf, moe_out_bf, 1, seq, "l1")

    kv5 = lambda a, b_: a.reshape(b_, -1, 2, N_KV, HEAD_DIM)
    keep = min(WINDOW, seq)
    win_p = kv5(kvw_p, bp)[:, seq - keep:]
    kw_full = jnp.concatenate([state_nsa_win, kv5(kvw_s, db)], axis=1)
    win_s = kw_full[:, kw_full.shape[1] - min(WINDOW, kw_full.shape[1]):]
    return (y_p.reshape(bp, seq, d), y_s.reshape(db, 1, d), kv5(kvc_p, bp), kv5(kvc_s, db), kv5(kvs_p, bp),
            kv5(kvs_s, db), win_p, win_s, v_s.reshape(db, 1, A_WIDTH), conv_p, conv_s, h_p, h_s)
```
